```python
import math
import jax, jax.numpy as jnp
from jax import lax
import numpy as np

D_MODEL = 1024
BATCH = 4
SEQ = 4096
DEPTH = 2

HEAD_DIM = 64
N_Q_HEADS = D_MODEL // HEAD_DIM
GQA_GROUP = 4
N_KV_HEADS = N_Q_HEADS // GQA_GROUP
WINDOW = 128
BLOCK = 128
NUM_BUCKETS = 32
MAX_DISTANCE = 128
ATTN_IN = (N_Q_HEADS + 2 * N_KV_HEADS) * HEAD_DIM
MLSTM_HEADS = 8
MLSTM_DV = D_MODEL // MLSTM_HEADS
MLSTM_DQK = MLSTM_DV // 2
MLSTM_CHUNK = 64
MLSTM_IN = 2 * MLSTM_HEADS * MLSTM_DQK + 2 * MLSTM_HEADS * MLSTM_DV + 4 * MLSTM_HEADS
N_EXPERTS = 16
CAPACITY_FACTOR = 2
D_EXPERT = 2 * D_MODEL
N_MIXERS = 2
N_ATTN_LAYERS = (DEPTH + 1) // 2
N_MLSTM_LAYERS = DEPTH // 2
RMS_EPS = 1e-6

kernel_name = "hybrid_swa_mlstm_ecmoe_encoder"


def rms_norm(x, g):
    xf = x.astype(jnp.float32)
    y = xf * lax.rsqrt(jnp.mean(xf * xf, axis=-1, keepdims=True) + RMS_EPS)
    return (y * g.astype(jnp.float32)).astype(x.dtype)


def t5_bucket(rel):
    nb = NUM_BUCKETS // 2
    ret = (rel > 0).astype(jnp.int32) * nb
    n = jnp.abs(rel)
    max_exact = nb // 2
    nf = jnp.maximum(n, 1).astype(jnp.float32)
    large = max_exact + (jnp.log(nf / max_exact) / math.log(MAX_DISTANCE / max_exact)
                         * (nb - max_exact)).astype(jnp.int32)
    large = jnp.minimum(large, nb - 1)
    return ret + jnp.where(n < max_exact, n, large)


def windowed_gqa(h, w_in, q_g, k_g, sink, w_out, rel_bias):
    B, S, _ = h.shape
    nb = S // BLOCK
    proj = h @ w_in
    q = proj[..., :N_Q_HEADS * HEAD_DIM].reshape(B, S, N_KV_HEADS, GQA_GROUP, HEAD_DIM)
    k = proj[..., N_Q_HEADS * HEAD_DIM:(N_Q_HEADS + N_KV_HEADS) * HEAD_DIM].reshape(B, S, N_KV_HEADS, HEAD_DIM)
    v = proj[..., (N_Q_HEADS + N_KV_HEADS) * HEAD_DIM:].reshape(B, S, N_KV_HEADS, HEAD_DIM)
    q = rms_norm(q, q_g) * (HEAD_DIM ** -0.5)
    k = rms_norm(k, k_g)
    qb = q.reshape(B, nb, BLOCK, N_KV_HEADS, GQA_GROUP, HEAD_DIM)

    def band(t):
        tp = jnp.pad(t, ((0, 0), (BLOCK, BLOCK), (0, 0), (0, 0))).reshape(B, nb + 2, BLOCK, N_KV_HEADS, HEAD_DIM)
        return jnp.concatenate([tp[:, :-2], tp[:, 1:-1], tp[:, 2:]], axis=2)

    kw, vw = band(k), band(v)
    qq = jnp.arange(BLOCK)[:, None]
    kk = jnp.arange(3 * BLOCK)[None, :]
    rel = kk - BLOCK - qq
    bias = rel_bias.astype(jnp.float32)[t5_bucket(rel)]
    bias = bias.transpose(2, 0, 1).reshape(N_KV_HEADS, GQA_GROUP, BLOCK, 3 * BLOCK)
    kpos = jnp.arange(nb)[:, None] * BLOCK + jnp.arange(3 * BLOCK)[None, :] - BLOCK
    mask = (jnp.abs(rel) <= WINDOW)[None] & ((kpos >= 0) & (kpos < S))[:, None, :]

    s = jnp.einsum('bnqhgd,bnkhd->bnhgqk', qb, kw).astype(jnp.float32) + bias
    s = jnp.where(mask[None, :, None, None], s, -jnp.inf)
    sk = sink.astype(jnp.float32).reshape(1, 1, N_KV_HEADS, GQA_GROUP, 1)
    m = jnp.maximum(jnp.max(s, axis=-1), sk)
    p = jnp.exp(s - m[..., None])
    den = jnp.sum(p, axis=-1) + jnp.exp(sk - m)
    p = p / den[..., None]
    o = jnp.einsum('bnhgqk,bnkhd->bnqhgd', p, vw.astype(jnp.float32))
    o = o.reshape(B, S, N_Q_HEADS * HEAD_DIM).astype(h.dtype)
    return o @ w_out


def mlstm_chunk_scan(q, k, v, ig, lf):
    B, H, S, dqk = q.shape
    dv = v.shape[-1]
    nc = S // MLSTM_CHUNK
    L = MLSTM_CHUNK

    def chunks(t):
        t = t.reshape((B, H, nc, L) + t.shape[3:])
        return jnp.moveaxis(t, 2, 0)

    tril = jnp.tril(jnp.ones((L, L), dtype=bool))

    def step(carry, inp):
        C, n, m = carry
        qc, kc, vc, ic, fc = inp
        b = jnp.cumsum(fc, axis=-1)
        Dm = jnp.where(tril, b[..., :, None] - b[..., None, :] + ic[..., None, :], -jnp.inf)
        a = b + m[..., None]
        mt = jnp.maximum(a, jnp.max(Dm, axis=-1))
        W = jnp.exp(Dm - mt[..., None])
        ea = jnp.exp(a - mt)
        Sw = W * jnp.einsum('bhld,bhsd->bhls', qc, kc)
        num = ea[..., None] * jnp.einsum('bhld,bhde->bhle', qc, C) + jnp.einsum('bhls,bhse->bhle', Sw, vc)
        den = ea * jnp.einsum('bhld,bhd->bhl', qc, n) + jnp.sum(Sw, axis=-1)
        hc = num / jnp.maximum(jnp.abs(den), jnp.exp(-mt))[..., None]
        bL = b[..., -1]
        g = bL[..., None] - b + ic
        m_new = jnp.maximum(bL + m, jnp.max(g, axis=-1))
        wC = jnp.exp(g - m_new[..., None])
        decay = jnp.exp(bL + m - m_new)
        C_new = decay[..., None, None] * C + jnp.einsum('bhs,bhsd,bhse->bhde', wC, kc, vc)
        n_new = decay[..., None] * n + jnp.einsum('bhs,bhsd->bhd', wC, kc)
        return (C_new, n_new, m_new), hc

    init = (jnp.zeros((B, H, dqk, dv), jnp.float32), jnp.zeros((B, H, dqk), jnp.float32),
            jnp.zeros((B, H), jnp.float32))
    _, hs = lax.scan(step, init, (chunks(q), chunks(k), chunks(v), chunks(ig), chunks(lf)))
    return jnp.moveaxis(hs, 0, 2).reshape(B, H, S, dv)


def bidir_mlstm(h, w_in, b_i, b_f, out_g, w_out):
    B, S, _ = h.shape
    proj = (h @ w_in).astype(jnp.float32)
    o1 = MLSTM_HEADS * MLSTM_DQK
    o2 = 2 * o1
    o3 = o2 + MLSTM_HEADS * MLSTM_DV
    o4 = o3 + MLSTM_HEADS * MLSTM_DV
    heads = lambda t, d: t.reshape(B, S, MLSTM_HEADS, d).transpose(0, 2, 1, 3)
    q = heads(proj[..., :o1], MLSTM_DQK)
    k = heads(proj[..., o1:o2], MLSTM_DQK) * (MLSTM_DQK ** -0.5)
    v = heads(proj[..., o2:o3], MLSTM_DV)
    og = proj[..., o3:o4]
    gates = proj[..., o4:].reshape(B, S, 4, MLSTM_HEADS).transpose(2, 0, 3, 1)
    bi = b_i.astype(jnp.float32)[:, None, :, None]
    bf = b_f.astype(jnp.float32)[:, None, :, None]
    ig_f, ig_b = gates[0] + bi[0], gates[2] + bi[1]
    lf_f, lf_b = jax.nn.log_sigmoid(gates[1] + bf[0]), jax.nn.log_sigmoid(gates[3] + bf[1])
    h_f = mlstm_chunk_scan(q, k, v, ig_f, lf_f)
    fl = lambda t: jnp.flip(t, axis=2)
    h_b = fl(mlstm_chunk_scan(fl(q), fl(k), fl(v), fl(ig_b), fl(lf_b)))
    hs = (h_f + h_b).transpose(0, 2, 1, 3)
    hs = rms_norm(hs, out_g.reshape(MLSTM_HEADS, MLSTM_DV)).reshape(B, S, MLSTM_HEADS * MLSTM_DV)
    y = (hs * jax.nn.sigmoid(og)).astype(h.dtype)
    return y @ w_out


def expert_choice_moe(h, w_r, w1, w3, w2):
    B, T, D = h.shape
    cap = CAPACITY_FACTOR * T // N_EXPERTS
    aff = jax.nn.softmax(jnp.einsum('btd,de->bte', h, w_r).astype(jnp.float32), axis=-1)
    g, idx = lax.top_k(jnp.swapaxes(aff, 1, 2), cap)
    xs = jax.vmap(lambda hb, ib: hb[ib])(h, idx)
    a = jnp.einsum('becd,edf->becf', xs, w1)
    u = jnp.einsum('becd,edf->becf', xs, w3)
    y = jnp.einsum('becf,efd->becd', jax.nn.silu(a) * u, w2)
    y = y * g[..., None].astype(y.dtype)
    return jax.vmap(lambda ib, yb: jnp.zeros((T, D), yb.dtype).at[ib.reshape(-1)].add(yb.reshape(-1, D)))(idx, y)


def setup_inputs(seed: int = 0) -> dict:
    key = jax.random.key(seed)
    ks = jax.random.split(key, 20)
    nrm = lambda k, shape, scale: jax.random.normal(k, shape, jnp.float32) * scale
    NA, NM, D = N_ATTN_LAYERS, N_MLSTM_LAYERS, D_MODEL
    return {
        "x": nrm(ks[0], (BATCH, SEQ, D), 1.0),
        "rel_bias": nrm(ks[1], (NUM_BUCKETS, N_Q_HEADS), 0.5),
        "attn_norm_g": 1.0 + nrm(ks[2], (NA, D), 0.02),
        "attn_w_in": nrm(ks[3], (NA, D, ATTN_IN), D ** -0.5),
        "attn_q_norm_g": 1.0 + nrm(ks[4], (NA, HEAD_DIM), 0.02),
        "attn_k_norm_g": 1.0 + nrm(ks[5], (NA, HEAD_DIM), 0.02),
        "attn_sink": nrm(ks[6], (NA, N_Q_HEADS), 0.5),
        "attn_w_out": nrm(ks[7], (NA, N_Q_HEADS * HEAD_DIM, D), (N_Q_HEADS * HEAD_DIM) ** -0.5),
        "mlstm_norm_g": 1.0 + nrm(ks[8], (NM, D), 0.02),
        "mlstm_w_in": nrm(ks[9], (NM, D, MLSTM_IN), D ** -0.5),
        "mlstm_b_i": nrm(ks[10], (NM, 2, MLSTM_HEADS), 0.1),
        "mlstm_b_f": jnp.linspace(3.0, 6.0, MLSTM_HEADS, dtype=jnp.float32)[None, None]
                      + nrm(ks[11], (NM, 2, MLSTM_HEADS), 0.1),
        "mlstm_out_norm_g": 1.0 + nrm(ks[12], (NM, MLSTM_HEADS * MLSTM_DV), 0.02),
        "mlstm_w_out": nrm(ks[13], (NM, MLSTM_HEADS * MLSTM_DV, D), (MLSTM_HEADS * MLSTM_DV) ** -0.5),
        "ffn_norm_g": 1.0 + nrm(ks[14], (DEPTH, D), 0.02),
        "router_w": nrm(ks[15], (DEPTH, D, N_EXPERTS), D ** -0.5),
        "expert_w1": nrm(ks[16], (DEPTH, N_EXPERTS, D, D_EXPERT), D ** -0.5),
        "expert_w3": nrm(ks[17], (DEPTH, N_EXPERTS, D, D_EXPERT), D ** -0.5),
        "expert_w2": nrm(ks[18], (DEPTH, N_EXPERTS, D_EXPERT, D), D_EXPERT ** -0.5),
    }


def reference(x, rel_bias, attn_norm_g, attn_w_in, attn_q_norm_g, attn_k_norm_g, attn_sink, attn_w_out,
              mlstm_norm_g, mlstm_w_in, mlstm_b_i, mlstm_b_f, mlstm_out_norm_g, mlstm_w_out,
              ffn_norm_g, router_w, expert_w1, expert_w3, expert_w2):
    for i in range(DEPTH):
        j = i // N_MIXERS
        if i % N_MIXERS == 0:
            x = x + windowed_gqa(rms_norm(x, attn_norm_g[j]), attn_w_in[j], attn_q_norm_g[j],
                                 attn_k_norm_g[j], attn_sink[j], attn_w_out[j], rel_bias)
        else:
            x = x + bidir_mlstm(rms_norm(x, mlstm_norm_g[j]), mlstm_w_in[j], mlstm_b_i[j], mlstm_b_f[j],
                                mlstm_out_norm_g[j], mlstm_w_out[j])
        x = x + expert_choice_moe(rms_norm(x, ffn_norm_g[i]), router_w[i], expert_w1[i], expert_w3[i],
                                  expert_w2[i])
    return x
```

```python
import functools
import math

import jax
import jax.numpy as jnp
from jax import lax
from jax.experimental import pallas as pl
from jax.experimental.pallas import tpu as pltpu

F32 = jnp.float32
BF16 = jnp.bfloat16
I32 = jnp.int32

RMS_EPS = 1e-6
NEG = -1e30
LANE = 128
VMEM_LIMIT = 56 * 1024 * 1024

HEAD_DIM = 64
GQA_GROUP = 4
ATT_BLOCK = 128
NUM_BUCKETS = 32
MAX_DISTANCE = 128
N_EXPERTS = 16
CAPACITY_FACTOR = 2
MLSTM_HEADS = 8


def _cparams(n_axes, vmem=VMEM_LIMIT):
    return pltpu.CompilerParams(dimension_semantics=("arbitrary",) * n_axes, vmem_limit_bytes=vmem)


def _rms(x, g):
    return x * lax.rsqrt(jnp.mean(x * x, axis=-1, keepdims=True) + RMS_EPS) * g


def _dot(a, b):
    return jnp.dot(a, b, preferred_element_type=F32)


def _dot_nt(a, b):
    return lax.dot_general(a, b, (((1,), (1,)), ((), ())), preferred_element_type=F32)


def _norm_mm_kernel(x_ref, g_ref, w_ref, o_ref):
    h = _rms(x_ref[...], g_ref[...])
    o_ref[...] = _dot(h.astype(BF16), w_ref[...]).astype(o_ref.dtype)


def norm_matmul(x, g, w, out_dtype, tm=512):
    n, d = x.shape
    m = w.shape[1]
    return pl.pallas_call(
        _norm_mm_kernel,
        grid=(n // tm,),
        in_specs=[pl.BlockSpec((tm, d), lambda i: (i, 0)),
                  pl.BlockSpec((1, d), lambda i: (0, 0)),
                  pl.BlockSpec((d, m), lambda i: (0, 0))],
        out_specs=pl.BlockSpec((tm, m), lambda i: (i, 0)),
        out_shape=jax.ShapeDtypeStruct((n, m), out_dtype),
        compiler_params=_cparams(1),
    )(x, g.reshape(1, d), w)


def _mm_res_router_kernel(a_ref, w_ref, x_ref, g_ref, wr_ref, x1_ref, hn_ref, aff_ref):
    x1 = x_ref[...] + _dot(a_ref[...], w_ref[...])
    x1_ref[...] = x1
    hn = _rms(x1, g_ref[...])
    hn_ref[...] = hn
    logits = _dot_nt(wr_ref[...], hn.astype(BF16))
    mx = jnp.max(logits, axis=0, keepdims=True)
    p = jnp.exp(logits - mx)
    aff_ref[0] = p / jnp.sum(p, axis=0, keepdims=True)


def mm_res_router(a, w, x, g, wr_t, batch, tm=512):
    n, d = x.shape
    k = a.shape[1]
    e = wr_t.shape[0]
    t = n // batch
    tpb = t // tm
    return pl.pallas_call(
        _mm_res_router_kernel,
        grid=(n // tm,),
        in_specs=[pl.BlockSpec((tm, k), lambda i: (i, 0)),
                  pl.BlockSpec((k, d), lambda i: (0, 0)),
                  pl.BlockSpec((tm, d), lambda i: (i, 0)),
                  pl.BlockSpec((1, d), lambda i: (0, 0)),
                  pl.BlockSpec((e, d), lambda i: (0, 0))],
        out_specs=[pl.BlockSpec((tm, d), lambda i: (i, 0)),
                   pl.BlockSpec((tm, d), lambda i: (i, 0)),
                   pl.BlockSpec((1, e, tm), lambda i: (i // tpb, 0, i % tpb))],
        out_shape=[jax.ShapeDtypeStruct((n, d), F32),
                   jax.ShapeDtypeStruct((n, d), F32),
                   jax.ShapeDtypeStruct((batch, e, t), F32)],
        compiler_params=_cparams(1),
    )(a, w, x, g.reshape(1, d), wr_t)


def _t5_bucket(rel):
    nb = NUM_BUCKETS // 2
    ret = (rel > 0).astype(jnp.int32) * nb
    n = jnp.abs(rel)
    max_exact = nb // 2
    nf = jnp.maximum(n, 1).astype(jnp.float32)
    large = max_exact + (jnp.log(nf / max_exact) / math.log(MAX_DISTANCE / max_exact)
                         * (nb - max_exact)).astype(jnp.int32)
    large = jnp.minimum(large, nb - 1)
    return ret + jnp.where(n < max_exact, n, large)


def _attn_bias_table(rel_bias):
    qq = jnp.arange(ATT_BLOCK)[:, None]
    kk = jnp.arange(3 * ATT_BLOCK)[None, :]
    rel = kk - ATT_BLOCK - qq
    bias = rel_bias.astype(F32)[_t5_bucket(rel)]
    bias = jnp.where((jnp.abs(rel) <= ATT_BLOCK)[..., None], bias, NEG)
    return bias.transpose(2, 0, 1)


def _attn_kernel(sink_ref, q_ref, kp_ref, kc_ref, kn_ref, bias_ref, qg_ref, kg_ref, o_ref, *, nb, n_kv):
    n = pl.program_id(1)
    blk = ATT_BLOCK
    dh = HEAD_DIM
    kv = jnp.concatenate([kp_ref[...], kc_ref[...], kn_ref[...]], axis=0)
    col = lax.broadcasted_iota(jnp.int32, (1, 3 * blk), 1)
    valid = jnp.logical_and(jnp.logical_or(n > 0, col >= blk),
                            jnp.logical_or(n < nb - 1, col < 2 * blk))
    edge = jnp.where(valid, 0.0, NEG)
    qg = qg_ref[...] * (dh ** -0.5)
    kg = kg_ref[...]
    for h in range(n_kv):
        k = _rms(kv[:, h * dh:(h + 1) * dh].astype(F32), kg).astype(BF16)
        v = kv[:, (n_kv + h) * dh:(n_kv + h + 1) * dh]
        for g in range(GQA_GROUP):
            hq = h * GQA_GROUP + g
            q = _rms(q_ref[:, hq * dh:(hq + 1) * dh].astype(F32), qg).astype(BF16)
            s = _dot_nt(q, k) + bias_ref[hq] + edge
            sk = sink_ref[hq]
            m = jnp.maximum(jnp.max(s, axis=-1, keepdims=True), sk)
            p = jnp.exp(s - m)
            den = jnp.sum(p, axis=-1, keepdims=True) + jnp.exp(sk - m)
            o = _dot(p.astype(BF16), v) / den
            o_ref[:, hq * dh:(hq + 1) * dh] = o.astype(o_ref.dtype)


def attention_core(proj, bias, q_g, k_g, sink, batch, seq):
    n, width = proj.shape
    hq = bias.shape[0]
    n_kv = hq // GQA_GROUP
    dq = hq * HEAD_DIM
    dkv = 2 * n_kv * HEAD_DIM
    assert width == dq + dkv and dq % dkv == 0
    kvcol = dq // dkv
    nb = seq // ATT_BLOCK
    blk = ATT_BLOCK
    row = lambda b, i: b * nb + i
    return pl.pallas_call(
        functools.partial(_attn_kernel, nb=nb, n_kv=n_kv),
        grid=(batch, nb),
        in_specs=[pl.BlockSpec(memory_space=pltpu.SMEM),
                  pl.BlockSpec((blk, dq), lambda b, i: (row(b, i), 0)),
                  pl.BlockSpec((blk, dkv), lambda b, i: (row(b, jnp.maximum(i - 1, 0)), kvcol)),
                  pl.BlockSpec((blk, dkv), lambda b, i: (row(b, i), kvcol)),
                  pl.BlockSpec((blk, dkv), lambda b, i: (row(b, jnp.minimum(i + 1, nb - 1)), kvcol)),
                  pl.BlockSpec((hq, blk, 3 * blk), lambda b, i: (0, 0, 0)),
                  pl.BlockSpec((1, HEAD_DIM), lambda b, i: (0, 0)),
                  pl.BlockSpec((1, HEAD_DIM), lambda b, i: (0, 0))],
        out_specs=pl.BlockSpec((blk, dq), lambda b, i: (row(b, i), 0)),
        out_shape=jax.ShapeDtypeStruct((n, dq), BF16),
        compiler_params=_cparams(2),
    )(sink.astype(F32), proj, proj, proj, proj, bias,
      q_g.reshape(1, HEAD_DIM).astype(F32), k_g.reshape(1, HEAD_DIM).astype(F32))


def _prefix_incl(x, tri):
    t = x.shape[1]
    outs = []
    carry = jnp.zeros((x.shape[0], 1), F32)
    for c in range(t // LANE):
        p = _dot(x[:, c * LANE:(c + 1) * LANE].astype(BF16), tri) + carry
        outs.append(p)
        carry = p[:, LANE - 1:LANE]
    return jnp.concatenate(outs, axis=1)


def _select_kernel(aff_ref, cmp_ref, offc_ref, rng_ref, pos_sc, dest_sc, vals_sc, *, cap, rchunk):
    aff = aff_ref[0]
    ne, t = aff.shape
    bits = pltpu.bitcast(aff, I32)

    def search(i, lo):
        cand = lo | lax.shift_left(jnp.int32(1), 30 - i)
        cnt = jnp.sum((bits >= cand).astype(I32), axis=1, keepdims=True)
        return jnp.where(cnt >= cap, cand, lo)

    thr = lax.fori_loop(0, 31, search, jnp.zeros((ne, 1), I32))
    gt = bits > thr
    eq = bits == thr
    need = (cap - jnp.sum(gt.astype(I32), axis=1, keepdims=True)).astype(F32)
    ri = lax.broadcasted_iota(I32, (LANE, LANE), 0)
    ci = lax.broadcasted_iota(I32, (LANE, LANE), 1)
    tri = (ri <= ci).astype(BF16)
    eqf = eq.astype(F32)
    rank_eq = _prefix_incl(eqf, tri) - eqf
    sel = jnp.logical_or(gt, jnp.logical_and(eq, rank_eq < need))
    self_ = sel.astype(F32)
    kt = jnp.sum(self_, axis=0, keepdims=True)
    pre = _prefix_incl(jnp.concatenate([self_, jnp.broadcast_to(kt, (8, t))], axis=0), tri)
    pos_sc[...] = jnp.where(sel, pre[:ne] - 1.0, -1.0)
    end = pre[ne:ne + 1]
    off = end - kt
    lr = lax.broadcasted_iota(I32, (ne, ne), 0)
    lc = lax.broadcasted_iota(I32, (ne, ne), 1)
    jexp = _dot((lc < lr).astype(BF16), self_.astype(BF16))
    dest_sc[...] = off + jexp

    tok = lax.broadcasted_iota(I32, (1, t), 1)
    vals_sc[0:1, :] = (tok >> 6).astype(F32)
    vals_sc[1:2, :] = (tok & 63).astype(F32)
    vals_sc[7:8, :] = jnp.zeros((1, t), F32)
    slot = lax.broadcasted_iota(I32, (cap, 1), 0).astype(F32)

    def compact(e, carry):
        d = dest_sc[pl.ds(e, 1), :]
        dh = jnp.floor(d * (1.0 / LANE))
        g = aff_ref[0, pl.ds(e, 1), :]
        g0 = g.astype(BF16).astype(F32)
        g1 = (g - g0).astype(BF16).astype(F32)
        vals_sc[2:3, :] = dh
        vals_sc[3:4, :] = d - dh * LANE
        vals_sc[4:5, :] = g0
        vals_sc[5:6, :] = g1
        vals_sc[6:7, :] = g - g0 - g1
        onehot = (pos_sc[pl.ds(e, 1), :] == slot).astype(BF16)
        cmp_ref[0, e] = _dot_nt(vals_sc[...].astype(BF16), onehot)
        return carry

    lax.fori_loop(0, ne, compact, 0)

    eh = jnp.floor(end * (1.0 / LANE))
    oh = jnp.floor(off * (1.0 / LANE))
    v4 = jnp.concatenate([oh, off - oh * LANE, eh, end - eh * LANE, jnp.zeros((4, t), F32)], axis=0)
    eye = (ri == ci).astype(BF16)
    for c in range(t // LANE):
        offc_ref[0, c * LANE:(c + 1) * LANE, :] = _dot_nt(eye, v4[:, c * LANE:(c + 1) * LANE].astype(BF16))

    nchunk = (ne * cap) // rchunk
    low = (lax.broadcasted_iota(I32, (nchunk, 1), 0) * rchunk).astype(F32)
    first = jnp.sum((end <= low).astype(F32), axis=1, keepdims=True)
    last = jnp.sum((end <= low + (rchunk - 1)).astype(F32), axis=1, keepdims=True)
    lane = lax.broadcasted_iota(I32, (nchunk, LANE), 1)
    tiles = jnp.where(lane < LANE // 2, jnp.floor(first * (1.0 / LANE)), jnp.floor(last * (1.0 / LANE)))
    rng_ref[0] = tiles.astype(I32)


def route_select(aff, cap, rchunk=128):
    batch, ne, t = aff.shape
    nchunk = ne * cap // rchunk
    return pl.pallas_call(
        functools.partial(_select_kernel, cap=cap, rchunk=rchunk),
        grid=(batch,),
        in_specs=[pl.BlockSpec((1, ne, t), lambda b: (b, 0, 0))],
        out_specs=[pl.BlockSpec((1, ne, 8, cap), lambda b: (b, 0, 0, 0)),
                   pl.BlockSpec((1, t, 8), lambda b: (b, 0, 0)),
                   pl.BlockSpec((1, nchunk, LANE), lambda b: (b, 0, 0))],
        out_shape=[jax.ShapeDtypeStruct((batch, ne, 8, cap), F32),
                   jax.ShapeDtypeStruct((batch, t, 8), F32),
                   jax.ShapeDtypeStruct((batch, nchunk, LANE), I32)],
        scratch_shapes=[pltpu.VMEM((ne, t), F32), pltpu.VMEM((ne, t), F32), pltpu.VMEM((8, t), F32)],
        compiler_params=_cparams(1),
    )(aff)


def _ffn_kernel(idx_sm, dest_sm, hn_hbm, cmp_ref, w1_ref, w3_ref, w2_ref, r_hbm,
                xs32, xsb, yacc, ysc, wb1, wb3, wb2, gsem, ssem, *, batch, cap, nf, rt):
    e = pl.program_id(0)
    f = pl.program_id(1)
    ne = pl.num_programs(0)
    rows = batch * cap
    base = e * rows

    def scatter_wait():
        pltpu.make_async_copy(ysc, r_hbm.at[pl.ds(0, rows), :], ssem).wait()

    @pl.when(f == 0)
    def _gather():
        def issue(r, c):
            t = idx_sm[base + r]
            pltpu.make_async_copy(hn_hbm.at[pl.ds(t, 1), :], xs32.at[pl.ds(r, 1), :], gsem).start()
            return c
        lax.fori_loop(0, rows, issue, 0)
        pltpu.make_async_copy(hn_hbm.at[pl.ds(0, rows), :], xs32, gsem).wait()
        xsb[...] = xs32[...].astype(BF16)

    wb1[...] = w1_ref[0].astype(BF16)
    wb3[...] = w3_ref[0].astype(BF16)
    wb2[...] = w2_ref[0].astype(BF16)
    for r in range(rows // rt):
        xs = xsb[r * rt:(r + 1) * rt, :]
        a = _dot(xs, wb1[...])
        u = _dot(xs, wb3[...])
        hmid = (a * jax.nn.sigmoid(a) * u).astype(BF16)
        y = _dot(hmid, wb2[...])

        @pl.when(f == 0)
        def _first():
            yacc[r * rt:(r + 1) * rt, :] = y

        @pl.when(f > 0)
        def _rest():
            yacc[r * rt:(r + 1) * rt, :] += y

    @pl.when(f == nf - 1)
    def _scatter():
        @pl.when(e > 0)
        def _prev():
            scatter_wait()
        ri = lax.broadcasted_iota(I32, (cap, cap), 0)
        ci = lax.broadcasted_iota(I32, (cap, cap), 1)
        eye = (ri == ci).astype(BF16)
        for b in range(batch):
            gt = _dot_nt(eye, cmp_ref[b, 0].astype(BF16))
            g = gt[:, 4:5] + gt[:, 5:6] + gt[:, 6:7]
            ysc[b * cap:(b + 1) * cap, :] = yacc[b * cap:(b + 1) * cap, :] * g

        def issue(r, c):
            d = dest_sm[base + r]
            pltpu.make_async_copy(ysc.at[pl.ds(r, 1), :], r_hbm.at[pl.ds(d, 1), :], ssem).start()
            return c
        lax.fori_loop(0, rows, issue, 0)

        @pl.when(e == ne - 1)
        def _last():
            scatter_wait()


def expert_ffn(hn, cmp, idx_flat, dest_flat, w1, w3, w2, cap, tf=512, rt=512):
    n, d = hn.shape
    batch, ne = cmp.shape[0], cmp.shape[1]
    dff = w1.shape[2]
    nf = dff // tf
    rows = batch * cap
    rt = min(rt, rows)
    grid_spec = pltpu.PrefetchScalarGridSpec(
        num_scalar_prefetch=2,
        grid=(ne, nf),
        in_specs=[pl.BlockSpec(memory_space=pl.ANY),
                  pl.BlockSpec((batch, 1, 8, cap), lambda e, f, *_: (0, e, 0, 0)),
                  pl.BlockSpec((1, d, tf), lambda e, f, *_: (e, 0, f)),
                  pl.BlockSpec((1, d, tf), lambda e, f, *_: (e, 0, f)),
                  pl.BlockSpec((1, tf, d), lambda e, f, *_: (e, f, 0))],
        out_specs=pl.BlockSpec(memory_space=pl.ANY),
        scratch_shapes=[pltpu.VMEM((rows, d), F32), pltpu.VMEM((rows, d), BF16),
                        pltpu.VMEM((rows, d), F32), pltpu.VMEM((rows, d), F32),
                        pltpu.VMEM((d, tf), BF16), pltpu.VMEM((d, tf), BF16), pltpu.VMEM((tf, d), BF16),
                        pltpu.SemaphoreType.DMA, pltpu.SemaphoreType.DMA])
    return pl.pallas_call(
        functools.partial(_ffn_kernel, batch=batch, cap=cap, nf=nf, rt=rt),
        grid_spec=grid_spec,
        out_shape=jax.ShapeDtypeStruct((batch * ne * cap, d), F32),
        compiler_params=_cparams(2),
    )(idx_flat, dest_flat, hn, cmp, w1, w3, w2)


def _combine_kernel(tlo_sm, thi_sm, r_ref, offc_ref, x_hbm, o_ref, sem, *, nchunk, rchunk):
    b = pl.program_id(0)
    j = pl.program_id(1)

    @pl.when(j == 0)
    def _init():
        cp = pltpu.make_async_copy(x_hbm.at[b], o_ref.at[0], sem)
        cp.start()
        cp.wait()

    r = r_ref[...]
    hi = r.astype(BF16)
    lo = (r - hi.astype(F32)).astype(BF16)
    rowid = (j * rchunk + lax.broadcasted_iota(I32, (1, rchunk), 1)).astype(F32)

    def tile(i, c):
        t0 = pl.multiple_of(i * LANE, LANE)
        oc = offc_ref[0, pl.ds(t0, LANE), :]
        off = oc[:, 0:1] * LANE + oc[:, 1:2]
        end = oc[:, 2:3] * LANE + oc[:, 3:4]
        p = jnp.logical_and(rowid >= off, rowid < end).astype(BF16)
        o_ref[0, pl.ds(t0, LANE), :] += _dot(p, hi) + _dot(p, lo)
        return c

    lax.fori_loop(tlo_sm[b * nchunk + j], thi_sm[b * nchunk + j] + 1, tile, 0)


def combine(rbuf, offc, tlo, thi, x, rchunk=128):
    batch, t, d = x.shape
    nchunk = rbuf.shape[0] // batch // rchunk
    grid_spec = pltpu.PrefetchScalarGridSpec(
        num_scalar_prefetch=2,
        grid=(batch, nchunk),
        in_specs=[pl.BlockSpec((rchunk, d), lambda b, j, *_: (b * nchunk + j, 0)),
                  pl.BlockSpec((1, t, 8), lambda b, j, *_: (b, 0, 0)),
                  pl.BlockSpec(memory_space=pl.ANY)],
        out_specs=pl.BlockSpec((1, t, d), lambda b, j, *_: (b, 0, 0)),
        scratch_shapes=[pltpu.SemaphoreType.DMA])
    return pl.pallas_call(
        functools.partial(_combine_kernel, nchunk=nchunk, rchunk=rchunk),
        grid_spec=grid_spec,
        out_shape=jax.ShapeDtypeStruct((batch, t, d), F32),
        compiler_params=_cparams(2),
    )(tlo, thi, rbuf, offc, x)


def moe_block(x1, hn, aff, w1, w3, w2, batch):
    n, d = x1.shape
    t = n // batch
    ne = aff.shape[1]
    cap = CAPACITY_FACTOR * t // ne
    cmp, offc, rng = route_select(aff, cap)
    ci = cmp.astype(I32)
    boff = jnp.arange(batch, dtype=I32)[:, None, None]
    idx = (ci[:, :, 0] * 64 + ci[:, :, 1] + boff * t).transpose(1, 0, 2).reshape(-1)
    dest = (ci[:, :, 2] * LANE + ci[:, :, 3] + boff * (ne * cap)).transpose(1, 0, 2).reshape(-1)
    rbuf = expert_ffn(hn, cmp, idx, dest, w1, w3, w2, cap)
    tlo = rng[:, :, 0].reshape(-1)
    thi = rng[:, :, LANE - 1].reshape(-1)
    return combine(rbuf, offc, tlo, thi, x1.reshape(batch, t, d))


MLSTM_L = 256


def _mlstm_proj_kernel(x_ref, g_ref, wm_ref, wkt_ref, wgc_ref, wgt_ref, main_ref, kt_ref, gc_ref, gt_ref, *, nsub):
    h = _rms(x_ref[...], g_ref[...]).astype(BF16)
    main_ref[...] = _dot(h, wm_ref[...]).astype(main_ref.dtype)
    gc_ref[...] = _dot(h, wgc_ref[...])
    kt = _dot_nt(wkt_ref[...], h).astype(kt_ref.dtype)
    gt = _dot_nt(wgt_ref[...], h)
    for j in range(nsub):
        kt_ref[j] = kt[:, j * MLSTM_L:(j + 1) * MLSTM_L]
        gt_ref[j] = gt[:, j * MLSTM_L:(j + 1) * MLSTM_L]


def mlstm_project(x, g, wm, wkt, wgc, wgt, tm=512):
    n, d = x.shape
    nsub = tm // MLSTM_L
    nck = n // MLSTM_L
    const = lambda i: (0, 0)
    return pl.pallas_call(
        functools.partial(_mlstm_proj_kernel, nsub=nsub),
        grid=(n // tm,),
        in_specs=[pl.BlockSpec((tm, d), lambda i: (i, 0)),
                  pl.BlockSpec((1, d), const),
                  pl.BlockSpec(wm.shape, const),
                  pl.BlockSpec(wkt.shape, const),
                  pl.BlockSpec(wgc.shape, const),
                  pl.BlockSpec(wgt.shape, const)],
        out_specs=[pl.BlockSpec((tm, wm.shape[1]), lambda i: (i, 0)),
                   pl.BlockSpec((nsub, wkt.shape[0], MLSTM_L), lambda i: (i, 0, 0)),
                   pl.BlockSpec((tm, LANE), lambda i: (i, 0)),
                   pl.BlockSpec((nsub, wgt.shape[0], MLSTM_L), lambda i: (i, 0, 0))],
        out_shape=[jax.ShapeDtypeStruct((n, wm.shape[1]), BF16),
                   jax.ShapeDtypeStruct((nck, wkt.shape[0], MLSTM_L), BF16),
                   jax.ShapeDtypeStruct((n, LANE), F32),
                   jax.ShapeDtypeStruct((nck, wgt.shape[0], MLSTM_L), F32)],
        compiler_params=_cparams(1),
    )(x, g.reshape(1, d), wm, wkt, wgc, wgt)


def _log_sigmoid(x):
    return jnp.minimum(x, 0.0) - jnp.log1p(jnp.exp(-jnp.abs(x)))


def _dot_hi(a, b):
    return jnp.dot(a, b, preferred_element_type=F32, precision=lax.Precision.HIGHEST)


def _gate_prep_kernel(gc_ref, gt_ref, bc_ref, bt_ref, oc_ref, ot_ref, *, nh):
    ln = MLSTM_L
    ri = lax.broadcasted_iota(I32, (ln, ln), 0)
    ci = lax.broadcasted_iota(I32, (ln, ln), 1)
    low = (ci <= ri).astype(F32)
    upp = (ci >= ri).astype(F32)
    ls = _log_sigmoid(gc_ref[...] + bc_ref[...])
    lane = lax.broadcasted_iota(I32, ls.shape, 1)
    oc_ref[...] = jnp.where(lane < nh, _dot_hi(low, ls), _dot_hi(upp, ls))
    pre = gt_ref[0] + bt_ref[...]
    lst = _log_sigmoid(pre)
    row = lax.broadcasted_iota(I32, pre.shape, 0)
    fwd = jnp.logical_and(row >= nh, row < 2 * nh)
    bwd = row >= 3 * nh
    ot_ref[0] = jnp.where(fwd, _dot_hi(lst, upp), jnp.where(bwd, _dot_hi(lst, low), pre))


def gate_prep(gc, gt, bias_c, bias_t, nh):
    n = gc.shape[0]
    nck, rows, ln = gt.shape
    return pl.pallas_call(
        functools.partial(_gate_prep_kernel, nh=nh),
        grid=(nck,),
        in_specs=[pl.BlockSpec((ln, LANE), lambda c: (c, 0)),
                  pl.BlockSpec((1, rows, ln), lambda c: (c, 0, 0)),
                  pl.BlockSpec((1, LANE), lambda c: (0, 0)),
                  pl.BlockSpec((rows, 1), lambda c: (0, 0))],
        out_specs=[pl.BlockSpec((ln, LANE), lambda c: (c, 0)),
                   pl.BlockSpec((1, rows, ln), lambda c: (c, 0, 0))],
        out_shape=[jax.ShapeDtypeStruct((n, LANE), F32),
                   jax.ShapeDtypeStruct((nck, rows, ln), F32)],
        compiler_params=_cparams(1),
    )(gc, gt, bias_c, bias_t)


def _mlstm_kernel(q_ref, kt_ref, v_ref, og_ref, gc_ref, gt_ref, ng_ref, y_ref, hf, hb, cst, mst, *, nc, nh, dqk, dv):
    p = pl.program_id(1)
    ln = MLSTM_L
    ri = lax.broadcasted_iota(I32, (ln, ln), 0)
    ci = lax.broadcasted_iota(I32, (ln, ln), 1)
    masks = (ci <= ri, ci >= ri)
    lane = lax.broadcasted_iota(I32, (ln, LANE), 1)
    ones_col = (lane == 0).astype(BF16)
    cst[...] = jnp.zeros_like(cst)
    mst[...] = jnp.zeros_like(mst)

    def step(c, carry):
        for hh in range(2):
            head = 2 * p + hh
            for d in range(2):
                chain = 2 * hh + d
                ck = c if d == 0 else nc - 1 - c
                r0 = pl.multiple_of(ck * ln, ln)
                q = q_ref[pl.ds(r0, ln), hh * dqk:(hh + 1) * dqk]
                kt = kt_ref[ck, hh * dqk:(hh + 1) * dqk, :]
                vaug = jnp.concatenate([v_ref[pl.ds(r0, ln), hh * dv:(hh + 1) * dv], ones_col], axis=1)
                gcol = gc_ref[pl.ds(r0, ln), :]
                bcol = jnp.sum(jnp.where(lane == d * nh + head, gcol, 0.0), axis=-1, keepdims=True)
                irow = gt_ref[ck, pl.ds(2 * d * nh + head, 1), :]
                brow = gt_ref[ck, pl.ds((2 * d + 1) * nh + head, 1), :]
                m = mst[chain]
                dm = jnp.where(masks[d], bcol - brow + irow, NEG)
                a = bcol + m
                mt = jnp.maximum(a, jnp.max(dm, axis=-1, keepdims=True))
                w = jnp.exp(dm - mt)
                ea = jnp.exp(a - mt)
                sw = (w * _dot(q, kt)).astype(BF16)
                cprev = cst[chain]
                tot = ea * _dot(q, cprev.astype(BF16)) + _dot(sw, vaug)
                hc = tot[:, :dv] / jnp.maximum(jnp.abs(tot[:, dv:dv + 1]), jnp.exp(-mt))
                edge = ln - 1 if d == 0 else 0
                bl = brow[:, edge:edge + 1]
                g = bl - brow + irow
                m_new = jnp.maximum(bl + m, jnp.max(g, axis=-1, keepdims=True))
                wc = jnp.exp(g - m_new)
                cst[chain] = jnp.exp(bl + m - m_new) * cprev + _dot((kt.astype(F32) * wc).astype(BF16), vaug)
                mst[chain] = m_new
                if d == 0:
                    hf[pl.ds(r0, ln), hh * dv:(hh + 1) * dv] = hc
                else:
                    hb[pl.ds(r0, ln), hh * dv:(hh + 1) * dv] = hc
        return carry

    lax.fori_loop(0, nc, step, 0)
    for hh in range(2):
        sl = slice(hh * dv, (hh + 1) * dv)
        hs = _rms(hf[:, sl] + hb[:, sl], ng_ref[:, sl])
        y_ref[:, sl] = (hs * jax.nn.sigmoid(og_ref[:, sl].astype(F32))).astype(y_ref.dtype)


def mlstm_core(main, ktc, gcol, grow, out_g, batch, seq, nh, dqk, dv):
    n = main.shape[0]
    nc = seq // MLSTM_L
    npair = nh // 2
    qblocks = nh * dqk // (2 * dqk)
    vblocks = (nh * dqk) // (2 * dv)
    ogblocks = (nh * dqk + nh * dv) // (2 * dv)
    return pl.pallas_call(
        functools.partial(_mlstm_kernel, nc=nc, nh=nh, dqk=dqk, dv=dv),
        grid=(batch, npair),
        in_specs=[pl.BlockSpec((seq, 2 * dqk), lambda b, p: (b, p)),
                  pl.BlockSpec((nc, 2 * dqk, MLSTM_L), lambda b, p: (b, p, 0)),
                  pl.BlockSpec((seq, 2 * dv), lambda b, p: (b, vblocks + p)),
                  pl.BlockSpec((seq, 2 * dv), lambda b, p: (b, ogblocks + p)),
                  pl.BlockSpec((seq, LANE), lambda b, p: (b, 0)),
                  pl.BlockSpec((nc, 4 * nh, MLSTM_L), lambda b, p: (b, 0, 0)),
                  pl.BlockSpec((1, 2 * dv), lambda b, p: (0, p))],
        out_specs=pl.BlockSpec((seq, 2 * dv), lambda b, p: (b, p)),
        out_shape=jax.ShapeDtypeStruct((n, nh * dv), BF16),
        scratch_shapes=[pltpu.VMEM((seq, 2 * dv), F32), pltpu.VMEM((seq, 2 * dv), F32),
                        pltpu.VMEM((4, dqk, 2 * dv), F32), pltpu.VMEM((4, 1, 1), F32)],
        compiler_params=_cparams(2),
    )(main, ktc, main, main, gcol, grow, out_g.reshape(1, nh * dv).astype(F32))


def mlstm_layer(x2d, batch, seq, norm_g, w_in, b_i, b_f, out_g, w_out, ffn_g, wr):
    d = x2d.shape[1]
    nh = MLSTM_HEADS
    dv = d // nh
    dqk = dv // 2
    o1, o2, o3, o4 = nh * dqk, 2 * nh * dqk, 2 * nh * dqk + nh * dv, 2 * nh * dqk + 2 * nh * dv
    wm = jnp.concatenate([w_in[:, :o1], w_in[:, o2:o4]], axis=1).astype(BF16)
    wkt = (w_in[:, o1:o2] * (dqk ** -0.5)).T.astype(BF16)
    wg = w_in[:, o4:]
    wgc = jnp.concatenate([wg[:, nh:2 * nh], wg[:, 3 * nh:], jnp.zeros((d, LANE - 2 * nh), F32)], axis=1).astype(BF16)
    wgt = wg.T.astype(BF16)
    main, ktc, gc, gt = mlstm_project(x2d, norm_g, wm, wkt, wgc, wgt)
    bias_c = jnp.concatenate([b_f[0], b_f[1], jnp.zeros((LANE - 2 * nh,), F32)]).reshape(1, LANE).astype(F32)
    bias_t = jnp.concatenate([b_i[0], b_f[0], b_i[1], b_f[1]]).reshape(4 * nh, 1).astype(F32)
    gcol, grow = gate_prep(gc, gt, bias_c, bias_t, nh)
    y = mlstm_core(main, ktc, gcol, grow, out_g, batch, seq, nh, dqk, dv)
    return mm_res_router(y, w_out.astype(BF16), x2d, ffn_g, wr.T.astype(BF16), batch)


def attention_layer(x2d, batch, seq, norm_g, w_in, q_g, k_g, sink, w_out, rel_bias, ffn_g, wr):
    proj = norm_matmul(x2d, norm_g, w_in.astype(BF16), BF16)
    o = attention_core(proj, _attn_bias_table(rel_bias), q_g, k_g, sink, batch, seq)
    return mm_res_router(o, w_out.astype(BF16), x2d, ffn_g, wr.T.astype(BF16), batch)


def kernel(x, rel_bias, attn_norm_g, attn_w_in, attn_q_norm_g, attn_k_norm_g, attn_sink, attn_w_out, mlstm_norm_g, mlstm_w_in, mlstm_b_i, mlstm_b_f, mlstm_out_norm_g, mlstm_w_out, ffn_norm_g, router_w, expert_w1, expert_w3, expert_w2):
    batch, seq, d = x.shape
    x2d = x.reshape(batch * seq, d)
    x1, hn, aff = attention_layer(x2d, batch, seq, attn_norm_g[0], attn_w_in[0], attn_q_norm_g[0],
                                  attn_k_norm_g[0], attn_sink[0], attn_w_out[0], rel_bias,
                                  ffn_norm_g[0], router_w[0])
    x = moe_block(x1, hn, aff, expert_w1[0], expert_w3[0], expert_w2[0], batch)
    x1, hn, aff = mlstm_layer(x.reshape(batch * seq, d), batch, seq, mlstm_norm_g[0], mlstm_w_in[0], mlstm_b_i[0],
                              mlstm_b_f[0], mlstm_out_norm_g[0], mlstm_w_out[0], ffn_norm_g[1], router_w[1])
    return moe_block(x1, hn, aff, expert_w1[1], expert_w3[1], expert_w2[1], batch)
```

```python
import functools
import math

import jax
import jax.numpy as jnp
from jax import lax
from jax.experimental import pallas as pl
from jax.experimental.pallas import tpu as pltpu

F32 = jnp.float32
BF16 = jnp.bfloat16
I32 = jnp.int32

RMS_EPS = 1e-6
NEG = -1e30
LANE = 128
VMEM_LIMIT = 56 * 1024 * 1024

HEAD_DIM = 64
GQA_GROUP = 4
ATT_BLOCK = 128
NUM_BUCKETS = 32
MAX_DISTANCE = 128
N_EXPERTS = 16
CAPACITY_FACTOR = 2
MLSTM_HEADS = 8


def _cparams(n_axes, vmem=VMEM_LIMIT):
    return pltpu.CompilerParams(dimension_semantics=("arbitrary",) * n_axes, vmem_limit_bytes=vmem)


def _rms(x, g):
    return x * lax.rsqrt(jnp.mean(x * x, axis=-1, keepdims=True) + RMS_EPS) * g


def _dot(a, b):
    return jnp.dot(a, b, preferred_element_type=F32)


def _dot_nt(a, b):
    return lax.dot_general(a, b, (((1,), (1,)), ((), ())), preferred_element_type=F32)


def _norm_mm_kernel(x_ref, g_ref, w_ref, o_ref):
    h = _rms(x_ref[...], g_ref[...])
    o_ref[...] = _dot(h.astype(BF16), w_ref[...]).astype(o_ref.dtype)


def norm_matmul(x, g, w, out_dtype, tm=512):
    n, d = x.shape
    m = w.shape[1]
    return pl.pallas_call(
        _norm_mm_kernel,
        grid=(n // tm,),
        in_specs=[pl.BlockSpec((tm, d), lambda i: (i, 0)),
                  pl.BlockSpec((1, d), lambda i: (0, 0)),
                  pl.BlockSpec((d, m), lambda i: (0, 0))],
        out_specs=pl.BlockSpec((tm, m), lambda i: (i, 0)),
        out_shape=jax.ShapeDtypeStruct((n, m), out_dtype),
        compiler_params=_cparams(1),
    )(x, g.reshape(1, d), w)


def _mm_res_router_kernel(a_ref, w_ref, x_ref, g_ref, wr_ref, x1_ref, hn_ref, aff_ref):
    x1 = x_ref[...] + _dot(a_ref[...], w_ref[...])
    x1_ref[...] = x1
    hn = _rms(x1, g_ref[...])
    hn_ref[...] = hn
    logits = _dot_nt(wr_ref[...], hn.astype(BF16))
    mx = jnp.max(logits, axis=0, keepdims=True)
    p = jnp.exp(logits - mx)
    aff_ref[0] = p / jnp.sum(p, axis=0, keepdims=True)


def mm_res_router(a, w, x, g, wr_t, batch, tm=512):
    n, d = x.shape
    k = a.shape[1]
    e = wr_t.shape[0]
    t = n // batch
    tpb = t // tm
    return pl.pallas_call(
        _mm_res_router_kernel,
        grid=(n // tm,),
        in_specs=[pl.BlockSpec((tm, k), lambda i: (i, 0)),
                  pl.BlockSpec((k, d), lambda i: (0, 0)),
                  pl.BlockSpec((tm, d), lambda i: (i, 0)),
                  pl.BlockSpec((1, d), lambda i: (0, 0)),
                  pl.BlockSpec((e, d), lambda i: (0, 0))],
        out_specs=[pl.BlockSpec((tm, d), lambda i: (i, 0)),
                   pl.BlockSpec((tm, d), lambda i: (i, 0)),
                   pl.BlockSpec((1, e, tm), lambda i: (i // tpb, 0, i % tpb))],
        out_shape=[jax.ShapeDtypeStruct((n, d), F32),
                   jax.ShapeDtypeStruct((n, d), F32),
                   jax.ShapeDtypeStruct((batch, e, t), F32)],
        compiler_params=_cparams(1),
    )(a, w, x, g.reshape(1, d), wr_t)


def _t5_bucket(rel):
    nb = NUM_BUCKETS // 2
    ret = (rel > 0).astype(jnp.int32) * nb
    n = jnp.abs(rel)
    max_exact = nb // 2
    nf = jnp.maximum(n, 1).astype(jnp.float32)
    large = max_exact + (jnp.log(nf / max_exact) / math.log(MAX_DISTANCE / max_exact)
                         * (nb - max_exact)).astype(jnp.int32)
    large = jnp.minimum(large, nb - 1)
    return ret + jnp.where(n < max_exact, n, large)


def _attn_bucket_table():
    qq = jnp.arange(ATT_BLOCK)[:, None]
    kk = jnp.arange(3 * ATT_BLOCK)[None, :]
    rel = kk - ATT_BLOCK - qq
    return jnp.where(jnp.abs(rel) <= ATT_BLOCK, _t5_bucket(rel), -1).astype(I32)


def _attn_kernel(sink_ref, rb_ref, q_ref, kp_ref, kc_ref, kn_ref, bucket_ref, qg_ref, kg_ref, o_ref, bias_sc,
                 *, nb, n_kv):
    n = pl.program_id(1)
    blk = ATT_BLOCK
    dh = HEAD_DIM

    @pl.when(jnp.logical_and(pl.program_id(0) == 0, n == 0))
    def _bias_table():
        bk = bucket_ref[...]
        for hq in range(n_kv * GQA_GROUP):
            acc = jnp.full(bk.shape, NEG, F32)
            for k in range(NUM_BUCKETS):
                acc = jnp.where(bk == k, rb_ref[k, hq], acc)
            bias_sc[hq] = acc

    kv = jnp.concatenate([kp_ref[...], kc_ref[...], kn_ref[...]], axis=0)
    col = lax.broadcasted_iota(jnp.int32, (1, 3 * blk), 1)
    valid = jnp.logical_and(jnp.logical_or(n > 0, col >= blk),
                            jnp.logical_or(n < nb - 1, col < 2 * blk))
    edge = jnp.where(valid, 0.0, NEG)
    qg = qg_ref[...] * (dh ** -0.5)
    kg = kg_ref[...]
    for h in range(n_kv):
        k = _rms(kv[:, h * dh:(h + 1) * dh].astype(F32), kg).astype(BF16)
        v = kv[:, (n_kv + h) * dh:(n_kv + h + 1) * dh]
        for g in range(GQA_GROUP):
            hq = h * GQA_GROUP + g
            q = _rms(q_ref[:, hq * dh:(hq + 1) * dh].astype(F32), qg).astype(BF16)
            s = _dot_nt(q, k) + bias_sc[hq] + edge
            sk = sink_ref[hq]
            m = jnp.maximum(jnp.max(s, axis=-1, keepdims=True), sk)
            p = jnp.exp(s - m)
            den = jnp.sum(p, axis=-1, keepdims=True) + jnp.exp(sk - m)
            o = _dot(p.astype(BF16), v) / den
            o_ref[:, hq * dh:(hq + 1) * dh] = o.astype(o_ref.dtype)


def attention_core(proj, rel_bias, q_g, k_g, sink, batch, seq):
    n, width = proj.shape
    hq = rel_bias.shape[1]
    n_kv = hq // GQA_GROUP
    dq = hq * HEAD_DIM
    dkv = 2 * n_kv * HEAD_DIM
    assert width == dq + dkv and dq % dkv == 0
    kvcol = dq // dkv
    nb = seq // ATT_BLOCK
    blk = ATT_BLOCK
    row = lambda b, i: b * nb + i
    return pl.pallas_call(
        functools.partial(_attn_kernel, nb=nb, n_kv=n_kv),
        grid=(batch, nb),
        in_specs=[pl.BlockSpec(memory_space=pltpu.SMEM),
                  pl.BlockSpec(memory_space=pltpu.SMEM),
                  pl.BlockSpec((blk, dq), lambda b, i: (row(b, i), 0)),
                  pl.BlockSpec((blk, dkv), lambda b, i: (row(b, jnp.maximum(i - 1, 0)), kvcol)),
                  pl.BlockSpec((blk, dkv), lambda b, i: (row(b, i), kvcol)),
                  pl.BlockSpec((blk, dkv), lambda b, i: (row(b, jnp.minimum(i + 1, nb - 1)), kvcol)),
                  pl.BlockSpec((blk, 3 * blk), lambda b, i: (0, 0)),
                  pl.BlockSpec((1, HEAD_DIM), lambda b, i: (0, 0)),
                  pl.BlockSpec((1, HEAD_DIM), lambda b, i: (0, 0))],
        out_specs=pl.BlockSpec((blk, dq), lambda b, i: (row(b, i), 0)),
        out_shape=jax.ShapeDtypeStruct((n, dq), BF16),
        scratch_shapes=[pltpu.VMEM((hq, blk, 3 * blk), F32)],
        compiler_params=_cparams(2),
    )(sink.astype(F32), rel_bias.astype(F32), proj, proj, proj, proj, _attn_bucket_table(),
      q_g.reshape(1, HEAD_DIM).astype(F32), k_g.reshape(1, HEAD_DIM).astype(F32))


def _prefix_incl(x, tri):
    t = x.shape[1]
    outs = []
    carry = jnp.zeros((x.shape[0], 1), F32)
    for c in range(t // LANE):
        p = _dot(x[:, c * LANE:(c + 1) * LANE].astype(BF16), tri) + carry
        outs.append(p)
        carry = p[:, LANE - 1:LANE]
    return jnp.concatenate(outs, axis=1)


def _select_kernel(aff_ref, cmp_ref, offc_ref, rng_ref, pos_sc, dest_sc, vals_sc, *, cap, rchunk):
    aff = aff_ref[0]
    ne, t = aff.shape
    bits = pltpu.bitcast(aff, I32)

    def search(i, lo):
        cand = lo | lax.shift_left(jnp.int32(1), 30 - i)
        cnt = jnp.sum((bits >= cand).astype(I32), axis=1, keepdims=True)
        return jnp.where(cnt >= cap, cand, lo)

    thr = lax.fori_loop(0, 31, search, jnp.zeros((ne, 1), I32))
    gt = bits > thr
    eq = bits == thr
    need = (cap - jnp.sum(gt.astype(I32), axis=1, keepdims=True)).astype(F32)
    ri = lax.broadcasted_iota(I32, (LANE, LANE), 0)
    ci = lax.broadcasted_iota(I32, (LANE, LANE), 1)
    tri = (ri <= ci).astype(BF16)
    eqf = eq.astype(F32)
    rank_eq = _prefix_incl(eqf, tri) - eqf
    sel = jnp.logical_or(gt, jnp.logical_and(eq, rank_eq < need))
    self_ = sel.astype(F32)
    kt = jnp.sum(self_, axis=0, keepdims=True)
    pre = _prefix_incl(jnp.concatenate([self_, jnp.broadcast_to(kt, (8, t))], axis=0), tri)
    pos_sc[...] = jnp.where(sel, pre[:ne] - 1.0, -1.0)
    end = pre[ne:ne + 1]
    off = end - kt
    lr = lax.broadcasted_iota(I32, (ne, ne), 0)
    lc = lax.broadcasted_iota(I32, (ne, ne), 1)
    jexp = _dot((lc < lr).astype(BF16), self_.astype(BF16))
    dest_sc[...] = off + jexp

    tok = lax.broadcasted_iota(I32, (1, t), 1)
    vals_sc[0:1, :] = (tok >> 6).astype(F32)
    vals_sc[1:2, :] = (tok & 63).astype(F32)
    vals_sc[7:8, :] = jnp.zeros((1, t), F32)
    slot = lax.broadcasted_iota(I32, (cap, 1), 0).astype(F32)

    def compact(e, carry):
        d = dest_sc[pl.ds(e, 1), :]
        dh = jnp.floor(d * (1.0 / LANE))
        g = aff_ref[0, pl.ds(e, 1), :]
        g0 = g.astype(BF16).astype(F32)
        g1 = (g - g0).astype(BF16).astype(F32)
        vals_sc[2:3, :] = dh
        vals_sc[3:4, :] = d - dh * LANE
        vals_sc[4:5, :] = g0
        vals_sc[5:6, :] = g1
        vals_sc[6:7, :] = g - g0 - g1
        onehot = (pos_sc[pl.ds(e, 1), :] == slot).astype(BF16)
        cmp_ref[0, e] = _dot_nt(vals_sc[...].astype(BF16), onehot)
        return carry

    lax.fori_loop(0, ne, compact, 0)

    eh = jnp.floor(end * (1.0 / LANE))
    oh = jnp.floor(off * (1.0 / LANE))
    v4 = jnp.concatenate([oh, off - oh * LANE, eh, end - eh * LANE, jnp.zeros((4, t), F32)], axis=0)
    eye = (ri == ci).astype(BF16)
    for c in range(t // LANE):
        offc_ref[0, c * LANE:(c + 1) * LANE, :] = _dot_nt(eye, v4[:, c * LANE:(c + 1) * LANE].astype(BF16))

    nchunk = (ne * cap) // rchunk
    low = (lax.broadcasted_iota(I32, (nchunk, 1), 0) * rchunk).astype(F32)
    first = jnp.sum((end <= low).astype(F32), axis=1, keepdims=True)
    last = jnp.sum((end <= low + (rchunk - 1)).astype(F32), axis=1, keepdims=True)
    lane = lax.broadcasted_iota(I32, (nchunk, LANE), 1)
    tiles = jnp.where(lane < LANE // 2, jnp.floor(first * (1.0 / LANE)), jnp.floor(last * (1.0 / LANE)))
    rng_ref[0] = tiles.astype(I32)


def route_select(aff, cap, rchunk=128):
    batch, ne, t = aff.shape
    nchunk = ne * cap // rchunk
    return pl.pallas_call(
        functools.partial(_select_kernel, cap=cap, rchunk=rchunk),
        grid=(batch,),
        in_specs=[pl.BlockSpec((1, ne, t), lambda b: (b, 0, 0))],
        out_specs=[pl.BlockSpec((1, ne, 8, cap), lambda b: (b, 0, 0, 0)),
                   pl.BlockSpec((1, t, 8), lambda b: (b, 0, 0)),
                   pl.BlockSpec((1, nchunk, LANE), lambda b: (b, 0, 0))],
        out_shape=[jax.ShapeDtypeStruct((batch, ne, 8, cap), F32),
                   jax.ShapeDtypeStruct((batch, t, 8), F32),
                   jax.ShapeDtypeStruct((batch, nchunk, LANE), I32)],
        scratch_shapes=[pltpu.VMEM((ne, t), F32), pltpu.VMEM((ne, t), F32), pltpu.VMEM((8, t), F32)],
        compiler_params=_cparams(1),
    )(aff)


def _ffn_kernel(idx_sm, dest_sm, hn_hbm, cmp_ref, w1_ref, w3_ref, w2_ref, r_hbm,
                xs32, xsb, yacc, ysc, wb1, wb3, wb2, gsem, ssem, *, batch, cap, nf, rt):
    e = pl.program_id(0)
    f = pl.program_id(1)
    ne = pl.num_programs(0)
    rows = batch * cap
    sub = rows // nf

    def gather_start(block, r):
        t = idx_sm[block * rows + r]
        pltpu.make_async_copy(hn_hbm.at[pl.ds(t, 1), :], xs32.at[pl.ds(r, 1), :], gsem).start()

    def gather_wait():
        pltpu.make_async_copy(hn_hbm.at[pl.ds(0, rows), :], xs32, gsem).wait()

    def scatter_start(block, r):
        d = dest_sm[block * rows + r]
        pltpu.make_async_copy(ysc.at[pl.ds(r, 1), :], r_hbm.at[pl.ds(d, 1), :], ssem).start()

    def scatter_wait():
        pltpu.make_async_copy(ysc, r_hbm.at[pl.ds(0, rows), :], ssem).wait()

    @pl.when(jnp.logical_and(e == 0, f == 0))
    def _prologue():
        def issue(r, c):
            gather_start(0, r)
            return c
        lax.fori_loop(0, rows, issue, 0)
        ysc[...] = jnp.zeros_like(ysc)

    @pl.when(f == 0)
    def _rows_ready():
        gather_wait()
        xsb[...] = xs32[...].astype(BF16)
        yacc[...] = jnp.zeros_like(yacc)

    for r in range(sub):
        gather_start(e + 1, f * sub + r)
        scatter_start(e, f * sub + r)

    wb1[...] = w1_ref[0, 0].astype(BF16)
    wb3[...] = w3_ref[0, 0].astype(BF16)
    wb2[...] = w2_ref[0, 0].astype(BF16)
    for r in range(rows // rt):
        xs = xsb[r * rt:(r + 1) * rt, :]
        a = _dot(xs, wb1[...])
        u = _dot(xs, wb3[...])
        hmid = (a * jax.nn.sigmoid(a) * u).astype(BF16)
        yacc[r * rt:(r + 1) * rt, :] += _dot(hmid, wb2[...])

    @pl.when(f == nf - 1)
    def _finish():
        scatter_wait()
        ri = lax.broadcasted_iota(I32, (cap, cap), 0)
        ci = lax.broadcasted_iota(I32, (cap, cap), 1)
        eye = (ri == ci).astype(BF16)
        for b in range(batch):
            gt = _dot_nt(eye, cmp_ref[b, 0].astype(BF16))
            g = gt[:, 4:5] + gt[:, 5:6] + gt[:, 6:7]
            ysc[b * cap:(b + 1) * cap, :] = yacc[b * cap:(b + 1) * cap, :] * g

        @pl.when(e == ne - 1)
        def _epilogue():
            def issue(r, c):
                scatter_start(ne, r)
                return c
            lax.fori_loop(0, rows, issue, 0)
            scatter_wait()
            gather_wait()


def expert_ffn(hn, cmp, idx_flat, dest_flat, w1, w3, w2, layer, cap, tf=512, rt=512):
    n, d = hn.shape
    batch, ne = cmp.shape[0], cmp.shape[1]
    dff = w1.shape[3]
    nf = dff // tf
    rows = batch * cap
    rt = min(rt, rows)
    grid_spec = pltpu.PrefetchScalarGridSpec(
        num_scalar_prefetch=2,
        grid=(ne, nf),
        in_specs=[pl.BlockSpec(memory_space=pl.ANY),
                  pl.BlockSpec((batch, 1, 8, cap), lambda e, f, *_: (0, e, 0, 0)),
                  pl.BlockSpec((1, 1, d, tf), lambda e, f, *_: (layer, e, 0, f)),
                  pl.BlockSpec((1, 1, d, tf), lambda e, f, *_: (layer, e, 0, f)),
                  pl.BlockSpec((1, 1, tf, d), lambda e, f, *_: (layer, e, f, 0))],
        out_specs=pl.BlockSpec(memory_space=pl.ANY),
        scratch_shapes=[pltpu.VMEM((rows, d), F32), pltpu.VMEM((rows, d), BF16),
                        pltpu.VMEM((rows, d), F32), pltpu.VMEM((rows, d), F32),
                        pltpu.VMEM((d, tf), BF16), pltpu.VMEM((d, tf), BF16), pltpu.VMEM((tf, d), BF16),
                        pltpu.SemaphoreType.DMA, pltpu.SemaphoreType.DMA])
    return pl.pallas_call(
        functools.partial(_ffn_kernel, batch=batch, cap=cap, nf=nf, rt=rt),
        grid_spec=grid_spec,
        out_shape=jax.ShapeDtypeStruct((batch * ne * cap + rows, d), F32),
        compiler_params=_cparams(2),
    )(idx_flat, dest_flat, hn, cmp, w1, w3, w2)


def _combine_kernel(tlo_sm, thi_sm, r_ref, offc_ref, x_hbm, o_ref, sem, *, nchunk, rchunk):
    b = pl.program_id(0)
    j = pl.program_id(1)

    @pl.when(j == 0)
    def _init():
        cp = pltpu.make_async_copy(x_hbm.at[b], o_ref.at[0], sem)
        cp.start()
        cp.wait()

    r = r_ref[...]
    hi = r.astype(BF16)
    lo = (r - hi.astype(F32)).astype(BF16)
    rowid = (j * rchunk + lax.broadcasted_iota(I32, (1, rchunk), 1)).astype(F32)

    def tile(i, c):
        t0 = pl.multiple_of(i * LANE, LANE)
        oc = offc_ref[0, pl.ds(t0, LANE), :]
        off = oc[:, 0:1] * LANE + oc[:, 1:2]
        end = oc[:, 2:3] * LANE + oc[:, 3:4]
        p = jnp.logical_and(rowid >= off, rowid < end).astype(BF16)
        o_ref[0, pl.ds(t0, LANE), :] += _dot(p, hi) + _dot(p, lo)
        return c

    lax.fori_loop(tlo_sm[b * nchunk + j], thi_sm[b * nchunk + j] + 1, tile, 0)


def combine(rbuf, offc, tlo, thi, x, rows_per_batch, rchunk=128):
    batch, t, d = x.shape
    nchunk = rows_per_batch // rchunk
    grid_spec = pltpu.PrefetchScalarGridSpec(
        num_scalar_prefetch=2,
        grid=(batch, nchunk),
        in_specs=[pl.BlockSpec((rchunk, d), lambda b, j, *_: (b * nchunk + j, 0)),
                  pl.BlockSpec((1, t, 8), lambda b, j, *_: (b, 0, 0)),
                  pl.BlockSpec(memory_space=pl.ANY)],
        out_specs=pl.BlockSpec((1, t, d), lambda b, j, *_: (b, 0, 0)),
        scratch_shapes=[pltpu.SemaphoreType.DMA])
    return pl.pallas_call(
        functools.partial(_combine_kernel, nchunk=nchunk, rchunk=rchunk),
        grid_spec=grid_spec,
        out_shape=jax.ShapeDtypeStruct((batch, t, d), F32),
        compiler_params=_cparams(2),
    )(tlo, thi, rbuf, offc, x)


def moe_block(x1, hn, aff, w1, w3, w2, layer, batch):
    n, d = x1.shape
    t = n // batch
    ne = aff.shape[1]
    cap = CAPACITY_FACTOR * t // ne
    rows = batch * cap
    cmp, offc, rng = route_select(aff, cap)
    ci = cmp.astype(I32)
    boff = jnp.arange(batch, dtype=I32)[:, None, None]
    idx = (ci[:, :, 0] * 64 + ci[:, :, 1] + boff * t).transpose(1, 0, 2).reshape(-1)
    dest = (ci[:, :, 2] * LANE + ci[:, :, 3] + boff * (ne * cap)).transpose(1, 0, 2).reshape(-1)
    idx = jnp.concatenate([idx, idx[:rows]])
    dest = jnp.concatenate([batch * ne * cap + jnp.arange(rows, dtype=I32), dest])
    rbuf = expert_ffn(hn, cmp, idx, dest, w1, w3, w2, layer, cap)
    tlo = rng[:, :, 0].reshape(-1)
    thi = rng[:, :, LANE - 1].reshape(-1)
    return combine(rbuf, offc, tlo, thi, x1.reshape(batch, t, d), ne * cap)


MLSTM_L = 256


def _mlstm_proj_kernel(x_ref, g_ref, wm_ref, wkt_ref, wgc_ref, wgt_ref, main_ref, kt_ref, gc_ref, gt_ref, *, nsub):
    h = _rms(x_ref[...], g_ref[...]).astype(BF16)
    main_ref[...] = _dot(h, wm_ref[...]).astype(main_ref.dtype)
    gc_ref[...] = _dot(h, wgc_ref[...])
    kt = _dot_nt(wkt_ref[...], h).astype(kt_ref.dtype)
    gt = _dot_nt(wgt_ref[...], h)
    for j in range(nsub):
        kt_ref[j] = kt[:, j * MLSTM_L:(j + 1) * MLSTM_L]
        gt_ref[j] = gt[:, j * MLSTM_L:(j + 1) * MLSTM_L]


def mlstm_project(x, g, wm, wkt, wgc, wgt, tm=512):
    n, d = x.shape
    nsub = tm // MLSTM_L
    nck = n // MLSTM_L
    const = lambda i: (0, 0)
    return pl.pallas_call(
        functools.partial(_mlstm_proj_kernel, nsub=nsub),
        grid=(n // tm,),
        in_specs=[pl.BlockSpec((tm, d), lambda i: (i, 0)),
                  pl.BlockSpec((1, d), const),
                  pl.BlockSpec(wm.shape, const),
                  pl.BlockSpec(wkt.shape, const),
                  pl.BlockSpec(wgc.shape, const),
                  pl.BlockSpec(wgt.shape, const)],
        out_specs=[pl.BlockSpec((tm, wm.shape[1]), lambda i: (i, 0)),
                   pl.BlockSpec((nsub, wkt.shape[0], MLSTM_L), lambda i: (i, 0, 0)),
                   pl.BlockSpec((tm, LANE), lambda i: (i, 0)),
                   pl.BlockSpec((nsub, wgt.shape[0], MLSTM_L), lambda i: (i, 0, 0))],
        out_shape=[jax.ShapeDtypeStruct((n, wm.shape[1]), BF16),
                   jax.ShapeDtypeStruct((nck, wkt.shape[0], MLSTM_L), BF16),
                   jax.ShapeDtypeStruct((n, LANE), F32),
                   jax.ShapeDtypeStruct((nck, wgt.shape[0], MLSTM_L), F32)],
        compiler_params=_cparams(1),
    )(x, g.reshape(1, d), wm, wkt, wgc, wgt)


def _log_sigmoid(x):
    return jnp.minimum(x, 0.0) - jnp.log1p(jnp.exp(-jnp.abs(x)))


def _dot_hi(a, b):
    return jnp.dot(a, b, preferred_element_type=F32, precision=lax.Precision.HIGHEST)


def _gate_prep_kernel(gc_ref, gt_ref, bc_ref, bt_ref, oc_ref, ot_ref, *, nh):
    ln = MLSTM_L
    ri = lax.broadcasted_iota(I32, (ln, ln), 0)
    ci = lax.broadcasted_iota(I32, (ln, ln), 1)
    low = (ci <= ri).astype(F32)
    upp = (ci >= ri).astype(F32)
    ls = _log_sigmoid(gc_ref[...] + bc_ref[...])
    lane = lax.broadcasted_iota(I32, ls.shape, 1)
    oc_ref[...] = jnp.where(lane < nh, _dot_hi(low, ls), _dot_hi(upp, ls))
    pre = gt_ref[0] + bt_ref[...]
    lst = _log_sigmoid(pre)
    row = lax.broadcasted_iota(I32, pre.shape, 0)
    fwd = jnp.logical_and(row >= nh, row < 2 * nh)
    bwd = row >= 3 * nh
    ot_ref[0] = jnp.where(fwd, _dot_hi(lst, upp), jnp.where(bwd, _dot_hi(lst, low), pre))


def gate_prep(gc, gt, bias_c, bias_t, nh):
    n = gc.shape[0]
    nck, rows, ln = gt.shape
    return pl.pallas_call(
        functools.partial(_gate_prep_kernel, nh=nh),
        grid=(nck,),
        in_specs=[pl.BlockSpec((ln, LANE), lambda c: (c, 0)),
                  pl.BlockSpec((1, rows, ln), lambda c: (c, 0, 0)),
                  pl.BlockSpec((1, LANE), lambda c: (0, 0)),
                  pl.BlockSpec((rows, 1), lambda c: (0, 0))],
        out_specs=[pl.BlockSpec((ln, LANE), lambda c: (c, 0)),
                   pl.BlockSpec((1, rows, ln), lambda c: (c, 0, 0))],
        out_shape=[jax.ShapeDtypeStruct((n, LANE), F32),
                   jax.ShapeDtypeStruct((nck, rows, ln), F32)],
        compiler_params=_cparams(1),
    )(gc, gt, bias_c, bias_t)


def _mlstm_kernel(q_ref, kt_ref, v_ref, og_ref, gc_ref, gt_ref, ng_ref, y_ref, hf, hb, cst, mst, *, nc, nh, dqk, dv):
    p = pl.program_id(1)
    ln = MLSTM_L
    ri = lax.broadcasted_iota(I32, (ln, ln), 0)
    ci = lax.broadcasted_iota(I32, (ln, ln), 1)
    masks = (ci <= ri, ci >= ri)
    lane = lax.broadcasted_iota(I32, (ln, LANE), 1)
    ones_col = (lane == 0).astype(BF16)
    cst[...] = jnp.zeros_like(cst)
    mst[...] = jnp.zeros_like(mst)

    def step(c, carry):
        for hh in range(2):
            head = 2 * p + hh
            for d in range(2):
                chain = 2 * hh + d
                ck = c if d == 0 else nc - 1 - c
                r0 = pl.multiple_of(ck * ln, ln)
                q = q_ref[pl.ds(r0, ln), hh * dqk:(hh + 1) * dqk]
                kt = kt_ref[ck, hh * dqk:(hh + 1) * dqk, :]
                vaug = jnp.concatenate([v_ref[pl.ds(r0, ln), hh * dv:(hh + 1) * dv], ones_col], axis=1)
                gcol = gc_ref[pl.ds(r0, ln), :]
                bcol = jnp.sum(jnp.where(lane == d * nh + head, gcol, 0.0), axis=-1, keepdims=True)
                irow = gt_ref[ck, pl.ds(2 * d * nh + head, 1), :]
                brow = gt_ref[ck, pl.ds((2 * d + 1) * nh + head, 1), :]
                m = mst[chain]
                dm = jnp.where(masks[d], bcol - brow + irow, NEG)
                a = bcol + m
                mt = jnp.maximum(a, jnp.max(dm, axis=-1, keepdims=True))
                w = jnp.exp(dm - mt)
                ea = jnp.exp(a - mt)
                sw = (w * _dot(q, kt)).astype(BF16)
                cprev = cst[chain]
                tot = ea * _dot(q, cprev.astype(BF16)) + _dot(sw, vaug)
                hc = tot[:, :dv] / jnp.maximum(jnp.abs(tot[:, dv:dv + 1]), jnp.exp(-mt))
                edge = ln - 1 if d == 0 else 0
                bl = brow[:, edge:edge + 1]
                g = bl - brow + irow
                m_new = jnp.maximum(bl + m, jnp.max(g, axis=-1, keepdims=True))
                wc = jnp.exp(g - m_new)
                cst[chain] = jnp.exp(bl + m - m_new) * cprev + _dot((kt.astype(F32) * wc).astype(BF16), vaug)
                mst[chain] = m_new
                if d == 0:
                    hf[pl.ds(r0, ln), hh * dv:(hh + 1) * dv] = hc
                else:
                    hb[pl.ds(r0, ln), hh * dv:(hh + 1) * dv] = hc
        return carry

    lax.fori_loop(0, nc, step, 0)
    for hh in range(2):
        sl = slice(hh * dv, (hh + 1) * dv)
        hs = _rms(hf[:, sl] + hb[:, sl], ng_ref[:, sl])
        y_ref[:, sl] = (hs * jax.nn.sigmoid(og_ref[:, sl].astype(F32))).astype(y_ref.dtype)


def mlstm_core(main, ktc, gcol, grow, out_g, batch, seq, nh, dqk, dv):
    n = main.shape[0]
    nc = seq // MLSTM_L
    npair = nh // 2
    qblocks = nh * dqk // (2 * dqk)
    vblocks = (nh * dqk) // (2 * dv)
    ogblocks = (nh * dqk + nh * dv) // (2 * dv)
    return pl.pallas_call(
        functools.partial(_mlstm_kernel, nc=nc, nh=nh, dqk=dqk, dv=dv),
        grid=(batch, npair),
        in_specs=[pl.BlockSpec((seq, 2 * dqk), lambda b, p: (b, p)),
                  pl.BlockSpec((nc, 2 * dqk, MLSTM_L), lambda b, p: (b, p, 0)),
                  pl.BlockSpec((seq, 2 * dv), lambda b, p: (b, vblocks + p)),
                  pl.BlockSpec((seq, 2 * dv), lambda b, p: (b, ogblocks + p)),
                  pl.BlockSpec((seq, LANE), lambda b, p: (b, 0)),
                  pl.BlockSpec((nc, 4 * nh, MLSTM_L), lambda b, p: (b, 0, 0)),
                  pl.BlockSpec((1, 2 * dv), lambda b, p: (0, p))],
        out_specs=pl.BlockSpec((seq, 2 * dv), lambda b, p: (b, p)),
        out_shape=jax.ShapeDtypeStruct((n, nh * dv), BF16),
        scratch_shapes=[pltpu.VMEM((seq, 2 * dv), F32), pltpu.VMEM((seq, 2 * dv), F32),
                        pltpu.VMEM((4, dqk, 2 * dv), F32), pltpu.VMEM((4, 1, 1), F32)],
        compiler_params=_cparams(2),
    )(main, ktc, main, main, gcol, grow, out_g.reshape(1, nh * dv).astype(F32))


def mlstm_layer(x2d, batch, seq, norm_g, w_in, b_i, b_f, out_g, w_out, ffn_g, wr):
    d = x2d.shape[1]
    nh = MLSTM_HEADS
    dv = d // nh
    dqk = dv // 2
    o1, o2, o3, o4 = nh * dqk, 2 * nh * dqk, 2 * nh * dqk + nh * dv, 2 * nh * dqk + 2 * nh * dv
    wm = jnp.concatenate([w_in[:, :o1], w_in[:, o2:o4]], axis=1).astype(BF16)
    wkt = (w_in[:, o1:o2] * (dqk ** -0.5)).T.astype(BF16)
    wg = w_in[:, o4:]
    wgc = jnp.concatenate([wg[:, nh:2 * nh], wg[:, 3 * nh:], jnp.zeros((d, LANE - 2 * nh), F32)], axis=1).astype(BF16)
    wgt = wg.T.astype(BF16)
    main, ktc, gc, gt = mlstm_project(x2d, norm_g, wm, wkt, wgc, wgt)
    bias_c = jnp.concatenate([b_f[0], b_f[1], jnp.zeros((LANE - 2 * nh,), F32)]).reshape(1, LANE).astype(F32)
    bias_t = jnp.concatenate([b_i[0], b_f[0], b_i[1], b_f[1]]).reshape(4 * nh, 1).astype(F32)
    gcol, grow = gate_prep(gc, gt, bias_c, bias_t, nh)
    y = mlstm_core(main, ktc, gcol, grow, out_g, batch, seq, nh, dqk, dv)
    return mm_res_router(y, w_out.astype(BF16), x2d, ffn_g, wr.T.astype(BF16), batch)


def attention_layer(x2d, batch, seq, norm_g, w_in, q_g, k_g, sink, w_out, rel_bias, ffn_g, wr):
    proj = norm_matmul(x2d, norm_g, w_in.astype(BF16), BF16)
    o = attention_core(proj, rel_bias, q_g, k_g, sink, batch, seq)
    return mm_res_router(o, w_out.astype(BF16), x2d, ffn_g, wr.T.astype(BF16), batch)


def kernel(x, rel_bias, attn_norm_g, attn_w_in, attn_q_norm_g, attn_k_norm_g, attn_sink, attn_w_out, mlstm_norm_g, mlstm_w_in, mlstm_b_i, mlstm_b_f, mlstm_out_norm_g, mlstm_w_out, ffn_norm_g, router_w, expert_w1, expert_w3, expert_w2):
    batch, seq, d = x.shape
    x2d = x.reshape(batch * seq, d)
    x1, hn, aff = attention_layer(x2d, batch, seq, attn_norm_g[0], attn_w_in[0], attn_q_norm_g[0],
                                  attn_k_norm_g[0], attn_sink[0], attn_w_out[0], rel_bias,
                                  ffn_norm_g[0], router_w[0])
    x = moe_block(x1, hn, aff, expert_w1, expert_w3, expert_w2, 0, batch)
    x1, hn, aff = mlstm_layer(x.reshape(batch * seq, d), batch, seq, mlstm_norm_g[0], mlstm_w_in[0], mlstm_b_i[0],
                              mlstm_b_f[0], mlstm_out_norm_g[0], mlstm_w_out[0], ffn_norm_g[1], router_w[1])
    return moe_block(x1, hn, aff, expert_w1, expert_w3, expert_w2, 1, batch)
```

```python
import functools
import math

import jax
import jax.numpy as jnp
from jax import lax
from jax.experimental import pallas as pl
from jax.experimental.pallas import tpu as pltpu

F32 = jnp.float32
BF16 = jnp.bfloat16
I32 = jnp.int32

RMS_EPS = 1e-6
NEG = -1e30
LOG2E = 1.4426950408889634
LANE = 128
VMEM_LIMIT = 56 * 1024 * 1024

HEAD_DIM = 64
GQA_GROUP = 4
ATT_BLOCK = 128
NUM_BUCKETS = 32
MAX_DISTANCE = 128
N_EXPERTS = 16
CAPACITY_FACTOR = 2
MLSTM_HEADS = 8


def _cparams(n_axes, vmem=VMEM_LIMIT):
    return pltpu.CompilerParams(dimension_semantics=("arbitrary",) * n_axes, vmem_limit_bytes=vmem)


def _rms(x, g):
    return x * lax.rsqrt(jnp.mean(x * x, axis=-1, keepdims=True) + RMS_EPS) * g


def _dot(a, b):
    return jnp.dot(a, b, preferred_element_type=F32)


def _dot_nt(a, b):
    return lax.dot_general(a, b, (((1,), (1,)), ((), ())), preferred_element_type=F32)


def _norm_mm_kernel(x_ref, g_ref, w_ref, o_ref):
    h = _rms(x_ref[...], g_ref[...])
    o_ref[...] = _dot(h.astype(BF16), w_ref[...]).astype(o_ref.dtype)


def norm_matmul(x, g, w, out_dtype, tm=512):
    n, d = x.shape
    m = w.shape[1]
    return pl.pallas_call(
        _norm_mm_kernel,
        grid=(n // tm,),
        in_specs=[pl.BlockSpec((tm, d), lambda i: (i, 0)),
                  pl.BlockSpec((1, d), lambda i: (0, 0)),
                  pl.BlockSpec((d, m), lambda i: (0, 0))],
        out_specs=pl.BlockSpec((tm, m), lambda i: (i, 0)),
        out_shape=jax.ShapeDtypeStruct((n, m), out_dtype),
        compiler_params=_cparams(1),
    )(x, g.reshape(1, d), w)


def _mm_res_router_kernel(a_ref, w_ref, x_ref, g_ref, wr_ref, x1_ref, hn_ref, aff_ref):
    x1 = x_ref[...] + _dot(a_ref[...], w_ref[...])
    x1_ref[...] = x1
    hn = _rms(x1, g_ref[...])
    hn_ref[...] = hn
    logits = _dot_nt(wr_ref[...], hn.astype(BF16))
    mx = jnp.max(logits, axis=0, keepdims=True)
    p = jnp.exp(logits - mx)
    aff_ref[0] = p / jnp.sum(p, axis=0, keepdims=True)


def mm_res_router(a, w, x, g, wr_t, batch, tm=512):
    n, d = x.shape
    k = a.shape[1]
    e = wr_t.shape[0]
    t = n // batch
    tpb = t // tm
    return pl.pallas_call(
        _mm_res_router_kernel,
        grid=(n // tm,),
        in_specs=[pl.BlockSpec((tm, k), lambda i: (i, 0)),
                  pl.BlockSpec((k, d), lambda i: (0, 0)),
                  pl.BlockSpec((tm, d), lambda i: (i, 0)),
                  pl.BlockSpec((1, d), lambda i: (0, 0)),
                  pl.BlockSpec((e, d), lambda i: (0, 0))],
        out_specs=[pl.BlockSpec((tm, d), lambda i: (i, 0)),
                   pl.BlockSpec((tm, d), lambda i: (i, 0)),
                   pl.BlockSpec((1, e, tm), lambda i: (i // tpb, 0, i % tpb))],
        out_shape=[jax.ShapeDtypeStruct((n, d), F32),
                   jax.ShapeDtypeStruct((n, d), F32),
                   jax.ShapeDtypeStruct((batch, e, t), F32)],
        compiler_params=_cparams(1),
    )(a, w, x, g.reshape(1, d), wr_t)


def _t5_bucket(rel):
    nb = NUM_BUCKETS // 2
    ret = (rel > 0).astype(jnp.int32) * nb
    n = jnp.abs(rel)
    max_exact = nb // 2
    nf = jnp.maximum(n, 1).astype(jnp.float32)
    large = max_exact + (jnp.log(nf / max_exact) / math.log(MAX_DISTANCE / max_exact)
                         * (nb - max_exact)).astype(jnp.int32)
    large = jnp.minimum(large, nb - 1)
    return ret + jnp.where(n < max_exact, n, large)


def _attn_bucket_table():
    qq = jnp.arange(ATT_BLOCK)[:, None]
    kk = jnp.arange(3 * ATT_BLOCK)[None, :]
    rel = kk - ATT_BLOCK - qq
    return jnp.where(jnp.abs(rel) <= ATT_BLOCK, _t5_bucket(rel), -1).astype(I32)


def _attn_kernel(sink_ref, rb_ref, q_ref, kp_ref, kc_ref, kn_ref, bucket_ref, qg_ref, kg_ref, o_ref, bias_sc,
                 *, nb, n_kv):
    n = pl.program_id(1)
    blk = ATT_BLOCK
    dh = HEAD_DIM

    @pl.when(jnp.logical_and(pl.program_id(0) == 0, n == 0))
    def _bias_table():
        bk = bucket_ref[...]
        for hq in range(n_kv * GQA_GROUP):
            acc = jnp.full(bk.shape, NEG, F32)
            for k in range(NUM_BUCKETS):
                acc = jnp.where(bk == k, rb_ref[k, hq] * LOG2E, acc)
            bias_sc[hq // GQA_GROUP, (hq % GQA_GROUP) * blk:(hq % GQA_GROUP + 1) * blk, :] = acc

    kv = jnp.concatenate([kp_ref[...], kc_ref[...], kn_ref[...]], axis=0)
    col = lax.broadcasted_iota(jnp.int32, (1, 3 * blk), 1)
    valid = jnp.logical_and(jnp.logical_or(n > 0, col >= blk),
                            jnp.logical_or(n < nb - 1, col < 2 * blk))
    edge = jnp.where(valid, 0.0, NEG)
    qg = qg_ref[...] * (dh ** -0.5 * LOG2E)
    kg = kg_ref[...]
    ones_col = (lax.broadcasted_iota(I32, (3 * blk, dh), 1) == 0).astype(BF16)
    grow = lax.broadcasted_iota(I32, (GQA_GROUP * blk, 1), 0)
    groups = range(n_kv)
    heads = [[h * GQA_GROUP + g for g in range(GQA_GROUP)] for h in groups]
    k = [_rms(kv[:, h * dh:(h + 1) * dh].astype(F32), kg).astype(BF16) for h in groups]
    vaug = [jnp.concatenate([kv[:, (n_kv + h) * dh:(n_kv + h + 1) * dh], ones_col], axis=1) for h in groups]
    q = [_rms(jnp.concatenate([q_ref[:, hq * dh:(hq + 1) * dh] for hq in heads[h]], axis=0).astype(F32), qg)
         .astype(BF16) for h in groups]
    sk = []
    for h in groups:
        col_sink = jnp.full((GQA_GROUP * blk, 1), sink_ref[heads[h][-1]] * LOG2E, F32)
        for g in reversed(range(GQA_GROUP - 1)):
            col_sink = jnp.where(grow < (g + 1) * blk, sink_ref[heads[h][g]] * LOG2E, col_sink)
        sk.append(col_sink)
    s = [_dot_nt(q[h], k[h]) + bias_sc[h] + edge for h in groups]
    m = [jnp.maximum(jnp.max(s[h], axis=-1, keepdims=True), sk[h]) for h in groups]
    p = [jnp.exp2(s[h] - m[h]).astype(BF16) for h in groups]
    oa = [_dot(p[h], vaug[h]) for h in groups]
    o = [oa[h][:, :dh] / (oa[h][:, dh:dh + 1] + jnp.exp2(sk[h] - m[h])) for h in groups]
    for h in groups:
        for g, hq in enumerate(heads[h]):
            o_ref[:, hq * dh:(hq + 1) * dh] = o[h][g * blk:(g + 1) * blk].astype(o_ref.dtype)


def attention_core(proj, rel_bias, q_g, k_g, sink, batch, seq):
    n, width = proj.shape
    hq = rel_bias.shape[1]
    n_kv = hq // GQA_GROUP
    dq = hq * HEAD_DIM
    dkv = 2 * n_kv * HEAD_DIM
    assert width == dq + dkv and dq % dkv == 0
    kvcol = dq // dkv
    nb = seq // ATT_BLOCK
    blk = ATT_BLOCK
    row = lambda b, i: b * nb + i
    return pl.pallas_call(
        functools.partial(_attn_kernel, nb=nb, n_kv=n_kv),
        grid=(batch, nb),
        in_specs=[pl.BlockSpec(memory_space=pltpu.SMEM),
                  pl.BlockSpec(memory_space=pltpu.SMEM),
                  pl.BlockSpec((blk, dq), lambda b, i: (row(b, i), 0)),
                  pl.BlockSpec((blk, dkv), lambda b, i: (row(b, jnp.maximum(i - 1, 0)), kvcol)),
                  pl.BlockSpec((blk, dkv), lambda b, i: (row(b, i), kvcol)),
                  pl.BlockSpec((blk, dkv), lambda b, i: (row(b, jnp.minimum(i + 1, nb - 1)), kvcol)),
                  pl.BlockSpec((blk, 3 * blk), lambda b, i: (0, 0)),
                  pl.BlockSpec((1, HEAD_DIM), lambda b, i: (0, 0)),
                  pl.BlockSpec((1, HEAD_DIM), lambda b, i: (0, 0))],
        out_specs=pl.BlockSpec((blk, dq), lambda b, i: (row(b, i), 0)),
        out_shape=jax.ShapeDtypeStruct((n, dq), BF16),
        scratch_shapes=[pltpu.VMEM((n_kv, GQA_GROUP * blk, 3 * blk), F32)],
        compiler_params=_cparams(2),
    )(sink.astype(F32), rel_bias.astype(F32), proj, proj, proj, proj, _attn_bucket_table(),
      q_g.reshape(1, HEAD_DIM).astype(F32), k_g.reshape(1, HEAD_DIM).astype(F32))


def _prefix_incl(x, tri):
    t = x.shape[1]
    outs = []
    carry = jnp.zeros((x.shape[0], 1), F32)
    for c in range(t // LANE):
        p = _dot(x[:, c * LANE:(c + 1) * LANE].astype(BF16), tri) + carry
        outs.append(p)
        carry = p[:, LANE - 1:LANE]
    return jnp.concatenate(outs, axis=1)


def _select_kernel(aff_ref, cmp_ref, offc_ref, rng_ref, pos_sc, dest_sc, vals_sc, *, cap, rchunk):
    aff = aff_ref[0]
    ne, t = aff.shape
    bits = pltpu.bitcast(aff, I32)

    def search(i, lo):
        cand = lo | lax.shift_left(jnp.int32(1), 30 - i)
        cnt = jnp.sum((bits >= cand).astype(I32), axis=1, keepdims=True)
        return jnp.where(cnt >= cap, cand, lo)

    thr = lax.fori_loop(0, 31, search, jnp.zeros((ne, 1), I32))
    gt = bits > thr
    eq = bits == thr
    need = (cap - jnp.sum(gt.astype(I32), axis=1, keepdims=True)).astype(F32)
    ri = lax.broadcasted_iota(I32, (LANE, LANE), 0)
    ci = lax.broadcasted_iota(I32, (LANE, LANE), 1)
    tri = (ri <= ci).astype(BF16)
    eqf = eq.astype(F32)
    rank_eq = _prefix_incl(eqf, tri) - eqf
    sel = jnp.logical_or(gt, jnp.logical_and(eq, rank_eq < need))
    self_ = sel.astype(F32)
    kt = jnp.sum(self_, axis=0, keepdims=True)
    pre = _prefix_incl(jnp.concatenate([self_, jnp.broadcast_to(kt, (8, t))], axis=0), tri)
    pos_sc[...] = jnp.where(sel, pre[:ne] - 1.0, -1.0)
    end = pre[ne:ne + 1]
    off = end - kt
    lr = lax.broadcasted_iota(I32, (ne, ne), 0)
    lc = lax.broadcasted_iota(I32, (ne, ne), 1)
    jexp = _dot((lc < lr).astype(BF16), self_.astype(BF16))
    dest_sc[...] = off + jexp

    tok = lax.broadcasted_iota(I32, (1, t), 1)
    vals_sc[0:1, :] = (tok >> 6).astype(F32)
    vals_sc[1:2, :] = (tok & 63).astype(F32)
    vals_sc[7:8, :] = jnp.zeros((1, t), F32)
    slot = lax.broadcasted_iota(I32, (cap, 1), 0).astype(F32)

    def compact(e, carry):
        d = dest_sc[pl.ds(e, 1), :]
        dh = jnp.floor(d * (1.0 / LANE))
        g = aff_ref[0, pl.ds(e, 1), :]
        g0 = g.astype(BF16).astype(F32)
        g1 = (g - g0).astype(BF16).astype(F32)
        vals_sc[2:3, :] = dh
        vals_sc[3:4, :] = d - dh * LANE
        vals_sc[4:5, :] = g0
        vals_sc[5:6, :] = g1
        vals_sc[6:7, :] = g - g0 - g1
        onehot = (pos_sc[pl.ds(e, 1), :] == slot).astype(BF16)
        cmp_ref[0, e] = _dot_nt(vals_sc[...].astype(BF16), onehot)
        return carry

    lax.fori_loop(0, ne, compact, 0)

    eh = jnp.floor(end * (1.0 / LANE))
    oh = jnp.floor(off * (1.0 / LANE))
    v4 = jnp.concatenate([oh, off - oh * LANE, eh, end - eh * LANE, jnp.zeros((4, t), F32)], axis=0)
    eye = (ri == ci).astype(BF16)
    for c in range(t // LANE):
        offc_ref[0, c * LANE:(c + 1) * LANE, :] = _dot_nt(eye, v4[:, c * LANE:(c + 1) * LANE].astype(BF16))

    nchunk = (ne * cap) // rchunk
    low = (lax.broadcasted_iota(I32, (nchunk, 1), 0) * rchunk).astype(F32)
    first = jnp.sum((end <= low).astype(F32), axis=1, keepdims=True)
    last = jnp.sum((end <= low + (rchunk - 1)).astype(F32), axis=1, keepdims=True)
    lane = lax.broadcasted_iota(I32, (nchunk, LANE), 1)
    tiles = jnp.where(lane < LANE // 2, jnp.floor(first * (1.0 / LANE)), jnp.floor(last * (1.0 / LANE)))
    rng_ref[0] = tiles.astype(I32)


COMBINE_ROWS = 256


def route_select(aff, cap, rchunk):
    batch, ne, t = aff.shape
    nchunk = ne * cap // rchunk
    return pl.pallas_call(
        functools.partial(_select_kernel, cap=cap, rchunk=rchunk),
        grid=(batch,),
        in_specs=[pl.BlockSpec((1, ne, t), lambda b: (b, 0, 0))],
        out_specs=[pl.BlockSpec((1, ne, 8, cap), lambda b: (b, 0, 0, 0)),
                   pl.BlockSpec((1, t, 8), lambda b: (b, 0, 0)),
                   pl.BlockSpec((1, nchunk, LANE), lambda b: (b, 0, 0))],
        out_shape=[jax.ShapeDtypeStruct((batch, ne, 8, cap), F32),
                   jax.ShapeDtypeStruct((batch, t, 8), F32),
                   jax.ShapeDtypeStruct((batch, nchunk, LANE), I32)],
        scratch_shapes=[pltpu.VMEM((ne, t), F32), pltpu.VMEM((ne, t), F32), pltpu.VMEM((8, t), F32)],
        compiler_params=_cparams(1),
    )(aff)


def _ffn_kernel(idx_sm, dest_sm, hn_hbm, cmp_ref, w1_ref, w3_ref, w2_ref, r_hbm,
                xs32, xsb, yacc, ysc, wb1, wb3, wb2, gsem, ssem, *, batch, cap, nf, rt):
    e = pl.program_id(0)
    f = pl.program_id(1)
    ne = pl.num_programs(0)
    rows = batch * cap
    sub = rows // nf

    def gather_start(block, r):
        t = idx_sm[block * rows + r]
        pltpu.make_async_copy(hn_hbm.at[pl.ds(t, 1), :], xs32.at[pl.ds(r, 1), :], gsem).start()

    def gather_wait():
        pltpu.make_async_copy(hn_hbm.at[pl.ds(0, rows), :], xs32, gsem).wait()

    def scatter_start(block, r):
        d = dest_sm[block * rows + r]
        pltpu.make_async_copy(ysc.at[pl.ds(r, 1), :], r_hbm.at[pl.ds(d, 1), :], ssem).start()

    def scatter_wait():
        pltpu.make_async_copy(ysc, r_hbm.at[pl.ds(0, rows), :], ssem).wait()

    @pl.when(jnp.logical_and(e == 0, f == 0))
    def _prologue():
        def issue(r, c):
            gather_start(0, r)
            return c
        lax.fori_loop(0, rows, issue, 0)
        ysc[...] = jnp.zeros_like(ysc)

    @pl.when(f == 0)
    def _rows_ready():
        gather_wait()
        xsb[...] = xs32[...].astype(BF16)
        yacc[...] = jnp.zeros_like(yacc)

    for r in range(sub):
        gather_start(e + 1, f * sub + r)
        scatter_start(e, f * sub + r)

    wb1[...] = w1_ref[0, 0].astype(BF16)
    wb3[...] = w3_ref[0, 0].astype(BF16)
    wb2[...] = w2_ref[0, 0].astype(BF16)
    for r in range(rows // rt):
        xs = xsb[r * rt:(r + 1) * rt, :]
        a = _dot(xs, wb1[...])
        u = _dot(xs, wb3[...])
        hmid = (a * jax.nn.sigmoid(a) * u).astype(BF16)
        yacc[r * rt:(r + 1) * rt, :] += _dot(hmid, wb2[...])

    @pl.when(f == nf - 1)
    def _finish():
        scatter_wait()
        ri = lax.broadcasted_iota(I32, (cap, cap), 0)
        ci = lax.broadcasted_iota(I32, (cap, cap), 1)
        eye = (ri == ci).astype(BF16)
        for b in range(batch):
            gt = _dot_nt(eye, cmp_ref[b, 0].astype(BF16))
            g = gt[:, 4:5] + gt[:, 5:6] + gt[:, 6:7]
            ysc[b * cap:(b + 1) * cap, :] = yacc[b * cap:(b + 1) * cap, :] * g

        @pl.when(e == ne - 1)
        def _epilogue():
            def issue(r, c):
                scatter_start(ne, r)
                return c
            lax.fori_loop(0, rows, issue, 0)
            scatter_wait()
            gather_wait()


def expert_ffn(hn, cmp, idx_flat, dest_flat, w1, w3, w2, layer, cap, tf=512, rt=512):
    n, d = hn.shape
    batch, ne = cmp.shape[0], cmp.shape[1]
    dff = w1.shape[3]
    nf = dff // tf
    rows = batch * cap
    rt = min(rt, rows)
    grid_spec = pltpu.PrefetchScalarGridSpec(
        num_scalar_prefetch=2,
        grid=(ne, nf),
        in_specs=[pl.BlockSpec(memory_space=pl.ANY),
                  pl.BlockSpec((batch, 1, 8, cap), lambda e, f, *_: (0, e, 0, 0)),
                  pl.BlockSpec((1, 1, d, tf), lambda e, f, *_: (layer, e, 0, f)),
                  pl.BlockSpec((1, 1, d, tf), lambda e, f, *_: (layer, e, 0, f)),
                  pl.BlockSpec((1, 1, tf, d), lambda e, f, *_: (layer, e, f, 0))],
        out_specs=pl.BlockSpec(memory_space=pl.ANY),
        scratch_shapes=[pltpu.VMEM((rows, d), F32), pltpu.VMEM((rows, d), BF16),
                        pltpu.VMEM((rows, d), F32), pltpu.VMEM((rows, d), F32),
                        pltpu.VMEM((d, tf), BF16), pltpu.VMEM((d, tf), BF16), pltpu.VMEM((tf, d), BF16),
                        pltpu.SemaphoreType.DMA, pltpu.SemaphoreType.DMA])
    return pl.pallas_call(
        functools.partial(_ffn_kernel, batch=batch, cap=cap, nf=nf, rt=rt),
        grid_spec=grid_spec,
        out_shape=jax.ShapeDtypeStruct((batch * ne * cap + rows, d), F32),
        compiler_params=_cparams(2),
    )(idx_flat, dest_flat, hn, cmp, w1, w3, w2)


def _combine_kernel(tlo_sm, thi_sm, r_ref, offc_ref, x_hbm, o_ref, sem, *, nchunk, rchunk):
    b = pl.program_id(0)
    j = pl.program_id(1)
    ntile = o_ref.shape[1] // LANE

    @pl.when(j == 0)
    def _init():
        cp = pltpu.make_async_copy(x_hbm.at[b], o_ref.at[0], sem)
        cp.start()
        cp.wait()

    r = r_ref[...]
    hi = r.astype(BF16)
    lo = (r - hi.astype(F32)).astype(BF16)
    rowid = (j * rchunk + lax.broadcasted_iota(I32, (1, rchunk), 1)).astype(F32)

    def contrib(i, valid):
        t0 = pl.multiple_of(i * LANE, LANE)
        oc = offc_ref[0, pl.ds(t0, LANE), :]
        off = oc[:, 0:1] * LANE + oc[:, 1:2]
        end = oc[:, 2:3] * LANE + oc[:, 3:4]
        p = jnp.logical_and(jnp.logical_and(rowid >= off, rowid < end), valid).astype(BF16)
        return t0, _dot(p, hi) + _dot(p, lo)

    def tile(i, c):
        t0, y = contrib(i, True)
        o_ref[0, pl.ds(t0, LANE), :] += y
        return c

    tlo = tlo_sm[b * nchunk + j]
    thi = thi_sm[b * nchunk + j]
    ta, ya = contrib(tlo, True)
    tb, yb = contrib(jnp.minimum(tlo + 1, ntile - 1), tlo + 1 <= thi)
    o_ref[0, pl.ds(ta, LANE), :] += ya
    o_ref[0, pl.ds(tb, LANE), :] += yb
    lax.fori_loop(tlo + 2, thi + 1, tile, 0)


def combine(rbuf, offc, tlo, thi, x, rows_per_batch, rchunk):
    batch, t, d = x.shape
    nchunk = rows_per_batch // rchunk
    grid_spec = pltpu.PrefetchScalarGridSpec(
        num_scalar_prefetch=2,
        grid=(batch, nchunk),
        in_specs=[pl.BlockSpec((rchunk, d), lambda b, j, *_: (b * nchunk + j, 0)),
                  pl.BlockSpec((1, t, 8), lambda b, j, *_: (b, 0, 0)),
                  pl.BlockSpec(memory_space=pl.ANY)],
        out_specs=pl.BlockSpec((1, t, d), lambda b, j, *_: (b, 0, 0)),
        scratch_shapes=[pltpu.SemaphoreType.DMA])
    return pl.pallas_call(
        functools.partial(_combine_kernel, nchunk=nchunk, rchunk=rchunk),
        grid_spec=grid_spec,
        out_shape=jax.ShapeDtypeStruct((batch, t, d), F32),
        compiler_params=_cparams(2),
    )(tlo, thi, rbuf, offc, x)


def moe_block(x1, hn, aff, w1, w3, w2, layer, batch):
    n, d = x1.shape
    t = n // batch
    ne = aff.shape[1]
    cap = CAPACITY_FACTOR * t // ne
    rows = batch * cap
    cmp, offc, rng = route_select(aff, cap, COMBINE_ROWS)
    ci = cmp.astype(I32)
    boff = jnp.arange(batch, dtype=I32)[:, None, None]
    idx = (ci[:, :, 0] * 64 + ci[:, :, 1] + boff * t).transpose(1, 0, 2).reshape(-1)
    dest = (ci[:, :, 2] * LANE + ci[:, :, 3] + boff * (ne * cap)).transpose(1, 0, 2).reshape(-1)
    idx = jnp.concatenate([idx, idx[:rows]])
    dest = jnp.concatenate([batch * ne * cap + jnp.arange(rows, dtype=I32), dest])
    rbuf = expert_ffn(hn, cmp, idx, dest, w1, w3, w2, layer, cap)
    tlo = rng[:, :, 0].reshape(-1)
    thi = rng[:, :, LANE - 1].reshape(-1)
    return combine(rbuf, offc, tlo, thi, x1.reshape(batch, t, d), ne * cap, COMBINE_ROWS)


MLSTM_L = 256


def _mlstm_proj_kernel(x_ref, g_ref, wm_ref, wkt_ref, wgc_ref, wgt_ref, main_ref, kt_ref, gc_ref, gt_ref, *, nsub):
    h = _rms(x_ref[...], g_ref[...]).astype(BF16)
    main_ref[...] = _dot(h, wm_ref[...]).astype(main_ref.dtype)
    gc_ref[...] = _dot(h, wgc_ref[...])
    kt = _dot_nt(wkt_ref[...], h).astype(kt_ref.dtype)
    gt = _dot_nt(wgt_ref[...], h)
    for j in range(nsub):
        kt_ref[j] = kt[:, j * MLSTM_L:(j + 1) * MLSTM_L]
        gt_ref[j] = gt[:, j * MLSTM_L:(j + 1) * MLSTM_L]


def mlstm_project(x, g, wm, wkt, wgc, wgt, tm=512):
    n, d = x.shape
    nsub = tm // MLSTM_L
    nck = n // MLSTM_L
    const = lambda i: (0, 0)
    return pl.pallas_call(
        functools.partial(_mlstm_proj_kernel, nsub=nsub),
        grid=(n // tm,),
        in_specs=[pl.BlockSpec((tm, d), lambda i: (i, 0)),
                  pl.BlockSpec((1, d), const),
                  pl.BlockSpec(wm.shape, const),
                  pl.BlockSpec(wkt.shape, const),
                  pl.BlockSpec(wgc.shape, const),
                  pl.BlockSpec(wgt.shape, const)],
        out_specs=[pl.BlockSpec((tm, wm.shape[1]), lambda i: (i, 0)),
                   pl.BlockSpec((nsub, wkt.shape[0], MLSTM_L), lambda i: (i, 0, 0)),
                   pl.BlockSpec((tm, LANE), lambda i: (i, 0)),
                   pl.BlockSpec((nsub, wgt.shape[0], MLSTM_L), lambda i: (i, 0, 0))],
        out_shape=[jax.ShapeDtypeStruct((n, wm.shape[1]), BF16),
                   jax.ShapeDtypeStruct((nck, wkt.shape[0], MLSTM_L), BF16),
                   jax.ShapeDtypeStruct((n, LANE), F32),
                   jax.ShapeDtypeStruct((nck, wgt.shape[0], MLSTM_L), F32)],
        compiler_params=_cparams(1),
    )(x, g.reshape(1, d), wm, wkt, wgc, wgt)


def _log_sigmoid(x):
    return jnp.minimum(x, 0.0) - jnp.log1p(jnp.exp(-jnp.abs(x)))


def _dot_hi(a, b):
    return jnp.dot(a, b, preferred_element_type=F32, precision=lax.Precision.HIGHEST)


def _gate_prep_kernel(gc_ref, gt_ref, bc_ref, bt_ref, oc_ref, ot_ref, *, nh):
    ln = MLSTM_L
    ri = lax.broadcasted_iota(I32, (ln, ln), 0)
    ci = lax.broadcasted_iota(I32, (ln, ln), 1)
    low = (ci <= ri).astype(F32)
    upp = (ci >= ri).astype(F32)
    ls = _log_sigmoid(gc_ref[...] + bc_ref[...])
    lane = lax.broadcasted_iota(I32, ls.shape, 1)
    oc_ref[...] = jnp.where(lane < nh, _dot_hi(low, ls), _dot_hi(upp, ls))
    pre = gt_ref[0] + bt_ref[...]
    lst = _log_sigmoid(pre)
    row = lax.broadcasted_iota(I32, pre.shape, 0)
    fwd = jnp.logical_and(row >= nh, row < 2 * nh)
    bwd = row >= 3 * nh
    ot_ref[0] = jnp.where(fwd, _dot_hi(lst, upp), jnp.where(bwd, _dot_hi(lst, low), pre))


def gate_prep(gc, gt, bias_c, bias_t, nh):
    n = gc.shape[0]
    nck, rows, ln = gt.shape
    return pl.pallas_call(
        functools.partial(_gate_prep_kernel, nh=nh),
        grid=(nck,),
        in_specs=[pl.BlockSpec((ln, LANE), lambda c: (c, 0)),
                  pl.BlockSpec((1, rows, ln), lambda c: (c, 0, 0)),
                  pl.BlockSpec((1, LANE), lambda c: (0, 0)),
                  pl.BlockSpec((rows, 1), lambda c: (0, 0))],
        out_specs=[pl.BlockSpec((ln, LANE), lambda c: (c, 0)),
                   pl.BlockSpec((1, rows, ln), lambda c: (c, 0, 0))],
        out_shape=[jax.ShapeDtypeStruct((n, LANE), F32),
                   jax.ShapeDtypeStruct((nck, rows, ln), F32)],
        compiler_params=_cparams(1),
    )(gc, gt, bias_c, bias_t)


def _mlstm_kernel(q_ref, kt_ref, v_ref, og_ref, gc_ref, gt_ref, ng_ref, y_ref, hf, hb, cst, *, nc, nh, dqk, dv):
    p = pl.program_id(1)
    ln = MLSTM_L
    ri = lax.broadcasted_iota(I32, (ln, ln), 0)
    ci = lax.broadcasted_iota(I32, (ln, ln), 1)
    masks = (ci <= ri, ci >= ri)
    lane = lax.broadcasted_iota(I32, (ln, LANE), 1)
    ones_col = (lane == 0).astype(BF16)
    cst[...] = jnp.zeros_like(cst)

    chains = [(hh, d) for hh in range(2) for d in range(2)]

    def step(c, ms):
        cks = [c if d == 0 else nc - 1 - c for _, d in chains]
        r0s = [pl.multiple_of(ck * ln, ln) for ck in cks]
        cprev = [cst[i] for i in range(4)]
        q = [q_ref[pl.ds(r0s[i], ln), hh * dqk:(hh + 1) * dqk] for i, (hh, _) in enumerate(chains)]
        kt = [kt_ref[cks[i], hh * dqk:(hh + 1) * dqk, :] for i, (hh, _) in enumerate(chains)]
        vaug = [jnp.concatenate([v_ref[pl.ds(r0s[i], ln), hh * dv:(hh + 1) * dv], ones_col], axis=1)
                for i, (hh, _) in enumerate(chains)]
        bcol = [jnp.sum(jnp.where(lane == d * nh + 2 * p + hh, gc_ref[pl.ds(r0s[i], ln), :], 0.0),
                        axis=-1, keepdims=True) for i, (hh, d) in enumerate(chains)]
        irow = [gt_ref[cks[i], pl.ds(2 * d * nh + 2 * p + hh, 1), :] for i, (hh, d) in enumerate(chains)]
        brow = [gt_ref[cks[i], pl.ds((2 * d + 1) * nh + 2 * p + hh, 1), :] for i, (hh, d) in enumerate(chains)]
        s = [_dot(q[i], kt[i]) for i in range(4)]
        inter = [_dot(q[i], cprev[i].astype(BF16)) for i in range(4)]
        dm = [jnp.where(masks[d], bcol[i] - brow[i] + irow[i], NEG) for i, (_, d) in enumerate(chains)]
        a = [bcol[i] + ms[i] for i in range(4)]
        mt = [jnp.maximum(a[i], jnp.max(dm[i], axis=-1, keepdims=True)) for i in range(4)]
        sw = [(jnp.exp(dm[i] - mt[i]) * s[i]).astype(BF16) for i in range(4)]
        tot = [jnp.exp(a[i] - mt[i]) * inter[i] + _dot(sw[i], vaug[i]) for i in range(4)]
        hc = [tot[i][:, :dv] / jnp.maximum(jnp.abs(tot[i][:, dv:dv + 1]), jnp.exp(-mt[i])) for i in range(4)]
        bl = [brow[i][:, (ln - 1 if d == 0 else 0):(ln if d == 0 else 1)] for i, (_, d) in enumerate(chains)]
        g = [bl[i] - brow[i] + irow[i] for i in range(4)]
        m_new = [jnp.maximum(bl[i] + ms[i], jnp.max(g[i], axis=-1, keepdims=True)) for i in range(4)]
        kw = [(kt[i].astype(F32) * jnp.exp(g[i] - m_new[i])).astype(BF16) for i in range(4)]
        c_new = [jnp.exp(bl[i] + ms[i] - m_new[i]) * cprev[i] + _dot(kw[i], vaug[i]) for i in range(4)]
        for i, (hh, d) in enumerate(chains):
            cst[i] = c_new[i]
            (hf if d == 0 else hb)[pl.ds(r0s[i], ln), hh * dv:(hh + 1) * dv] = hc[i]
        return tuple(m_new)

    lax.fori_loop(0, nc, step, tuple(jnp.zeros((1, 1), F32) for _ in range(4)))
    for hh in range(2):
        sl = slice(hh * dv, (hh + 1) * dv)
        hs = _rms(hf[:, sl] + hb[:, sl], ng_ref[:, sl])
        y_ref[:, sl] = (hs * jax.nn.sigmoid(og_ref[:, sl].astype(F32))).astype(y_ref.dtype)


def mlstm_core(main, ktc, gcol, grow, out_g, batch, seq, nh, dqk, dv):
    n = main.shape[0]
    nc = seq // MLSTM_L
    npair = nh // 2
    qblocks = nh * dqk // (2 * dqk)
    vblocks = (nh * dqk) // (2 * dv)
    ogblocks = (nh * dqk + nh * dv) // (2 * dv)
    return pl.pallas_call(
        functools.partial(_mlstm_kernel, nc=nc, nh=nh, dqk=dqk, dv=dv),
        grid=(batch, npair),
        in_specs=[pl.BlockSpec((seq, 2 * dqk), lambda b, p: (b, p)),
                  pl.BlockSpec((nc, 2 * dqk, MLSTM_L), lambda b, p: (b, p, 0)),
                  pl.BlockSpec((seq, 2 * dv), lambda b, p: (b, vblocks + p)),
                  pl.BlockSpec((seq, 2 * dv), lambda b, p: (b, ogblocks + p)),
                  pl.BlockSpec((seq, LANE), lambda b, p: (b, 0)),
                  pl.BlockSpec((nc, 4 * nh, MLSTM_L), lambda b, p: (b, 0, 0)),
                  pl.BlockSpec((1, 2 * dv), lambda b, p: (0, p))],
        out_specs=pl.BlockSpec((seq, 2 * dv), lambda b, p: (b, p)),
        out_shape=jax.ShapeDtypeStruct((n, nh * dv), BF16),
        scratch_shapes=[pltpu.VMEM((seq, 2 * dv), F32), pltpu.VMEM((seq, 2 * dv), F32),
                        pltpu.VMEM((4, dqk, 2 * dv), F32)],
        compiler_params=_cparams(2),
    )(main, ktc, main, main, gcol, grow, out_g.reshape(1, nh * dv).astype(F32))


def mlstm_layer(x2d, batch, seq, norm_g, w_in, b_i, b_f, out_g, w_out, ffn_g, wr):
    d = x2d.shape[1]
    nh = MLSTM_HEADS
    dv = d // nh
    dqk = dv // 2
    o1, o2, o3, o4 = nh * dqk, 2 * nh * dqk, 2 * nh * dqk + nh * dv, 2 * nh * dqk + 2 * nh * dv
    wm = jnp.concatenate([w_in[:, :o1], w_in[:, o2:o4]], axis=1).astype(BF16)
    wkt = (w_in[:, o1:o2] * (dqk ** -0.5)).T.astype(BF16)
    wg = w_in[:, o4:]
    wgc = jnp.concatenate([wg[:, nh:2 * nh], wg[:, 3 * nh:], jnp.zeros((d, LANE - 2 * nh), F32)], axis=1).astype(BF16)
    wgt = wg.T.astype(BF16)
    main, ktc, gc, gt = mlstm_project(x2d, norm_g, wm, wkt, wgc, wgt)
    bias_c = jnp.concatenate([b_f[0], b_f[1], jnp.zeros((LANE - 2 * nh,), F32)]).reshape(1, LANE).astype(F32)
    bias_t = jnp.concatenate([b_i[0], b_f[0], b_i[1], b_f[1]]).reshape(4 * nh, 1).astype(F32)
    gcol, grow = gate_prep(gc, gt, bias_c, bias_t, nh)
    y = mlstm_core(main, ktc, gcol, grow, out_g, batch, seq, nh, dqk, dv)
    return mm_res_router(y, w_out.astype(BF16), x2d, ffn_g, wr.T.astype(BF16), batch)


def attention_layer(x2d, batch, seq, norm_g, w_in, q_g, k_g, sink, w_out, rel_bias, ffn_g, wr):
    proj = norm_matmul(x2d, norm_g, w_in.astype(BF16), BF16)
    o = attention_core(proj, rel_bias, q_g, k_g, sink, batch, seq)
    return mm_res_router(o, w_out.astype(BF16), x2d, ffn_g, wr.T.astype(BF16), batch)


def kernel(x, rel_bias, attn_norm_g, attn_w_in, attn_q_norm_g, attn_k_norm_g, attn_sink, attn_w_out, mlstm_norm_g, mlstm_w_in, mlstm_b_i, mlstm_b_f, mlstm_out_norm_g, mlstm_w_out, ffn_norm_g, router_w, expert_w1, expert_w3, expert_w2):
    batch, seq, d = x.shape
    x2d = x.reshape(batch * seq, d)
    x1, hn, aff = attention_layer(x2d, batch, seq, attn_norm_g[0], attn_w_in[0], attn_q_norm_g[0],
                                  attn_k_norm_g[0], attn_sink[0], attn_w_out[0], rel_bias,
                                  ffn_norm_g[0], router_w[0])
    x = moe_block(x1, hn, aff, expert_w1, expert_w3, expert_w2, 0, batch)
    x1, hn, aff = mlstm_layer(x.reshape(batch * seq, d), batch, seq, mlstm_norm_g[0], mlstm_w_in[0], mlstm_b_i[0],
                              mlstm_b_f[0], mlstm_out_norm_g[0], mlstm_w_out[0], ffn_norm_g[1], router_w[1])
    return moe_block(x1, hn, aff, expert_w1, expert_w3, expert_w2, 1, batch)
```

```python
import functools
import math

import jax
import jax.numpy as jnp
from jax import lax
from jax.experimental import pallas as pl
from jax.experimental.pallas import tpu as pltpu

F32 = jnp.float32
BF16 = jnp.bfloat16
I32 = jnp.int32
U32 = jnp.uint32

RMS_EPS = 1e-6
NEG = -1e30
LOG2E = 1.4426950408889634
LANE = 128
VMEM_LIMIT = 56 * 1024 * 1024

HEAD_DIM = 64
GQA_GROUP = 4
ATT_BLOCK = 128
NUM_BUCKETS = 32
MAX_DISTANCE = 128
N_EXPERTS = 16
CAPACITY_FACTOR = 2
MLSTM_HEADS = 8


def _cparams(n_axes, vmem=VMEM_LIMIT):
    return pltpu.CompilerParams(dimension_semantics=("arbitrary",) * n_axes, vmem_limit_bytes=vmem)


def _rms(x, g):
    return x * lax.rsqrt(jnp.mean(x * x, axis=-1, keepdims=True) + RMS_EPS) * g


def _dot(a, b):
    return jnp.dot(a, b, preferred_element_type=F32)


def _dot_nt(a, b):
    return lax.dot_general(a, b, (((1,), (1,)), ((), ())), preferred_element_type=F32)


def _attn_proj_kernel(x_ref, g_ref, wq_ref, wk_ref, wvt_ref, qg_ref, kg_ref, seg_ref, q_ref, k_ref, vt_ref):
    h = _rms(x_ref[...], g_ref[...]).astype(BF16)
    seg = seg_ref[...]
    w = seg.shape[0]

    def head_norm(t, gain_ref, out_ref):
        for j in range(t.shape[1] // w):
            tj = t[:, j * w:(j + 1) * w]
            ms = _dot((tj * tj).astype(BF16), seg)
            out_ref[:, j * w:(j + 1) * w] = (tj * lax.rsqrt(ms + RMS_EPS) * gain_ref[:, j * w:(j + 1) * w]
                                             ).astype(out_ref.dtype)

    head_norm(_dot(h, wq_ref[...]), qg_ref, q_ref)
    head_norm(_dot(h, wk_ref[...]), kg_ref, k_ref)
    vt_ref[...] = _dot_nt(wvt_ref[...], h).astype(vt_ref.dtype)


def attention_project(x, g, w_in, q_g, k_g, n_q, n_kv, tm=512):
    n, d = x.shape
    dh = HEAD_DIM
    dq, dk = n_q * dh, n_kv * dh
    wq = w_in[:, :dq].astype(BF16)
    wk = w_in[:, dq:dq + dk].astype(BF16)
    wvt = w_in[:, dq + dk:].T.astype(BF16)
    qg = jnp.tile(q_g.astype(F32) * (dh ** -0.5 * LOG2E), n_q).reshape(1, dq)
    kg = jnp.tile(k_g.astype(F32), n_kv).reshape(1, dk)
    hid = jnp.arange(dk) // dh
    seg = jnp.where(hid[:, None] == hid[None, :], 1.0 / dh, 0.0).astype(BF16)
    const2 = lambda i: (0, 0)
    return pl.pallas_call(
        _attn_proj_kernel,
        grid=(n // tm,),
        in_specs=[pl.BlockSpec((tm, d), lambda i: (i, 0)),
                  pl.BlockSpec((1, d), const2),
                  pl.BlockSpec((d, dq), const2),
                  pl.BlockSpec((d, dk), const2),
                  pl.BlockSpec((dk, d), const2),
                  pl.BlockSpec((1, dq), const2),
                  pl.BlockSpec((1, dk), const2),
                  pl.BlockSpec((dk, dk), const2)],
        out_specs=[pl.BlockSpec((tm, dq), lambda i: (i, 0)),
                   pl.BlockSpec((tm, dk), lambda i: (i, 0)),
                   pl.BlockSpec((dk, tm), lambda i: (0, i))],
        out_shape=[jax.ShapeDtypeStruct((n, dq), BF16),
                   jax.ShapeDtypeStruct((n, dk), BF16),
                   jax.ShapeDtypeStruct((dk, n), BF16)],
        compiler_params=_cparams(1),
    )(x, g.reshape(1, d), wq, wk, wvt, qg, kg, seg)


def _mm_res_router_kernel(a_ref, w_ref, x_ref, g_ref, wr_ref, x1_ref, hn_ref, aff_ref, *, a_transposed):
    if a_transposed:
        y = lax.dot_general(a_ref[...], w_ref[...], (((0,), (0,)), ((), ())), preferred_element_type=F32)
    else:
        y = _dot(a_ref[...], w_ref[...])
    x1 = x_ref[...] + y
    x1_ref[...] = x1
    hn = _rms(x1, g_ref[...])
    hn_ref[...] = hn
    logits = _dot_nt(wr_ref[...], hn.astype(BF16))
    mx = jnp.max(logits, axis=0, keepdims=True)
    p = jnp.exp(logits - mx)
    aff_ref[0] = p / jnp.sum(p, axis=0, keepdims=True)


def mm_res_router(a, w, x, g, wr_t, batch, a_transposed=False, tm=512):
    n, d = x.shape
    k = w.shape[0]
    e = wr_t.shape[0]
    t = n // batch
    tpb = t // tm
    a_spec = pl.BlockSpec((k, tm), lambda i: (0, i)) if a_transposed else pl.BlockSpec((tm, k), lambda i: (i, 0))
    return pl.pallas_call(
        functools.partial(_mm_res_router_kernel, a_transposed=a_transposed),
        grid=(n // tm,),
        in_specs=[a_spec,
                  pl.BlockSpec((k, d), lambda i: (0, 0)),
                  pl.BlockSpec((tm, d), lambda i: (i, 0)),
                  pl.BlockSpec((1, d), lambda i: (0, 0)),
                  pl.BlockSpec((e, d), lambda i: (0, 0))],
        out_specs=[pl.BlockSpec((tm, d), lambda i: (i, 0)),
                   pl.BlockSpec((tm, d), lambda i: (i, 0)),
                   pl.BlockSpec((1, e, tm), lambda i: (i // tpb, 0, i % tpb))],
        out_shape=[jax.ShapeDtypeStruct((n, d), F32),
                   jax.ShapeDtypeStruct((n, d), F32),
                   jax.ShapeDtypeStruct((batch, e, t), F32)],
        compiler_params=_cparams(1),
    )(a, w, x, g.reshape(1, d), wr_t)


def _t5_bucket(rel):
    nb = NUM_BUCKETS // 2
    ret = (rel > 0).astype(jnp.int32) * nb
    n = jnp.abs(rel)
    max_exact = nb // 2
    nf = jnp.maximum(n, 1).astype(jnp.float32)
    large = max_exact + (jnp.log(nf / max_exact) / math.log(MAX_DISTANCE / max_exact)
                         * (nb - max_exact)).astype(jnp.int32)
    large = jnp.minimum(large, nb - 1)
    return ret + jnp.where(n < max_exact, n, large)


def _attn_bucket_table():
    kk = jnp.arange(3 * ATT_BLOCK)[:, None]
    qq = jnp.arange(ATT_BLOCK)[None, :]
    rel = kk - ATT_BLOCK - qq
    return jnp.where(jnp.abs(rel) <= ATT_BLOCK, _t5_bucket(rel), -1).astype(I32)


def _attn_kernel(sink_ref, rb_ref, q_ref, kp_ref, kc_ref, kn_ref, vp_ref, vc_ref, vn_ref, bucket_ref, ot_ref,
                 bias_sc, *, nb, n_kv):
    n = pl.program_id(1)
    blk = ATT_BLOCK
    dh = HEAD_DIM
    gw = GQA_GROUP * blk

    @pl.when(jnp.logical_and(pl.program_id(0) == 0, n == 0))
    def _bias_table():
        bk = bucket_ref[...]
        for hq in range(n_kv * GQA_GROUP):
            acc = jnp.full(bk.shape, NEG, F32)
            for k in range(NUM_BUCKETS):
                acc = jnp.where(bk == k, rb_ref[k, hq] * LOG2E, acc)
            bias_sc[hq // GQA_GROUP, :, (hq % GQA_GROUP) * blk:(hq % GQA_GROUP + 1) * blk] = acc

    keys = jnp.concatenate([kp_ref[...], kc_ref[...], kn_ref[...]], axis=0)
    vt = jnp.concatenate([vp_ref[...], vc_ref[...], vn_ref[...]], axis=1)
    kidx = lax.broadcasted_iota(I32, (3 * blk, gw), 0)
    valid = jnp.logical_and(jnp.logical_or(n > 0, kidx >= blk),
                            jnp.logical_or(n < nb - 1, kidx < 2 * blk))
    ones_rows = (lax.broadcasted_iota(I32, (16, 3 * blk), 0) == 0).astype(BF16)
    lane = lax.broadcasted_iota(I32, (1, gw), 1)
    groups = range(n_kv)
    heads = [[h * GQA_GROUP + g for g in range(GQA_GROUP)] for h in groups]
    sk = []
    for h in groups:
        row_sink = jnp.full((1, gw), sink_ref[heads[h][-1]] * LOG2E, F32)
        for g in reversed(range(GQA_GROUP - 1)):
            row_sink = jnp.where(lane < (g + 1) * blk, sink_ref[heads[h][g]] * LOG2E, row_sink)
        sk.append(row_sink)
    q = [jnp.concatenate([q_ref[:, hq * dh:(hq + 1) * dh] for hq in heads[h]], axis=0) for h in groups]
    vaug = [jnp.concatenate([vt[h * dh:(h + 1) * dh, :], ones_rows], axis=0) for h in groups]
    s = [jnp.where(valid, _dot_nt(keys[:, h * dh:(h + 1) * dh], q[h]) + bias_sc[h], NEG) for h in groups]
    m = [jnp.maximum(jnp.max(s[h], axis=0, keepdims=True), sk[h]) for h in groups]
    p = [jnp.exp2(s[h] - m[h]).astype(BF16) for h in groups]
    oa = [_dot(vaug[h], p[h]) for h in groups]
    o = [oa[h][:dh] / (oa[h][dh:dh + 1] + jnp.exp2(sk[h] - m[h])) for h in groups]
    for h in groups:
        for g, hq in enumerate(heads[h]):
            ot_ref[hq * dh:(hq + 1) * dh, :] = o[h][:, g * blk:(g + 1) * blk].astype(ot_ref.dtype)


def attention_core(q, k, vt, rel_bias, sink, batch, seq):
    n, dq = q.shape
    dk = k.shape[1]
    hq = rel_bias.shape[1]
    n_kv = hq // GQA_GROUP
    nb = seq // ATT_BLOCK
    blk = ATT_BLOCK
    prv = lambda b, i: b * nb + jnp.maximum(i - 1, 0)
    cur = lambda b, i: b * nb + i
    nxt = lambda b, i: b * nb + jnp.minimum(i + 1, nb - 1)
    return pl.pallas_call(
        functools.partial(_attn_kernel, nb=nb, n_kv=n_kv),
        grid=(batch, nb),
        in_specs=[pl.BlockSpec(memory_space=pltpu.SMEM),
                  pl.BlockSpec(memory_space=pltpu.SMEM),
                  pl.BlockSpec((blk, dq), lambda b, i: (cur(b, i), 0)),
                  pl.BlockSpec((blk, dk), lambda b, i: (prv(b, i), 0)),
                  pl.BlockSpec((blk, dk), lambda b, i: (cur(b, i), 0)),
                  pl.BlockSpec((blk, dk), lambda b, i: (nxt(b, i), 0)),
                  pl.BlockSpec((dk, blk), lambda b, i: (0, prv(b, i))),
                  pl.BlockSpec((dk, blk), lambda b, i: (0, cur(b, i))),
                  pl.BlockSpec((dk, blk), lambda b, i: (0, nxt(b, i))),
                  pl.BlockSpec((3 * blk, blk), lambda b, i: (0, 0))],
        out_specs=pl.BlockSpec((dq, blk), lambda b, i: (0, cur(b, i))),
        out_shape=jax.ShapeDtypeStruct((dq, n), BF16),
        scratch_shapes=[pltpu.VMEM((n_kv, 3 * blk, GQA_GROUP * blk), F32)],
        compiler_params=_cparams(2),
    )(sink.astype(F32), rel_bias.astype(F32), q, k, k, k, vt, vt, vt, _attn_bucket_table())


def _prefix_incl(x, tri):
    t = x.shape[1]
    xb = x.astype(BF16)
    local = [_dot(xb[:, c * LANE:(c + 1) * LANE], tri) for c in range(t // LANE)]
    outs = []
    carry = jnp.zeros((x.shape[0], 1), F32)
    for p in local:
        p = p + carry
        outs.append(p)
        carry = p[:, LANE - 1:LANE]
    return jnp.concatenate(outs, axis=1)


def _select_kernel(aff_ref, cmp_ref, offc_ref, rng_ref, pos_sc, dest_sc, vals_sc, *, cap, rchunk):
    aff = aff_ref[0]
    ne, t = aff.shape
    bits = pltpu.bitcast(aff, I32)

    def search(i, lo):
        cand = lo | lax.shift_left(jnp.int32(1), 30 - i)
        cnt = jnp.sum((bits >= cand).astype(I32), axis=1, keepdims=True)
        return jnp.where(cnt >= cap, cand, lo)

    thr = lax.fori_loop(0, 31, search, jnp.zeros((ne, 1), I32))
    gt = bits > thr
    eq = bits == thr
    need = (cap - jnp.sum(gt.astype(I32), axis=1, keepdims=True)).astype(F32)
    ri = lax.broadcasted_iota(I32, (LANE, LANE), 0)
    ci = lax.broadcasted_iota(I32, (LANE, LANE), 1)
    tri = (ri <= ci).astype(BF16)
    eqf = eq.astype(F32)
    rank_eq = _prefix_incl(eqf, tri) - eqf
    sel = jnp.logical_or(gt, jnp.logical_and(eq, rank_eq < need))
    self_ = sel.astype(F32)
    kt = jnp.sum(self_, axis=0, keepdims=True)
    pre = _prefix_incl(jnp.concatenate([self_, jnp.broadcast_to(kt, (8, t))], axis=0), tri)
    pos_sc[...] = jnp.where(sel, pre[:ne] - 1.0, -1.0)
    end = pre[ne:ne + 1]
    off = end - kt
    lr = lax.broadcasted_iota(I32, (ne, ne), 0)
    lc = lax.broadcasted_iota(I32, (ne, ne), 1)
    jexp = _dot((lc < lr).astype(BF16), self_.astype(BF16))
    dest_sc[...] = off + jexp

    tok = lax.broadcasted_iota(I32, (1, t), 1)
    vals_sc[0:1, :] = (tok >> 6).astype(F32)
    vals_sc[1:2, :] = (tok & 63).astype(F32)
    vals_sc[7:8, :] = jnp.zeros((1, t), F32)
    slot = lax.broadcasted_iota(I32, (cap, 1), 0).astype(F32)

    def compact(e, carry):
        d = dest_sc[pl.ds(e, 1), :]
        dh = jnp.floor(d * (1.0 / LANE))
        g = aff_ref[0, pl.ds(e, 1), :]
        g0 = g.astype(BF16).astype(F32)
        g1 = (g - g0).astype(BF16).astype(F32)
        vals_sc[2:3, :] = dh
        vals_sc[3:4, :] = d - dh * LANE
        vals_sc[4:5, :] = g0
        vals_sc[5:6, :] = g1
        vals_sc[6:7, :] = g - g0 - g1
        onehot = (pos_sc[pl.ds(e, 1), :] == slot).astype(BF16)
        cmp_ref[0, e] = _dot_nt(vals_sc[...].astype(BF16), onehot)
        return carry

    lax.fori_loop(0, ne, compact, 0)

    eh = jnp.floor(end * (1.0 / LANE))
    oh = jnp.floor(off * (1.0 / LANE))
    v4 = jnp.concatenate([oh, off - oh * LANE, eh, end - eh * LANE, jnp.zeros((4, t), F32)], axis=0)
    eye = (ri == ci).astype(BF16)
    for c in range(t // LANE):
        offc_ref[0, c * LANE:(c + 1) * LANE, :] = _dot_nt(eye, v4[:, c * LANE:(c + 1) * LANE].astype(BF16))

    nchunk = (ne * cap) // rchunk
    low = (lax.broadcasted_iota(I32, (nchunk, 1), 0) * rchunk).astype(F32)
    first = jnp.sum((end <= low).astype(F32), axis=1, keepdims=True)
    last = jnp.sum((end <= low + (rchunk - 1)).astype(F32), axis=1, keepdims=True)
    lane = lax.broadcasted_iota(I32, (nchunk, LANE), 1)
    tiles = jnp.where(lane < LANE // 2, jnp.floor(first * (1.0 / LANE)), jnp.floor(last * (1.0 / LANE)))
    rng_ref[0] = tiles.astype(I32)


COMBINE_ROWS = 256


def route_select(aff, cap, rchunk):
    batch, ne, t = aff.shape
    nchunk = ne * cap // rchunk
    return pl.pallas_call(
        functools.partial(_select_kernel, cap=cap, rchunk=rchunk),
        grid=(batch,),
        in_specs=[pl.BlockSpec((1, ne, t), lambda b: (b, 0, 0))],
        out_specs=[pl.BlockSpec((1, ne, 8, cap), lambda b: (b, 0, 0, 0)),
                   pl.BlockSpec((1, t, 8), lambda b: (b, 0, 0)),
                   pl.BlockSpec((1, nchunk, LANE), lambda b: (b, 0, 0))],
        out_shape=[jax.ShapeDtypeStruct((batch, ne, 8, cap), F32),
                   jax.ShapeDtypeStruct((batch, t, 8), F32),
                   jax.ShapeDtypeStruct((batch, nchunk, LANE), I32)],
        scratch_shapes=[pltpu.VMEM((ne, t), F32), pltpu.VMEM((ne, t), F32), pltpu.VMEM((8, t), F32)],
        compiler_params=_cparams(1),
    )(aff)


def _ffn_kernel(idx_sm, dest_sm, hn_hbm, cmp_ref, w1_ref, w3_ref, w2_ref, r_hbm,
                xs32, xsb, yacc, ysc, wb1, wb3, wb2, gsem, ssem, *, batch, cap, nf, rt):
    e = pl.program_id(0)
    f = pl.program_id(1)
    ne = pl.num_programs(0)
    rows = batch * cap
    sub = rows // nf

    def gather_start(block, r):
        t = idx_sm[block * rows + r]
        pltpu.make_async_copy(hn_hbm.at[pl.ds(t, 1), :], xs32.at[pl.ds(r, 1), :], gsem).start()

    def gather_wait():
        pltpu.make_async_copy(hn_hbm.at[pl.ds(0, rows), :], xs32, gsem).wait()

    def scatter_start(block, r):
        d = dest_sm[block * rows + r]
        pltpu.make_async_copy(ysc.at[pl.ds(r, 1), :], r_hbm.at[pl.ds(d, 1), :], ssem).start()

    def scatter_wait():
        pltpu.make_async_copy(ysc, r_hbm.at[pl.ds(0, rows), :], ssem).wait()

    @pl.when(jnp.logical_and(e == 0, f == 0))
    def _prologue():
        def issue(r, c):
            gather_start(0, r)
            return c
        lax.fori_loop(0, rows, issue, 0)
        ysc[...] = jnp.zeros_like(ysc)

    @pl.when(f == 0)
    def _rows_ready():
        gather_wait()
        xsb[...] = xs32[...].astype(BF16)
        yacc[...] = jnp.zeros_like(yacc)

    for r in range(sub):
        gather_start(e + 1, f * sub + r)
        scatter_start(e, f * sub + r)

    wb1[...] = w1_ref[0, 0].astype(BF16)
    wb3[...] = w3_ref[0, 0].astype(BF16)
    wb2[...] = w2_ref[0, 0].astype(BF16)
    for r in range(rows // rt):
        xs = xsb[r * rt:(r + 1) * rt, :]
        a = _dot(xs, wb1[...])
        u = _dot(xs, wb3[...])
        hmid = (a * jax.nn.sigmoid(a) * u).astype(BF16)
        yacc[r * rt:(r + 1) * rt, :] += _dot(hmid, wb2[...])

    @pl.when(f == nf - 1)
    def _finish():
        scatter_wait()
        ri = lax.broadcasted_iota(I32, (cap, cap), 0)
        ci = lax.broadcasted_iota(I32, (cap, cap), 1)
        eye = (ri == ci).astype(BF16)
        for b in range(batch):
            gt = _dot_nt(eye, cmp_ref[b, 0].astype(BF16))
            g = gt[:, 4:5] + gt[:, 5:6] + gt[:, 6:7]
            y = yacc[b * cap:(b + 1) * cap, :] * g
            half = y.shape[1] // 2
            hi = pltpu.bitcast(y[:, :half].astype(BF16).astype(F32), U32)
            lo = pltpu.bitcast(y[:, half:].astype(BF16).astype(F32), U32)
            ysc[b * cap:(b + 1) * cap, :] = hi | lax.shift_right_logical(lo, jnp.uint32(16))

        @pl.when(e == ne - 1)
        def _epilogue():
            def issue(r, c):
                scatter_start(ne, r)
                return c
            lax.fori_loop(0, rows, issue, 0)
            scatter_wait()
            gather_wait()


def expert_ffn(hn, cmp, idx_flat, dest_flat, w1, w3, w2, layer, cap, tf=512, rt=512):
    n, d = hn.shape
    batch, ne = cmp.shape[0], cmp.shape[1]
    dff = w1.shape[3]
    nf = dff // tf
    rows = batch * cap
    rt = min(rt, rows)
    grid_spec = pltpu.PrefetchScalarGridSpec(
        num_scalar_prefetch=2,
        grid=(ne, nf),
        in_specs=[pl.BlockSpec(memory_space=pl.ANY),
                  pl.BlockSpec((batch, 1, 8, cap), lambda e, f, *_: (0, e, 0, 0)),
                  pl.BlockSpec((1, 1, d, tf), lambda e, f, *_: (layer, e, 0, f)),
                  pl.BlockSpec((1, 1, d, tf), lambda e, f, *_: (layer, e, 0, f)),
                  pl.BlockSpec((1, 1, tf, d), lambda e, f, *_: (layer, e, f, 0))],
        out_specs=pl.BlockSpec(memory_space=pl.ANY),
        scratch_shapes=[pltpu.VMEM((rows, d), F32), pltpu.VMEM((rows, d), BF16),
                        pltpu.VMEM((rows, d), F32), pltpu.VMEM((rows, d // 2), U32),
                        pltpu.VMEM((d, tf), BF16), pltpu.VMEM((d, tf), BF16), pltpu.VMEM((tf, d), BF16),
                        pltpu.SemaphoreType.DMA, pltpu.SemaphoreType.DMA])
    return pl.pallas_call(
        functools.partial(_ffn_kernel, batch=batch, cap=cap, nf=nf, rt=rt),
        grid_spec=grid_spec,
        out_shape=jax.ShapeDtypeStruct((batch * ne * cap + rows, d // 2), U32),
        compiler_params=_cparams(2),
    )(idx_flat, dest_flat, hn, cmp, w1, w3, w2)


def _combine_kernel(tlo_sm, thi_sm, r_ref, offc_ref, x_hbm, o_ref, sem, *, nchunk, rchunk):
    b = pl.program_id(0)
    j = pl.program_id(1)
    ntile = o_ref.shape[1] // LANE

    @pl.when(j == 0)
    def _init():
        cp = pltpu.make_async_copy(x_hbm.at[b], o_ref.at[0], sem)
        cp.start()
        cp.wait()

    r = r_ref[...]
    half = r.shape[1]
    left = pltpu.bitcast(r & jnp.uint32(0xFFFF0000), F32).astype(BF16)
    right = pltpu.bitcast(lax.shift_left(r, jnp.uint32(16)), F32).astype(BF16)
    rowid = (j * rchunk + lax.broadcasted_iota(I32, (1, rchunk), 1)).astype(F32)

    def contrib(i, valid):
        t0 = pl.multiple_of(i * LANE, LANE)
        oc = offc_ref[0, pl.ds(t0, LANE), :]
        off = oc[:, 0:1] * LANE + oc[:, 1:2]
        end = oc[:, 2:3] * LANE + oc[:, 3:4]
        p = jnp.logical_and(jnp.logical_and(rowid >= off, rowid < end), valid).astype(BF16)
        return t0, (_dot(p, left), _dot(p, right))

    def add(t0, y):
        o_ref[0, pl.ds(t0, LANE), :half] += y[0]
        o_ref[0, pl.ds(t0, LANE), half:] += y[1]

    def tile(i, c):
        add(*contrib(i, True))
        return c

    tlo = tlo_sm[b * nchunk + j]
    thi = thi_sm[b * nchunk + j]
    ta, ya = contrib(tlo, True)
    tb, yb = contrib(jnp.minimum(tlo + 1, ntile - 1), tlo + 1 <= thi)
    add(ta, ya)
    add(tb, yb)
    lax.fori_loop(tlo + 2, thi + 1, tile, 0)


def combine(rbuf, offc, tlo, thi, x, rows_per_batch, rchunk):
    batch, t, d = x.shape
    nchunk = rows_per_batch // rchunk
    grid_spec = pltpu.PrefetchScalarGridSpec(
        num_scalar_prefetch=2,
        grid=(batch, nchunk),
        in_specs=[pl.BlockSpec((rchunk, d // 2), lambda b, j, *_: (b * nchunk + j, 0)),
                  pl.BlockSpec((1, t, 8), lambda b, j, *_: (b, 0, 0)),
                  pl.BlockSpec(memory_space=pl.ANY)],
        out_specs=pl.BlockSpec((1, t, d), lambda b, j, *_: (b, 0, 0)),
        scratch_shapes=[pltpu.SemaphoreType.DMA])
    return pl.pallas_call(
        functools.partial(_combine_kernel, nchunk=nchunk, rchunk=rchunk),
        grid_spec=grid_spec,
        out_shape=jax.ShapeDtypeStruct((batch, t, d), F32),
        compiler_params=_cparams(2),
    )(tlo, thi, rbuf, offc, x)


def moe_block(x1, hn, aff, w1, w3, w2, layer, batch):
    n, d = x1.shape
    t = n // batch
    ne = aff.shape[1]
    cap = CAPACITY_FACTOR * t // ne
    rows = batch * cap
    cmp, offc, rng = route_select(aff, cap, COMBINE_ROWS)
    ci = cmp.astype(I32)
    boff = jnp.arange(batch, dtype=I32)[:, None, None]
    idx = (ci[:, :, 0] * 64 + ci[:, :, 1] + boff * t).transpose(1, 0, 2).reshape(-1)
    dest = (ci[:, :, 2] * LANE + ci[:, :, 3] + boff * (ne * cap)).transpose(1, 0, 2).reshape(-1)
    idx = jnp.concatenate([idx, idx[:rows]])
    dest = jnp.concatenate([batch * ne * cap + jnp.arange(rows, dtype=I32), dest])
    rbuf = expert_ffn(hn, cmp, idx, dest, w1, w3, w2, layer, cap)
    tlo = rng[:, :, 0].reshape(-1)
    thi = rng[:, :, LANE - 1].reshape(-1)
    return combine(rbuf, offc, tlo, thi, x1.reshape(batch, t, d), ne * cap, COMBINE_ROWS)


MLSTM_L = 256


def _log_sigmoid(x):
    return jnp.minimum(x, 0.0) - jnp.log1p(jnp.exp(-jnp.abs(x)))


def _split3(x):
    x0 = x.astype(BF16)
    r = x - x0.astype(F32)
    x1 = r.astype(BF16)
    return x0, x1, (r - x1.astype(F32)).astype(BF16)


def _mlstm_proj_kernel(x_ref, g_ref, wm_ref, wkt_ref, wgc_ref, wgt_ref, bc_ref, bt_ref,
                       main_ref, kt_ref, gc_ref, gt_ref, *, nsub, nh):
    ln = MLSTM_L
    h = _rms(x_ref[...], g_ref[...]).astype(BF16)
    main_ref[...] = _dot(h, wm_ref[...]).astype(main_ref.dtype)
    kt = _dot_nt(wkt_ref[...], h).astype(kt_ref.dtype)
    ls = _log_sigmoid(_dot(h, wgc_ref[...]) + bc_ref[...])
    pre = _dot_nt(wgt_ref[...], h) + bt_ref[...]
    lst = _log_sigmoid(pre)
    ri = lax.broadcasted_iota(I32, (ln, ln), 0)
    ci = lax.broadcasted_iota(I32, (ln, ln), 1)
    low = (ci <= ri).astype(BF16)
    upp = (ci >= ri).astype(BF16)
    lane = lax.broadcasted_iota(I32, (ln, LANE), 1)
    row = lax.broadcasted_iota(I32, (pre.shape[0], ln), 0)
    fwd = jnp.logical_and(row >= nh, row < 2 * nh)
    bwd = row >= 3 * nh
    for j in range(nsub):
        sl = slice(j * ln, (j + 1) * ln)
        kt_ref[j] = kt[:, sl]
        pc = _split3(ls[sl])
        gc_ref[sl, :] = jnp.where(lane < nh, sum(_dot(low, x) for x in pc), sum(_dot(upp, x) for x in pc))
        pt = _split3(lst[:, sl])
        gt_ref[j] = jnp.where(fwd, sum(_dot(x, upp) for x in pt),
                              jnp.where(bwd, sum(_dot(x, low) for x in pt), pre[:, sl]))


def mlstm_project(x, g, wm, wkt, wgc, wgt, bias_c, bias_t, nh, tm=512):
    n, d = x.shape
    nsub = tm // MLSTM_L
    nck = n // MLSTM_L
    const = lambda i: (0, 0)
    return pl.pallas_call(
        functools.partial(_mlstm_proj_kernel, nsub=nsub, nh=nh),
        grid=(n // tm,),
        in_specs=[pl.BlockSpec((tm, d), lambda i: (i, 0)),
                  pl.BlockSpec((1, d), const),
                  pl.BlockSpec(wm.shape, const),
                  pl.BlockSpec(wkt.shape, const),
                  pl.BlockSpec(wgc.shape, const),
                  pl.BlockSpec(wgt.shape, const),
                  pl.BlockSpec(bias_c.shape, const),
                  pl.BlockSpec(bias_t.shape, const)],
        out_specs=[pl.BlockSpec((tm, wm.shape[1]), lambda i: (i, 0)),
                   pl.BlockSpec((nsub, wkt.shape[0], MLSTM_L), lambda i: (i, 0, 0)),
                   pl.BlockSpec((tm, LANE), lambda i: (i, 0)),
                   pl.BlockSpec((nsub, wgt.shape[0], MLSTM_L), lambda i: (i, 0, 0))],
        out_shape=[jax.ShapeDtypeStruct((n, wm.shape[1]), BF16),
                   jax.ShapeDtypeStruct((nck, wkt.shape[0], MLSTM_L), BF16),
                   jax.ShapeDtypeStruct((n, LANE), F32),
                   jax.ShapeDtypeStruct((nck, wgt.shape[0], MLSTM_L), F32)],
        compiler_params=_cparams(1),
    )(x, g.reshape(1, d), wm, wkt, wgc, wgt, bias_c, bias_t)


def _mlstm_kernel(q_ref, kt_ref, v_ref, og_ref, gc_ref, gt_ref, ng_ref, y_ref, hf, hb, cst, *, nc, nh, dqk, dv):
    p = pl.program_id(1)
    ln = MLSTM_L
    ri = lax.broadcasted_iota(I32, (ln, ln), 0)
    ci = lax.broadcasted_iota(I32, (ln, ln), 1)
    masks = (ci <= ri, ci >= ri)
    lane = lax.broadcasted_iota(I32, (ln, LANE), 1)
    ones_col = (lane == 0).astype(BF16)
    cst[...] = jnp.zeros_like(cst)

    chains = [(hh, d) for hh in range(2) for d in range(2)]

    def step(c, ms):
        cks = [c if d == 0 else nc - 1 - c for _, d in chains]
        r0s = [pl.multiple_of(ck * ln, ln) for ck in cks]
        cprev = [cst[i] for i in range(4)]
        q = [q_ref[pl.ds(r0s[i], ln), hh * dqk:(hh + 1) * dqk] for i, (hh, _) in enumerate(chains)]
        kt = [kt_ref[cks[i], hh * dqk:(hh + 1) * dqk, :] for i, (hh, _) in enumerate(chains)]
        vaug = [jnp.concatenate([v_ref[pl.ds(r0s[i], ln), hh * dv:(hh + 1) * dv], ones_col], axis=1)
                for i, (hh, _) in enumerate(chains)]
        bcol = [jnp.sum(jnp.where(lane == d * nh + 2 * p + hh, gc_ref[pl.ds(r0s[i], ln), :], 0.0),
                        axis=-1, keepdims=True) for i, (hh, d) in enumerate(chains)]
        irow = [gt_ref[cks[i], pl.ds(2 * d * nh + 2 * p + hh, 1), :] for i, (hh, d) in enumerate(chains)]
        brow = [gt_ref[cks[i], pl.ds((2 * d + 1) * nh + 2 * p + hh, 1), :] for i, (hh, d) in enumerate(chains)]
        s = [_dot(q[i], kt[i]) for i in range(4)]
        inter = [_dot(q[i], cprev[i].astype(BF16)) for i in range(4)]
        dm = [jnp.where(masks[d], bcol[i] - brow[i] + irow[i], NEG) for i, (_, d) in enumerate(chains)]
        a = [bcol[i] + ms[i] for i in range(4)]
        mt = [jnp.maximum(a[i], jnp.max(dm[i], axis=-1, keepdims=True)) for i in range(4)]
        sw = [(jnp.exp(dm[i] - mt[i]) * s[i]).astype(BF16) for i in range(4)]
        tot = [jnp.exp(a[i] - mt[i]) * inter[i] + _dot(sw[i], vaug[i]) for i in range(4)]
        hc = [tot[i][:, :dv] / jnp.maximum(jnp.abs(tot[i][:, dv:dv + 1]), jnp.exp(-mt[i])) for i in range(4)]
        bl = [brow[i][:, (ln - 1 if d == 0 else 0):(ln if d == 0 else 1)] for i, (_, d) in enumerate(chains)]
        g = [bl[i] - brow[i] + irow[i] for i in range(4)]
        m_new = [jnp.maximum(bl[i] + ms[i], jnp.max(g[i], axis=-1, keepdims=True)) for i in range(4)]
        kw = [(kt[i].astype(F32) * jnp.exp(g[i] - m_new[i])).astype(BF16) for i in range(4)]
        c_new = [jnp.exp(bl[i] + ms[i] - m_new[i]) * cprev[i] + _dot(kw[i], vaug[i]) for i in range(4)]
        for i, (hh, d) in enumerate(chains):
            cst[i] = c_new[i]
            (hf if d == 0 else hb)[pl.ds(r0s[i], ln), hh * dv:(hh + 1) * dv] = hc[i]
        return tuple(m_new)

    lax.fori_loop(0, nc, step, tuple(jnp.zeros((1, 1), F32) for _ in range(4)))
    for hh in range(2):
        sl = slice(hh * dv, (hh + 1) * dv)
        hs = _rms(hf[:, sl] + hb[:, sl], ng_ref[:, sl])
        y_ref[:, sl] = (hs * jax.nn.sigmoid(og_ref[:, sl].astype(F32))).astype(y_ref.dtype)


def mlstm_core(main, ktc, gcol, grow, out_g, batch, seq, nh, dqk, dv):
    n = main.shape[0]
    nc = seq // MLSTM_L
    npair = nh // 2
    qblocks = nh * dqk // (2 * dqk)
    vblocks = (nh * dqk) // (2 * dv)
    ogblocks = (nh * dqk + nh * dv) // (2 * dv)
    return pl.pallas_call(
        functools.partial(_mlstm_kernel, nc=nc, nh=nh, dqk=dqk, dv=dv),
        grid=(batch, npair),
        in_specs=[pl.BlockSpec((seq, 2 * dqk), lambda b, p: (b, p)),
                  pl.BlockSpec((nc, 2 * dqk, MLSTM_L), lambda b, p: (b, p, 0)),
                  pl.BlockSpec((seq, 2 * dv), lambda b, p: (b, vblocks + p)),
                  pl.BlockSpec((seq, 2 * dv), lambda b, p: (b, ogblocks + p)),
                  pl.BlockSpec((seq, LANE), lambda b, p: (b, 0)),
                  pl.BlockSpec((nc, 4 * nh, MLSTM_L), lambda b, p: (b, 0, 0)),
                  pl.BlockSpec((1, 2 * dv), lambda b, p: (0, p))],
        out_specs=pl.BlockSpec((seq, 2 * dv), lambda b, p: (b, p)),
        out_shape=jax.ShapeDtypeStruct((n, nh * dv), BF16),
        scratch_shapes=[pltpu.VMEM((seq, 2 * dv), F32), pltpu.VMEM((seq, 2 * dv), F32),
                        pltpu.VMEM((4, dqk, 2 * dv), F32)],
        compiler_params=_cparams(2),
    )(main, ktc, main, main, gcol, grow, out_g.reshape(1, nh * dv).astype(F32))


def mlstm_layer(x2d, batch, seq, norm_g, w_in, b_i, b_f, out_g, w_out, ffn_g, wr):
    d = x2d.shape[1]
    nh = MLSTM_HEADS
    dv = d // nh
    dqk = dv // 2
    o1, o2, o3, o4 = nh * dqk, 2 * nh * dqk, 2 * nh * dqk + nh * dv, 2 * nh * dqk + 2 * nh * dv
    wm = jnp.concatenate([w_in[:, :o1], w_in[:, o2:o4]], axis=1).astype(BF16)
    wkt = (w_in[:, o1:o2] * (dqk ** -0.5)).T.astype(BF16)
    wg = w_in[:, o4:]
    wgc = jnp.concatenate([wg[:, nh:2 * nh], wg[:, 3 * nh:], jnp.zeros((d, LANE - 2 * nh), F32)], axis=1).astype(BF16)
    wgt = wg.T.astype(BF16)
    bias_c = jnp.concatenate([b_f[0], b_f[1], jnp.zeros((LANE - 2 * nh,), F32)]).reshape(1, LANE).astype(F32)
    bias_t = jnp.concatenate([b_i[0], b_f[0], b_i[1], b_f[1]]).reshape(4 * nh, 1).astype(F32)
    main, ktc, gcol, grow = mlstm_project(x2d, norm_g, wm, wkt, wgc, wgt, bias_c, bias_t, nh)
    y = mlstm_core(main, ktc, gcol, grow, out_g, batch, seq, nh, dqk, dv)
    return mm_res_router(y, w_out.astype(BF16), x2d, ffn_g, wr.T.astype(BF16), batch)


def attention_layer(x2d, batch, seq, norm_g, w_in, q_g, k_g, sink, w_out, rel_bias, ffn_g, wr):
    n_q = rel_bias.shape[1]
    q, k, vt = attention_project(x2d, norm_g, w_in, q_g, k_g, n_q, n_q // GQA_GROUP)
    ot = attention_core(q, k, vt, rel_bias, sink, batch, seq)
    return mm_res_router(ot, w_out.astype(BF16), x2d, ffn_g, wr.T.astype(BF16), batch, a_transposed=True)


def kernel(x, rel_bias, attn_norm_g, attn_w_in, attn_q_norm_g, attn_k_norm_g, attn_sink, attn_w_out, mlstm_norm_g, mlstm_w_in, mlstm_b_i, mlstm_b_f, mlstm_out_norm_g, mlstm_w_out, ffn_norm_g, router_w, expert_w1, expert_w3, expert_w2):
    batch, seq, d = x.shape
    x2d = x.reshape(batch * seq, d)
    x1, hn, aff = attention_layer(x2d, batch, seq, attn_norm_g[0], attn_w_in[0], attn_q_norm_g[0],
                                  attn_k_norm_g[0], attn_sink[0], attn_w_out[0], rel_bias,
                                  ffn_norm_g[0], router_w[0])
    x = moe_block(x1, hn, aff, expert_w1, expert_w3, expert_w2, 0, batch)
    x1, hn, aff = mlstm_layer(x.reshape(batch * seq, d), batch, seq, mlstm_norm_g[0], mlstm_w_in[0], mlstm_b_i[0],
                              mlstm_b_f[0], mlstm_out_norm_g[0], mlstm_w_out[0], ffn_norm_g[1], router_w[1])
    return moe_block(x1, hn, aff, expert_w1, expert_w3, expert_w2, 1, batch)
```

```python
import functools
import math

import jax
import jax.numpy as jnp
from jax import lax
from jax.experimental import pallas as pl
from jax.experimental.pallas import tpu as pltpu

F32 = jnp.float32
BF16 = jnp.bfloat16
I32 = jnp.int32
U32 = jnp.uint32

RMS_EPS = 1e-6
NEG = -1e30
LOG2E = 1.4426950408889634
LANE = 128
VMEM_LIMIT = 56 * 1024 * 1024

HEAD_DIM = 64
GQA_GROUP = 4
ATT_BLOCK = 128
NUM_BUCKETS = 32
MAX_DISTANCE = 128
N_EXPERTS = 16
CAPACITY_FACTOR = 2
MLSTM_HEADS = 8


def _cparams(n_axes, vmem=VMEM_LIMIT):
    return pltpu.CompilerParams(dimension_semantics=("arbitrary",) * n_axes, vmem_limit_bytes=vmem)


def _rms(x, g):
    return x * lax.rsqrt(jnp.mean(x * x, axis=-1, keepdims=True) + RMS_EPS) * g


def _dot(a, b):
    return jnp.dot(a, b, preferred_element_type=F32)


def _dot_nt(a, b):
    return lax.dot_general(a, b, (((1,), (1,)), ((), ())), preferred_element_type=F32)


def _attn_proj_kernel(x_ref, g_ref, wq_ref, wk_ref, wvt_ref, qg_ref, kg_ref, seg_ref, q_ref, k_ref, vt_ref):
    h = _rms(x_ref[...], g_ref[...]).astype(BF16)
    seg = seg_ref[...]
    w = seg.shape[0]

    def head_norm(t, gain_ref, out_ref):
        for j in range(t.shape[1] // w):
            tj = t[:, j * w:(j + 1) * w]
            ms = _dot((tj * tj).astype(BF16), seg)
            out_ref[:, j * w:(j + 1) * w] = (tj * lax.rsqrt(ms + RMS_EPS) * gain_ref[:, j * w:(j + 1) * w]
                                             ).astype(out_ref.dtype)

    head_norm(_dot(h, wq_ref[...]), qg_ref, q_ref)
    head_norm(_dot(h, wk_ref[...]), kg_ref, k_ref)
    vt_ref[...] = _dot_nt(wvt_ref[...], h).astype(vt_ref.dtype)


def attention_project(x, g, w_in, q_g, k_g, n_q, n_kv, tm=512):
    n, d = x.shape
    dh = HEAD_DIM
    dq, dk = n_q * dh, n_kv * dh
    wq = w_in[:, :dq].astype(BF16)
    wk = w_in[:, dq:dq + dk].astype(BF16)
    wvt = w_in[:, dq + dk:].T.astype(BF16)
    qg = jnp.tile(q_g.astype(F32) * (dh ** -0.5 * LOG2E), n_q).reshape(1, dq)
    kg = jnp.tile(k_g.astype(F32), n_kv).reshape(1, dk)
    hid = jnp.arange(dk) // dh
    seg = jnp.where(hid[:, None] == hid[None, :], 1.0 / dh, 0.0).astype(BF16)
    const2 = lambda i: (0, 0)
    return pl.pallas_call(
        _attn_proj_kernel,
        grid=(n // tm,),
        in_specs=[pl.BlockSpec((tm, d), lambda i: (i, 0)),
                  pl.BlockSpec((1, d), const2),
                  pl.BlockSpec((d, dq), const2),
                  pl.BlockSpec((d, dk), const2),
                  pl.BlockSpec((dk, d), const2),
                  pl.BlockSpec((1, dq), const2),
                  pl.BlockSpec((1, dk), const2),
                  pl.BlockSpec((dk, dk), const2)],
        out_specs=[pl.BlockSpec((tm, dq), lambda i: (i, 0)),
                   pl.BlockSpec((tm, dk), lambda i: (i, 0)),
                   pl.BlockSpec((dk, tm), lambda i: (0, i))],
        out_shape=[jax.ShapeDtypeStruct((n, dq), BF16),
                   jax.ShapeDtypeStruct((n, dk), BF16),
                   jax.ShapeDtypeStruct((dk, n), BF16)],
        compiler_params=_cparams(1),
    )(x, g.reshape(1, d), wq, wk, wvt, qg, kg, seg)


def _mm_res_router_kernel(a_ref, w_ref, x_ref, g_ref, wr_ref, x1_ref, hn_ref, aff_ref, *, a_transposed):
    if a_transposed:
        y = lax.dot_general(a_ref[...], w_ref[...], (((0,), (0,)), ((), ())), preferred_element_type=F32)
    else:
        y = _dot(a_ref[...], w_ref[...])
    x1 = x_ref[...] + y
    x1_ref[...] = x1
    hn = _rms(x1, g_ref[...])
    hn_ref[...] = hn
    logits = _dot_nt(wr_ref[...], hn.astype(BF16))
    mx = jnp.max(logits, axis=0, keepdims=True)
    p = jnp.exp(logits - mx)
    aff_ref[0] = p / jnp.sum(p, axis=0, keepdims=True)


def mm_res_router(a, w, x, g, wr_t, batch, a_transposed=False, tm=512):
    n, d = x.shape
    k = w.shape[0]
    e = wr_t.shape[0]
    t = n // batch
    tpb = t // tm
    a_spec = pl.BlockSpec((k, tm), lambda i: (0, i)) if a_transposed else pl.BlockSpec((tm, k), lambda i: (i, 0))
    return pl.pallas_call(
        functools.partial(_mm_res_router_kernel, a_transposed=a_transposed),
        grid=(n // tm,),
        in_specs=[a_spec,
                  pl.BlockSpec((k, d), lambda i: (0, 0)),
                  pl.BlockSpec((tm, d), lambda i: (i, 0)),
                  pl.BlockSpec((1, d), lambda i: (0, 0)),
                  pl.BlockSpec((e, d), lambda i: (0, 0))],
        out_specs=[pl.BlockSpec((tm, d), lambda i: (i, 0)),
                   pl.BlockSpec((tm, d), lambda i: (i, 0)),
                   pl.BlockSpec((1, e, tm), lambda i: (i // tpb, 0, i % tpb))],
        out_shape=[jax.ShapeDtypeStruct((n, d), F32),
                   jax.ShapeDtypeStruct((n, d), F32),
                   jax.ShapeDtypeStruct((batch, e, t), F32)],
        compiler_params=_cparams(1),
    )(a, w, x, g.reshape(1, d), wr_t)


def _t5_bucket(rel):
    nb = NUM_BUCKETS // 2
    ret = (rel > 0).astype(jnp.int32) * nb
    n = jnp.abs(rel)
    max_exact = nb // 2
    nf = jnp.maximum(n, 1).astype(jnp.float32)
    large = max_exact + (jnp.log(nf / max_exact) / math.log(MAX_DISTANCE / max_exact)
                         * (nb - max_exact)).astype(jnp.int32)
    large = jnp.minimum(large, nb - 1)
    return ret + jnp.where(n < max_exact, n, large)


def _attn_bucket_table():
    kk = jnp.arange(3 * ATT_BLOCK)[:, None]
    qq = jnp.arange(ATT_BLOCK)[None, :]
    rel = kk - ATT_BLOCK - qq
    return jnp.where(jnp.abs(rel) <= ATT_BLOCK, _t5_bucket(rel), -1).astype(I32)


def _attn_kernel(sink_ref, rb_ref, q_ref, kp_ref, kc_ref, kn_ref, vp_ref, vc_ref, vn_ref, bucket_ref, ot_ref,
                 bias_sc, *, nb, n_kv):
    n = pl.program_id(1)
    blk = ATT_BLOCK
    dh = HEAD_DIM
    gw = GQA_GROUP * blk

    @pl.when(jnp.logical_and(pl.program_id(0) == 0, n == 0))
    def _bias_table():
        bk = bucket_ref[...]
        for hq in range(n_kv * GQA_GROUP):
            acc = jnp.full(bk.shape, NEG, F32)
            for k in range(NUM_BUCKETS):
                acc = jnp.where(bk == k, rb_ref[k, hq] * LOG2E, acc)
            bias_sc[hq // GQA_GROUP, :, (hq % GQA_GROUP) * blk:(hq % GQA_GROUP + 1) * blk] = acc

    keys = jnp.concatenate([kp_ref[...], kc_ref[...], kn_ref[...]], axis=0)
    vt = jnp.concatenate([vp_ref[...], vc_ref[...], vn_ref[...]], axis=1)
    kidx = lax.broadcasted_iota(I32, (3 * blk, gw), 0)
    valid = jnp.logical_and(jnp.logical_or(n > 0, kidx >= blk),
                            jnp.logical_or(n < nb - 1, kidx < 2 * blk))
    ones_rows = (lax.broadcasted_iota(I32, (16, 3 * blk), 0) == 0).astype(BF16)
    lane = lax.broadcasted_iota(I32, (1, gw), 1)
    groups = range(n_kv)
    heads = [[h * GQA_GROUP + g for g in range(GQA_GROUP)] for h in groups]
    sk = []
    for h in groups:
        row_sink = jnp.full((1, gw), sink_ref[heads[h][-1]] * LOG2E, F32)
        for g in reversed(range(GQA_GROUP - 1)):
            row_sink = jnp.where(lane < (g + 1) * blk, sink_ref[heads[h][g]] * LOG2E, row_sink)
        sk.append(row_sink)
    q = [jnp.concatenate([q_ref[:, hq * dh:(hq + 1) * dh] for hq in heads[h]], axis=0) for h in groups]
    vaug = [jnp.concatenate([vt[h * dh:(h + 1) * dh, :], ones_rows], axis=0) for h in groups]
    s = [jnp.where(valid, _dot_nt(keys[:, h * dh:(h + 1) * dh], q[h]) + bias_sc[h], NEG) for h in groups]
    m = [jnp.maximum(jnp.max(s[h], axis=0, keepdims=True), sk[h]) for h in groups]
    p = [jnp.exp2(s[h] - m[h]).astype(BF16) for h in groups]
    oa = [_dot(vaug[h], p[h]) for h in groups]
    o = [oa[h][:dh] / (oa[h][dh:dh + 1] + jnp.exp2(sk[h] - m[h])) for h in groups]
    for h in groups:
        for g, hq in enumerate(heads[h]):
            ot_ref[hq * dh:(hq + 1) * dh, :] = o[h][:, g * blk:(g + 1) * blk].astype(ot_ref.dtype)


def attention_core(q, k, vt, rel_bias, sink, batch, seq):
    n, dq = q.shape
    dk = k.shape[1]
    hq = rel_bias.shape[1]
    n_kv = hq // GQA_GROUP
    nb = seq // ATT_BLOCK
    blk = ATT_BLOCK
    prv = lambda b, i: b * nb + jnp.maximum(i - 1, 0)
    cur = lambda b, i: b * nb + i
    nxt = lambda b, i: b * nb + jnp.minimum(i + 1, nb - 1)
    return pl.pallas_call(
        functools.partial(_attn_kernel, nb=nb, n_kv=n_kv),
        grid=(batch, nb),
        in_specs=[pl.BlockSpec(memory_space=pltpu.SMEM),
                  pl.BlockSpec(memory_space=pltpu.SMEM),
                  pl.BlockSpec((blk, dq), lambda b, i: (cur(b, i), 0)),
                  pl.BlockSpec((blk, dk), lambda b, i: (prv(b, i), 0)),
                  pl.BlockSpec((blk, dk), lambda b, i: (cur(b, i), 0)),
                  pl.BlockSpec((blk, dk), lambda b, i: (nxt(b, i), 0)),
                  pl.BlockSpec((dk, blk), lambda b, i: (0, prv(b, i))),
                  pl.BlockSpec((dk, blk), lambda b, i: (0, cur(b, i))),
                  pl.BlockSpec((dk, blk), lambda b, i: (0, nxt(b, i))),
                  pl.BlockSpec((3 * blk, blk), lambda b, i: (0, 0))],
        out_specs=pl.BlockSpec((dq, blk), lambda b, i: (0, cur(b, i))),
        out_shape=jax.ShapeDtypeStruct((dq, n), BF16),
        scratch_shapes=[pltpu.VMEM((n_kv, 3 * blk, GQA_GROUP * blk), F32)],
        compiler_params=_cparams(2),
    )(sink.astype(F32), rel_bias.astype(F32), q, k, k, k, vt, vt, vt, _attn_bucket_table())


def _prefix_incl(x, tri):
    t = x.shape[1]
    xb = x.astype(BF16)
    local = [_dot(xb[:, c * LANE:(c + 1) * LANE], tri) for c in range(t // LANE)]
    outs = []
    carry = jnp.zeros((x.shape[0], 1), F32)
    for p in local:
        p = p + carry
        outs.append(p)
        carry = p[:, LANE - 1:LANE]
    return jnp.concatenate(outs, axis=1)


def _select_kernel(aff_ref, cmp_ref, offc_ref, rng_ref, pos_sc, dest_sc, vals_sc, *, cap, rchunk):
    aff = aff_ref[0]
    ne, t = aff.shape
    bits = pltpu.bitcast(aff, I32)

    def search(i, lo):
        cand = lo | lax.shift_left(jnp.int32(1), 30 - i)
        cnt = jnp.sum((bits >= cand).astype(I32), axis=1, keepdims=True)
        return jnp.where(cnt >= cap, cand, lo)

    thr = lax.fori_loop(0, 31, search, jnp.zeros((ne, 1), I32))
    gt = bits > thr
    eq = bits == thr
    need = (cap - jnp.sum(gt.astype(I32), axis=1, keepdims=True)).astype(F32)
    ri = lax.broadcasted_iota(I32, (LANE, LANE), 0)
    ci = lax.broadcasted_iota(I32, (LANE, LANE), 1)
    tri = (ri <= ci).astype(BF16)
    eqf = eq.astype(F32)
    rank_eq = _prefix_incl(eqf, tri) - eqf
    sel = jnp.logical_or(gt, jnp.logical_and(eq, rank_eq < need))
    self_ = sel.astype(F32)
    kt = jnp.sum(self_, axis=0, keepdims=True)
    pre = _prefix_incl(jnp.concatenate([self_, jnp.broadcast_to(kt, (8, t))], axis=0), tri)
    pos_sc[...] = jnp.where(sel, pre[:ne] - 1.0, -1.0)
    end = pre[ne:ne + 1]
    off = end - kt
    lr = lax.broadcasted_iota(I32, (ne, ne), 0)
    lc = lax.broadcasted_iota(I32, (ne, ne), 1)
    jexp = _dot((lc < lr).astype(BF16), self_.astype(BF16))
    dest_sc[...] = off + jexp

    tok = lax.broadcasted_iota(I32, (1, t), 1)
    vals_sc[0:1, :] = (tok >> 6).astype(F32)
    vals_sc[1:2, :] = (tok & 63).astype(F32)
    vals_sc[7:8, :] = jnp.zeros((1, t), F32)
    slot = lax.broadcasted_iota(I32, (cap, 1), 0).astype(F32)

    def compact(e, carry):
        d = dest_sc[pl.ds(e, 1), :]
        dh = jnp.floor(d * (1.0 / LANE))
        g = aff_ref[0, pl.ds(e, 1), :]
        g0 = g.astype(BF16).astype(F32)
        g1 = (g - g0).astype(BF16).astype(F32)
        vals_sc[2:3, :] = dh
        vals_sc[3:4, :] = d - dh * LANE
        vals_sc[4:5, :] = g0
        vals_sc[5:6, :] = g1
        vals_sc[6:7, :] = g - g0 - g1
        onehot = (pos_sc[pl.ds(e, 1), :] == slot).astype(BF16)
        cmp_ref[0, e] = _dot_nt(vals_sc[...].astype(BF16), onehot)
        return carry

    lax.fori_loop(0, ne, compact, 0)

    eh = jnp.floor(end * (1.0 / LANE))
    oh = jnp.floor(off * (1.0 / LANE))
    v4 = jnp.concatenate([oh, off - oh * LANE, eh, end - eh * LANE, jnp.zeros((4, t), F32)], axis=0)
    eye = (ri == ci).astype(BF16)
    for c in range(t // LANE):
        offc_ref[0, c * LANE:(c + 1) * LANE, :] = _dot_nt(eye, v4[:, c * LANE:(c + 1) * LANE].astype(BF16))

    nchunk = (ne * cap) // rchunk
    low = (lax.broadcasted_iota(I32, (nchunk, 1), 0) * rchunk).astype(F32)
    first = jnp.sum((end <= low).astype(F32), axis=1, keepdims=True)
    last = jnp.sum((end <= low + (rchunk - 1)).astype(F32), axis=1, keepdims=True)
    lane = lax.broadcasted_iota(I32, (nchunk, LANE), 1)
    tiles = jnp.where(lane < LANE // 2, jnp.floor(first * (1.0 / LANE)), jnp.floor(last * (1.0 / LANE)))
    rng_ref[0] = tiles.astype(I32)


COMBINE_ROWS = 256


def route_select(aff, cap, rchunk):
    batch, ne, t = aff.shape
    nchunk = ne * cap // rchunk
    return pl.pallas_call(
        functools.partial(_select_kernel, cap=cap, rchunk=rchunk),
        grid=(batch,),
        in_specs=[pl.BlockSpec((1, ne, t), lambda b: (b, 0, 0))],
        out_specs=[pl.BlockSpec((1, ne, 8, cap), lambda b: (b, 0, 0, 0)),
                   pl.BlockSpec((1, t, 8), lambda b: (b, 0, 0)),
                   pl.BlockSpec((1, nchunk, LANE), lambda b: (b, 0, 0))],
        out_shape=[jax.ShapeDtypeStruct((batch, ne, 8, cap), F32),
                   jax.ShapeDtypeStruct((batch, t, 8), F32),
                   jax.ShapeDtypeStruct((batch, nchunk, LANE), I32)],
        scratch_shapes=[pltpu.VMEM((ne, t), F32), pltpu.VMEM((ne, t), F32), pltpu.VMEM((8, t), F32)],
        compiler_params=_cparams(1),
    )(aff)


def _ffn_kernel(idx_sm, dest_sm, hn_hbm, cmp_ref, w1_ref, w3_ref, w2_ref, r_hbm,
                xs32, xsb, yacc, ysc, wb1, wb3, wb2, gsem, ssem, *, batch, cap, nf, rt):
    e = pl.program_id(0)
    f = pl.program_id(1)
    ne = pl.num_programs(0)
    rows = batch * cap
    sub = rows // nf

    def gather_start(block, r):
        t = idx_sm[block * rows + r]
        pltpu.make_async_copy(hn_hbm.at[pl.ds(t, 1), :], xs32.at[pl.ds(r, 1), :], gsem).start()

    def gather_wait():
        pltpu.make_async_copy(hn_hbm.at[pl.ds(0, rows), :], xs32, gsem).wait()

    def scatter_start(block, r):
        d = dest_sm[block * rows + r]
        pltpu.make_async_copy(ysc.at[pl.ds(r, 1), :], r_hbm.at[pl.ds(d, 1), :], ssem).start()

    def scatter_wait():
        pltpu.make_async_copy(ysc, r_hbm.at[pl.ds(0, rows), :], ssem).wait()

    @pl.when(jnp.logical_and(e == 0, f == 0))
    def _prologue():
        def issue(r, c):
            gather_start(0, r)
            return c
        lax.fori_loop(0, rows, issue, 0)
        ysc[...] = jnp.zeros_like(ysc)

    @pl.when(f == 0)
    def _rows_ready():
        gather_wait()
        xsb[...] = xs32[...].astype(BF16)
        yacc[...] = jnp.zeros_like(yacc)

    for r in range(sub):
        gather_start(e + 1, f * sub + r)
        scatter_start(e, f * sub + r)

    wb1[...] = w1_ref[0, 0].astype(BF16)
    wb3[...] = w3_ref[0, 0].astype(BF16)
    wb2[...] = w2_ref[0, 0].astype(BF16)
    for r in range(rows // rt):
        xs = xsb[r * rt:(r + 1) * rt, :]
        a = _dot(xs, wb1[...])
        u = _dot(xs, wb3[...])
        hmid = (a * jax.nn.sigmoid(a) * u).astype(BF16)
        yacc[r * rt:(r + 1) * rt, :] += _dot(hmid, wb2[...])

    @pl.when(f == nf - 1)
    def _finish():
        scatter_wait()
        ri = lax.broadcasted_iota(I32, (cap, cap), 0)
        ci = lax.broadcasted_iota(I32, (cap, cap), 1)
        eye = (ri == ci).astype(BF16)
        for b in range(batch):
            gt = _dot_nt(eye, cmp_ref[b, 0].astype(BF16))
            g = gt[:, 4:5] + gt[:, 5:6] + gt[:, 6:7]
            y = yacc[b * cap:(b + 1) * cap, :] * g
            half = y.shape[1] // 2
            hi = pltpu.bitcast(y[:, :half].astype(BF16).astype(F32), U32)
            lo = pltpu.bitcast(y[:, half:].astype(BF16).astype(F32), U32)
            ysc[b * cap:(b + 1) * cap, :] = hi | lax.shift_right_logical(lo, jnp.uint32(16))

        @pl.when(e == ne - 1)
        def _epilogue():
            def issue(r, c):
                scatter_start(ne, r)
                return c
            lax.fori_loop(0, rows, issue, 0)
            scatter_wait()
            gather_wait()


def expert_ffn(hn, cmp, idx_flat, dest_flat, w1, w3, w2, layer, cap, tf=512, rt=512):
    n, d = hn.shape
    batch, ne = cmp.shape[0], cmp.shape[1]
    dff = w1.shape[3]
    nf = dff // tf
    rows = batch * cap
    rt = min(rt, rows)
    grid_spec = pltpu.PrefetchScalarGridSpec(
        num_scalar_prefetch=2,
        grid=(ne, nf),
        in_specs=[pl.BlockSpec(memory_space=pl.ANY),
                  pl.BlockSpec((batch, 1, 8, cap), lambda e, f, *_: (0, e, 0, 0)),
                  pl.BlockSpec((1, 1, d, tf), lambda e, f, *_: (layer, e, 0, f)),
                  pl.BlockSpec((1, 1, d, tf), lambda e, f, *_: (layer, e, 0, f)),
                  pl.BlockSpec((1, 1, tf, d), lambda e, f, *_: (layer, e, f, 0))],
        out_specs=pl.BlockSpec(memory_space=pl.ANY),
        scratch_shapes=[pltpu.VMEM((rows, d), F32), pltpu.VMEM((rows, d), BF16),
                        pltpu.VMEM((rows, d), F32), pltpu.VMEM((rows, d // 2), U32),
                        pltpu.VMEM((d, tf), BF16), pltpu.VMEM((d, tf), BF16), pltpu.VMEM((tf, d), BF16),
                        pltpu.SemaphoreType.DMA, pltpu.SemaphoreType.DMA])
    return pl.pallas_call(
        functools.partial(_ffn_kernel, batch=batch, cap=cap, nf=nf, rt=rt),
        grid_spec=grid_spec,
        out_shape=jax.ShapeDtypeStruct((batch * ne * cap + rows, d // 2), U32),
        compiler_params=_cparams(2),
    )(idx_flat, dest_flat, hn, cmp, w1, w3, w2)


def _combine_kernel(tlo_sm, thi_sm, r_ref, offc_ref, x_hbm, o_ref, sem, *, nchunk, rchunk):
    b = pl.program_id(0)
    j = pl.program_id(1)
    ntile = o_ref.shape[1] // LANE

    @pl.when(j == 0)
    def _init():
        cp = pltpu.make_async_copy(x_hbm.at[b], o_ref.at[0], sem)
        cp.start()
        cp.wait()

    r = r_ref[...]
    half = r.shape[1]
    left = pltpu.bitcast(r & jnp.uint32(0xFFFF0000), F32).astype(BF16)
    right = pltpu.bitcast(lax.shift_left(r, jnp.uint32(16)), F32).astype(BF16)
    rowid = (j * rchunk + lax.broadcasted_iota(I32, (1, rchunk), 1)).astype(F32)

    def contrib(i, valid):
        t0 = pl.multiple_of(i * LANE, LANE)
        oc = offc_ref[0, pl.ds(t0, LANE), :]
        off = oc[:, 0:1] * LANE + oc[:, 1:2]
        end = oc[:, 2:3] * LANE + oc[:, 3:4]
        p = jnp.logical_and(jnp.logical_and(rowid >= off, rowid < end), valid).astype(BF16)
        return t0, (_dot(p, left), _dot(p, right))

    def add(t0, y):
        o_ref[0, pl.ds(t0, LANE), :half] += y[0]
        o_ref[0, pl.ds(t0, LANE), half:] += y[1]

    def tile(i, c):
        add(*contrib(i, True))
        return c

    tlo = tlo_sm[b * nchunk + j]
    thi = thi_sm[b * nchunk + j]
    ta, ya = contrib(tlo, True)
    tb, yb = contrib(jnp.minimum(tlo + 1, ntile - 1), tlo + 1 <= thi)
    add(ta, ya)
    add(tb, yb)
    lax.fori_loop(tlo + 2, thi + 1, tile, 0)


def combine(rbuf, offc, tlo, thi, x, rows_per_batch, rchunk):
    batch, t, d = x.shape
    nchunk = rows_per_batch // rchunk
    grid_spec = pltpu.PrefetchScalarGridSpec(
        num_scalar_prefetch=2,
        grid=(batch, nchunk),
        in_specs=[pl.BlockSpec((rchunk, d // 2), lambda b, j, *_: (b * nchunk + j, 0)),
                  pl.BlockSpec((1, t, 8), lambda b, j, *_: (b, 0, 0)),
                  pl.BlockSpec(memory_space=pl.ANY)],
        out_specs=pl.BlockSpec((1, t, d), lambda b, j, *_: (b, 0, 0)),
        scratch_shapes=[pltpu.SemaphoreType.DMA])
    return pl.pallas_call(
        functools.partial(_combine_kernel, nchunk=nchunk, rchunk=rchunk),
        grid_spec=grid_spec,
        out_shape=jax.ShapeDtypeStruct((batch, t, d), F32),
        compiler_params=_cparams(2),
    )(tlo, thi, rbuf, offc, x)


def moe_block(x1, hn, aff, w1, w3, w2, layer, batch):
    n, d = x1.shape
    t = n // batch
    ne = aff.shape[1]
    cap = CAPACITY_FACTOR * t // ne
    rows = batch * cap
    cmp, offc, rng = route_select(aff, cap, COMBINE_ROWS)
    ci = cmp.astype(I32)
    boff = jnp.arange(batch, dtype=I32)[:, None, None]
    idx = (ci[:, :, 0] * 64 + ci[:, :, 1] + boff * t).transpose(1, 0, 2).reshape(-1)
    dest = (ci[:, :, 2] * LANE + ci[:, :, 3] + boff * (ne * cap)).transpose(1, 0, 2).reshape(-1)
    idx = jnp.concatenate([idx, idx[:rows]])
    dest = jnp.concatenate([batch * ne * cap + jnp.arange(rows, dtype=I32), dest])
    rbuf = expert_ffn(hn, cmp, idx, dest, w1, w3, w2, layer, cap)
    tlo = rng[:, :, 0].reshape(-1)
    thi = rng[:, :, LANE - 1].reshape(-1)
    return combine(rbuf, offc, tlo, thi, x1.reshape(batch, t, d), ne * cap, COMBINE_ROWS)


MLSTM_L = 256


def _log_sigmoid(x):
    return jnp.minimum(x, 0.0) - jnp.log1p(jnp.exp(-jnp.abs(x)))


def _split3(x):
    x0 = x.astype(BF16)
    r = x - x0.astype(F32)
    x1 = r.astype(BF16)
    return x0, x1, (r - x1.astype(F32)).astype(BF16)


def _mlstm_proj_kernel(x_ref, g_ref, wm_ref, wkt_ref, wgt_ref, bt_ref, main_ref, kt_ref, gt_ref, *, nsub, nh):
    ln = MLSTM_L
    h = _rms(x_ref[...], g_ref[...]).astype(BF16)
    main_ref[...] = _dot(h, wm_ref[...]).astype(main_ref.dtype)
    kt = _dot_nt(wkt_ref[...], h).astype(kt_ref.dtype)
    pre = _dot_nt(wgt_ref[...], h) + bt_ref[...]
    lst = _log_sigmoid(pre)
    ri = lax.broadcasted_iota(I32, (ln, ln), 0)
    ci = lax.broadcasted_iota(I32, (ln, ln), 1)
    low = (ci <= ri).astype(BF16)
    upp = (ci >= ri).astype(BF16)
    row = lax.broadcasted_iota(I32, (pre.shape[0], ln), 0)
    fwd = jnp.logical_and(row >= nh, row < 2 * nh)
    bwd = row >= 3 * nh
    for j in range(nsub):
        sl = slice(j * ln, (j + 1) * ln)
        kt_ref[j] = kt[:, sl]
        pt = _split3(lst[:, sl])
        gt_ref[j] = jnp.where(fwd, sum(_dot(x, upp) for x in pt),
                              jnp.where(bwd, sum(_dot(x, low) for x in pt), pre[:, sl]))


def mlstm_project(x, g, wm, wkt, wgt, bias_t, nh, tm=512):
    n, d = x.shape
    nsub = tm // MLSTM_L
    nck = n // MLSTM_L
    const = lambda i: (0, 0)
    return pl.pallas_call(
        functools.partial(_mlstm_proj_kernel, nsub=nsub, nh=nh),
        grid=(n // tm,),
        in_specs=[pl.BlockSpec((tm, d), lambda i: (i, 0)),
                  pl.BlockSpec((1, d), const),
                  pl.BlockSpec(wm.shape, const),
                  pl.BlockSpec(wkt.shape, const),
                  pl.BlockSpec(wgt.shape, const),
                  pl.BlockSpec(bias_t.shape, const)],
        out_specs=[pl.BlockSpec((tm, wm.shape[1]), lambda i: (i, 0)),
                   pl.BlockSpec((nsub, wkt.shape[0], MLSTM_L), lambda i: (i, 0, 0)),
                   pl.BlockSpec((nsub, wgt.shape[0], MLSTM_L), lambda i: (i, 0, 0))],
        out_shape=[jax.ShapeDtypeStruct((n, wm.shape[1]), BF16),
                   jax.ShapeDtypeStruct((nck, wkt.shape[0], MLSTM_L), BF16),
                   jax.ShapeDtypeStruct((nck, wgt.shape[0], MLSTM_L), F32)],
        compiler_params=_cparams(1),
    )(x, g.reshape(1, d), wm, wkt, wgt, bias_t)


def _mlstm_kernel(q_ref, kt_ref, v_ref, og_ref, gt_ref, ng_ref, y_ref, hf, hb, cst, b_sc, u_sc, cm_sc,
                  *, nc, nh, dqk, dv):
    p = pl.program_id(1)
    ln = MLSTM_L
    ri = lax.broadcasted_iota(I32, (ln, ln), 0)
    ci = lax.broadcasted_iota(I32, (ln, ln), 1)
    masks = (ci <= ri, ci >= ri)
    ones_blk = jnp.ones((ln, dv), BF16)
    lane8 = lax.broadcasted_iota(I32, (8, ln), 1)
    row8 = lax.broadcasted_iota(I32, (8, ln), 0)
    bwd_row = (row8 % 2) == 1
    edge = lane8 == jnp.where(bwd_row, 0, ln - 1)
    kk = lax.broadcasted_iota(I32, (48, 2 * dv), 0) % 16
    cc = lax.broadcasted_iota(I32, (48, 2 * dv), 1)
    sel = [jnp.logical_or(jnp.logical_and(kk == i, cc < dv), jnp.logical_and(kk == 4 + i, cc >= dv)).astype(BF16)
           for i in range(4)]
    zero4 = jnp.zeros((4, ln), F32)
    cst[...] = jnp.zeros_like(cst)
    chains = [(hh, d) for hh in range(2) for d in range(2)]

    def gate_rows(c, which):
        return [gt_ref[c if d == 0 else nc - 1 - c, pl.ds((2 * d + which) * nh + 2 * p + hh, 1), :]
                for hh, d in chains] + [zero4]
    b_all = jnp.concatenate([x for c in range(nc) for x in gate_rows(c, 1)], axis=0)
    u_all = jnp.concatenate([x for c in range(nc) for x in gate_rows(c, 0)], axis=0) - b_all
    lane_a = lax.broadcasted_iota(I32, u_all.shape, 1)
    bwd_a = (lax.broadcasted_iota(I32, u_all.shape, 0) % 2) == 1
    cf = cb = u_all
    sh = 1
    while sh < ln:
        cf = jnp.maximum(cf, jnp.where(lane_a >= sh, pltpu.roll(cf, sh, axis=1), NEG))
        cb = jnp.maximum(cb, jnp.where(lane_a < ln - sh, pltpu.roll(cb, ln - sh, axis=1), NEG))
        sh *= 2
    b_sc[...] = b_all.reshape(nc, 8, ln)
    u_sc[...] = u_all.reshape(nc, 8, ln)
    cm_sc[...] = jnp.where(bwd_a, cb, cf).reshape(nc, 8, ln)

    def step(c, ms):
        cks = [c if d == 0 else nc - 1 - c for _, d in chains]
        r0s = [pl.multiple_of(ck * ln, ln) for ck in cks]
        cprev = [cst[i] for i in range(4)]
        q = [q_ref[pl.ds(r0s[i], ln), hh * dqk:(hh + 1) * dqk] for i, (hh, _) in enumerate(chains)]
        kt = [kt_ref[cks[i], hh * dqk:(hh + 1) * dqk, :] for i, (hh, _) in enumerate(chains)]
        vaug = [jnp.concatenate([v_ref[pl.ds(r0s[i], ln), hh * dv:(hh + 1) * dv], ones_blk], axis=1)
                for i, (hh, _) in enumerate(chains)]
        s = [_dot(q[i], kt[i]) for i in range(4)]
        inter = [_dot(q[i], cprev[i].astype(BF16)) for i in range(4)]
        b = b_sc[c]
        u = u_sc[c]
        m_run = jnp.maximum(ms, cm_sc[c])
        m_end = jnp.max(jnp.where(edge, m_run, NEG), axis=1, keepdims=True)
        b_end = jnp.sum(jnp.where(edge, b, 0.0), axis=1, keepdims=True)
        wc = jnp.exp(u - m_end)
        decay = jnp.exp(ms - m_end)
        rows = jnp.concatenate([m_run[:4] * LOG2E, jnp.exp(-(b + m_run))[:4], zero4, zero4], axis=0)
        stack = jnp.concatenate(_split3(rows), axis=0)
        u2 = u * LOG2E
        ms2 = ms * LOG2E
        bc = [lax.dot_general(stack, sel[i], (((0,), (0,)), ((), ())), preferred_element_type=F32)
              for i in range(4)]
        mb = [jnp.concatenate([bc[i][:, :dv]] * (ln // dv), axis=1) for i in range(4)]
        sw = [(jnp.exp2(jnp.where(masks[d], u2[i:i + 1, :] - mb[i], NEG)) * s[i]).astype(BF16)
              for i, (_, d) in enumerate(chains)]
        intra = [_dot(sw[i], vaug[i]) for i in range(4)]
        ea = [jnp.exp2(ms2[i:i + 1, :] - bc[i][:, :dv]) for i in range(4)]
        num = [ea[i] * inter[i][:, :dv] + intra[i][:, :dv] for i in range(4)]
        den = [ea[i] * inter[i][:, dv:] + intra[i][:, dv:] for i in range(4)]
        hc = [num[i] / jnp.maximum(jnp.abs(den[i]), bc[i][:, dv:]) for i in range(4)]
        kw = [(kt[i].astype(F32) * wc[i:i + 1, :]).astype(BF16) for i in range(4)]
        c_new = [decay[i:i + 1, :] * cprev[i] + _dot(kw[i], vaug[i]) for i in range(4)]
        for i, (hh, d) in enumerate(chains):
            cst[i] = c_new[i]
            (hf if d == 0 else hb)[pl.ds(r0s[i], ln), hh * dv:(hh + 1) * dv] = hc[i]
        return b_end + m_end

    lax.fori_loop(0, nc, step, jnp.zeros((8, 1), F32))
    for hh in range(2):
        sl = slice(hh * dv, (hh + 1) * dv)
        hs = _rms(hf[:, sl] + hb[:, sl], ng_ref[:, sl])
        y_ref[:, sl] = (hs * jax.nn.sigmoid(og_ref[:, sl].astype(F32))).astype(y_ref.dtype)


def mlstm_core(main, ktc, grow, out_g, batch, seq, nh, dqk, dv):
    n = main.shape[0]
    nc = seq // MLSTM_L
    npair = nh // 2
    vblocks = (nh * dqk) // (2 * dv)
    ogblocks = (nh * dqk + nh * dv) // (2 * dv)
    return pl.pallas_call(
        functools.partial(_mlstm_kernel, nc=nc, nh=nh, dqk=dqk, dv=dv),
        grid=(batch, npair),
        in_specs=[pl.BlockSpec((seq, 2 * dqk), lambda b, p: (b, p)),
                  pl.BlockSpec((nc, 2 * dqk, MLSTM_L), lambda b, p: (b, p, 0)),
                  pl.BlockSpec((seq, 2 * dv), lambda b, p: (b, vblocks + p)),
                  pl.BlockSpec((seq, 2 * dv), lambda b, p: (b, ogblocks + p)),
                  pl.BlockSpec((nc, 4 * nh, MLSTM_L), lambda b, p: (b, 0, 0)),
                  pl.BlockSpec((1, 2 * dv), lambda b, p: (0, p))],
        out_specs=pl.BlockSpec((seq, 2 * dv), lambda b, p: (b, p)),
        out_shape=jax.ShapeDtypeStruct((n, nh * dv), BF16),
        scratch_shapes=[pltpu.VMEM((seq, 2 * dv), F32), pltpu.VMEM((seq, 2 * dv), F32),
                        pltpu.VMEM((4, dqk, 2 * dv), F32)] + [pltpu.VMEM((nc, 8, MLSTM_L), F32)] * 3,
        compiler_params=_cparams(2),
    )(main, ktc, main, main, grow, out_g.reshape(1, nh * dv).astype(F32))


def mlstm_layer(x2d, batch, seq, norm_g, w_in, b_i, b_f, out_g, w_out, ffn_g, wr):
    d = x2d.shape[1]
    nh = MLSTM_HEADS
    dv = d // nh
    dqk = dv // 2
    o1, o2, o3, o4 = nh * dqk, 2 * nh * dqk, 2 * nh * dqk + nh * dv, 2 * nh * dqk + 2 * nh * dv
    wm = jnp.concatenate([w_in[:, :o1], w_in[:, o2:o4]], axis=1).astype(BF16)
    wkt = (w_in[:, o1:o2] * (dqk ** -0.5)).T.astype(BF16)
    wgt = w_in[:, o4:].T.astype(BF16)
    bias_t = jnp.concatenate([b_i[0], b_f[0], b_i[1], b_f[1]]).reshape(4 * nh, 1).astype(F32)
    main, ktc, grow = mlstm_project(x2d, norm_g, wm, wkt, wgt, bias_t, nh)
    y = mlstm_core(main, ktc, grow, out_g, batch, seq, nh, dqk, dv)
    return mm_res_router(y, w_out.astype(BF16), x2d, ffn_g, wr.T.astype(BF16), batch)


def attention_layer(x2d, batch, seq, norm_g, w_in, q_g, k_g, sink, w_out, rel_bias, ffn_g, wr):
    n_q = rel_bias.shape[1]
    q, k, vt = attention_project(x2d, norm_g, w_in, q_g, k_g, n_q, n_q // GQA_GROUP)
    ot = attention_core(q, k, vt, rel_bias, sink, batch, seq)
    return mm_res_router(ot, w_out.astype(BF16), x2d, ffn_g, wr.T.astype(BF16), batch, a_transposed=True)


def kernel(x, rel_bias, attn_norm_g, attn_w_in, attn_q_norm_g, attn_k_norm_g, attn_sink, attn_w_out, mlstm_norm_g, mlstm_w_in, mlstm_b_i, mlstm_b_f, mlstm_out_norm_g, mlstm_w_out, ffn_norm_g, router_w, expert_w1, expert_w3, expert_w2):
    batch, seq, d = x.shape
    x2d = x.reshape(batch * seq, d)
    x1, hn, aff = attention_layer(x2d, batch, seq, attn_norm_g[0], attn_w_in[0], attn_q_norm_g[0],
                                  attn_k_norm_g[0], attn_sink[0], attn_w_out[0], rel_bias,
                                  ffn_norm_g[0], router_w[0])
    x = moe_block(x1, hn, aff, expert_w1, expert_w3, expert_w2, 0, batch)
    x1, hn, aff = mlstm_layer(x.reshape(batch * seq, d), batch, seq, mlstm_norm_g[0], mlstm_w_in[0], mlstm_b_i[0],
                              mlstm_b_f[0], mlstm_out_norm_g[0], mlstm_w_out[0], ffn_norm_g[1], router_w[1])
    return moe_block(x1, hn, aff, expert_w1, expert_w3, expert_w2, 1, batch)
```

```python
import functools
import math

import jax
import jax.numpy as jnp
from jax import lax
from jax.experimental import pallas as pl
from jax.experimental.pallas import tpu as pltpu

F32 = jnp.float32
BF16 = jnp.bfloat16
I32 = jnp.int32
U32 = jnp.uint32

RMS_EPS = 1e-6
NEG = -1e30
LOG2E = 1.4426950408889634
LANE = 128
VMEM_LIMIT = 56 * 1024 * 1024

HEAD_DIM = 64
GQA_GROUP = 4
ATT_BLOCK = 128
NUM_BUCKETS = 32
MAX_DISTANCE = 128
N_EXPERTS = 16
CAPACITY_FACTOR = 2
MLSTM_HEADS = 8


def _cparams(n_axes, vmem=VMEM_LIMIT):
    return pltpu.CompilerParams(dimension_semantics=("arbitrary",) * n_axes, vmem_limit_bytes=vmem)


def _rms(x, g):
    return x * lax.rsqrt(jnp.mean(x * x, axis=-1, keepdims=True) + RMS_EPS) * g


def _dot(a, b):
    return jnp.dot(a, b, preferred_element_type=F32)


def _dot_nt(a, b):
    return lax.dot_general(a, b, (((1,), (1,)), ((), ())), preferred_element_type=F32)


def _attn_proj_kernel(x_ref, g_ref, wq_ref, wk_ref, wvt_ref, qg_ref, kg_ref, seg_ref, q_ref, k_ref, vt_ref):
    h = _rms(x_ref[...], g_ref[...]).astype(BF16)
    seg = seg_ref[...]
    w = seg.shape[0]

    def head_norm(t, gain_ref, out_ref):
        for j in range(t.shape[1] // w):
            tj = t[:, j * w:(j + 1) * w]
            ms = _dot((tj * tj).astype(BF16), seg)
            out_ref[:, j * w:(j + 1) * w] = (tj * lax.rsqrt(ms + RMS_EPS) * gain_ref[:, j * w:(j + 1) * w]
                                             ).astype(out_ref.dtype)

    head_norm(_dot(h, wq_ref[...]), qg_ref, q_ref)
    head_norm(_dot(h, wk_ref[...]), kg_ref, k_ref)
    vt_ref[...] = _dot_nt(wvt_ref[...], h).astype(vt_ref.dtype)


def attention_project(x, g, w_in, q_g, k_g, n_q, n_kv, tm=512):
    n, d = x.shape
    dh = HEAD_DIM
    dq, dk = n_q * dh, n_kv * dh
    wq = w_in[:, :dq].astype(BF16)
    wk = w_in[:, dq:dq + dk].astype(BF16)
    wvt = w_in[:, dq + dk:].T.astype(BF16)
    qg = jnp.tile(q_g.astype(F32) * (dh ** -0.5 * LOG2E), n_q).reshape(1, dq)
    kg = jnp.tile(k_g.astype(F32), n_kv).reshape(1, dk)
    hid = jnp.arange(dk) // dh
    seg = jnp.where(hid[:, None] == hid[None, :], 1.0 / dh, 0.0).astype(BF16)
    const2 = lambda i: (0, 0)
    return pl.pallas_call(
        _attn_proj_kernel,
        grid=(n // tm,),
        in_specs=[pl.BlockSpec((tm, d), lambda i: (i, 0)),
                  pl.BlockSpec((1, d), const2),
                  pl.BlockSpec((d, dq), const2),
                  pl.BlockSpec((d, dk), const2),
                  pl.BlockSpec((dk, d), const2),
                  pl.BlockSpec((1, dq), const2),
                  pl.BlockSpec((1, dk), const2),
                  pl.BlockSpec((dk, dk), const2)],
        out_specs=[pl.BlockSpec((tm, dq), lambda i: (i, 0)),
                   pl.BlockSpec((tm, dk), lambda i: (i, 0)),
                   pl.BlockSpec((dk, tm), lambda i: (0, i))],
        out_shape=[jax.ShapeDtypeStruct((n, dq), BF16),
                   jax.ShapeDtypeStruct((n, dk), BF16),
                   jax.ShapeDtypeStruct((dk, n), BF16)],
        compiler_params=_cparams(1),
    )(x, g.reshape(1, d), wq, wk, wvt, qg, kg, seg)


def _mm_res_router_kernel(a_ref, w_ref, x_ref, g_ref, wr_ref, x1_ref, hn_ref, aff_ref, *, a_transposed):
    if a_transposed:
        y = lax.dot_general(a_ref[...], w_ref[...], (((0,), (0,)), ((), ())), preferred_element_type=F32)
    else:
        y = _dot(a_ref[...], w_ref[...])
    x1 = x_ref[...] + y
    x1_ref[...] = x1
    hn = _rms(x1, g_ref[...])
    hn_ref[...] = hn
    logits = _dot_nt(wr_ref[...], hn.astype(BF16))
    mx = jnp.max(logits, axis=0, keepdims=True)
    p = jnp.exp(logits - mx)
    aff_ref[0] = p / jnp.sum(p, axis=0, keepdims=True)


def mm_res_router(a, w, x, g, wr_t, batch, a_transposed=False, tm=512):
    n, d = x.shape
    k = w.shape[0]
    e = wr_t.shape[0]
    t = n // batch
    tpb = t // tm
    a_spec = pl.BlockSpec((k, tm), lambda i: (0, i)) if a_transposed else pl.BlockSpec((tm, k), lambda i: (i, 0))
    return pl.pallas_call(
        functools.partial(_mm_res_router_kernel, a_transposed=a_transposed),
        grid=(n // tm,),
        in_specs=[a_spec,
                  pl.BlockSpec((k, d), lambda i: (0, 0)),
                  pl.BlockSpec((tm, d), lambda i: (i, 0)),
                  pl.BlockSpec((1, d), lambda i: (0, 0)),
                  pl.BlockSpec((e, d), lambda i: (0, 0))],
        out_specs=[pl.BlockSpec((tm, d), lambda i: (i, 0)),
                   pl.BlockSpec((tm, d), lambda i: (i, 0)),
                   pl.BlockSpec((1, e, tm), lambda i: (i // tpb, 0, i % tpb))],
        out_shape=[jax.ShapeDtypeStruct((n, d), F32),
                   jax.ShapeDtypeStruct((n, d), F32),
                   jax.ShapeDtypeStruct((batch, e, t), F32)],
        compiler_params=_cparams(1),
    )(a, w, x, g.reshape(1, d), wr_t)


def _t5_bucket(rel):
    nb = NUM_BUCKETS // 2
    ret = (rel > 0).astype(jnp.int32) * nb
    n = jnp.abs(rel)
    max_exact = nb // 2
    nf = jnp.maximum(n, 1).astype(jnp.float32)
    large = max_exact + (jnp.log(nf / max_exact) / math.log(MAX_DISTANCE / max_exact)
                         * (nb - max_exact)).astype(jnp.int32)
    large = jnp.minimum(large, nb - 1)
    return ret + jnp.where(n < max_exact, n, large)


def _attn_bucket_table():
    kk = jnp.arange(3 * ATT_BLOCK)[:, None]
    qq = jnp.arange(ATT_BLOCK)[None, :]
    rel = kk - ATT_BLOCK - qq
    return jnp.where(jnp.abs(rel) <= ATT_BLOCK, _t5_bucket(rel), -1).astype(I32)


def _attn_kernel(sink_ref, rb_ref, q_ref, kp_ref, kc_ref, kn_ref, vp_ref, vc_ref, vn_ref, bucket_ref, ot_ref,
                 bias_sc, *, nb, n_kv):
    n = pl.program_id(1)
    blk = ATT_BLOCK
    dh = HEAD_DIM
    gw = GQA_GROUP * blk

    @pl.when(jnp.logical_and(pl.program_id(0) == 0, n == 0))
    def _bias_table():
        bk = bucket_ref[...]
        for hq in range(n_kv * GQA_GROUP):
            acc = jnp.full(bk.shape, NEG, F32)
            for k in range(NUM_BUCKETS):
                acc = jnp.where(bk == k, rb_ref[k, hq] * LOG2E, acc)
            bias_sc[hq // GQA_GROUP, :, (hq % GQA_GROUP) * blk:(hq % GQA_GROUP + 1) * blk] = acc

    keys = jnp.concatenate([kp_ref[...], kc_ref[...], kn_ref[...]], axis=0)
    vt = jnp.concatenate([vp_ref[...], vc_ref[...], vn_ref[...]], axis=1)
    kidx = lax.broadcasted_iota(I32, (3 * blk, gw), 0)
    valid = jnp.logical_and(jnp.logical_or(n > 0, kidx >= blk),
                            jnp.logical_or(n < nb - 1, kidx < 2 * blk))
    ones_rows = (lax.broadcasted_iota(I32, (16, 3 * blk), 0) == 0).astype(BF16)
    lane = lax.broadcasted_iota(I32, (1, gw), 1)
    groups = range(n_kv)
    heads = [[h * GQA_GROUP + g for g in range(GQA_GROUP)] for h in groups]
    sk = []
    for h in groups:
        row_sink = jnp.full((1, gw), sink_ref[heads[h][-1]] * LOG2E, F32)
        for g in reversed(range(GQA_GROUP - 1)):
            row_sink = jnp.where(lane < (g + 1) * blk, sink_ref[heads[h][g]] * LOG2E, row_sink)
        sk.append(row_sink)
    q = [jnp.concatenate([q_ref[:, hq * dh:(hq + 1) * dh] for hq in heads[h]], axis=0) for h in groups]
    vaug = [jnp.concatenate([vt[h * dh:(h + 1) * dh, :], ones_rows], axis=0) for h in groups]
    s = [jnp.where(valid, _dot_nt(keys[:, h * dh:(h + 1) * dh], q[h]) + bias_sc[h], NEG) for h in groups]
    m = [jnp.maximum(jnp.max(s[h], axis=0, keepdims=True), sk[h]) for h in groups]
    p = [jnp.exp2(s[h] - m[h]).astype(BF16) for h in groups]
    oa = [_dot(vaug[h], p[h]) for h in groups]
    o = [oa[h][:dh] / (oa[h][dh:dh + 1] + jnp.exp2(sk[h] - m[h])) for h in groups]
    for h in groups:
        for g, hq in enumerate(heads[h]):
            ot_ref[hq * dh:(hq + 1) * dh, :] = o[h][:, g * blk:(g + 1) * blk].astype(ot_ref.dtype)


def attention_core(q, k, vt, rel_bias, sink, batch, seq):
    n, dq = q.shape
    dk = k.shape[1]
    hq = rel_bias.shape[1]
    n_kv = hq // GQA_GROUP
    nb = seq // ATT_BLOCK
    blk = ATT_BLOCK
    prv = lambda b, i: b * nb + jnp.maximum(i - 1, 0)
    cur = lambda b, i: b * nb + i
    nxt = lambda b, i: b * nb + jnp.minimum(i + 1, nb - 1)
    return pl.pallas_call(
        functools.partial(_attn_kernel, nb=nb, n_kv=n_kv),
        grid=(batch, nb),
        in_specs=[pl.BlockSpec(memory_space=pltpu.SMEM),
                  pl.BlockSpec(memory_space=pltpu.SMEM),
                  pl.BlockSpec((blk, dq), lambda b, i: (cur(b, i), 0)),
                  pl.BlockSpec((blk, dk), lambda b, i: (prv(b, i), 0)),
                  pl.BlockSpec((blk, dk), lambda b, i: (cur(b, i), 0)),
                  pl.BlockSpec((blk, dk), lambda b, i: (nxt(b, i), 0)),
                  pl.BlockSpec((dk, blk), lambda b, i: (0, prv(b, i))),
                  pl.BlockSpec((dk, blk), lambda b, i: (0, cur(b, i))),
                  pl.BlockSpec((dk, blk), lambda b, i: (0, nxt(b, i))),
                  pl.BlockSpec((3 * blk, blk), lambda b, i: (0, 0))],
        out_specs=pl.BlockSpec((dq, blk), lambda b, i: (0, cur(b, i))),
        out_shape=jax.ShapeDtypeStruct((dq, n), BF16),
        scratch_shapes=[pltpu.VMEM((n_kv, 3 * blk, GQA_GROUP * blk), F32)],
        compiler_params=_cparams(2),
    )(sink.astype(F32), rel_bias.astype(F32), q, k, k, k, vt, vt, vt, _attn_bucket_table())


def _prefix_incl(x, tri):
    t = x.shape[1]
    xb = x.astype(BF16)
    local = [_dot(xb[:, c * LANE:(c + 1) * LANE], tri) for c in range(t // LANE)]
    outs = []
    carry = jnp.zeros((x.shape[0], 1), F32)
    for p in local:
        p = p + carry
        outs.append(p)
        carry = p[:, LANE - 1:LANE]
    return jnp.concatenate(outs, axis=1)


def _select_kernel(aff_ref, cmp_ref, offc_ref, rng_ref, pos_sc, dest_sc, vals_sc, *, cap, rchunk):
    aff = aff_ref[0]
    ne, t = aff.shape
    bits = pltpu.bitcast(aff, I32)

    def search(i, lo):
        cand = lo | lax.shift_left(jnp.int32(1), 30 - i)
        cnt = jnp.sum((bits >= cand).astype(I32), axis=1, keepdims=True)
        return jnp.where(cnt >= cap, cand, lo)

    thr = lax.fori_loop(0, 31, search, jnp.zeros((ne, 1), I32))
    gt = bits > thr
    eq = bits == thr
    need = (cap - jnp.sum(gt.astype(I32), axis=1, keepdims=True)).astype(F32)
    ri = lax.broadcasted_iota(I32, (LANE, LANE), 0)
    ci = lax.broadcasted_iota(I32, (LANE, LANE), 1)
    tri = (ri <= ci).astype(BF16)
    eqf = eq.astype(F32)
    rank_eq = _prefix_incl(eqf, tri) - eqf
    sel = jnp.logical_or(gt, jnp.logical_and(eq, rank_eq < need))
    self_ = sel.astype(F32)
    kt = jnp.sum(self_, axis=0, keepdims=True)
    pre = _prefix_incl(jnp.concatenate([self_, jnp.broadcast_to(kt, (8, t))], axis=0), tri)
    pos_sc[...] = jnp.where(sel, pre[:ne] - 1.0, -1.0)
    end = pre[ne:ne + 1]
    off = end - kt
    lr = lax.broadcasted_iota(I32, (ne, ne), 0)
    lc = lax.broadcasted_iota(I32, (ne, ne), 1)
    jexp = _dot((lc < lr).astype(BF16), self_.astype(BF16))
    dest_sc[...] = off + jexp

    tok = lax.broadcasted_iota(I32, (1, t), 1)
    vals_sc[0:1, :] = (tok >> 6).astype(F32)
    vals_sc[1:2, :] = (tok & 63).astype(F32)
    vals_sc[7:8, :] = jnp.zeros((1, t), F32)
    slot = lax.broadcasted_iota(I32, (cap, 1), 0).astype(F32)

    def compact(e, carry):
        d = dest_sc[pl.ds(e, 1), :]
        dh = jnp.floor(d * (1.0 / LANE))
        g = aff_ref[0, pl.ds(e, 1), :]
        g0 = g.astype(BF16).astype(F32)
        g1 = (g - g0).astype(BF16).astype(F32)
        vals_sc[2:3, :] = dh
        vals_sc[3:4, :] = d - dh * LANE
        vals_sc[4:5, :] = g0
        vals_sc[5:6, :] = g1
        vals_sc[6:7, :] = g - g0 - g1
        onehot = (pos_sc[pl.ds(e, 1), :] == slot).astype(BF16)
        cmp_ref[0, e] = _dot_nt(vals_sc[...].astype(BF16), onehot)
        return carry

    lax.fori_loop(0, ne, compact, 0)

    eh = jnp.floor(end * (1.0 / LANE))
    oh = jnp.floor(off * (1.0 / LANE))
    v4 = jnp.concatenate([oh, off - oh * LANE, eh, end - eh * LANE, jnp.zeros((4, t), F32)], axis=0)
    eye = (ri == ci).astype(BF16)
    for c in range(t // LANE):
        offc_ref[0, c * LANE:(c + 1) * LANE, :] = _dot_nt(eye, v4[:, c * LANE:(c + 1) * LANE].astype(BF16))

    nchunk = (ne * cap) // rchunk
    low = (lax.broadcasted_iota(I32, (nchunk, 1), 0) * rchunk).astype(F32)
    first = jnp.sum((end <= low).astype(F32), axis=1, keepdims=True)
    last = jnp.sum((end <= low + (rchunk - 1)).astype(F32), axis=1, keepdims=True)
    lane = lax.broadcasted_iota(I32, (nchunk, LANE), 1)
    tiles = jnp.where(lane < LANE // 2, jnp.floor(first * (1.0 / LANE)), jnp.floor(last * (1.0 / LANE)))
    rng_ref[0] = tiles.astype(I32)


COMBINE_ROWS = 512
COMBINE_TILES = 3


def route_select(aff, cap, rchunk):
    batch, ne, t = aff.shape
    nchunk = ne * cap // rchunk
    return pl.pallas_call(
        functools.partial(_select_kernel, cap=cap, rchunk=rchunk),
        grid=(batch,),
        in_specs=[pl.BlockSpec((1, ne, t), lambda b: (b, 0, 0))],
        out_specs=[pl.BlockSpec((1, ne, 8, cap), lambda b: (b, 0, 0, 0)),
                   pl.BlockSpec((1, t, 8), lambda b: (b, 0, 0)),
                   pl.BlockSpec((1, nchunk, LANE), lambda b: (b, 0, 0))],
        out_shape=[jax.ShapeDtypeStruct((batch, ne, 8, cap), F32),
                   jax.ShapeDtypeStruct((batch, t, 8), F32),
                   jax.ShapeDtypeStruct((batch, nchunk, LANE), I32)],
        scratch_shapes=[pltpu.VMEM((ne, t), F32), pltpu.VMEM((ne, t), F32), pltpu.VMEM((8, t), F32)],
        compiler_params=_cparams(1),
    )(aff)


def _ffn_kernel(idx_sm, dest_sm, hn_hbm, cmp_ref, w1_ref, w3_ref, w2_ref, r_hbm,
                xs32, xsb, yacc, ysc, wb1, wb3, wb2, gsem, ssem, *, batch, cap, nf, rt):
    e = pl.program_id(0)
    f = pl.program_id(1)
    ne = pl.num_programs(0)
    rows = batch * cap
    sub = rows // nf

    def gather_start(block, r):
        t = idx_sm[block * rows + r]
        pltpu.make_async_copy(hn_hbm.at[pl.ds(t, 1), :], xs32.at[pl.ds(r, 1), :], gsem).start()

    def gather_wait():
        pltpu.make_async_copy(hn_hbm.at[pl.ds(0, rows), :], xs32, gsem).wait()

    def scatter_start(block, r):
        d = dest_sm[block * rows + r]
        pltpu.make_async_copy(ysc.at[pl.ds(r, 1), :], r_hbm.at[pl.ds(d, 1), :], ssem).start()

    def scatter_wait():
        pltpu.make_async_copy(ysc, r_hbm.at[pl.ds(0, rows), :], ssem).wait()

    @pl.when(jnp.logical_and(e == 0, f == 0))
    def _prologue():
        def issue(r, c):
            gather_start(0, r)
            return c
        lax.fori_loop(0, rows, issue, 0)
        ysc[...] = jnp.zeros_like(ysc)

    @pl.when(f == 0)
    def _rows_ready():
        gather_wait()
        xsb[...] = xs32[...].astype(BF16)
        yacc[...] = jnp.zeros_like(yacc)

    for r in range(sub):
        gather_start(e + 1, f * sub + r)
        scatter_start(e, f * sub + r)

    wb1[...] = w1_ref[0, 0].astype(BF16)
    for r in range(rows // rt):
        xs = xsb[r * rt:(r + 1) * rt, :]
        a = _dot(xs, wb1[...])
        if r == 0:
            wb3[...] = w3_ref[0, 0].astype(BF16)
        u = _dot(xs, wb3[...])
        hmid = (a * jax.nn.sigmoid(a) * u).astype(BF16)
        if r == 0:
            wb2[...] = w2_ref[0, 0].astype(BF16)
        yacc[r * rt:(r + 1) * rt, :] += _dot(hmid, wb2[...])

    @pl.when(f == nf - 1)
    def _finish():
        scatter_wait()
        ri = lax.broadcasted_iota(I32, (cap, cap), 0)
        ci = lax.broadcasted_iota(I32, (cap, cap), 1)
        eye = (ri == ci).astype(BF16)
        for b in range(batch):
            gt = _dot_nt(eye, cmp_ref[b, 0].astype(BF16))
            g = gt[:, 4:5] + gt[:, 5:6] + gt[:, 6:7]
            y = yacc[b * cap:(b + 1) * cap, :] * g
            half = y.shape[1] // 2
            hi = pltpu.bitcast(y[:, :half].astype(BF16).astype(F32), U32)
            lo = pltpu.bitcast(y[:, half:].astype(BF16).astype(F32), U32)
            ysc[b * cap:(b + 1) * cap, :] = hi | lax.shift_right_logical(lo, jnp.uint32(16))

        @pl.when(e == ne - 1)
        def _epilogue():
            def issue(r, c):
                scatter_start(ne, r)
                return c
            lax.fori_loop(0, rows, issue, 0)
            scatter_wait()
            gather_wait()


def expert_ffn(hn, cmp, idx_flat, dest_flat, w1, w3, w2, layer, cap, tf=512, rt=512):
    n, d = hn.shape
    batch, ne = cmp.shape[0], cmp.shape[1]
    dff = w1.shape[3]
    nf = dff // tf
    rows = batch * cap
    rt = min(rt, rows)
    grid_spec = pltpu.PrefetchScalarGridSpec(
        num_scalar_prefetch=2,
        grid=(ne, nf),
        in_specs=[pl.BlockSpec(memory_space=pl.ANY),
                  pl.BlockSpec((batch, 1, 8, cap), lambda e, f, *_: (0, e, 0, 0)),
                  pl.BlockSpec((1, 1, d, tf), lambda e, f, *_: (layer, e, 0, f)),
                  pl.BlockSpec((1, 1, d, tf), lambda e, f, *_: (layer, e, 0, f)),
                  pl.BlockSpec((1, 1, tf, d), lambda e, f, *_: (layer, e, f, 0))],
        out_specs=pl.BlockSpec(memory_space=pl.ANY),
        scratch_shapes=[pltpu.VMEM((rows, d), F32), pltpu.VMEM((rows, d), BF16),
                        pltpu.VMEM((rows, d), F32), pltpu.VMEM((rows, d // 2), U32),
                        pltpu.VMEM((d, tf), BF16), pltpu.VMEM((d, tf), BF16), pltpu.VMEM((tf, d), BF16),
                        pltpu.SemaphoreType.DMA, pltpu.SemaphoreType.DMA])
    return pl.pallas_call(
        functools.partial(_ffn_kernel, batch=batch, cap=cap, nf=nf, rt=rt),
        grid_spec=grid_spec,
        out_shape=jax.ShapeDtypeStruct((batch * ne * cap + rows, d // 2), U32),
        compiler_params=_cparams(2),
    )(idx_flat, dest_flat, hn, cmp, w1, w3, w2)


def _combine_kernel(tlo_sm, thi_sm, r_ref, offc_ref, x_hbm, o_hbm, acc, xsem, osem, *, batch, nchunk, rchunk):
    g = pl.program_id(0)
    b = g // nchunk
    j = g % nchunk
    slot = b % 2
    ntile = acc.shape[1] // LANE

    def x_copy(bb):
        return pltpu.make_async_copy(x_hbm.at[bb], acc.at[bb % 2], xsem.at[bb % 2])

    def o_copy(bb):
        return pltpu.make_async_copy(acc.at[bb % 2], o_hbm.at[bb], osem.at[bb % 2])

    @pl.when(g == 0)
    def _first():
        x_copy(0).start()

    @pl.when(j == 0)
    def _batch_start():
        x_copy(b).wait()

    @pl.when(j == nchunk // 2)
    def _mid():
        @pl.when(b > 0)
        def _():
            o_copy(b - 1).wait()

        @pl.when(b + 1 < batch)
        def _():
            x_copy(b + 1).start()

    r = r_ref[...]
    half = r.shape[1]
    left = pltpu.bitcast(r & jnp.uint32(0xFFFF0000), F32).astype(BF16)
    right = pltpu.bitcast(lax.shift_left(r, jnp.uint32(16)), F32).astype(BF16)
    rowid = (j * rchunk + lax.broadcasted_iota(I32, (1, rchunk), 1)).astype(F32)

    def contrib(i, valid):
        t0 = pl.multiple_of(i * LANE, LANE)
        oc = offc_ref[0, pl.ds(t0, LANE), :]
        off = oc[:, 0:1] * LANE + oc[:, 1:2]
        end = oc[:, 2:3] * LANE + oc[:, 3:4]
        p = jnp.logical_and(jnp.logical_and(rowid >= off, rowid < end), valid).astype(BF16)
        return t0, (_dot(p, left), _dot(p, right))

    def add(t0, y):
        acc[slot, pl.ds(t0, LANE), :half] += y[0]
        acc[slot, pl.ds(t0, LANE), half:] += y[1]

    def tile(i, c):
        add(*contrib(i, True))
        return c

    tlo = tlo_sm[g]
    thi = thi_sm[g]
    parts = [contrib(jnp.minimum(tlo + k, ntile - 1), tlo + k <= thi) for k in range(COMBINE_TILES)]
    for t0, y in parts:
        add(t0, y)
    lax.fori_loop(tlo + COMBINE_TILES, thi + 1, tile, 0)

    @pl.when(j == nchunk - 1)
    def _batch_end():
        o_copy(b).start()

        @pl.when(b == batch - 1)
        def _():
            o_copy(b).wait()


def combine(rbuf, offc, tlo, thi, x, rows_per_batch, rchunk):
    batch, t, d = x.shape
    nchunk = rows_per_batch // rchunk
    assert nchunk >= 2
    grid_spec = pltpu.PrefetchScalarGridSpec(
        num_scalar_prefetch=2,
        grid=(batch * nchunk,),
        in_specs=[pl.BlockSpec((rchunk, d // 2), lambda g, *_: (g, 0)),
                  pl.BlockSpec((1, t, 8), lambda g, *_: (g // nchunk, 0, 0)),
                  pl.BlockSpec(memory_space=pl.ANY)],
        out_specs=pl.BlockSpec(memory_space=pl.ANY),
        scratch_shapes=[pltpu.VMEM((2, t, d), F32), pltpu.SemaphoreType.DMA((2,)), pltpu.SemaphoreType.DMA((2,))])
    return pl.pallas_call(
        functools.partial(_combine_kernel, batch=batch, nchunk=nchunk, rchunk=rchunk),
        grid_spec=grid_spec,
        out_shape=jax.ShapeDtypeStruct((batch, t, d), F32),
        compiler_params=_cparams(1),
    )(tlo, thi, rbuf, offc, x)


def moe_block(x1, hn, aff, w1, w3, w2, layer, batch):
    n, d = x1.shape
    t = n // batch
    ne = aff.shape[1]
    cap = CAPACITY_FACTOR * t // ne
    rows = batch * cap
    cmp, offc, rng = route_select(aff, cap, COMBINE_ROWS)
    ci = cmp.astype(I32)
    boff = jnp.arange(batch, dtype=I32)[:, None, None]
    idx = (ci[:, :, 0] * 64 + ci[:, :, 1] + boff * t).transpose(1, 0, 2).reshape(-1)
    dest = (ci[:, :, 2] * LANE + ci[:, :, 3] + boff * (ne * cap)).transpose(1, 0, 2).reshape(-1)
    idx = jnp.concatenate([idx, idx[:rows]])
    dest = jnp.concatenate([batch * ne * cap + jnp.arange(rows, dtype=I32), dest])
    rbuf = expert_ffn(hn, cmp, idx, dest, w1, w3, w2, layer, cap)
    tlo = rng[:, :, 0].reshape(-1)
    thi = rng[:, :, LANE - 1].reshape(-1)
    return combine(rbuf, offc, tlo, thi, x1.reshape(batch, t, d), ne * cap, COMBINE_ROWS)


MLSTM_L = 256


def _log_sigmoid(x):
    return jnp.minimum(x, 0.0) - jnp.log1p(jnp.exp(-jnp.abs(x)))


def _split3(x):
    x0 = x.astype(BF16)
    r = x - x0.astype(F32)
    x1 = r.astype(BF16)
    return x0, x1, (r - x1.astype(F32)).astype(BF16)


def _mlstm_proj_kernel(x_ref, g_ref, wm_ref, wkt_ref, wgt_ref, bt_ref, main_ref, kt_ref, gt_ref, *, nsub, nh):
    ln = MLSTM_L
    h = _rms(x_ref[...], g_ref[...]).astype(BF16)
    main_ref[...] = _dot(h, wm_ref[...]).astype(main_ref.dtype)
    kt = _dot_nt(wkt_ref[...], h).astype(kt_ref.dtype)
    pre = _dot_nt(wgt_ref[...], h) + bt_ref[...]
    lst = _log_sigmoid(pre)
    ri = lax.broadcasted_iota(I32, (ln, ln), 0)
    ci = lax.broadcasted_iota(I32, (ln, ln), 1)
    low = (ci <= ri).astype(BF16)
    upp = (ci >= ri).astype(BF16)
    row = lax.broadcasted_iota(I32, (pre.shape[0], ln), 0)
    fwd = jnp.logical_and(row >= nh, row < 2 * nh)
    bwd = row >= 3 * nh
    for j in range(nsub):
        sl = slice(j * ln, (j + 1) * ln)
        kt_ref[j] = kt[:, sl]
        pt = _split3(lst[:, sl])
        gt_ref[j] = jnp.where(fwd, sum(_dot(x, upp) for x in pt),
                              jnp.where(bwd, sum(_dot(x, low) for x in pt), pre[:, sl]))


def mlstm_project(x, g, wm, wkt, wgt, bias_t, nh, tm=512):
    n, d = x.shape
    nsub = tm // MLSTM_L
    nck = n // MLSTM_L
    const = lambda i: (0, 0)
    return pl.pallas_call(
        functools.partial(_mlstm_proj_kernel, nsub=nsub, nh=nh),
        grid=(n // tm,),
        in_specs=[pl.BlockSpec((tm, d), lambda i: (i, 0)),
                  pl.BlockSpec((1, d), const),
                  pl.BlockSpec(wm.shape, const),
                  pl.BlockSpec(wkt.shape, const),
                  pl.BlockSpec(wgt.shape, const),
                  pl.BlockSpec(bias_t.shape, const)],
        out_specs=[pl.BlockSpec((tm, wm.shape[1]), lambda i: (i, 0)),
                   pl.BlockSpec((nsub, wkt.shape[0], MLSTM_L), lambda i: (i, 0, 0)),
                   pl.BlockSpec((nsub, wgt.shape[0], MLSTM_L), lambda i: (i, 0, 0))],
        out_shape=[jax.ShapeDtypeStruct((n, wm.shape[1]), BF16),
                   jax.ShapeDtypeStruct((nck, wkt.shape[0], MLSTM_L), BF16),
                   jax.ShapeDtypeStruct((nck, wgt.shape[0], MLSTM_L), F32)],
        compiler_params=_cparams(1),
    )(x, g.reshape(1, d), wm, wkt, wgt, bias_t)


def _mlstm_kernel(q_ref, kt_ref, v_ref, og_ref, gt_ref, ng_ref, y_ref, hf, hb, cst, b_sc, u_sc, cm_sc,
                  *, nc, nh, dqk, dv):
    p = pl.program_id(1)
    ln = MLSTM_L
    ri = lax.broadcasted_iota(I32, (ln, ln), 0)
    ci = lax.broadcasted_iota(I32, (ln, ln), 1)
    masks = (ci <= ri, ci >= ri)
    ones_blk = jnp.ones((ln, dv), BF16)
    lane8 = lax.broadcasted_iota(I32, (8, ln), 1)
    row8 = lax.broadcasted_iota(I32, (8, ln), 0)
    bwd_row = (row8 % 2) == 1
    edge = lane8 == jnp.where(bwd_row, 0, ln - 1)
    kk = lax.broadcasted_iota(I32, (48, 2 * dv), 0) % 16
    cc = lax.broadcasted_iota(I32, (48, 2 * dv), 1)
    sel = [jnp.logical_or(jnp.logical_and(kk == i, cc < dv), jnp.logical_and(kk == 4 + i, cc >= dv)).astype(BF16)
           for i in range(4)]
    zero4 = jnp.zeros((4, ln), F32)
    cst[...] = jnp.zeros_like(cst)
    chains = [(hh, d) for hh in range(2) for d in range(2)]

    def gate_rows(c, which):
        return [gt_ref[c if d == 0 else nc - 1 - c, pl.ds((2 * d + which) * nh + 2 * p + hh, 1), :]
                for hh, d in chains] + [zero4]
    b_all = jnp.concatenate([x for c in range(nc) for x in gate_rows(c, 1)], axis=0)
    u_all = jnp.concatenate([x for c in range(nc) for x in gate_rows(c, 0)], axis=0) - b_all
    lane_a = lax.broadcasted_iota(I32, u_all.shape, 1)
    bwd_a = (lax.broadcasted_iota(I32, u_all.shape, 0) % 2) == 1
    cf = cb = u_all
    sh = 1
    while sh < ln:
        cf = jnp.maximum(cf, jnp.where(lane_a >= sh, pltpu.roll(cf, sh, axis=1), NEG))
        cb = jnp.maximum(cb, jnp.where(lane_a < ln - sh, pltpu.roll(cb, ln - sh, axis=1), NEG))
        sh *= 2
    b_sc[...] = b_all.reshape(nc, 8, ln)
    u_sc[...] = u_all.reshape(nc, 8, ln)
    cm_sc[...] = jnp.where(bwd_a, cb, cf).reshape(nc, 8, ln)

    def step(c, ms):
        cks = [c if d == 0 else nc - 1 - c for _, d in chains]
        r0s = [pl.multiple_of(ck * ln, ln) for ck in cks]
        cprev = [cst[i] for i in range(4)]
        q = [q_ref[pl.ds(r0s[i], ln), hh * dqk:(hh + 1) * dqk] for i, (hh, _) in enumerate(chains)]
        kt = [kt_ref[cks[i], hh * dqk:(hh + 1) * dqk, :] for i, (hh, _) in enumerate(chains)]
        vaug = [jnp.concatenate([v_ref[pl.ds(r0s[i], ln), hh * dv:(hh + 1) * dv], ones_blk], axis=1)
                for i, (hh, _) in enumerate(chains)]
        s = [_dot(q[i], kt[i]) for i in range(4)]
        inter = [_dot(q[i], cprev[i].astype(BF16)) for i in range(4)]
        b = b_sc[c]
        u = u_sc[c]
        m_run = jnp.maximum(ms, cm_sc[c])
        m_end = jnp.max(jnp.where(edge, m_run, NEG), axis=1, keepdims=True)
        b_end = jnp.sum(jnp.where(edge, b, 0.0), axis=1, keepdims=True)
        wc = jnp.exp(u - m_end)
        decay = jnp.exp(ms - m_end)
        rows = jnp.concatenate([m_run[:4] * LOG2E, jnp.exp(-(b + m_run))[:4], zero4, zero4], axis=0)
        stack = jnp.concatenate(_split3(rows), axis=0)
        u2 = u * LOG2E
        ms2 = ms * LOG2E
        bc = [lax.dot_general(stack, sel[i], (((0,), (0,)), ((), ())), preferred_element_type=F32)
              for i in range(4)]
        mb = [jnp.concatenate([bc[i][:, :dv]] * (ln // dv), axis=1) for i in range(4)]
        sw = [(jnp.exp2(jnp.where(masks[d], u2[i:i + 1, :] - mb[i], NEG)) * s[i]).astype(BF16)
              for i, (_, d) in enumerate(chains)]
        intra = [_dot(sw[i], vaug[i]) for i in range(4)]
        ea = [jnp.exp2(ms2[i:i + 1, :] - bc[i][:, :dv]) for i in range(4)]
        num = [ea[i] * inter[i][:, :dv] + intra[i][:, :dv] for i in range(4)]
        den = [ea[i] * inter[i][:, dv:] + intra[i][:, dv:] for i in range(4)]
        hc = [num[i] / jnp.maximum(jnp.abs(den[i]), bc[i][:, dv:]) for i in range(4)]
        kw = [(kt[i].astype(F32) * wc[i:i + 1, :]).astype(BF16) for i in range(4)]
        c_new = [decay[i:i + 1, :] * cprev[i] + _dot(kw[i], vaug[i]) for i in range(4)]
        for i, (hh, d) in enumerate(chains):
            cst[i] = c_new[i]
            (hf if d == 0 else hb)[pl.ds(r0s[i], ln), hh * dv:(hh + 1) * dv] = hc[i]
        return b_end + m_end

    lax.fori_loop(0, nc, step, jnp.zeros((8, 1), F32))
    for hh in range(2):
        sl = slice(hh * dv, (hh + 1) * dv)
        hs = _rms(hf[:, sl] + hb[:, sl], ng_ref[:, sl])
        y_ref[:, sl] = (hs * jax.nn.sigmoid(og_ref[:, sl].astype(F32))).astype(y_ref.dtype)


def mlstm_core(main, ktc, grow, out_g, batch, seq, nh, dqk, dv):
    n = main.shape[0]
    nc = seq // MLSTM_L
    npair = nh // 2
    vblocks = (nh * dqk) // (2 * dv)
    ogblocks = (nh * dqk + nh * dv) // (2 * dv)
    return pl.pallas_call(
        functools.partial(_mlstm_kernel, nc=nc, nh=nh, dqk=dqk, dv=dv),
        grid=(batch, npair),
        in_specs=[pl.BlockSpec((seq, 2 * dqk), lambda b, p: (b, p)),
                  pl.BlockSpec((nc, 2 * dqk, MLSTM_L), lambda b, p: (b, p, 0)),
                  pl.BlockSpec((seq, 2 * dv), lambda b, p: (b, vblocks + p)),
                  pl.BlockSpec((seq, 2 * dv), lambda b, p: (b, ogblocks + p)),
                  pl.BlockSpec((nc, 4 * nh, MLSTM_L), lambda b, p: (b, 0, 0)),
                  pl.BlockSpec((1, 2 * dv), lambda b, p: (0, p))],
        out_specs=pl.BlockSpec((seq, 2 * dv), lambda b, p: (b, p)),
        out_shape=jax.ShapeDtypeStruct((n, nh * dv), BF16),
        scratch_shapes=[pltpu.VMEM((seq, 2 * dv), F32), pltpu.VMEM((seq, 2 * dv), F32),
                        pltpu.VMEM((4, dqk, 2 * dv), F32)] + [pltpu.VMEM((nc, 8, MLSTM_L), F32)] * 3,
        compiler_params=_cparams(2),
    )(main, ktc, main, main, grow, out_g.reshape(1, nh * dv).astype(F32))


def mlstm_layer(x2d, batch, seq, norm_g, w_in, b_i, b_f, out_g, w_out, ffn_g, wr):
    d = x2d.shape[1]
    nh = MLSTM_HEADS
    dv = d // nh
    dqk = dv // 2
    o1, o2, o3, o4 = nh * dqk, 2 * nh * dqk, 2 * nh * dqk + nh * dv, 2 * nh * dqk + 2 * nh * dv
    wm = jnp.concatenate([w_in[:, :o1], w_in[:, o2:o4]], axis=1).astype(BF16)
    wkt = (w_in[:, o1:o2] * (dqk ** -0.5)).T.astype(BF16)
    wgt = w_in[:, o4:].T.astype(BF16)
    bias_t = jnp.concatenate([b_i[0], b_f[0], b_i[1], b_f[1]]).reshape(4 * nh, 1).astype(F32)
    main, ktc, grow = mlstm_project(x2d, norm_g, wm, wkt, wgt, bias_t, nh)
    y = mlstm_core(main, ktc, grow, out_g, batch, seq, nh, dqk, dv)
    return mm_res_router(y, w_out.astype(BF16), x2d, ffn_g, wr.T.astype(BF16), batch)


def attention_layer(x2d, batch, seq, norm_g, w_in, q_g, k_g, sink, w_out, rel_bias, ffn_g, wr):
    n_q = rel_bias.shape[1]
    q, k, vt = attention_project(x2d, norm_g, w_in, q_g, k_g, n_q, n_q // GQA_GROUP)
    ot = attention_core(q, k, vt, rel_bias, sink, batch, seq)
    return mm_res_router(ot, w_out.astype(BF16), x2d, ffn_g, wr.T.astype(BF16), batch, a_transposed=True)


def kernel(x, rel_bias, attn_norm_g, attn_w_in, attn_q_norm_g, attn_k_norm_g, attn_sink, attn_w_out, mlstm_norm_g, mlstm_w_in, mlstm_b_i, mlstm_b_f, mlstm_out_norm_g, mlstm_w_out, ffn_norm_g, router_w, expert_w1, expert_w3, expert_w2):
    batch, seq, d = x.shape
    x2d = x.reshape(batch * seq, d)
    x1, hn, aff = attention_layer(x2d, batch, seq, attn_norm_g[0], attn_w_in[0], attn_q_norm_g[0],
                                  attn_k_norm_g[0], attn_sink[0], attn_w_out[0], rel_bias,
                                  ffn_norm_g[0], router_w[0])
    x = moe_block(x1, hn, aff, expert_w1, expert_w3, expert_w2, 0, batch)
    x1, hn, aff = mlstm_layer(x.reshape(batch * seq, d), batch, seq, mlstm_norm_g[0], mlstm_w_in[0], mlstm_b_i[0],
                              mlstm_b_f[0], mlstm_out_norm_g[0], mlstm_w_out[0], ffn_norm_g[1], router_w[1])
    return moe_block(x1, hn, aff, expert_w1, expert_w3, expert_w2, 1, batch)
```

```python
import functools
import math

import jax
import jax.numpy as jnp
from jax import lax
from jax.experimental import pallas as pl
from jax.experimental.pallas import tpu as pltpu

F32 = jnp.float32
BF16 = jnp.bfloat16
I32 = jnp.int32
U32 = jnp.uint32

RMS_EPS = 1e-6
NEG = -1e30
LOG2E = 1.4426950408889634
LANE = 128
VMEM_LIMIT = 56 * 1024 * 1024
PROJ_ROWS = 1024

HEAD_DIM = 64
GQA_GROUP = 4
ATT_BLOCK = 128
NUM_BUCKETS = 32
MAX_DISTANCE = 128
N_EXPERTS = 16
CAPACITY_FACTOR = 2
MLSTM_HEADS = 8


def _cparams(n_axes, vmem=VMEM_LIMIT):
    return pltpu.CompilerParams(dimension_semantics=("arbitrary",) * n_axes, vmem_limit_bytes=vmem)


def _rms(x, g):
    return x * lax.rsqrt(jnp.mean(x * x, axis=-1, keepdims=True) + RMS_EPS) * g


def _dot(a, b):
    return jnp.dot(a, b, preferred_element_type=F32)


def _pack_bf16_pairs(x):
    half = x.shape[1] // 2
    hi = pltpu.bitcast(x[:, :half].astype(BF16).astype(F32), U32)
    lo = pltpu.bitcast(x[:, half:].astype(BF16).astype(F32), U32)
    return hi | lax.shift_right_logical(lo, jnp.uint32(16))


def _unpack_bf16_pairs(w):
    left = pltpu.bitcast(w & jnp.uint32(0xFFFF0000), F32).astype(BF16)
    right = pltpu.bitcast(lax.shift_left(w, jnp.uint32(16)), F32).astype(BF16)
    return left, right


def _dot_nt(a, b):
    return lax.dot_general(a, b, (((1,), (1,)), ((), ())), preferred_element_type=F32)


def _attn_proj_kernel(x_ref, g_ref, wq_ref, wk_ref, wvt_ref, qg_ref, kg_ref, seg_ref, q_ref, k_ref, vt_ref):
    h = _rms(x_ref[...], g_ref[...]).astype(BF16)
    seg = seg_ref[...]
    w = seg.shape[0]

    def head_norm(t, gain_ref, out_ref):
        for j in range(t.shape[1] // w):
            tj = t[:, j * w:(j + 1) * w]
            ms = _dot((tj * tj).astype(BF16), seg)
            out_ref[:, j * w:(j + 1) * w] = (tj * lax.rsqrt(ms + RMS_EPS) * gain_ref[:, j * w:(j + 1) * w]
                                             ).astype(out_ref.dtype)

    head_norm(_dot(h, wq_ref[...]), qg_ref, q_ref)
    head_norm(_dot(h, wk_ref[...]), kg_ref, k_ref)
    vt_ref[...] = _dot_nt(wvt_ref[...], h).astype(vt_ref.dtype)


def attention_project(x, g, w_in, q_g, k_g, n_q, n_kv, tm=PROJ_ROWS):
    n, d = x.shape
    dh = HEAD_DIM
    dq, dk = n_q * dh, n_kv * dh
    wq = w_in[:, :dq].astype(BF16)
    wk = w_in[:, dq:dq + dk].astype(BF16)
    wvt = w_in[:, dq + dk:].T.astype(BF16)
    qg = jnp.tile(q_g.astype(F32) * (dh ** -0.5 * LOG2E), n_q).reshape(1, dq)
    kg = jnp.tile(k_g.astype(F32), n_kv).reshape(1, dk)
    hid = jnp.arange(dk) // dh
    seg = jnp.where(hid[:, None] == hid[None, :], 1.0 / dh, 0.0).astype(BF16)
    const2 = lambda i: (0, 0)
    return pl.pallas_call(
        _attn_proj_kernel,
        grid=(n // tm,),
        in_specs=[pl.BlockSpec((tm, d), lambda i: (i, 0)),
                  pl.BlockSpec((1, d), const2),
                  pl.BlockSpec((d, dq), const2),
                  pl.BlockSpec((d, dk), const2),
                  pl.BlockSpec((dk, d), const2),
                  pl.BlockSpec((1, dq), const2),
                  pl.BlockSpec((1, dk), const2),
                  pl.BlockSpec((dk, dk), const2)],
        out_specs=[pl.BlockSpec((tm, dq), lambda i: (i, 0)),
                   pl.BlockSpec((tm, dk), lambda i: (i, 0)),
                   pl.BlockSpec((dk, tm), lambda i: (0, i))],
        out_shape=[jax.ShapeDtypeStruct((n, dq), BF16),
                   jax.ShapeDtypeStruct((n, dk), BF16),
                   jax.ShapeDtypeStruct((dk, n), BF16)],
        compiler_params=_cparams(1),
    )(x, g.reshape(1, d), wq, wk, wvt, qg, kg, seg)


def _mm_res_router_kernel(a_ref, w_ref, x_ref, g_ref, wr_ref, x1_ref, hn_ref, aff_ref, *, a_transposed):
    if a_transposed:
        y = lax.dot_general(a_ref[...], w_ref[...], (((0,), (0,)), ((), ())), preferred_element_type=F32)
    else:
        y = _dot(a_ref[...], w_ref[...])
    x1 = x_ref[...] + y
    x1_ref[...] = x1
    hn = _rms(x1, g_ref[...])
    hn_ref[...] = _pack_bf16_pairs(hn)
    logits = _dot_nt(wr_ref[...], hn.astype(BF16))
    mx = jnp.max(logits, axis=0, keepdims=True)
    p = jnp.exp(logits - mx)
    aff_ref[0] = p / jnp.sum(p, axis=0, keepdims=True)


def mm_res_router(a, w, x, g, wr_t, batch, a_transposed=False, tm=PROJ_ROWS):
    n, d = x.shape
    k = w.shape[0]
    e = wr_t.shape[0]
    t = n // batch
    tm = min(tm, t)
    tpb = t // tm
    a_spec = pl.BlockSpec((k, tm), lambda i: (0, i)) if a_transposed else pl.BlockSpec((tm, k), lambda i: (i, 0))
    return pl.pallas_call(
        functools.partial(_mm_res_router_kernel, a_transposed=a_transposed),
        grid=(n // tm,),
        in_specs=[a_spec,
                  pl.BlockSpec((k, d), lambda i: (0, 0)),
                  pl.BlockSpec((tm, d), lambda i: (i, 0)),
                  pl.BlockSpec((1, d), lambda i: (0, 0)),
                  pl.BlockSpec((e, d), lambda i: (0, 0))],
        out_specs=[pl.BlockSpec((tm, d), lambda i: (i, 0)),
                   pl.BlockSpec((tm, d // 2), lambda i: (i, 0)),
                   pl.BlockSpec((1, e, tm), lambda i: (i // tpb, 0, i % tpb))],
        out_shape=[jax.ShapeDtypeStruct((n, d), F32),
                   jax.ShapeDtypeStruct((n, d // 2), U32),
                   jax.ShapeDtypeStruct((batch, e, t), F32)],
        compiler_params=_cparams(1),
    )(a, w, x, g.reshape(1, d), wr_t)


def _t5_bucket(rel):
    nb = NUM_BUCKETS // 2
    ret = (rel > 0).astype(jnp.int32) * nb
    n = jnp.abs(rel)
    max_exact = nb // 2
    nf = jnp.maximum(n, 1).astype(jnp.float32)
    large = max_exact + (jnp.log(nf / max_exact) / math.log(MAX_DISTANCE / max_exact)
                         * (nb - max_exact)).astype(jnp.int32)
    large = jnp.minimum(large, nb - 1)
    return ret + jnp.where(n < max_exact, n, large)


def _attn_bucket_table():
    kk = jnp.arange(3 * ATT_BLOCK)[:, None]
    qq = jnp.arange(ATT_BLOCK)[None, :]
    rel = kk - ATT_BLOCK - qq
    return jnp.where(jnp.abs(rel) <= ATT_BLOCK, _t5_bucket(rel), -1).astype(I32)


ATT_QBLOCKS = 2


def _attn_kernel(sink_ref, rb_ref, q_ref, *rest, nb, n_kv):
    nq = ATT_QBLOCKS
    k_refs = rest[:nq + 2]
    v_refs = rest[nq + 2:2 * (nq + 2)]
    bucket_ref, ot_ref, bias_sc = rest[2 * (nq + 2):]
    n = pl.program_id(1)
    blk = ATT_BLOCK
    dh = HEAD_DIM
    gw = GQA_GROUP * blk

    @pl.when(jnp.logical_and(pl.program_id(0) == 0, n == 0))
    def _bias_table():
        bk = bucket_ref[...]
        for hq in range(n_kv * GQA_GROUP):
            acc = jnp.full(bk.shape, NEG, F32)
            for k in range(NUM_BUCKETS):
                acc = jnp.where(bk == k, rb_ref[k, hq] * LOG2E, acc)
            bias_sc[hq // GQA_GROUP, :, (hq % GQA_GROUP) * blk:(hq % GQA_GROUP + 1) * blk] = acc

    kblk = [r[...] for r in k_refs]
    vblk = [r[...] for r in v_refs]
    kidx = lax.broadcasted_iota(I32, (3 * blk, gw), 0)
    ones_rows = (lax.broadcasted_iota(I32, (16, 3 * blk), 0) == 0).astype(BF16)
    lane = lax.broadcasted_iota(I32, (1, gw), 1)
    heads = [[h * GQA_GROUP + g for g in range(GQA_GROUP)] for h in range(n_kv)]
    sk = []
    for h in range(n_kv):
        row_sink = jnp.full((1, gw), sink_ref[heads[h][-1]] * LOG2E, F32)
        for g in reversed(range(GQA_GROUP - 1)):
            row_sink = jnp.where(lane < (g + 1) * blk, sink_ref[heads[h][g]] * LOG2E, row_sink)
        sk.append(row_sink)
    keys, vt, valid = [], [], []
    for j in range(nq):
        i = n * nq + j
        keys.append(jnp.concatenate(kblk[j:j + 3], axis=0))
        vt.append(jnp.concatenate(vblk[j:j + 3], axis=1))
        valid.append(jnp.logical_and(jnp.logical_or(i > 0, kidx >= blk),
                                     jnp.logical_or(i < nb - 1, kidx < 2 * blk)))
    units = [(j, h) for j in range(nq) for h in range(n_kv)]
    q = [jnp.concatenate([q_ref[j * blk:(j + 1) * blk, hq * dh:(hq + 1) * dh] for hq in heads[h]], axis=0)
         for j, h in units]
    vaug = [jnp.concatenate([vt[j][h * dh:(h + 1) * dh, :], ones_rows], axis=0) for j, h in units]
    s = [jnp.where(valid[j], _dot_nt(keys[j][:, h * dh:(h + 1) * dh], q[u]) + bias_sc[h], NEG)
         for u, (j, h) in enumerate(units)]
    m = [jnp.maximum(jnp.max(s[u], axis=0, keepdims=True), sk[h]) for u, (j, h) in enumerate(units)]
    p = [jnp.exp2(s[u] - m[u]).astype(BF16) for u in range(len(units))]
    oa = [_dot(vaug[u], p[u]) for u in range(len(units))]
    o = [oa[u][:dh] / (oa[u][dh:dh + 1] + jnp.exp2(sk[h] - m[u])) for u, (j, h) in enumerate(units)]
    for u, (j, h) in enumerate(units):
        for g, hq in enumerate(heads[h]):
            ot_ref[hq * dh:(hq + 1) * dh, j * blk:(j + 1) * blk] = o[u][:, g * blk:(g + 1) * blk].astype(ot_ref.dtype)


def attention_core(q, k, vt, rel_bias, sink, batch, seq):
    n, dq = q.shape
    dk = k.shape[1]
    hq = rel_bias.shape[1]
    n_kv = hq // GQA_GROUP
    nb = seq // ATT_BLOCK
    blk = ATT_BLOCK
    nq = ATT_QBLOCKS
    assert nb % nq == 0

    def key_block(off):
        return lambda b, i: b * nb + jnp.clip(i * nq + off, 0, nb - 1)

    offs = range(-1, nq + 1)
    return pl.pallas_call(
        functools.partial(_attn_kernel, nb=nb, n_kv=n_kv),
        grid=(batch, nb // nq),
        in_specs=[pl.BlockSpec(memory_space=pltpu.SMEM),
                  pl.BlockSpec(memory_space=pltpu.SMEM),
                  pl.BlockSpec((nq * blk, dq), lambda b, i: (b * (nb // nq) + i, 0))]
                 + [pl.BlockSpec((blk, dk), (lambda f: lambda b, i: (f(b, i), 0))(key_block(o))) for o in offs]
                 + [pl.BlockSpec((dk, blk), (lambda f: lambda b, i: (0, f(b, i)))(key_block(o))) for o in offs]
                 + [pl.BlockSpec((3 * blk, blk), lambda b, i: (0, 0))],
        out_specs=pl.BlockSpec((dq, nq * blk), lambda b, i: (0, b * (nb // nq) + i)),
        out_shape=jax.ShapeDtypeStruct((dq, n), BF16),
        scratch_shapes=[pltpu.VMEM((n_kv, 3 * blk, GQA_GROUP * blk), F32)],
        compiler_params=_cparams(2),
    )(sink.astype(F32), rel_bias.astype(F32), q, *([k] * (nq + 2)), *([vt] * (nq + 2)), _attn_bucket_table())


def _prefix_incl(x, tri):
    t = x.shape[1]
    xb = x.astype(BF16)
    local = [_dot(xb[:, c * LANE:(c + 1) * LANE], tri) for c in range(t // LANE)]
    outs = []
    carry = jnp.zeros((x.shape[0], 1), F32)
    for p in local:
        p = p + carry
        outs.append(p)
        carry = p[:, LANE - 1:LANE]
    return jnp.concatenate(outs, axis=1)


def _select_kernel(aff_ref, cmp_ref, offc_ref, rng_ref, pos_sc, dest_sc, vals_sc, *, cap, rchunk):
    aff = aff_ref[0]
    ne, t = aff.shape
    bits = pltpu.bitcast(aff, I32)

    def search(i, lo):
        cand = lo | lax.shift_left(jnp.int32(1), 30 - i)
        cnt = jnp.sum((bits >= cand).astype(I32), axis=1, keepdims=True)
        return jnp.where(cnt >= cap, cand, lo)

    thr = lax.fori_loop(0, 31, search, jnp.zeros((ne, 1), I32))
    gt = bits > thr
    eq = bits == thr
    need = (cap - jnp.sum(gt.astype(I32), axis=1, keepdims=True)).astype(F32)
    ri = lax.broadcasted_iota(I32, (LANE, LANE), 0)
    ci = lax.broadcasted_iota(I32, (LANE, LANE), 1)
    tri = (ri <= ci).astype(BF16)
    eqf = eq.astype(F32)
    rank_eq = _prefix_incl(eqf, tri) - eqf
    sel = jnp.logical_or(gt, jnp.logical_and(eq, rank_eq < need))
    self_ = sel.astype(F32)
    kt = jnp.sum(self_, axis=0, keepdims=True)
    pre = _prefix_incl(jnp.concatenate([self_, jnp.broadcast_to(kt, (8, t))], axis=0), tri)
    pos_sc[...] = jnp.where(sel, pre[:ne] - 1.0, -1.0)
    end = pre[ne:ne + 1]
    off = end - kt
    lr = lax.broadcasted_iota(I32, (ne, ne), 0)
    lc = lax.broadcasted_iota(I32, (ne, ne), 1)
    jexp = _dot((lc < lr).astype(BF16), self_.astype(BF16))
    dest_sc[...] = off + jexp

    tok = lax.broadcasted_iota(I32, (1, t), 1)
    vals_sc[0:1, :] = (tok >> 6).astype(F32)
    vals_sc[1:2, :] = (tok & 63).astype(F32)
    vals_sc[7:8, :] = jnp.zeros((1, t), F32)
    slot = lax.broadcasted_iota(I32, (cap, 1), 0).astype(F32)

    def compact(e, carry):
        d = dest_sc[pl.ds(e, 1), :]
        dh = jnp.floor(d * (1.0 / LANE))
        g = aff_ref[0, pl.ds(e, 1), :]
        g0 = g.astype(BF16).astype(F32)
        g1 = (g - g0).astype(BF16).astype(F32)
        vals_sc[2:3, :] = dh
        vals_sc[3:4, :] = d - dh * LANE
        vals_sc[4:5, :] = g0
        vals_sc[5:6, :] = g1
        vals_sc[6:7, :] = g - g0 - g1
        onehot = (pos_sc[pl.ds(e, 1), :] == slot).astype(BF16)
        cmp_ref[0, e] = _dot_nt(vals_sc[...].astype(BF16), onehot)
        return carry

    lax.fori_loop(0, ne, compact, 0)

    eh = jnp.floor(end * (1.0 / LANE))
    oh = jnp.floor(off * (1.0 / LANE))
    v4 = jnp.concatenate([oh, off - oh * LANE, eh, end - eh * LANE, jnp.zeros((4, t), F32)], axis=0)
    eye = (ri == ci).astype(BF16)
    for c in range(t // LANE):
        offc_ref[0, c * LANE:(c + 1) * LANE, :] = _dot_nt(eye, v4[:, c * LANE:(c + 1) * LANE].astype(BF16))

    nchunk = (ne * cap) // rchunk
    low = (lax.broadcasted_iota(I32, (nchunk, 1), 0) * rchunk).astype(F32)
    first = jnp.sum((end <= low).astype(F32), axis=1, keepdims=True)
    last = jnp.sum((end <= low + (rchunk - 1)).astype(F32), axis=1, keepdims=True)
    lane = lax.broadcasted_iota(I32, (nchunk, LANE), 1)
    tiles = jnp.where(lane < LANE // 2, jnp.floor(first * (1.0 / LANE)), jnp.floor(last * (1.0 / LANE)))
    rng_ref[0] = tiles.astype(I32)


COMBINE_ROWS = 512
COMBINE_TILES = 3


def route_select(aff, cap, rchunk):
    batch, ne, t = aff.shape
    nchunk = ne * cap // rchunk
    return pl.pallas_call(
        functools.partial(_select_kernel, cap=cap, rchunk=rchunk),
        grid=(batch,),
        in_specs=[pl.BlockSpec((1, ne, t), lambda b: (b, 0, 0))],
        out_specs=[pl.BlockSpec((1, ne, 8, cap), lambda b: (b, 0, 0, 0)),
                   pl.BlockSpec((1, t, 8), lambda b: (b, 0, 0)),
                   pl.BlockSpec((1, nchunk, LANE), lambda b: (b, 0, 0))],
        out_shape=[jax.ShapeDtypeStruct((batch, ne, 8, cap), F32),
                   jax.ShapeDtypeStruct((batch, t, 8), F32),
                   jax.ShapeDtypeStruct((batch, nchunk, LANE), I32)],
        scratch_shapes=[pltpu.VMEM((ne, t), F32), pltpu.VMEM((ne, t), F32), pltpu.VMEM((8, t), F32)],
        compiler_params=_cparams(1),
    )(aff)


def _ffn_kernel(idx_sm, dest_sm, hn_hbm, cmp_ref, w1_ref, w3_ref, w2_ref, r_hbm,
                xsu, xsb, yacc, ysc, wb1, wb3, wb2, gsem, ssem, *, batch, cap, nf, rt):
    e = pl.program_id(0)
    f = pl.program_id(1)
    ne = pl.num_programs(0)
    rows = batch * cap
    sub = rows // nf

    def gather_start(block, r):
        t = idx_sm[block * rows + r]
        pltpu.make_async_copy(hn_hbm.at[pl.ds(t, 1), :], xsu.at[pl.ds(r, 1), :], gsem).start()

    def gather_wait():
        pltpu.make_async_copy(hn_hbm.at[pl.ds(0, rows), :], xsu, gsem).wait()

    def scatter_start(block, r):
        d = dest_sm[block * rows + r]
        pltpu.make_async_copy(ysc.at[pl.ds(r, 1), :], r_hbm.at[pl.ds(d, 1), :], ssem).start()

    def scatter_wait():
        pltpu.make_async_copy(ysc, r_hbm.at[pl.ds(0, rows), :], ssem).wait()

    @pl.when(jnp.logical_and(e == 0, f == 0))
    def _prologue():
        def issue(r, c):
            gather_start(0, r)
            return c
        lax.fori_loop(0, rows, issue, 0)
        ysc[...] = jnp.zeros_like(ysc)

    @pl.when(f == 0)
    def _rows_ready():
        gather_wait()
        half = xsu.shape[1]
        xsb[:, :half], xsb[:, half:] = _unpack_bf16_pairs(xsu[...])
        yacc[...] = jnp.zeros_like(yacc)

    for r in range(sub):
        gather_start(e + 1, f * sub + r)
        scatter_start(e, f * sub + r)

    wb1[...] = w1_ref[0, 0].astype(BF16)
    for r in range(rows // rt):
        xs = xsb[r * rt:(r + 1) * rt, :]
        a = _dot(xs, wb1[...])
        if r == 0:
            wb3[...] = w3_ref[0, 0].astype(BF16)
        u = _dot(xs, wb3[...])
        hmid = (a * jax.nn.sigmoid(a) * u).astype(BF16)
        if r == 0:
            wb2[...] = w2_ref[0, 0].astype(BF16)
        yacc[r * rt:(r + 1) * rt, :] += _dot(hmid, wb2[...])

    @pl.when(f == nf - 1)
    def _finish():
        scatter_wait()
        ri = lax.broadcasted_iota(I32, (cap, cap), 0)
        ci = lax.broadcasted_iota(I32, (cap, cap), 1)
        eye = (ri == ci).astype(BF16)
        for b in range(batch):
            gt = _dot_nt(eye, cmp_ref[b, 0].astype(BF16))
            g = gt[:, 4:5] + gt[:, 5:6] + gt[:, 6:7]
            ysc[b * cap:(b + 1) * cap, :] = _pack_bf16_pairs(yacc[b * cap:(b + 1) * cap, :] * g)

        @pl.when(e == ne - 1)
        def _epilogue():
            def issue(r, c):
                scatter_start(ne, r)
                return c
            lax.fori_loop(0, rows, issue, 0)
            scatter_wait()
            gather_wait()


def expert_ffn(hn, cmp, idx_flat, dest_flat, w1, w3, w2, layer, cap, tf=512, rt=512):
    d = w1.shape[2]
    batch, ne = cmp.shape[0], cmp.shape[1]
    dff = w1.shape[3]
    nf = dff // tf
    rows = batch * cap
    rt = min(rt, rows)
    grid_spec = pltpu.PrefetchScalarGridSpec(
        num_scalar_prefetch=2,
        grid=(ne, nf),
        in_specs=[pl.BlockSpec(memory_space=pl.ANY),
                  pl.BlockSpec((batch, 1, 8, cap), lambda e, f, *_: (0, e, 0, 0)),
                  pl.BlockSpec((1, 1, d, tf), lambda e, f, *_: (layer, e, 0, f)),
                  pl.BlockSpec((1, 1, d, tf), lambda e, f, *_: (layer, e, 0, f)),
                  pl.BlockSpec((1, 1, tf, d), lambda e, f, *_: (layer, e, f, 0))],
        out_specs=pl.BlockSpec(memory_space=pl.ANY),
        scratch_shapes=[pltpu.VMEM((rows, d // 2), U32), pltpu.VMEM((rows, d), BF16),
                        pltpu.VMEM((rows, d), F32), pltpu.VMEM((rows, d // 2), U32),
                        pltpu.VMEM((d, tf), BF16), pltpu.VMEM((d, tf), BF16), pltpu.VMEM((tf, d), BF16),
                        pltpu.SemaphoreType.DMA, pltpu.SemaphoreType.DMA])
    return pl.pallas_call(
        functools.partial(_ffn_kernel, batch=batch, cap=cap, nf=nf, rt=rt),
        grid_spec=grid_spec,
        out_shape=jax.ShapeDtypeStruct((batch * ne * cap + rows, d // 2), U32),
        compiler_params=_cparams(2),
    )(idx_flat, dest_flat, hn, cmp, w1, w3, w2)


def _combine_kernel(tlo_sm, thi_sm, r_ref, offc_ref, x_hbm, o_hbm, acc, xsem, osem, *, batch, nchunk, rchunk):
    g = pl.program_id(0)
    b = g // nchunk
    j = g % nchunk
    slot = b % 2
    ntile = acc.shape[1] // LANE

    def x_copy(bb):
        return pltpu.make_async_copy(x_hbm.at[bb], acc.at[bb % 2], xsem.at[bb % 2])

    def o_copy(bb):
        return pltpu.make_async_copy(acc.at[bb % 2], o_hbm.at[bb], osem.at[bb % 2])

    @pl.when(g == 0)
    def _first():
        x_copy(0).start()

    @pl.when(j == 0)
    def _batch_start():
        x_copy(b).wait()

    @pl.when(j == nchunk // 2)
    def _mid():
        @pl.when(b > 0)
        def _():
            o_copy(b - 1).wait()

        @pl.when(b + 1 < batch)
        def _():
            x_copy(b + 1).start()

    half = r_ref.shape[1]
    left, right = _unpack_bf16_pairs(r_ref[...])
    rowid = (j * rchunk + lax.broadcasted_iota(I32, (1, rchunk), 1)).astype(F32)

    def contrib(i, valid):
        t0 = pl.multiple_of(i * LANE, LANE)
        oc = offc_ref[0, pl.ds(t0, LANE), :]
        off = oc[:, 0:1] * LANE + oc[:, 1:2]
        end = oc[:, 2:3] * LANE + oc[:, 3:4]
        p = jnp.logical_and(jnp.logical_and(rowid >= off, rowid < end), valid).astype(BF16)
        return t0, (_dot(p, left), _dot(p, right))

    def add(t0, y):
        acc[slot, pl.ds(t0, LANE), :half] += y[0]
        acc[slot, pl.ds(t0, LANE), half:] += y[1]

    def tile(i, c):
        add(*contrib(i, True))
        return c

    tlo = tlo_sm[g]
    thi = thi_sm[g]
    parts = [contrib(jnp.minimum(tlo + k, ntile - 1), tlo + k <= thi) for k in range(COMBINE_TILES)]
    for t0, y in parts:
        add(t0, y)
    lax.fori_loop(tlo + COMBINE_TILES, thi + 1, tile, 0)

    @pl.when(j == nchunk - 1)
    def _batch_end():
        o_copy(b).start()

        @pl.when(b == batch - 1)
        def _():
            o_copy(b).wait()


def combine(rbuf, offc, tlo, thi, x, rows_per_batch, rchunk):
    batch, t, d = x.shape
    nchunk = rows_per_batch // rchunk
    assert nchunk >= 2
    grid_spec = pltpu.PrefetchScalarGridSpec(
        num_scalar_prefetch=2,
        grid=(batch * nchunk,),
        in_specs=[pl.BlockSpec((rchunk, d // 2), lambda g, *_: (g, 0)),
                  pl.BlockSpec((1, t, 8), lambda g, *_: (g // nchunk, 0, 0)),
                  pl.BlockSpec(memory_space=pl.ANY)],
        out_specs=pl.BlockSpec(memory_space=pl.ANY),
        scratch_shapes=[pltpu.VMEM((2, t, d), F32), pltpu.SemaphoreType.DMA((2,)), pltpu.SemaphoreType.DMA((2,))])
    return pl.pallas_call(
        functools.partial(_combine_kernel, batch=batch, nchunk=nchunk, rchunk=rchunk),
        grid_spec=grid_spec,
        out_shape=jax.ShapeDtypeStruct((batch, t, d), F32),
        compiler_params=_cparams(1),
    )(tlo, thi, rbuf, offc, x)


def moe_block(x1, hn, aff, w1, w3, w2, layer, batch):
    n, d = x1.shape
    t = n // batch
    ne = aff.shape[1]
    cap = CAPACITY_FACTOR * t // ne
    rows = batch * cap
    cmp, offc, rng = route_select(aff, cap, COMBINE_ROWS)
    ci = cmp.astype(I32)
    boff = jnp.arange(batch, dtype=I32)[:, None, None]
    idx = (ci[:, :, 0] * 64 + ci[:, :, 1] + boff * t).transpose(1, 0, 2).reshape(-1)
    dest = (ci[:, :, 2] * LANE + ci[:, :, 3] + boff * (ne * cap)).transpose(1, 0, 2).reshape(-1)
    idx = jnp.concatenate([idx, idx[:rows]])
    dest = jnp.concatenate([batch * ne * cap + jnp.arange(rows, dtype=I32), dest])
    rbuf = expert_ffn(hn, cmp, idx, dest, w1, w3, w2, layer, cap)
    tlo = rng[:, :, 0].reshape(-1)
    thi = rng[:, :, LANE - 1].reshape(-1)
    return combine(rbuf, offc, tlo, thi, x1.reshape(batch, t, d), ne * cap, COMBINE_ROWS)


MLSTM_L = 256


def _log_sigmoid(x):
    return jnp.minimum(x, 0.0) - jnp.log1p(jnp.exp(-jnp.abs(x)))


def _split3(x):
    x0 = x.astype(BF16)
    r = x - x0.astype(F32)
    x1 = r.astype(BF16)
    return x0, x1, (r - x1.astype(F32)).astype(BF16)


def _mlstm_proj_kernel(x_ref, g_ref, wm_ref, wkt_ref, wgt_ref, bt_ref, main_ref, kt_ref, gt_ref, *, nsub, nh):
    ln = MLSTM_L
    h = _rms(x_ref[...], g_ref[...]).astype(BF16)
    main_ref[...] = _dot(h, wm_ref[...]).astype(main_ref.dtype)
    kt = _dot_nt(wkt_ref[...], h).astype(kt_ref.dtype)
    pre = _dot_nt(wgt_ref[...], h) + bt_ref[...]
    lst = _log_sigmoid(pre)
    ri = lax.broadcasted_iota(I32, (ln, ln), 0)
    ci = lax.broadcasted_iota(I32, (ln, ln), 1)
    low = (ci <= ri).astype(BF16)
    upp = (ci >= ri).astype(BF16)
    row = lax.broadcasted_iota(I32, (pre.shape[0], ln), 0)
    fwd = jnp.logical_and(row >= nh, row < 2 * nh)
    bwd = row >= 3 * nh
    for j in range(nsub):
        sl = slice(j * ln, (j + 1) * ln)
        kt_ref[j] = kt[:, sl]
        pt = _split3(lst[:, sl])
        gt_ref[j] = jnp.where(fwd, sum(_dot(x, upp) for x in pt),
                              jnp.where(bwd, sum(_dot(x, low) for x in pt), pre[:, sl]))


def mlstm_project(x, g, wm, wkt, wgt, bias_t, nh, tm=PROJ_ROWS):
    n, d = x.shape
    nsub = tm // MLSTM_L
    nck = n // MLSTM_L
    const = lambda i: (0, 0)
    return pl.pallas_call(
        functools.partial(_mlstm_proj_kernel, nsub=nsub, nh=nh),
        grid=(n // tm,),
        in_specs=[pl.BlockSpec((tm, d), lambda i: (i, 0)),
                  pl.BlockSpec((1, d), const),
                  pl.BlockSpec(wm.shape, const),
                  pl.BlockSpec(wkt.shape, const),
                  pl.BlockSpec(wgt.shape, const),
                  pl.BlockSpec(bias_t.shape, const)],
        out_specs=[pl.BlockSpec((tm, wm.shape[1]), lambda i: (i, 0)),
                   pl.BlockSpec((nsub, wkt.shape[0], MLSTM_L), lambda i: (i, 0, 0)),
                   pl.BlockSpec((nsub, wgt.shape[0], MLSTM_L), lambda i: (i, 0, 0))],
        out_shape=[jax.ShapeDtypeStruct((n, wm.shape[1]), BF16),
                   jax.ShapeDtypeStruct((nck, wkt.shape[0], MLSTM_L), BF16),
                   jax.ShapeDtypeStruct((nck, wgt.shape[0], MLSTM_L), F32)],
        compiler_params=_cparams(1),
    )(x, g.reshape(1, d), wm, wkt, wgt, bias_t)


def _mlstm_kernel(q_ref, kt_ref, v_ref, og_ref, gt_ref, ng_ref, y_ref, hf, hb, cst, b_sc, u_sc, cm_sc,
                  *, nc, nh, dqk, dv):
    p = pl.program_id(1)
    ln = MLSTM_L
    ri = lax.broadcasted_iota(I32, (ln, ln), 0)
    ci = lax.broadcasted_iota(I32, (ln, ln), 1)
    masks = (ci <= ri, ci >= ri)
    ones_blk = jnp.ones((ln, dv), BF16)
    lane8 = lax.broadcasted_iota(I32, (8, ln), 1)
    row8 = lax.broadcasted_iota(I32, (8, ln), 0)
    bwd_row = (row8 % 2) == 1
    edge = lane8 == jnp.where(bwd_row, 0, ln - 1)
    kk = lax.broadcasted_iota(I32, (48, 2 * dv), 0) % 16
    cc = lax.broadcasted_iota(I32, (48, 2 * dv), 1)
    sel = [jnp.logical_or(jnp.logical_and(kk == i, cc < dv), jnp.logical_and(kk == 4 + i, cc >= dv)).astype(BF16)
           for i in range(4)]
    zero4 = jnp.zeros((4, ln), F32)
    cst[...] = jnp.zeros_like(cst)
    chains = [(hh, d) for hh in range(2) for d in range(2)]

    def gate_rows(c, which):
        return [gt_ref[c if d == 0 else nc - 1 - c, pl.ds((2 * d + which) * nh + 2 * p + hh, 1), :]
                for hh, d in chains] + [zero4]
    b_all = jnp.concatenate([x for c in range(nc) for x in gate_rows(c, 1)], axis=0)
    u_all = jnp.concatenate([x for c in range(nc) for x in gate_rows(c, 0)], axis=0) - b_all
    lane_a = lax.broadcasted_iota(I32, u_all.shape, 1)
    bwd_a = (lax.broadcasted_iota(I32, u_all.shape, 0) % 2) == 1
    cf = cb = u_all
    sh = 1
    while sh < ln:
        cf = jnp.maximum(cf, jnp.where(lane_a >= sh, pltpu.roll(cf, sh, axis=1), NEG))
        cb = jnp.maximum(cb, jnp.where(lane_a < ln - sh, pltpu.roll(cb, ln - sh, axis=1), NEG))
        sh *= 2
    b_sc[...] = b_all.reshape(nc, 8, ln)
    u_sc[...] = u_all.reshape(nc, 8, ln)
    cm_sc[...] = jnp.where(bwd_a, cb, cf).reshape(nc, 8, ln)

    def step(c, ms):
        cks = [c if d == 0 else nc - 1 - c for _, d in chains]
        r0s = [pl.multiple_of(ck * ln, ln) for ck in cks]
        cprev = [cst[i] for i in range(4)]
        q = [q_ref[pl.ds(r0s[i], ln), hh * dqk:(hh + 1) * dqk] for i, (hh, _) in enumerate(chains)]
        kt = [kt_ref[cks[i], hh * dqk:(hh + 1) * dqk, :] for i, (hh, _) in enumerate(chains)]
        vaug = [jnp.concatenate([v_ref[pl.ds(r0s[i], ln), hh * dv:(hh + 1) * dv], ones_blk], axis=1)
                for i, (hh, _) in enumerate(chains)]
        s = [_dot(q[i], kt[i]) for i in range(4)]
        inter = [_dot(q[i], cprev[i].astype(BF16)) for i in range(4)]
        b = b_sc[c]
        u = u_sc[c]
        m_run = jnp.maximum(ms, cm_sc[c])
        m_end = jnp.max(jnp.where(edge, m_run, NEG), axis=1, keepdims=True)
        b_end = jnp.sum(jnp.where(edge, b, 0.0), axis=1, keepdims=True)
        wc = jnp.exp(u - m_end)
        decay = jnp.exp(ms - m_end)
        rows = jnp.concatenate([m_run[:4] * LOG2E, jnp.exp(-(b + m_run))[:4], zero4, zero4], axis=0)
        stack = jnp.concatenate(_split3(rows), axis=0)
        u2 = u * LOG2E
        ms2 = ms * LOG2E
        bc = [lax.dot_general(stack, sel[i], (((0,), (0,)), ((), ())), preferred_element_type=F32)
              for i in range(4)]
        mb = [jnp.concatenate([bc[i][:, :dv]] * (ln // dv), axis=1) for i in range(4)]
        sw = [(jnp.exp2(jnp.where(masks[d], u2[i:i + 1, :] - mb[i], NEG)) * s[i]).astype(BF16)
              for i, (_, d) in enumerate(chains)]
        intra = [_dot(sw[i], vaug[i]) for i in range(4)]
        ea = [jnp.exp2(ms2[i:i + 1, :] - bc[i][:, :dv]) for i in range(4)]
        num = [ea[i] * inter[i][:, :dv] + intra[i][:, :dv] for i in range(4)]
        den = [ea[i] * inter[i][:, dv:] + intra[i][:, dv:] for i in range(4)]
        hc = [num[i] / jnp.maximum(jnp.abs(den[i]), bc[i][:, dv:]) for i in range(4)]
        kw = [(kt[i].astype(F32) * wc[i:i + 1, :]).astype(BF16) for i in range(4)]
        c_new = [decay[i:i + 1, :] * cprev[i] + _dot(kw[i], vaug[i]) for i in range(4)]
        for i, (hh, d) in enumerate(chains):
            cst[i] = c_new[i]
            (hf if d == 0 else hb)[pl.ds(r0s[i], ln), hh * dv:(hh + 1) * dv] = hc[i]
        return b_end + m_end

    lax.fori_loop(0, nc, step, jnp.zeros((8, 1), F32))
    for hh in range(2):
        sl = slice(hh * dv, (hh + 1) * dv)
        hs = _rms(hf[:, sl] + hb[:, sl], ng_ref[:, sl])
        y_ref[:, sl] = (hs * jax.nn.sigmoid(og_ref[:, sl].astype(F32))).astype(y_ref.dtype)


def mlstm_core(main, ktc, grow, out_g, batch, seq, nh, dqk, dv):
    n = main.shape[0]
    nc = seq // MLSTM_L
    npair = nh // 2
    vblocks = (nh * dqk) // (2 * dv)
    ogblocks = (nh * dqk + nh * dv) // (2 * dv)
    return pl.pallas_call(
        functools.partial(_mlstm_kernel, nc=nc, nh=nh, dqk=dqk, dv=dv),
        grid=(batch, npair),
        in_specs=[pl.BlockSpec((seq, 2 * dqk), lambda b, p: (b, p)),
                  pl.BlockSpec((nc, 2 * dqk, MLSTM_L), lambda b, p: (b, p, 0)),
                  pl.BlockSpec((seq, 2 * dv), lambda b, p: (b, vblocks + p)),
                  pl.BlockSpec((seq, 2 * dv), lambda b, p: (b, ogblocks + p)),
                  pl.BlockSpec((nc, 4 * nh, MLSTM_L), lambda b, p: (b, 0, 0)),
                  pl.BlockSpec((1, 2 * dv), lambda b, p: (0, p))],
        out_specs=pl.BlockSpec((seq, 2 * dv), lambda b, p: (b, p)),
        out_shape=jax.ShapeDtypeStruct((n, nh * dv), BF16),
        scratch_shapes=[pltpu.VMEM((seq, 2 * dv), F32), pltpu.VMEM((seq, 2 * dv), F32),
                        pltpu.VMEM((4, dqk, 2 * dv), F32)] + [pltpu.VMEM((nc, 8, MLSTM_L), F32)] * 3,
        compiler_params=_cparams(2),
    )(main, ktc, main, main, grow, out_g.reshape(1, nh * dv).astype(F32))


def mlstm_layer(x2d, batch, seq, norm_g, w_in, b_i, b_f, out_g, w_out, ffn_g, wr):
    d = x2d.shape[1]
    nh = MLSTM_HEADS
    dv = d // nh
    dqk = dv // 2
    o1, o2, o3, o4 = nh * dqk, 2 * nh * dqk, 2 * nh * dqk + nh * dv, 2 * nh * dqk + 2 * nh * dv
    wm = jnp.concatenate([w_in[:, :o1], w_in[:, o2:o4]], axis=1).astype(BF16)
    wkt = (w_in[:, o1:o2] * (dqk ** -0.5)).T.astype(BF16)
    wgt = w_in[:, o4:].T.astype(BF16)
    bias_t = jnp.concatenate([b_i[0], b_f[0], b_i[1], b_f[1]]).reshape(4 * nh, 1).astype(F32)
    main, ktc, grow = mlstm_project(x2d, norm_g, wm, wkt, wgt, bias_t, nh)
    y = mlstm_core(main, ktc, grow, out_g, batch, seq, nh, dqk, dv)
    return mm_res_router(y, w_out.astype(BF16), x2d, ffn_g, wr.T.astype(BF16), batch)


def attention_layer(x2d, batch, seq, norm_g, w_in, q_g, k_g, sink, w_out, rel_bias, ffn_g, wr):
    n_q = rel_bias.shape[1]
    q, k, vt = attention_project(x2d, norm_g, w_in, q_g, k_g, n_q, n_q // GQA_GROUP)
    ot = attention_core(q, k, vt, rel_bias, sink, batch, seq)
    return mm_res_router(ot, w_out.astype(BF16), x2d, ffn_g, wr.T.astype(BF16), batch, a_transposed=True)


def kernel(x, rel_bias, attn_norm_g, attn_w_in, attn_q_norm_g, attn_k_norm_g, attn_sink, attn_w_out, mlstm_norm_g, mlstm_w_in, mlstm_b_i, mlstm_b_f, mlstm_out_norm_g, mlstm_w_out, ffn_norm_g, router_w, expert_w1, expert_w3, expert_w2):
    batch, seq, d = x.shape
    x2d = x.reshape(batch * seq, d)
    x1, hn, aff = attention_layer(x2d, batch, seq, attn_norm_g[0], attn_w_in[0], attn_q_norm_g[0],
                                  attn_k_norm_g[0], attn_sink[0], attn_w_out[0], rel_bias,
                                  ffn_norm_g[0], router_w[0])
    x = moe_block(x1, hn, aff, expert_w1, expert_w3, expert_w2, 0, batch)
    x1, hn, aff = mlstm_layer(x.reshape(batch * seq, d), batch, seq, mlstm_norm_g[0], mlstm_w_in[0], mlstm_b_i[0],
                              mlstm_b_f[0], mlstm_out_norm_g[0], mlstm_w_out[0], ffn_norm_g[1], router_w[1])
    return moe_block(x1, hn, aff, expert_w1, expert_w3, expert_w2, 1, batch)
```

```python
import functools
import math

import jax
import jax.numpy as jnp
from jax import lax
from jax.experimental import pallas as pl
from jax.experimental.pallas import tpu as pltpu

F32 = jnp.float32
BF16 = jnp.bfloat16
I32 = jnp.int32
U32 = jnp.uint32

RMS_EPS = 1e-6
NEG = -1e30
LOG2E = 1.4426950408889634
LANE = 128
MXU_DIM = 256
VMEM_LIMIT = 56 * 1024 * 1024
PROJ_ROWS = 1024

HEAD_DIM = 64
GQA_GROUP = 4
ATT_BLOCK = 128
NUM_BUCKETS = 32
MAX_DISTANCE = 128
CAPACITY_FACTOR = 2
MLSTM_HEADS = 8


def _cparams(n_axes, vmem=VMEM_LIMIT):
    return pltpu.CompilerParams(dimension_semantics=("arbitrary",) * n_axes, vmem_limit_bytes=vmem)


def _rms(x, g):
    return x * lax.rsqrt(jnp.mean(x * x, axis=-1, keepdims=True) + RMS_EPS) * g


def _dot(a, b):
    return jnp.dot(a, b, preferred_element_type=F32)


def _pack_bf16_pairs(x):
    half = x.shape[1] // 2
    hi = pltpu.bitcast(x[:, :half].astype(BF16).astype(F32), U32)
    lo = pltpu.bitcast(x[:, half:].astype(BF16).astype(F32), U32)
    return hi | lax.shift_right_logical(lo, jnp.uint32(16))


def _unpack_bf16_pairs(w):
    left = pltpu.bitcast(w & jnp.uint32(0xFFFF0000), F32).astype(BF16)
    right = pltpu.bitcast(lax.shift_left(w, jnp.uint32(16)), F32).astype(BF16)
    return left, right


def _dot_nt(a, b):
    return lax.dot_general(a, b, (((1,), (1,)), ((), ())), preferred_element_type=F32)


def _attn_proj_kernel(x_ref, g_ref, wq_ref, wk_ref, wvt_ref, qg_ref, kg_ref, seg_ref, q_ref, k_ref, vt_ref):
    h = _rms(x_ref[...], g_ref[...]).astype(BF16)
    seg = seg_ref[...]
    w = seg.shape[0]

    def head_norm(t, gain_ref, out_ref):
        for j in range(t.shape[1] // w):
            tj = t[:, j * w:(j + 1) * w]
            ms = _dot((tj * tj).astype(BF16), seg)
            out_ref[:, j * w:(j + 1) * w] = (tj * lax.rsqrt(ms + RMS_EPS) * gain_ref[:, j * w:(j + 1) * w]
                                             ).astype(out_ref.dtype)

    head_norm(_dot(h, wq_ref[...]), qg_ref, q_ref)
    head_norm(_dot(h, wk_ref[...]), kg_ref, k_ref)
    vt_ref[...] = _dot_nt(wvt_ref[...], h).astype(vt_ref.dtype)


def attention_project(x, g, w_in, q_g, k_g, n_q, n_kv, tm=PROJ_ROWS):
    n, d = x.shape
    dh = HEAD_DIM
    dq, dk = n_q * dh, n_kv * dh
    wq = w_in[:, :dq].astype(BF16)
    wk = w_in[:, dq:dq + dk].astype(BF16)
    wvt = w_in[:, dq + dk:].T.astype(BF16)
    qg = jnp.tile(q_g.astype(F32) * (dh ** -0.5 * LOG2E), n_q).reshape(1, dq)
    kg = jnp.tile(k_g.astype(F32), n_kv).reshape(1, dk)
    assert dq % MXU_DIM == 0 and dk % MXU_DIM == 0 and MXU_DIM % dh == 0
    hid = jnp.arange(MXU_DIM) // dh
    seg = jnp.where(hid[:, None] == hid[None, :], 1.0 / dh, 0.0).astype(BF16)
    const2 = lambda i: (0, 0)
    return pl.pallas_call(
        _attn_proj_kernel,
        grid=(n // tm,),
        in_specs=[pl.BlockSpec((tm, d), lambda i: (i, 0)),
                  pl.BlockSpec((1, d), const2),
                  pl.BlockSpec((d, dq), const2),
                  pl.BlockSpec((d, dk), const2),
                  pl.BlockSpec((dk, d), const2),
                  pl.BlockSpec((1, dq), const2),
                  pl.BlockSpec((1, dk), const2),
                  pl.BlockSpec((MXU_DIM, MXU_DIM), const2)],
        out_specs=[pl.BlockSpec((tm, dq), lambda i: (i, 0)),
                   pl.BlockSpec((tm, dk), lambda i: (i, 0)),
                   pl.BlockSpec((dk, tm), lambda i: (0, i))],
        out_shape=[jax.ShapeDtypeStruct((n, dq), BF16),
                   jax.ShapeDtypeStruct((n, dk), BF16),
                   jax.ShapeDtypeStruct((dk, n), BF16)],
        compiler_params=_cparams(1),
    )(x, g.reshape(1, d), wq, wk, wvt, qg, kg, seg)


def _mm_res_router_kernel(a_ref, w_ref, x_ref, g_ref, wr_ref, x1_ref, hn_ref, aff_ref, *, a_transposed):
    if a_transposed:
        y = lax.dot_general(a_ref[...], w_ref[...], (((0,), (0,)), ((), ())), preferred_element_type=F32)
    else:
        y = _dot(a_ref[...], w_ref[...])
    x1 = x_ref[...] + y
    x1_ref[...] = x1
    hn = _rms(x1, g_ref[...])
    hn_ref[...] = _pack_bf16_pairs(hn)
    logits = _dot_nt(wr_ref[...], hn.astype(BF16))
    mx = jnp.max(logits, axis=0, keepdims=True)
    p = jnp.exp(logits - mx)
    aff_ref[0] = p / jnp.sum(p, axis=0, keepdims=True)


def mm_res_router(a, w, x, g, wr_t, batch, a_transposed=False, tm=PROJ_ROWS):
    n, d = x.shape
    k = w.shape[0]
    e = wr_t.shape[0]
    t = n // batch
    tm = min(tm, t)
    tpb = t // tm
    a_spec = pl.BlockSpec((k, tm), lambda i: (0, i)) if a_transposed else pl.BlockSpec((tm, k), lambda i: (i, 0))
    return pl.pallas_call(
        functools.partial(_mm_res_router_kernel, a_transposed=a_transposed),
        grid=(n // tm,),
        in_specs=[a_spec,
                  pl.BlockSpec((k, d), lambda i: (0, 0)),
                  pl.BlockSpec((tm, d), lambda i: (i, 0)),
                  pl.BlockSpec((1, d), lambda i: (0, 0)),
                  pl.BlockSpec((e, d), lambda i: (0, 0))],
        out_specs=[pl.BlockSpec((tm, d), lambda i: (i, 0)),
                   pl.BlockSpec((tm, d // 2), lambda i: (i, 0)),
                   pl.BlockSpec((1, e, tm), lambda i: (i // tpb, 0, i % tpb))],
        out_shape=[jax.ShapeDtypeStruct((n, d), F32),
                   jax.ShapeDtypeStruct((n, d // 2), U32),
                   jax.ShapeDtypeStruct((batch, e, t), F32)],
        compiler_params=_cparams(1),
    )(a, w, x, g.reshape(1, d), wr_t)


def _t5_bucket(rel):
    nb = NUM_BUCKETS // 2
    ret = (rel > 0).astype(jnp.int32) * nb
    n = jnp.abs(rel)
    max_exact = nb // 2
    nf = jnp.maximum(n, 1).astype(jnp.float32)
    large = max_exact + (jnp.log(nf / max_exact) / math.log(MAX_DISTANCE / max_exact)
                         * (nb - max_exact)).astype(jnp.int32)
    large = jnp.minimum(large, nb - 1)
    return ret + jnp.where(n < max_exact, n, large)


def _attn_bucket_table():
    kk = jnp.arange(3 * ATT_BLOCK)[:, None]
    qq = jnp.arange(ATT_BLOCK)[None, :]
    rel = kk - ATT_BLOCK - qq
    return jnp.where(jnp.abs(rel) <= ATT_BLOCK, _t5_bucket(rel), -1).astype(I32)


ATT_QBLOCKS = 2


def _attn_kernel(sink_ref, rb_ref, q_ref, *rest, nb, n_kv):
    nq = ATT_QBLOCKS
    k_refs = rest[:nq + 2]
    v_refs = rest[nq + 2:2 * (nq + 2)]
    bucket_ref, ot_ref, bias_sc = rest[2 * (nq + 2):]
    n = pl.program_id(1)
    blk = ATT_BLOCK
    dh = HEAD_DIM
    gw = GQA_GROUP * blk

    @pl.when(jnp.logical_and(pl.program_id(0) == 0, n == 0))
    def _bias_table():
        bk = bucket_ref[...]
        for hq in range(n_kv * GQA_GROUP):
            acc = jnp.full(bk.shape, NEG, F32)
            for k in range(NUM_BUCKETS):
                acc = jnp.where(bk == k, rb_ref[k, hq] * LOG2E, acc)
            bias_sc[hq // GQA_GROUP, :, (hq % GQA_GROUP) * blk:(hq % GQA_GROUP + 1) * blk] = acc

    kblk = [r[...] for r in k_refs]
    vblk = [r[...] for r in v_refs]
    kidx = lax.broadcasted_iota(I32, (3 * blk, gw), 0)
    ones_rows = (lax.broadcasted_iota(I32, (16, 3 * blk), 0) == 0).astype(BF16)
    lane = lax.broadcasted_iota(I32, (1, gw), 1)
    heads = [[h * GQA_GROUP + g for g in range(GQA_GROUP)] for h in range(n_kv)]
    sk = []
    for h in range(n_kv):
        row_sink = jnp.full((1, gw), sink_ref[heads[h][-1]] * LOG2E, F32)
        for g in reversed(range(GQA_GROUP - 1)):
            row_sink = jnp.where(lane < (g + 1) * blk, sink_ref[heads[h][g]] * LOG2E, row_sink)
        sk.append(row_sink)
    keys, vt, valid = [], [], []
    for j in range(nq):
        i = n * nq + j
        keys.append(jnp.concatenate(kblk[j:j + 3], axis=0))
        vt.append(jnp.concatenate(vblk[j:j + 3], axis=1))
        valid.append(jnp.logical_and(jnp.logical_or(i > 0, kidx >= blk),
                                     jnp.logical_or(i < nb - 1, kidx < 2 * blk)))
    units = [(j, h) for j in range(nq) for h in range(n_kv)]
    q = [jnp.concatenate([q_ref[j * blk:(j + 1) * blk, hq * dh:(hq + 1) * dh] for hq in heads[h]], axis=0)
         for j, h in units]
    vaug = [jnp.concatenate([vt[j][h * dh:(h + 1) * dh, :], ones_rows], axis=0) for j, h in units]
    s = [jnp.where(valid[j], _dot_nt(keys[j][:, h * dh:(h + 1) * dh], q[u]) + bias_sc[h], NEG)
         for u, (j, h) in enumerate(units)]
    m = [jnp.maximum(jnp.max(s[u], axis=0, keepdims=True), sk[h]) for u, (j, h) in enumerate(units)]
    p = [jnp.exp2(s[u] - m[u]).astype(BF16) for u in range(len(units))]
    oa = [_dot(vaug[u], p[u]) for u in range(len(units))]
    o = [oa[u][:dh] / (oa[u][dh:dh + 1] + jnp.exp2(sk[h] - m[u])) for u, (j, h) in enumerate(units)]
    for u, (j, h) in enumerate(units):
        for g, hq in enumerate(heads[h]):
            ot_ref[hq * dh:(hq + 1) * dh, j * blk:(j + 1) * blk] = o[u][:, g * blk:(g + 1) * blk].astype(ot_ref.dtype)


def attention_core(q, k, vt, rel_bias, sink, batch, seq):
    n, dq = q.shape
    dk = k.shape[1]
    hq = rel_bias.shape[1]
    n_kv = hq // GQA_GROUP
    nb = seq // ATT_BLOCK
    blk = ATT_BLOCK
    nq = ATT_QBLOCKS
    assert nb % nq == 0

    def key_block(off):
        return lambda b, i: b * nb + jnp.clip(i * nq + off, 0, nb - 1)

    offs = range(-1, nq + 1)
    return pl.pallas_call(
        functools.partial(_attn_kernel, nb=nb, n_kv=n_kv),
        grid=(batch, nb // nq),
        in_specs=[pl.BlockSpec(memory_space=pltpu.SMEM),
                  pl.BlockSpec(memory_space=pltpu.SMEM),
                  pl.BlockSpec((nq * blk, dq), lambda b, i: (b * (nb // nq) + i, 0))]
                 + [pl.BlockSpec((blk, dk), (lambda f: lambda b, i: (f(b, i), 0))(key_block(o))) for o in offs]
                 + [pl.BlockSpec((dk, blk), (lambda f: lambda b, i: (0, f(b, i)))(key_block(o))) for o in offs]
                 + [pl.BlockSpec((3 * blk, blk), lambda b, i: (0, 0))],
        out_specs=pl.BlockSpec((dq, nq * blk), lambda b, i: (0, b * (nb // nq) + i)),
        out_shape=jax.ShapeDtypeStruct((dq, n), BF16),
        scratch_shapes=[pltpu.VMEM((n_kv, 3 * blk, GQA_GROUP * blk), F32)],
        compiler_params=_cparams(2),
    )(sink.astype(F32), rel_bias.astype(F32), q, *([k] * (nq + 2)), *([vt] * (nq + 2)), _attn_bucket_table())


def _prefix_incl(x, tri):
    r, t = x.shape
    nck = t // LANE
    assert nck <= LANE
    xb = x.astype(BF16)
    local = [_dot(xb[:, c * LANE:(c + 1) * LANE], tri) for c in range(nck)]
    tot = jnp.concatenate([p[:, LANE - 1:LANE] for p in local] + [jnp.zeros((r, LANE - nck), F32)], axis=1)
    lane = lax.broadcasted_iota(I32, tot.shape, 1)
    inc = tot
    sh = 1
    while sh < nck:
        inc = inc + jnp.where(lane >= sh, pltpu.roll(inc, sh, axis=1), 0.0)
        sh *= 2
    offs = inc - tot
    return jnp.concatenate([local[c] + offs[:, c:c + 1] for c in range(nck)], axis=1)


BF16_EXACT = 256
TOK_SPLIT = 64


def _select_kernel(aff_ref, cmp_ref, offc_ref, rng_ref, pos_sc, dest_sc, vals_sc, *, cap, rchunk):
    aff = aff_ref[0]
    ne, t = aff.shape
    bits = pltpu.bitcast(aff, I32)

    def search(i, lo):
        cand = lo | lax.shift_left(jnp.int32(1), 30 - i)
        cnt = jnp.sum((bits >= cand).astype(I32), axis=1, keepdims=True)
        return jnp.where(cnt >= cap, cand, lo)

    thr = lax.fori_loop(0, 31, search, jnp.zeros((ne, 1), I32))
    gt = bits > thr
    eq = bits == thr
    need = (cap - jnp.sum(gt.astype(I32), axis=1, keepdims=True)).astype(F32)
    ri = lax.broadcasted_iota(I32, (LANE, LANE), 0)
    ci = lax.broadcasted_iota(I32, (LANE, LANE), 1)
    tri = (ri <= ci).astype(BF16)
    eqf = eq.astype(F32)
    rank_eq = _prefix_incl(eqf, tri) - eqf
    sel = jnp.logical_or(gt, jnp.logical_and(eq, rank_eq < need))
    self_ = sel.astype(F32)
    kt = jnp.sum(self_, axis=0, keepdims=True)
    pre = _prefix_incl(jnp.concatenate([self_, jnp.broadcast_to(kt, (8, t))], axis=0), tri)
    pos_sc[...] = jnp.where(sel, pre[:ne] - 1.0, -1.0)
    end = pre[ne:ne + 1]
    off = end - kt
    lr = lax.broadcasted_iota(I32, (ne, ne), 0)
    lc = lax.broadcasted_iota(I32, (ne, ne), 1)
    jexp = _dot((lc < lr).astype(BF16), self_.astype(BF16))
    dest_sc[...] = off + jexp

    tok = lax.broadcasted_iota(I32, (1, t), 1)
    vals_sc[0:1, :] = (tok // TOK_SPLIT).astype(F32)
    vals_sc[1:2, :] = (tok % TOK_SPLIT).astype(F32)
    vals_sc[7:8, :] = jnp.zeros((1, t), F32)
    slot = lax.broadcasted_iota(I32, (cap, 1), 0).astype(F32)

    def compact(e, carry):
        d = dest_sc[pl.ds(e, 1), :]
        dh = jnp.floor(d * (1.0 / LANE))
        g = aff_ref[0, pl.ds(e, 1), :]
        g0 = g.astype(BF16).astype(F32)
        g1 = (g - g0).astype(BF16).astype(F32)
        vals_sc[2:3, :] = dh
        vals_sc[3:4, :] = d - dh * LANE
        vals_sc[4:5, :] = g0
        vals_sc[5:6, :] = g1
        vals_sc[6:7, :] = g - g0 - g1
        onehot = (pos_sc[pl.ds(e, 1), :] == slot).astype(BF16)
        cmp_ref[0, e] = _dot_nt(vals_sc[...].astype(BF16), onehot)
        return carry

    lax.fori_loop(0, ne, compact, 0)

    eh = jnp.floor(end * (1.0 / LANE))
    oh = jnp.floor(off * (1.0 / LANE))
    v4 = jnp.concatenate([oh, off - oh * LANE, eh, end - eh * LANE, jnp.zeros((4, t), F32)], axis=0)
    eye = (ri == ci).astype(BF16)
    for c in range(t // LANE):
        offc_ref[0, c * LANE:(c + 1) * LANE, :] = _dot_nt(eye, v4[:, c * LANE:(c + 1) * LANE].astype(BF16))

    nchunk = (ne * cap) // rchunk
    low = (lax.broadcasted_iota(I32, (nchunk, 1), 0) * rchunk).astype(F32)
    first = jnp.sum((end <= low).astype(F32), axis=1, keepdims=True)
    last = jnp.sum((end <= low + (rchunk - 1)).astype(F32), axis=1, keepdims=True)
    lane = lax.broadcasted_iota(I32, (nchunk, LANE), 1)
    tiles = jnp.where(lane < LANE // 2, jnp.floor(first * (1.0 / LANE)), jnp.floor(last * (1.0 / LANE)))
    rng_ref[0] = tiles.astype(I32)


COMBINE_ROWS = 512
COMBINE_TILES = 3


def route_select(aff, cap, rchunk):
    batch, ne, t = aff.shape
    nchunk = ne * cap // rchunk
    assert t <= TOK_SPLIT * BF16_EXACT and ne * cap <= LANE * BF16_EXACT and LANE <= BF16_EXACT
    return pl.pallas_call(
        functools.partial(_select_kernel, cap=cap, rchunk=rchunk),
        grid=(batch,),
        in_specs=[pl.BlockSpec((1, ne, t), lambda b: (b, 0, 0))],
        out_specs=[pl.BlockSpec((1, ne, 8, cap), lambda b: (b, 0, 0, 0)),
                   pl.BlockSpec((1, t, 8), lambda b: (b, 0, 0)),
                   pl.BlockSpec((1, nchunk, LANE), lambda b: (b, 0, 0))],
        out_shape=[jax.ShapeDtypeStruct((batch, ne, 8, cap), F32),
                   jax.ShapeDtypeStruct((batch, t, 8), F32),
                   jax.ShapeDtypeStruct((batch, nchunk, LANE), I32)],
        scratch_shapes=[pltpu.VMEM((ne, t), F32), pltpu.VMEM((ne, t), F32), pltpu.VMEM((8, t), F32)],
        compiler_params=_cparams(1),
    )(aff)


def _ffn_kernel(idx_sm, dest_sm, hn_hbm, cmp_ref, w1_ref, w3_ref, w2_ref, r_hbm,
                xsu, xsb, yacc, ysc, wb1, wb3, wb2, gsem, ssem, *, batch, cap, nf, rt):
    e = pl.program_id(0)
    f = pl.program_id(1)
    ne = pl.num_programs(0)
    rows = batch * cap
    sub = rows // nf

    def gather_start(block, r):
        t = idx_sm[block * rows + r]
        pltpu.make_async_copy(hn_hbm.at[pl.ds(t, 1), :], xsu.at[pl.ds(r, 1), :], gsem).start()

    def gather_wait():
        pltpu.make_async_copy(hn_hbm.at[pl.ds(0, rows), :], xsu, gsem).wait()

    def scatter_start(block, r):
        d = dest_sm[block * rows + r]
        pltpu.make_async_copy(ysc.at[pl.ds(r, 1), :], r_hbm.at[pl.ds(d, 1), :], ssem).start()

    def scatter_wait():
        pltpu.make_async_copy(ysc, r_hbm.at[pl.ds(0, rows), :], ssem).wait()

    @pl.when(jnp.logical_and(e == 0, f == 0))
    def _prologue():
        def issue(r, c):
            gather_start(0, r)
            return c
        lax.fori_loop(0, rows, issue, 0)
        ysc[...] = jnp.zeros_like(ysc)

    @pl.when(f == 0)
    def _rows_ready():
        gather_wait()
        half = xsu.shape[1]
        xsb[:, :half], xsb[:, half:] = _unpack_bf16_pairs(xsu[...])
        yacc[...] = jnp.zeros_like(yacc)

    for r in range(sub):
        gather_start(e + 1, f * sub + r)
        scatter_start(e, f * sub + r)

    wb1[...] = w1_ref[0, 0].astype(BF16)
    for r in range(rows // rt):
        xs = xsb[r * rt:(r + 1) * rt, :]
        a = _dot(xs, wb1[...])
        if r == 0:
            wb3[...] = w3_ref[0, 0].astype(BF16)
        u = _dot(xs, wb3[...])
        hmid = (a * jax.nn.sigmoid(a) * u).astype(BF16)
        if r == 0:
            wb2[...] = w2_ref[0, 0].astype(BF16)
        yacc[r * rt:(r + 1) * rt, :] += _dot(hmid, wb2[...])

    @pl.when(f == nf - 1)
    def _finish():
        scatter_wait()
        ri = lax.broadcasted_iota(I32, (cap, cap), 0)
        ci = lax.broadcasted_iota(I32, (cap, cap), 1)
        eye = (ri == ci).astype(BF16)
        rows8 = cmp_ref[...].reshape(batch * 8, cap).astype(BF16)
        gt = _dot_nt(eye, rows8)
        for b in range(batch):
            g = gt[:, 8 * b + 4:8 * b + 5] + gt[:, 8 * b + 5:8 * b + 6] + gt[:, 8 * b + 6:8 * b + 7]
            ysc[b * cap:(b + 1) * cap, :] = _pack_bf16_pairs(yacc[b * cap:(b + 1) * cap, :] * g)

        @pl.when(e == ne - 1)
        def _epilogue():
            def issue(r, c):
                scatter_start(ne, r)
                return c
            lax.fori_loop(0, rows, issue, 0)
            scatter_wait()
            gather_wait()


def expert_ffn(hn, cmp, idx_flat, dest_flat, w1, w3, w2, layer, cap, tf=512, rt=512):
    d = w1.shape[2]
    batch, ne = cmp.shape[0], cmp.shape[1]
    dff = w1.shape[3]
    nf = dff // tf
    rows = batch * cap
    rt = min(rt, rows)
    grid_spec = pltpu.PrefetchScalarGridSpec(
        num_scalar_prefetch=2,
        grid=(ne, nf),
        in_specs=[pl.BlockSpec(memory_space=pl.ANY),
                  pl.BlockSpec((batch, 1, 8, cap), lambda e, f, *_: (0, e, 0, 0)),
                  pl.BlockSpec((1, 1, d, tf), lambda e, f, *_: (layer, e, 0, f)),
                  pl.BlockSpec((1, 1, d, tf), lambda e, f, *_: (layer, e, 0, f)),
                  pl.BlockSpec((1, 1, tf, d), lambda e, f, *_: (layer, e, f, 0))],
        out_specs=pl.BlockSpec(memory_space=pl.ANY),
        scratch_shapes=[pltpu.VMEM((rows, d // 2), U32), pltpu.VMEM((rows, d), BF16),
                        pltpu.VMEM((rows, d), F32), pltpu.VMEM((rows, d // 2), U32),
                        pltpu.VMEM((d, tf), BF16), pltpu.VMEM((d, tf), BF16), pltpu.VMEM((tf, d), BF16),
                        pltpu.SemaphoreType.DMA, pltpu.SemaphoreType.DMA])
    return pl.pallas_call(
        functools.partial(_ffn_kernel, batch=batch, cap=cap, nf=nf, rt=rt),
        grid_spec=grid_spec,
        out_shape=jax.ShapeDtypeStruct((batch * ne * cap + rows, d // 2), U32),
        compiler_params=_cparams(2),
    )(idx_flat, dest_flat, hn, cmp, w1, w3, w2)


def _combine_kernel(tlo_sm, thi_sm, r_ref, offc_ref, x_hbm, o_hbm, acc, xsem, osem, *, batch, nchunk, rchunk):
    g = pl.program_id(0)
    b = g // nchunk
    j = g % nchunk
    slot = b % 2
    ntile = acc.shape[1] // LANE

    def x_copy(bb):
        return pltpu.make_async_copy(x_hbm.at[bb], acc.at[bb % 2], xsem.at[bb % 2])

    def o_copy(bb):
        return pltpu.make_async_copy(acc.at[bb % 2], o_hbm.at[bb], osem.at[bb % 2])

    @pl.when(g == 0)
    def _first():
        x_copy(0).start()

    @pl.when(j == 0)
    def _batch_start():
        x_copy(b).wait()

    @pl.when(j == nchunk // 2)
    def _mid():
        @pl.when(b > 0)
        def _():
            o_copy(b - 1).wait()

        @pl.when(b + 1 < batch)
        def _():
            x_copy(b + 1).start()

    half = r_ref.shape[1]
    left, right = _unpack_bf16_pairs(r_ref[...])
    rowid = (j * rchunk + lax.broadcasted_iota(I32, (1, rchunk), 1)).astype(F32)

    def contrib(i, valid):
        t0 = pl.multiple_of(i * LANE, LANE)
        oc = offc_ref[0, pl.ds(t0, LANE), :]
        off = oc[:, 0:1] * LANE + oc[:, 1:2]
        end = oc[:, 2:3] * LANE + oc[:, 3:4]
        p = jnp.logical_and(jnp.logical_and(rowid >= off, rowid < end), valid).astype(BF16)
        return t0, (_dot(p, left), _dot(p, right))

    def add(t0, y):
        acc[slot, pl.ds(t0, LANE), :half] += y[0]
        acc[slot, pl.ds(t0, LANE), half:] += y[1]

    def tile(i, c):
        add(*contrib(i, True))
        return c

    tlo = tlo_sm[g]
    thi = thi_sm[g]
    parts = [contrib(jnp.minimum(tlo + k, ntile - 1), tlo + k <= thi) for k in range(COMBINE_TILES)]
    for t0, y in parts:
        add(t0, y)
    lax.fori_loop(tlo + COMBINE_TILES, thi + 1, tile, 0)

    @pl.when(j == nchunk - 1)
    def _batch_end():
        o_copy(b).start()

        @pl.when(b == batch - 1)
        def _():
            o_copy(b).wait()


def combine(rbuf, offc, tlo, thi, x, rows_per_batch, rchunk):
    batch, t, d = x.shape
    nchunk = rows_per_batch // rchunk
    assert nchunk >= 2
    grid_spec = pltpu.PrefetchScalarGridSpec(
        num_scalar_prefetch=2,
        grid=(batch * nchunk,),
        in_specs=[pl.BlockSpec((rchunk, d // 2), lambda g, *_: (g, 0)),
                  pl.BlockSpec((1, t, 8), lambda g, *_: (g // nchunk, 0, 0)),
                  pl.BlockSpec(memory_space=pl.ANY)],
        out_specs=pl.BlockSpec(memory_space=pl.ANY),
        scratch_shapes=[pltpu.VMEM((2, t, d), F32), pltpu.SemaphoreType.DMA((2,)), pltpu.SemaphoreType.DMA((2,))])
    return pl.pallas_call(
        functools.partial(_combine_kernel, batch=batch, nchunk=nchunk, rchunk=rchunk),
        grid_spec=grid_spec,
        out_shape=jax.ShapeDtypeStruct((batch, t, d), F32),
        compiler_params=_cparams(1),
    )(tlo, thi, rbuf, offc, x)


def moe_block(x1, hn, aff, w1, w3, w2, layer, batch):
    n, d = x1.shape
    t = n // batch
    ne = aff.shape[1]
    cap = CAPACITY_FACTOR * t // ne
    rows = batch * cap
    cmp, offc, rng = route_select(aff, cap, COMBINE_ROWS)
    ci = cmp.astype(I32)
    boff = jnp.arange(batch, dtype=I32)[:, None, None]
    idx = (ci[:, :, 0] * TOK_SPLIT + ci[:, :, 1] + boff * t).transpose(1, 0, 2).reshape(-1)
    dest = (ci[:, :, 2] * LANE + ci[:, :, 3] + boff * (ne * cap)).transpose(1, 0, 2).reshape(-1)
    idx = jnp.concatenate([idx, idx[:rows]])
    dest = jnp.concatenate([batch * ne * cap + jnp.arange(rows, dtype=I32), dest])
    rbuf = expert_ffn(hn, cmp, idx, dest, w1, w3, w2, layer, cap)
    tlo = rng[:, :, 0].reshape(-1)
    thi = rng[:, :, LANE - 1].reshape(-1)
    return combine(rbuf, offc, tlo, thi, x1.reshape(batch, t, d), ne * cap, COMBINE_ROWS)


MLSTM_L = 256


def _log_sigmoid(x):
    return jnp.minimum(x, 0.0) - jnp.log1p(jnp.exp(-jnp.abs(x)))


def _split3(x):
    x0 = x.astype(BF16)
    r = x - x0.astype(F32)
    x1 = r.astype(BF16)
    return x0, x1, (r - x1.astype(F32)).astype(BF16)


def _mlstm_proj_kernel(x_ref, g_ref, wm_ref, wkt_ref, wgt_ref, bt_ref, main_ref, kt_ref, gt_ref, *, nsub, nh):
    ln = MLSTM_L
    h = _rms(x_ref[...], g_ref[...]).astype(BF16)
    main_ref[...] = _dot(h, wm_ref[...]).astype(main_ref.dtype)
    kt = _dot_nt(wkt_ref[...], h).astype(kt_ref.dtype)
    pre = _dot_nt(wgt_ref[...], h) + bt_ref[...]
    lst = _log_sigmoid(pre)
    ri = lax.broadcasted_iota(I32, (ln, ln), 0)
    ci = lax.broadcasted_iota(I32, (ln, ln), 1)
    low = (ci <= ri).astype(BF16)
    upp = (ci >= ri).astype(BF16)
    row = lax.broadcasted_iota(I32, (pre.shape[0], ln), 0)
    fwd = jnp.logical_and(row >= nh, row < 2 * nh)
    bwd = row >= 3 * nh
    for j in range(nsub):
        sl = slice(j * ln, (j + 1) * ln)
        kt_ref[j] = kt[:, sl]
        pt = _split3(lst[:, sl])
        gt_ref[j] = jnp.where(fwd, sum(_dot(x, upp) for x in pt),
                              jnp.where(bwd, sum(_dot(x, low) for x in pt), pre[:, sl]))


def mlstm_project(x, g, wm, wkt, wgt, bias_t, nh, tm=PROJ_ROWS):
    n, d = x.shape
    nsub = tm // MLSTM_L
    nck = n // MLSTM_L
    const = lambda i: (0, 0)
    return pl.pallas_call(
        functools.partial(_mlstm_proj_kernel, nsub=nsub, nh=nh),
        grid=(n // tm,),
        in_specs=[pl.BlockSpec((tm, d), lambda i: (i, 0)),
                  pl.BlockSpec((1, d), const),
                  pl.BlockSpec(wm.shape, const),
                  pl.BlockSpec(wkt.shape, const),
                  pl.BlockSpec(wgt.shape, const),
                  pl.BlockSpec(bias_t.shape, const)],
        out_specs=[pl.BlockSpec((tm, wm.shape[1]), lambda i: (i, 0)),
                   pl.BlockSpec((nsub, wkt.shape[0], MLSTM_L), lambda i: (i, 0, 0)),
                   pl.BlockSpec((nsub, wgt.shape[0], MLSTM_L), lambda i: (i, 0, 0))],
        out_shape=[jax.ShapeDtypeStruct((n, wm.shape[1]), BF16),
                   jax.ShapeDtypeStruct((nck, wkt.shape[0], MLSTM_L), BF16),
                   jax.ShapeDtypeStruct((nck, wgt.shape[0], MLSTM_L), F32)],
        compiler_params=_cparams(1),
    )(x, g.reshape(1, d), wm, wkt, wgt, bias_t)


def _mlstm_kernel(q_ref, kt_ref, v_ref, og_ref, gt_ref, ng_ref, y_ref, hf, hb, cst, b_sc, u_sc, cm_sc,
                  *, nc, nh, dqk, dv):
    p = pl.program_id(1)
    ln = MLSTM_L
    ri = lax.broadcasted_iota(I32, (ln, ln), 0)
    ci = lax.broadcasted_iota(I32, (ln, ln), 1)
    masks = (ci <= ri, ci >= ri)
    ones_blk = jnp.ones((ln, dv), BF16)
    lane8 = lax.broadcasted_iota(I32, (8, ln), 1)
    row8 = lax.broadcasted_iota(I32, (8, ln), 0)
    bwd_row = (row8 % 2) == 1
    edge = lane8 == jnp.where(bwd_row, 0, ln - 1)
    kk = lax.broadcasted_iota(I32, (48, 2 * dv), 0) % 16
    cc = lax.broadcasted_iota(I32, (48, 2 * dv), 1)
    sel = [jnp.logical_or(jnp.logical_and(kk == i, cc < dv), jnp.logical_and(kk == 4 + i, cc >= dv)).astype(BF16)
           for i in range(4)]
    zero4 = jnp.zeros((4, ln), F32)
    cst[...] = jnp.zeros_like(cst)
    chains = [(hh, d) for hh in range(2) for d in range(2)]

    def gate_rows(c, which):
        return [gt_ref[c if d == 0 else nc - 1 - c, pl.ds((2 * d + which) * nh + 2 * p + hh, 1), :]
                for hh, d in chains] + [zero4]
    b_all = jnp.concatenate([x for c in range(nc) for x in gate_rows(c, 1)], axis=0)
    u_all = jnp.concatenate([x for c in range(nc) for x in gate_rows(c, 0)], axis=0) - b_all
    lane_a = lax.broadcasted_iota(I32, u_all.shape, 1)
    bwd_a = (lax.broadcasted_iota(I32, u_all.shape, 0) % 2) == 1
    cf = cb = u_all
    sh = 1
    while sh < ln:
        cf = jnp.maximum(cf, jnp.where(lane_a >= sh, pltpu.roll(cf, sh, axis=1), NEG))
        cb = jnp.maximum(cb, jnp.where(lane_a < ln - sh, pltpu.roll(cb, ln - sh, axis=1), NEG))
        sh *= 2
    b_sc[...] = b_all.reshape(nc, 8, ln)
    u_sc[...] = u_all.reshape(nc, 8, ln)
    cm_sc[...] = jnp.where(bwd_a, cb, cf).reshape(nc, 8, ln)

    def step(c, ms):
        cks = [c if d == 0 else nc - 1 - c for _, d in chains]
        r0s = [pl.multiple_of(ck * ln, ln) for ck in cks]
        cprev = [cst[i] for i in range(4)]
        q = [q_ref[pl.ds(r0s[i], ln), hh * dqk:(hh + 1) * dqk] for i, (hh, _) in enumerate(chains)]
        kt = [kt_ref[cks[i], hh * dqk:(hh + 1) * dqk, :] for i, (hh, _) in enumerate(chains)]
        vaug = [jnp.concatenate([v_ref[pl.ds(r0s[i], ln), hh * dv:(hh + 1) * dv], ones_blk], axis=1)
                for i, (hh, _) in enumerate(chains)]
        s = [_dot(q[i], kt[i]) for i in range(4)]
        inter = [_dot(q[i], cprev[i].astype(BF16)) for i in range(4)]
        b = b_sc[c]
        u = u_sc[c]
        m_run = jnp.maximum(ms, cm_sc[c])
        m_end = jnp.max(jnp.where(edge, m_run, NEG), axis=1, keepdims=True)
        b_end = jnp.sum(jnp.where(edge, b, 0.0), axis=1, keepdims=True)
        wc = jnp.exp(u - m_end)
        decay = jnp.exp(ms - m_end)
        rows = jnp.concatenate([m_run[:4] * LOG2E, jnp.exp(-(b + m_run))[:4], zero4, zero4], axis=0)
        stack = jnp.concatenate(_split3(rows), axis=0)
        u2 = u * LOG2E
        ms2 = ms * LOG2E
        bc = [lax.dot_general(stack, sel[i], (((0,), (0,)), ((), ())), preferred_element_type=F32)
              for i in range(4)]
        mb = [jnp.concatenate([bc[i][:, :dv]] * (ln // dv), axis=1) for i in range(4)]
        sw = [(jnp.exp2(jnp.where(masks[d], u2[i:i + 1, :] - mb[i], NEG)) * s[i]).astype(BF16)
              for i, (_, d) in enumerate(chains)]
        intra = [_dot(sw[i], vaug[i]) for i in range(4)]
        ea = [jnp.exp2(ms2[i:i + 1, :] - bc[i][:, :dv]) for i in range(4)]
        num = [ea[i] * inter[i][:, :dv] + intra[i][:, :dv] for i in range(4)]
        den = [ea[i] * inter[i][:, dv:] + intra[i][:, dv:] for i in range(4)]
        hc = [num[i] / jnp.maximum(jnp.abs(den[i]), bc[i][:, dv:]) for i in range(4)]
        kw = [(kt[i].astype(F32) * wc[i:i + 1, :]).astype(BF16) for i in range(4)]
        c_new = [decay[i:i + 1, :] * cprev[i] + _dot(kw[i], vaug[i]) for i in range(4)]
        for i, (hh, d) in enumerate(chains):
            cst[i] = c_new[i]
            (hf if d == 0 else hb)[pl.ds(r0s[i], ln), hh * dv:(hh + 1) * dv] = hc[i]
        return b_end + m_end

    lax.fori_loop(0, nc, step, jnp.zeros((8, 1), F32))
    for hh in range(2):
        sl = slice(hh * dv, (hh + 1) * dv)
        hs = _rms(hf[:, sl] + hb[:, sl], ng_ref[:, sl])
        y_ref[:, sl] = (hs * jax.nn.sigmoid(og_ref[:, sl].astype(F32))).astype(y_ref.dtype)


def mlstm_core(main, ktc, grow, out_g, batch, seq, nh, dqk, dv):
    n = main.shape[0]
    nc = seq // MLSTM_L
    npair = nh // 2
    vblocks = (nh * dqk) // (2 * dv)
    ogblocks = (nh * dqk + nh * dv) // (2 * dv)
    return pl.pallas_call(
        functools.partial(_mlstm_kernel, nc=nc, nh=nh, dqk=dqk, dv=dv),
        grid=(batch, npair),
        in_specs=[pl.BlockSpec((seq, 2 * dqk), lambda b, p: (b, p)),
                  pl.BlockSpec((nc, 2 * dqk, MLSTM_L), lambda b, p: (b, p, 0)),
                  pl.BlockSpec((seq, 2 * dv), lambda b, p: (b, vblocks + p)),
                  pl.BlockSpec((seq, 2 * dv), lambda b, p: (b, ogblocks + p)),
                  pl.BlockSpec((nc, 4 * nh, MLSTM_L), lambda b, p: (b, 0, 0)),
                  pl.BlockSpec((1, 2 * dv), lambda b, p: (0, p))],
        out_specs=pl.BlockSpec((seq, 2 * dv), lambda b, p: (b, p)),
        out_shape=jax.ShapeDtypeStruct((n, nh * dv), BF16),
        scratch_shapes=[pltpu.VMEM((seq, 2 * dv), F32), pltpu.VMEM((seq, 2 * dv), F32),
                        pltpu.VMEM((4, dqk, 2 * dv), F32)] + [pltpu.VMEM((nc, 8, MLSTM_L), F32)] * 3,
        compiler_params=_cparams(2),
    )(main, ktc, main, main, grow, out_g.reshape(1, nh * dv).astype(F32))


def mlstm_layer(x2d, batch, seq, norm_g, w_in, b_i, b_f, out_g, w_out, ffn_g, wr):
    d = x2d.shape[1]
    nh = MLSTM_HEADS
    dv = d // nh
    dqk = dv // 2
    o1, o2, o3, o4 = nh * dqk, 2 * nh * dqk, 2 * nh * dqk + nh * dv, 2 * nh * dqk + 2 * nh * dv
    wm = jnp.concatenate([w_in[:, :o1], w_in[:, o2:o4]], axis=1).astype(BF16)
    wkt = (w_in[:, o1:o2] * (dqk ** -0.5)).T.astype(BF16)
    wgt = w_in[:, o4:].T.astype(BF16)
    bias_t = jnp.concatenate([b_i[0], b_f[0], b_i[1], b_f[1]]).reshape(4 * nh, 1).astype(F32)
    main, ktc, grow = mlstm_project(x2d, norm_g, wm, wkt, wgt, bias_t, nh)
    y = mlstm_core(main, ktc, grow, out_g, batch, seq, nh, dqk, dv)
    return mm_res_router(y, w_out.astype(BF16), x2d, ffn_g, wr.T.astype(BF16), batch)


def attention_layer(x2d, batch, seq, norm_g, w_in, q_g, k_g, sink, w_out, rel_bias, ffn_g, wr):
    n_q = rel_bias.shape[1]
    q, k, vt = attention_project(x2d, norm_g, w_in, q_g, k_g, n_q, n_q // GQA_GROUP)
    ot = attention_core(q, k, vt, rel_bias, sink, batch, seq)
    return mm_res_router(ot, w_out.astype(BF16), x2d, ffn_g, wr.T.astype(BF16), batch, a_transposed=True)


def kernel(x, rel_bias, attn_norm_g, attn_w_in, attn_q_norm_g, attn_k_norm_g, attn_sink, attn_w_out, mlstm_norm_g, mlstm_w_in, mlstm_b_i, mlstm_b_f, mlstm_out_norm_g, mlstm_w_out, ffn_norm_g, router_w, expert_w1, expert_w3, expert_w2):
    batch, seq, d = x.shape
    x2d = x.reshape(batch * seq, d)
    x1, hn, aff = attention_layer(x2d, batch, seq, attn_norm_g[0], attn_w_in[0], attn_q_norm_g[0],
                                  attn_k_norm_g[0], attn_sink[0], attn_w_out[0], rel_bias,
                                  ffn_norm_g[0], router_w[0])
    x = moe_block(x1, hn, aff, expert_w1, expert_w3, expert_w2, 0, batch)
    x1, hn, aff = mlstm_layer(x.reshape(batch * seq, d), batch, seq, mlstm_norm_g[0], mlstm_w_in[0], mlstm_b_i[0],
                              mlstm_b_f[0], mlstm_out_norm_g[0], mlstm_w_out[0], ffn_norm_g[1], router_w[1])
    return moe_block(x1, hn, aff, expert_w1, expert_w3, expert_w2, 1, batch)
```

```python
import functools
import math

import jax
import jax.numpy as jnp
from jax import lax
from jax.experimental import pallas as pl
from jax.experimental.pallas import tpu as pltpu

F32 = jnp.float32
BF16 = jnp.bfloat16
I32 = jnp.int32
U32 = jnp.uint32

RMS_EPS = 1e-6
NEG = -1e30
LOG2E = 1.4426950408889634
LANE = 128
MXU_DIM = 256
VMEM_LIMIT = 56 * 1024 * 1024
PROJ_ROWS = 1024

HEAD_DIM = 64
GQA_GROUP = 4
ATT_BLOCK = 128
NUM_BUCKETS = 32
MAX_DISTANCE = 128
CAPACITY_FACTOR = 2
MLSTM_HEADS = 8


def _cparams(n_axes, vmem=VMEM_LIMIT):
    return pltpu.CompilerParams(dimension_semantics=("arbitrary",) * n_axes, vmem_limit_bytes=vmem)


def _rms(x, g):
    return x * lax.rsqrt(jnp.mean(x * x, axis=-1, keepdims=True) + RMS_EPS) * g


def _dot(a, b):
    return jnp.dot(a, b, preferred_element_type=F32)


def _pack_bf16_pairs(x):
    half = x.shape[1] // 2
    hi = pltpu.bitcast(x[:, :half].astype(BF16).astype(F32), U32)
    lo = pltpu.bitcast(x[:, half:].astype(BF16).astype(F32), U32)
    return hi | lax.shift_right_logical(lo, jnp.uint32(16))


def _unpack_bf16_pairs(w):
    left = pltpu.bitcast(w & jnp.uint32(0xFFFF0000), F32).astype(BF16)
    right = pltpu.bitcast(lax.shift_left(w, jnp.uint32(16)), F32).astype(BF16)
    return left, right


def _dot_nt(a, b):
    return lax.dot_general(a, b, (((1,), (1,)), ((), ())), preferred_element_type=F32)


def _attn_proj_kernel(x_ref, g_ref, wq_ref, wk_ref, wvt_ref, qg_ref, kg_ref, seg_ref, q_ref, k_ref, vt_ref):
    h = _rms(x_ref[...], g_ref[...]).astype(BF16)
    seg = seg_ref[...]
    w = seg.shape[0]

    def head_norm(t, gain_ref, out_ref):
        for j in range(t.shape[1] // w):
            tj = t[:, j * w:(j + 1) * w]
            ms = _dot((tj * tj).astype(BF16), seg)
            out_ref[:, j * w:(j + 1) * w] = (tj * lax.rsqrt(ms + RMS_EPS) * gain_ref[:, j * w:(j + 1) * w]
                                             ).astype(out_ref.dtype)

    head_norm(_dot(h, wq_ref[...]), qg_ref, q_ref)
    head_norm(_dot(h, wk_ref[...]), kg_ref, k_ref)
    vt_ref[...] = _dot_nt(wvt_ref[...], h).astype(vt_ref.dtype)


def attention_project(x, g, w_in, q_g, k_g, n_q, n_kv, tm=PROJ_ROWS):
    n, d = x.shape
    dh = HEAD_DIM
    dq, dk = n_q * dh, n_kv * dh
    wq = w_in[:, :dq].astype(BF16)
    wk = w_in[:, dq:dq + dk].astype(BF16)
    wvt = w_in[:, dq + dk:].T.astype(BF16)
    qg = jnp.tile(q_g.astype(F32) * (dh ** -0.5 * LOG2E), n_q).reshape(1, dq)
    kg = jnp.tile(k_g.astype(F32), n_kv).reshape(1, dk)
    assert dq % MXU_DIM == 0 and dk % MXU_DIM == 0 and MXU_DIM % dh == 0
    hid = jnp.arange(MXU_DIM) // dh
    seg = jnp.where(hid[:, None] == hid[None, :], 1.0 / dh, 0.0).astype(BF16)
    const2 = lambda i: (0, 0)
    return pl.pallas_call(
        _attn_proj_kernel,
        grid=(n // tm,),
        in_specs=[pl.BlockSpec((tm, d), lambda i: (i, 0)),
                  pl.BlockSpec((1, d), const2),
                  pl.BlockSpec((d, dq), const2),
                  pl.BlockSpec((d, dk), const2),
                  pl.BlockSpec((dk, d), const2),
                  pl.BlockSpec((1, dq), const2),
                  pl.BlockSpec((1, dk), const2),
                  pl.BlockSpec((MXU_DIM, MXU_DIM), const2)],
        out_specs=[pl.BlockSpec((tm, dq), lambda i: (i, 0)),
                   pl.BlockSpec((tm, dk), lambda i: (i, 0)),
                   pl.BlockSpec((dk, tm), lambda i: (0, i))],
        out_shape=[jax.ShapeDtypeStruct((n, dq), BF16),
                   jax.ShapeDtypeStruct((n, dk), BF16),
                   jax.ShapeDtypeStruct((dk, n), BF16)],
        compiler_params=_cparams(1),
    )(x, g.reshape(1, d), wq, wk, wvt, qg, kg, seg)


def _mm_res_router_kernel(a_ref, w_ref, x_ref, g_ref, wr_ref, x1_ref, hn_ref, aff_ref, *, a_transposed):
    if a_transposed:
        y = lax.dot_general(a_ref[...], w_ref[...], (((0,), (0,)), ((), ())), preferred_element_type=F32)
    else:
        y = _dot(a_ref[...], w_ref[...])
    x1 = x_ref[...] + y
    x1_ref[...] = x1
    hn = _rms(x1, g_ref[...])
    hn_ref[...] = _pack_bf16_pairs(hn)
    logits = _dot_nt(wr_ref[...], hn.astype(BF16))
    mx = jnp.max(logits, axis=0, keepdims=True)
    p = jnp.exp(logits - mx)
    aff_ref[0] = p / jnp.sum(p, axis=0, keepdims=True)


def mm_res_router(a, w, x, g, wr_t, batch, a_transposed=False, tm=PROJ_ROWS):
    n, d = x.shape
    k = w.shape[0]
    e = wr_t.shape[0]
    t = n // batch
    tm = min(tm, t)
    tpb = t // tm
    a_spec = pl.BlockSpec((k, tm), lambda i: (0, i)) if a_transposed else pl.BlockSpec((tm, k), lambda i: (i, 0))
    return pl.pallas_call(
        functools.partial(_mm_res_router_kernel, a_transposed=a_transposed),
        grid=(n // tm,),
        in_specs=[a_spec,
                  pl.BlockSpec((k, d), lambda i: (0, 0)),
                  pl.BlockSpec((tm, d), lambda i: (i, 0)),
                  pl.BlockSpec((1, d), lambda i: (0, 0)),
                  pl.BlockSpec((e, d), lambda i: (0, 0))],
        out_specs=[pl.BlockSpec((tm, d), lambda i: (i, 0)),
                   pl.BlockSpec((tm, d // 2), lambda i: (i, 0)),
                   pl.BlockSpec((1, e, tm), lambda i: (i // tpb, 0, i % tpb))],
        out_shape=[jax.ShapeDtypeStruct((n, d), F32),
                   jax.ShapeDtypeStruct((n, d // 2), U32),
                   jax.ShapeDtypeStruct((batch, e, t), F32)],
        compiler_params=_cparams(1),
    )(a, w, x, g.reshape(1, d), wr_t)


def _t5_bucket(rel):
    nb = NUM_BUCKETS // 2
    ret = (rel > 0).astype(jnp.int32) * nb
    n = jnp.abs(rel)
    max_exact = nb // 2
    nf = jnp.maximum(n, 1).astype(jnp.float32)
    large = max_exact + (jnp.log(nf / max_exact) / math.log(MAX_DISTANCE / max_exact)
                         * (nb - max_exact)).astype(jnp.int32)
    large = jnp.minimum(large, nb - 1)
    return ret + jnp.where(n < max_exact, n, large)


def _attn_bucket_table():
    kk = jnp.arange(3 * ATT_BLOCK)[:, None]
    qq = jnp.arange(ATT_BLOCK)[None, :]
    rel = kk - ATT_BLOCK - qq
    return jnp.where(jnp.abs(rel) <= ATT_BLOCK, _t5_bucket(rel), -1).astype(I32)


ATT_QBLOCKS = 8


def _attn_kernel(sink_ref, rb_ref, q_ref, *rest, nb, n_kv):
    nq = ATT_QBLOCKS
    k_refs = rest[:nq + 2]
    v_refs = rest[nq + 2:2 * (nq + 2)]
    bucket_ref, ot_ref, bias_sc = rest[2 * (nq + 2):]
    n = pl.program_id(1)
    blk = ATT_BLOCK
    dh = HEAD_DIM
    gw = GQA_GROUP * blk

    @pl.when(jnp.logical_and(pl.program_id(0) == 0, n == 0))
    def _bias_table():
        bk = bucket_ref[...]
        for hq in range(n_kv * GQA_GROUP):
            acc = jnp.full(bk.shape, NEG, F32)
            for k in range(NUM_BUCKETS):
                acc = jnp.where(bk == k, rb_ref[k, hq] * LOG2E, acc)
            bias_sc[hq // GQA_GROUP, :, (hq % GQA_GROUP) * blk:(hq % GQA_GROUP + 1) * blk] = acc

    kblk = [r[...] for r in k_refs]
    vblk = [r[...] for r in v_refs]
    kidx = lax.broadcasted_iota(I32, (3 * blk, gw), 0)
    ones_rows = (lax.broadcasted_iota(I32, (16, 3 * blk), 0) == 0).astype(BF16)
    lane = lax.broadcasted_iota(I32, (1, gw), 1)
    heads = [[h * GQA_GROUP + g for g in range(GQA_GROUP)] for h in range(n_kv)]
    sk = []
    for h in range(n_kv):
        row_sink = jnp.full((1, gw), sink_ref[heads[h][-1]] * LOG2E, F32)
        for g in reversed(range(GQA_GROUP - 1)):
            row_sink = jnp.where(lane < (g + 1) * blk, sink_ref[heads[h][g]] * LOG2E, row_sink)
        sk.append(row_sink)
    keys, vt, valid = [], [], []
    for j in range(nq):
        i = n * nq + j
        keys.append(jnp.concatenate(kblk[j:j + 3], axis=0))
        vt.append(jnp.concatenate(vblk[j:j + 3], axis=1))
        valid.append(jnp.logical_and(jnp.logical_or(i > 0, kidx >= blk),
                                     jnp.logical_or(i < nb - 1, kidx < 2 * blk)))
    units = [(j, h) for j in range(nq) for h in range(n_kv)]
    q = [jnp.concatenate([q_ref[j * blk:(j + 1) * blk, hq * dh:(hq + 1) * dh] for hq in heads[h]], axis=0)
         for j, h in units]
    vaug = [jnp.concatenate([vt[j][h * dh:(h + 1) * dh, :], ones_rows], axis=0) for j, h in units]
    s = [jnp.where(valid[j], _dot_nt(keys[j][:, h * dh:(h + 1) * dh], q[u]) + bias_sc[h], NEG)
         for u, (j, h) in enumerate(units)]
    m = [jnp.maximum(jnp.max(s[u], axis=0, keepdims=True), sk[h]) for u, (j, h) in enumerate(units)]
    p = [jnp.exp2(s[u] - m[u]).astype(BF16) for u in range(len(units))]
    oa = [_dot(vaug[u], p[u]) for u in range(len(units))]
    o = [oa[u][:dh] / (oa[u][dh:dh + 1] + jnp.exp2(sk[h] - m[u])) for u, (j, h) in enumerate(units)]
    for u, (j, h) in enumerate(units):
        for g, hq in enumerate(heads[h]):
            ot_ref[hq * dh:(hq + 1) * dh, j * blk:(j + 1) * blk] = o[u][:, g * blk:(g + 1) * blk].astype(ot_ref.dtype)


def attention_core(q, k, vt, rel_bias, sink, batch, seq):
    n, dq = q.shape
    dk = k.shape[1]
    hq = rel_bias.shape[1]
    n_kv = hq // GQA_GROUP
    nb = seq // ATT_BLOCK
    blk = ATT_BLOCK
    nq = ATT_QBLOCKS
    assert nb % nq == 0

    def key_block(off):
        return lambda b, i: b * nb + jnp.clip(i * nq + off, 0, nb - 1)

    offs = range(-1, nq + 1)
    return pl.pallas_call(
        functools.partial(_attn_kernel, nb=nb, n_kv=n_kv),
        grid=(batch, nb // nq),
        in_specs=[pl.BlockSpec(memory_space=pltpu.SMEM),
                  pl.BlockSpec(memory_space=pltpu.SMEM),
                  pl.BlockSpec((nq * blk, dq), lambda b, i: (b * (nb // nq) + i, 0))]
                 + [pl.BlockSpec((blk, dk), (lambda f: lambda b, i: (f(b, i), 0))(key_block(o))) for o in offs]
                 + [pl.BlockSpec((dk, blk), (lambda f: lambda b, i: (0, f(b, i)))(key_block(o))) for o in offs]
                 + [pl.BlockSpec((3 * blk, blk), lambda b, i: (0, 0))],
        out_specs=pl.BlockSpec((dq, nq * blk), lambda b, i: (0, b * (nb // nq) + i)),
        out_shape=jax.ShapeDtypeStruct((dq, n), BF16),
        scratch_shapes=[pltpu.VMEM((n_kv, 3 * blk, GQA_GROUP * blk), F32)],
        compiler_params=_cparams(2),
    )(sink.astype(F32), rel_bias.astype(F32), q, *([k] * (nq + 2)), *([vt] * (nq + 2)), _attn_bucket_table())


def _prefix_incl(x, tri):
    r, t = x.shape
    nck = t // LANE
    assert nck <= LANE
    xb = x.astype(BF16)
    local = [_dot(xb[:, c * LANE:(c + 1) * LANE], tri) for c in range(nck)]
    tot = jnp.concatenate([p[:, LANE - 1:LANE] for p in local] + [jnp.zeros((r, LANE - nck), F32)], axis=1)
    lane = lax.broadcasted_iota(I32, tot.shape, 1)
    inc = tot
    sh = 1
    while sh < nck:
        inc = inc + jnp.where(lane >= sh, pltpu.roll(inc, sh, axis=1), 0.0)
        sh *= 2
    offs = inc - tot
    return jnp.concatenate([local[c] + offs[:, c:c + 1] for c in range(nck)], axis=1)


BF16_EXACT = 256
TOK_SPLIT = 64


def _select_kernel(aff_ref, cmp_ref, idx_ref, dest_ref, offc_ref, rng_ref, pos_sc, dest_sc, vals_sc, *, cap, rchunk):
    aff = aff_ref[0]
    ne, t = aff.shape
    bidx = pl.program_id(0)
    bits = pltpu.bitcast(aff, I32)

    def search(i, lo):
        cand = lo | lax.shift_left(jnp.int32(1), 30 - i)
        cnt = jnp.sum((bits >= cand).astype(I32), axis=1, keepdims=True)
        return jnp.where(cnt >= cap, cand, lo)

    thr = lax.fori_loop(0, 31, search, jnp.zeros((ne, 1), I32))
    gt = bits > thr
    eq = bits == thr
    need = (cap - jnp.sum(gt.astype(I32), axis=1, keepdims=True)).astype(F32)
    ri = lax.broadcasted_iota(I32, (LANE, LANE), 0)
    ci = lax.broadcasted_iota(I32, (LANE, LANE), 1)
    tri = (ri <= ci).astype(BF16)
    eqf = eq.astype(F32)
    rank_eq = _prefix_incl(eqf, tri) - eqf
    sel = jnp.logical_or(gt, jnp.logical_and(eq, rank_eq < need))
    self_ = sel.astype(F32)
    kt = jnp.sum(self_, axis=0, keepdims=True)
    pre = _prefix_incl(jnp.concatenate([self_, jnp.broadcast_to(kt, (8, t))], axis=0), tri)
    pos_sc[...] = jnp.where(sel, pre[:ne] - 1.0, -1.0)
    end = pre[ne:ne + 1]
    off = end - kt
    lr = lax.broadcasted_iota(I32, (ne, ne), 0)
    lc = lax.broadcasted_iota(I32, (ne, ne), 1)
    jexp = _dot((lc < lr).astype(BF16), self_.astype(BF16))
    dest_sc[...] = off + jexp

    tok = lax.broadcasted_iota(I32, (1, t), 1)
    vals_sc[0:1, :] = (tok // TOK_SPLIT).astype(F32)
    vals_sc[1:2, :] = (tok % TOK_SPLIT).astype(F32)
    vals_sc[7:8, :] = jnp.zeros((1, t), F32)
    slot = lax.broadcasted_iota(I32, (cap, 1), 0).astype(F32)

    def compact(e, carry):
        d = dest_sc[pl.ds(e, 1), :]
        dh = jnp.floor(d * (1.0 / LANE))
        g = aff_ref[0, pl.ds(e, 1), :]
        g0 = g.astype(BF16).astype(F32)
        g1 = (g - g0).astype(BF16).astype(F32)
        vals_sc[2:3, :] = dh
        vals_sc[3:4, :] = d - dh * LANE
        vals_sc[4:5, :] = g0
        vals_sc[5:6, :] = g1
        vals_sc[6:7, :] = g - g0 - g1
        onehot = (pos_sc[pl.ds(e, 1), :] == slot).astype(BF16)
        out = _dot_nt(vals_sc[...].astype(BF16), onehot)
        cmp_ref[0, e] = out
        idx_ref[e, 0] = (out[0:1] * TOK_SPLIT + out[1:2]).astype(I32) + bidx * t
        dest_ref[e + 1, 0] = (out[2:3] * LANE + out[3:4]).astype(I32) + bidx * (ne * cap)
        return carry

    lax.fori_loop(0, ne, compact, 0)
    idx_ref[ne, 0] = idx_ref[0, 0]
    dest_ref[0, 0] = (pl.num_programs(0) * ne + bidx) * cap + lax.broadcasted_iota(I32, (1, cap), 1)

    eh = jnp.floor(end * (1.0 / LANE))
    oh = jnp.floor(off * (1.0 / LANE))
    v4 = jnp.concatenate([oh, off - oh * LANE, eh, end - eh * LANE, jnp.zeros((4, t), F32)], axis=0)
    eye = (ri == ci).astype(BF16)
    for c in range(t // LANE):
        offc_ref[0, c * LANE:(c + 1) * LANE, :] = _dot_nt(eye, v4[:, c * LANE:(c + 1) * LANE].astype(BF16))

    nchunk = (ne * cap) // rchunk
    low = (lax.broadcasted_iota(I32, (nchunk, 1), 0) * rchunk).astype(F32)
    first = jnp.sum((end <= low).astype(F32), axis=1, keepdims=True)
    last = jnp.sum((end <= low + (rchunk - 1)).astype(F32), axis=1, keepdims=True)
    lane = lax.broadcasted_iota(I32, (nchunk, LANE), 1)
    tiles = jnp.where(lane < LANE // 2, jnp.floor(first * (1.0 / LANE)), jnp.floor(last * (1.0 / LANE)))
    rng_ref[0] = tiles.astype(I32)


COMBINE_ROWS = 1024
COMBINE_TILES = 5


def route_select(aff, cap, rchunk):
    batch, ne, t = aff.shape
    nchunk = ne * cap // rchunk
    assert t <= TOK_SPLIT * BF16_EXACT and ne * cap <= LANE * BF16_EXACT and LANE <= BF16_EXACT
    return pl.pallas_call(
        functools.partial(_select_kernel, cap=cap, rchunk=rchunk),
        grid=(batch,),
        in_specs=[pl.BlockSpec((1, ne, t), lambda b: (b, 0, 0))],
        out_specs=[pl.BlockSpec((1, ne, 8, cap), lambda b: (b, 0, 0, 0)),
                   pl.BlockSpec((ne + 1, 1, 1, cap), lambda b: (0, b, 0, 0)),
                   pl.BlockSpec((ne + 1, 1, 1, cap), lambda b: (0, b, 0, 0)),
                   pl.BlockSpec((1, t, 8), lambda b: (b, 0, 0)),
                   pl.BlockSpec((1, nchunk, LANE), lambda b: (b, 0, 0))],
        out_shape=[jax.ShapeDtypeStruct((batch, ne, 8, cap), F32),
                   jax.ShapeDtypeStruct((ne + 1, batch, 1, cap), I32),
                   jax.ShapeDtypeStruct((ne + 1, batch, 1, cap), I32),
                   jax.ShapeDtypeStruct((batch, t, 8), F32),
                   jax.ShapeDtypeStruct((batch, nchunk, LANE), I32)],
        scratch_shapes=[pltpu.VMEM((ne, t), F32), pltpu.VMEM((ne, t), F32), pltpu.VMEM((8, t), F32)],
        compiler_params=_cparams(1),
    )(aff)


def _ffn_kernel(idx_sm, dest_sm, hn_hbm, cmp_ref, w1_ref, w3_ref, w2_ref, r_hbm,
                xsu, xsb, yacc, ysc, wb1, wb3, wb2, gsem, ssem, *, batch, cap, nf, rt):
    e = pl.program_id(0)
    f = pl.program_id(1)
    ne = pl.num_programs(0)
    rows = batch * cap
    sub = rows // nf

    def gather_start(block, r):
        t = idx_sm[block * rows + r]
        pltpu.make_async_copy(hn_hbm.at[pl.ds(t, 1), :], xsu.at[pl.ds(r, 1), :], gsem).start()

    def gather_wait():
        pltpu.make_async_copy(hn_hbm.at[pl.ds(0, rows), :], xsu, gsem).wait()

    def scatter_start(block, r):
        d = dest_sm[block * rows + r]
        pltpu.make_async_copy(ysc.at[pl.ds(r, 1), :], r_hbm.at[pl.ds(d, 1), :], ssem).start()

    def scatter_wait():
        pltpu.make_async_copy(ysc, r_hbm.at[pl.ds(0, rows), :], ssem).wait()

    @pl.when(jnp.logical_and(e == 0, f == 0))
    def _prologue():
        def issue(r, c):
            gather_start(0, r)
            return c
        lax.fori_loop(0, rows, issue, 0)
        ysc[...] = jnp.zeros_like(ysc)

    @pl.when(f == 0)
    def _rows_ready():
        gather_wait()
        half = xsu.shape[1]
        xsb[:, :half], xsb[:, half:] = _unpack_bf16_pairs(xsu[...])
        yacc[...] = jnp.zeros_like(yacc)

    for r in range(sub):
        gather_start(e + 1, f * sub + r)
        scatter_start(e, f * sub + r)

    wb1[...] = w1_ref[0, 0].astype(BF16)
    for r in range(rows // rt):
        xs = xsb[r * rt:(r + 1) * rt, :]
        a = _dot(xs, wb1[...])
        if r == 0:
            wb3[...] = w3_ref[0, 0].astype(BF16)
        u = _dot(xs, wb3[...])
        hmid = (a * jax.nn.sigmoid(a) * u).astype(BF16)
        if r == 0:
            wb2[...] = w2_ref[0, 0].astype(BF16)
        yacc[r * rt:(r + 1) * rt, :] += _dot(hmid, wb2[...])

    @pl.when(f == nf - 1)
    def _finish():
        scatter_wait()
        ri = lax.broadcasted_iota(I32, (cap, cap), 0)
        ci = lax.broadcasted_iota(I32, (cap, cap), 1)
        eye = (ri == ci).astype(BF16)
        rows8 = cmp_ref[...].reshape(batch * 8, cap).astype(BF16)
        gt = _dot_nt(eye, rows8)
        for b in range(batch):
            g = gt[:, 8 * b + 4:8 * b + 5] + gt[:, 8 * b + 5:8 * b + 6] + gt[:, 8 * b + 6:8 * b + 7]
            ysc[b * cap:(b + 1) * cap, :] = _pack_bf16_pairs(yacc[b * cap:(b + 1) * cap, :] * g)

        @pl.when(e == ne - 1)
        def _epilogue():
            def issue(r, c):
                scatter_start(ne, r)
                return c
            lax.fori_loop(0, rows, issue, 0)
            scatter_wait()
            gather_wait()


def expert_ffn(hn, cmp, idx_flat, dest_flat, w1, w3, w2, layer, cap, tf=512, rt=512):
    d = w1.shape[2]
    batch, ne = cmp.shape[0], cmp.shape[1]
    dff = w1.shape[3]
    nf = dff // tf
    rows = batch * cap
    rt = min(rt, rows)
    grid_spec = pltpu.PrefetchScalarGridSpec(
        num_scalar_prefetch=2,
        grid=(ne, nf),
        in_specs=[pl.BlockSpec(memory_space=pl.ANY),
                  pl.BlockSpec((batch, 1, 8, cap), lambda e, f, *_: (0, e, 0, 0)),
                  pl.BlockSpec((1, 1, d, tf), lambda e, f, *_: (layer, e, 0, f)),
                  pl.BlockSpec((1, 1, d, tf), lambda e, f, *_: (layer, e, 0, f)),
                  pl.BlockSpec((1, 1, tf, d), lambda e, f, *_: (layer, e, f, 0))],
        out_specs=pl.BlockSpec(memory_space=pl.ANY),
        scratch_shapes=[pltpu.VMEM((rows, d // 2), U32), pltpu.VMEM((rows, d), BF16),
                        pltpu.VMEM((rows, d), F32), pltpu.VMEM((rows, d // 2), U32),
                        pltpu.VMEM((d, tf), BF16), pltpu.VMEM((d, tf), BF16), pltpu.VMEM((tf, d), BF16),
                        pltpu.SemaphoreType.DMA, pltpu.SemaphoreType.DMA])
    return pl.pallas_call(
        functools.partial(_ffn_kernel, batch=batch, cap=cap, nf=nf, rt=rt),
        grid_spec=grid_spec,
        out_shape=jax.ShapeDtypeStruct((batch * ne * cap + rows, d // 2), U32),
        compiler_params=_cparams(2),
    )(idx_flat, dest_flat, hn, cmp, w1, w3, w2)


def _combine_kernel(tlo_sm, thi_sm, r_ref, offc_ref, x_hbm, o_hbm, acc, xsem, osem, *, batch, nchunk, rchunk):
    g = pl.program_id(0)
    b = g // nchunk
    j = g % nchunk
    slot = b % 2
    ntile = acc.shape[1] // LANE

    def x_copy(bb):
        return pltpu.make_async_copy(x_hbm.at[bb], acc.at[bb % 2], xsem.at[bb % 2])

    def o_copy(bb):
        return pltpu.make_async_copy(acc.at[bb % 2], o_hbm.at[bb], osem.at[bb % 2])

    @pl.when(g == 0)
    def _first():
        x_copy(0).start()

    @pl.when(j == 0)
    def _batch_start():
        x_copy(b).wait()

    @pl.when(j == nchunk // 2)
    def _mid():
        @pl.when(b > 0)
        def _():
            o_copy(b - 1).wait()

        @pl.when(b + 1 < batch)
        def _():
            x_copy(b + 1).start()

    half = r_ref.shape[1]
    left, right = _unpack_bf16_pairs(r_ref[...])
    rowid = (j * rchunk + lax.broadcasted_iota(I32, (1, rchunk), 1)).astype(F32)

    def contrib(i, valid):
        t0 = pl.multiple_of(i * LANE, LANE)
        oc = offc_ref[0, pl.ds(t0, LANE), :]
        off = oc[:, 0:1] * LANE + oc[:, 1:2]
        end = oc[:, 2:3] * LANE + oc[:, 3:4]
        p = jnp.logical_and(jnp.logical_and(rowid >= off, rowid < end), valid).astype(BF16)
        return t0, (_dot(p, left), _dot(p, right))

    def add(t0, y):
        acc[slot, pl.ds(t0, LANE), :half] += y[0]
        acc[slot, pl.ds(t0, LANE), half:] += y[1]

    def tile(i, c):
        add(*contrib(i, True))
        return c

    tlo = tlo_sm[g]
    thi = thi_sm[g]
    parts = [contrib(jnp.minimum(tlo + k, ntile - 1), tlo + k <= thi) for k in range(COMBINE_TILES)]
    for t0, y in parts:
        add(t0, y)
    lax.fori_loop(tlo + COMBINE_TILES, thi + 1, tile, 0)

    @pl.when(j == nchunk - 1)
    def _batch_end():
        o_copy(b).start()

        @pl.when(b == batch - 1)
        def _():
            o_copy(b).wait()


def combine(rbuf, offc, tlo, thi, x, rows_per_batch, rchunk):
    batch, t, d = x.shape
    nchunk = rows_per_batch // rchunk
    assert nchunk >= 2
    grid_spec = pltpu.PrefetchScalarGridSpec(
        num_scalar_prefetch=2,
        grid=(batch * nchunk,),
        in_specs=[pl.BlockSpec((rchunk, d // 2), lambda g, *_: (g, 0)),
                  pl.BlockSpec((1, t, 8), lambda g, *_: (g // nchunk, 0, 0)),
                  pl.BlockSpec(memory_space=pl.ANY)],
        out_specs=pl.BlockSpec(memory_space=pl.ANY),
        scratch_shapes=[pltpu.VMEM((2, t, d), F32), pltpu.SemaphoreType.DMA((2,)), pltpu.SemaphoreType.DMA((2,))])
    return pl.pallas_call(
        functools.partial(_combine_kernel, batch=batch, nchunk=nchunk, rchunk=rchunk),
        grid_spec=grid_spec,
        out_shape=jax.ShapeDtypeStruct((batch, t, d), F32),
        compiler_params=_cparams(1),
    )(tlo, thi, rbuf, offc, x)


def moe_block(x1, hn, aff, w1, w3, w2, layer, batch):
    n, d = x1.shape
    t = n // batch
    ne = aff.shape[1]
    cap = CAPACITY_FACTOR * t // ne
    cmp, idx, dest, offc, rng = route_select(aff, cap, COMBINE_ROWS)
    rbuf = expert_ffn(hn, cmp, idx.reshape(-1), dest.reshape(-1), w1, w3, w2, layer, cap)
    tlo = rng[:, :, 0].reshape(-1)
    thi = rng[:, :, LANE - 1].reshape(-1)
    return combine(rbuf, offc, tlo, thi, x1.reshape(batch, t, d), ne * cap, COMBINE_ROWS)


MLSTM_L = 256


def _log_sigmoid(x):
    return jnp.minimum(x, 0.0) - jnp.log1p(jnp.exp(-jnp.abs(x)))


def _split3(x):
    x0 = x.astype(BF16)
    r = x - x0.astype(F32)
    x1 = r.astype(BF16)
    return x0, x1, (r - x1.astype(F32)).astype(BF16)


def _mlstm_proj_kernel(x_ref, g_ref, wm_ref, wkt_ref, wgt_ref, bt_ref, main_ref, kt_ref, gt_ref, *, nsub, nh):
    ln = MLSTM_L
    h = _rms(x_ref[...], g_ref[...]).astype(BF16)
    main_ref[...] = _dot(h, wm_ref[...]).astype(main_ref.dtype)
    kt = _dot_nt(wkt_ref[...], h).astype(kt_ref.dtype)
    pre = _dot_nt(wgt_ref[...], h) + bt_ref[...]
    lst = _log_sigmoid(pre)
    ri = lax.broadcasted_iota(I32, (ln, ln), 0)
    ci = lax.broadcasted_iota(I32, (ln, ln), 1)
    low = (ci <= ri).astype(BF16)
    upp = (ci >= ri).astype(BF16)
    row = lax.broadcasted_iota(I32, (pre.shape[0], ln), 0)
    fwd = jnp.logical_and(row >= nh, row < 2 * nh)
    bwd = row >= 3 * nh
    for j in range(nsub):
        sl = slice(j * ln, (j + 1) * ln)
        kt_ref[j] = kt[:, sl]
        pt = _split3(lst[:, sl])
        gt_ref[j] = jnp.where(fwd, sum(_dot(x, upp) for x in pt),
                              jnp.where(bwd, sum(_dot(x, low) for x in pt), pre[:, sl]))


def mlstm_project(x, g, wm, wkt, wgt, bias_t, nh, tm=PROJ_ROWS):
    n, d = x.shape
    nsub = tm // MLSTM_L
    nck = n // MLSTM_L
    const = lambda i: (0, 0)
    return pl.pallas_call(
        functools.partial(_mlstm_proj_kernel, nsub=nsub, nh=nh),
        grid=(n // tm,),
        in_specs=[pl.BlockSpec((tm, d), lambda i: (i, 0)),
                  pl.BlockSpec((1, d), const),
                  pl.BlockSpec(wm.shape, const),
                  pl.BlockSpec(wkt.shape, const),
                  pl.BlockSpec(wgt.shape, const),
                  pl.BlockSpec(bias_t.shape, const)],
        out_specs=[pl.BlockSpec((tm, wm.shape[1]), lambda i: (i, 0)),
                   pl.BlockSpec((nsub, wkt.shape[0], MLSTM_L), lambda i: (i, 0, 0)),
                   pl.BlockSpec((nsub, wgt.shape[0], MLSTM_L), lambda i: (i, 0, 0))],
        out_shape=[jax.ShapeDtypeStruct((n, wm.shape[1]), BF16),
                   jax.ShapeDtypeStruct((nck, wkt.shape[0], MLSTM_L), BF16),
                   jax.ShapeDtypeStruct((nck, wgt.shape[0], MLSTM_L), F32)],
        compiler_params=_cparams(1),
    )(x, g.reshape(1, d), wm, wkt, wgt, bias_t)


def _mlstm_kernel(q_ref, kt_ref, v_ref, og_ref, gt_ref, ng_ref, y_ref, hf, hb, cst, b_sc, u_sc, cm_sc,
                  *, nc, nh, dqk, dv):
    p = pl.program_id(1)
    ln = MLSTM_L
    ri = lax.broadcasted_iota(I32, (ln, ln), 0)
    ci = lax.broadcasted_iota(I32, (ln, ln), 1)
    masks = (ci <= ri, ci >= ri)
    ones_blk = jnp.ones((ln, dv), BF16)
    lane8 = lax.broadcasted_iota(I32, (8, ln), 1)
    row8 = lax.broadcasted_iota(I32, (8, ln), 0)
    bwd_row = (row8 % 2) == 1
    edge = lane8 == jnp.where(bwd_row, 0, ln - 1)
    kk = lax.broadcasted_iota(I32, (48, 2 * dv), 0) % 16
    cc = lax.broadcasted_iota(I32, (48, 2 * dv), 1)
    sel = [jnp.logical_or(jnp.logical_and(kk == i, cc < dv), jnp.logical_and(kk == 4 + i, cc >= dv)).astype(BF16)
           for i in range(4)]
    zero4 = jnp.zeros((4, ln), F32)
    cst[...] = jnp.zeros_like(cst)
    chains = [(hh, d) for hh in range(2) for d in range(2)]

    def gate_rows(c, which):
        return [gt_ref[c if d == 0 else nc - 1 - c, pl.ds((2 * d + which) * nh + 2 * p + hh, 1), :]
                for hh, d in chains] + [zero4]
    b_all = jnp.concatenate([x for c in range(nc) for x in gate_rows(c, 1)], axis=0)
    u_all = jnp.concatenate([x for c in range(nc) for x in gate_rows(c, 0)], axis=0) - b_all
    lane_a = lax.broadcasted_iota(I32, u_all.shape, 1)
    bwd_a = (lax.broadcasted_iota(I32, u_all.shape, 0) % 2) == 1
    cf = cb = u_all
    sh = 1
    while sh < ln:
        cf = jnp.maximum(cf, jnp.where(lane_a >= sh, pltpu.roll(cf, sh, axis=1), NEG))
        cb = jnp.maximum(cb, jnp.where(lane_a < ln - sh, pltpu.roll(cb, ln - sh, axis=1), NEG))
        sh *= 2
    b_sc[...] = b_all.reshape(nc, 8, ln)
    u_sc[...] = u_all.reshape(nc, 8, ln)
    cm_sc[...] = jnp.where(bwd_a, cb, cf).reshape(nc, 8, ln)

    def step(c, ms):
        cks = [c if d == 0 else nc - 1 - c for _, d in chains]
        r0s = [pl.multiple_of(ck * ln, ln) for ck in cks]
        cprev = [cst[i] for i in range(4)]
        q = [q_ref[pl.ds(r0s[i], ln), hh * dqk:(hh + 1) * dqk] for i, (hh, _) in enumerate(chains)]
        kt = [kt_ref[cks[i], hh * dqk:(hh + 1) * dqk, :] for i, (hh, _) in enumerate(chains)]
        vaug = [jnp.concatenate([v_ref[pl.ds(r0s[i], ln), hh * dv:(hh + 1) * dv], ones_blk], axis=1)
                for i, (hh, _) in enumerate(chains)]
        s = [_dot(q[i], kt[i]) for i in range(4)]
        inter = [_dot(q[i], cprev[i].astype(BF16)) for i in range(4)]
        b = b_sc[c]
        u = u_sc[c]
        m_run = jnp.maximum(ms, cm_sc[c])
        m_end = jnp.max(jnp.where(edge, m_run, NEG), axis=1, keepdims=True)
        b_end = jnp.sum(jnp.where(edge, b, 0.0), axis=1, keepdims=True)
        wc = jnp.exp(u - m_end)
        decay = jnp.exp(ms - m_end)
        rows = jnp.concatenate([m_run[:4] * LOG2E, jnp.exp(-(b + m_run))[:4], zero4, zero4], axis=0)
        stack = jnp.concatenate(_split3(rows), axis=0)
        u2 = u * LOG2E
        ms2 = ms * LOG2E
        bc = [lax.dot_general(stack, sel[i], (((0,), (0,)), ((), ())), preferred_element_type=F32)
              for i in range(4)]
        mb = [jnp.concatenate([bc[i][:, :dv]] * (ln // dv), axis=1) for i in range(4)]
        sw = [(jnp.exp2(jnp.where(masks[d], u2[i:i + 1, :] - mb[i], NEG)) * s[i]).astype(BF16)
              for i, (_, d) in enumerate(chains)]
        intra = [_dot(sw[i], vaug[i]) for i in range(4)]
        ea = [jnp.exp2(ms2[i:i + 1, :] - bc[i][:, :dv]) for i in range(4)]
        num = [ea[i] * inter[i][:, :dv] + intra[i][:, :dv] for i in range(4)]
        den = [ea[i] * inter[i][:, dv:] + intra[i][:, dv:] for i in range(4)]
        hc = [num[i] / jnp.maximum(jnp.abs(den[i]), bc[i][:, dv:]) for i in range(4)]
        kw = [(kt[i].astype(F32) * wc[i:i + 1, :]).astype(BF16) for i in range(4)]
        c_new = [decay[i:i + 1, :] * cprev[i] + _dot(kw[i], vaug[i]) for i in range(4)]
        for i, (hh, d) in enumerate(chains):
            cst[i] = c_new[i]
            (hf if d == 0 else hb)[pl.ds(r0s[i], ln), hh * dv:(hh + 1) * dv] = hc[i]
        return b_end + m_end

    lax.fori_loop(0, nc, step, jnp.zeros((8, 1), F32))
    for hh in range(2):
        sl = slice(hh * dv, (hh + 1) * dv)
        hs = _rms(hf[:, sl] + hb[:, sl], ng_ref[:, sl])
        y_ref[:, sl] = (hs * jax.nn.sigmoid(og_ref[:, sl].astype(F32))).astype(y_ref.dtype)


def mlstm_core(main, ktc, grow, out_g, batch, seq, nh, dqk, dv):
    n = main.shape[0]
    nc = seq // MLSTM_L
    npair = nh // 2
    vblocks = (nh * dqk) // (2 * dv)
    ogblocks = (nh * dqk + nh * dv) // (2 * dv)
    return pl.pallas_call(
        functools.partial(_mlstm_kernel, nc=nc, nh=nh, dqk=dqk, dv=dv),
        grid=(batch, npair),
        in_specs=[pl.BlockSpec((seq, 2 * dqk), lambda b, p: (b, p)),
                  pl.BlockSpec((nc, 2 * dqk, MLSTM_L), lambda b, p: (b, p, 0)),
                  pl.BlockSpec((seq, 2 * dv), lambda b, p: (b, vblocks + p)),
                  pl.BlockSpec((seq, 2 * dv), lambda b, p: (b, ogblocks + p)),
                  pl.BlockSpec((nc, 4 * nh, MLSTM_L), lambda b, p: (b, 0, 0)),
                  pl.BlockSpec((1, 2 * dv), lambda b, p: (0, p))],
        out_specs=pl.BlockSpec((seq, 2 * dv), lambda b, p: (b, p)),
        out_shape=jax.ShapeDtypeStruct((n, nh * dv), BF16),
        scratch_shapes=[pltpu.VMEM((seq, 2 * dv), F32), pltpu.VMEM((seq, 2 * dv), F32),
                        pltpu.VMEM((4, dqk, 2 * dv), F32)] + [pltpu.VMEM((nc, 8, MLSTM_L), F32)] * 3,
        compiler_params=_cparams(2),
    )(main, ktc, main, main, grow, out_g.reshape(1, nh * dv).astype(F32))


def mlstm_layer(x2d, batch, seq, norm_g, w_in, b_i, b_f, out_g, w_out, ffn_g, wr):
    d = x2d.shape[1]
    nh = MLSTM_HEADS
    dv = d // nh
    dqk = dv // 2
    o1, o2, o3, o4 = nh * dqk, 2 * nh * dqk, 2 * nh * dqk + nh * dv, 2 * nh * dqk + 2 * nh * dv
    wm = jnp.concatenate([w_in[:, :o1], w_in[:, o2:o4]], axis=1).astype(BF16)
    wkt = (w_in[:, o1:o2] * (dqk ** -0.5)).T.astype(BF16)
    wgt = w_in[:, o4:].T.astype(BF16)
    bias_t = jnp.concatenate([b_i[0], b_f[0], b_i[1], b_f[1]]).reshape(4 * nh, 1).astype(F32)
    main, ktc, grow = mlstm_project(x2d, norm_g, wm, wkt, wgt, bias_t, nh)
    y = mlstm_core(main, ktc, grow, out_g, batch, seq, nh, dqk, dv)
    return mm_res_router(y, w_out.astype(BF16), x2d, ffn_g, wr.T.astype(BF16), batch)


def attention_layer(x2d, batch, seq, norm_g, w_in, q_g, k_g, sink, w_out, rel_bias, ffn_g, wr):
    n_q = rel_bias.shape[1]
    q, k, vt = attention_project(x2d, norm_g, w_in, q_g, k_g, n_q, n_q // GQA_GROUP)
    ot = attention_core(q, k, vt, rel_bias, sink, batch, seq)
    return mm_res_router(ot, w_out.astype(BF16), x2d, ffn_g, wr.T.astype(BF16), batch, a_transposed=True)


def kernel(x, rel_bias, attn_norm_g, attn_w_in, attn_q_norm_g, attn_k_norm_g, attn_sink, attn_w_out, mlstm_norm_g, mlstm_w_in, mlstm_b_i, mlstm_b_f, mlstm_out_norm_g, mlstm_w_out, ffn_norm_g, router_w, expert_w1, expert_w3, expert_w2):
    batch, seq, d = x.shape
    x2d = x.reshape(batch * seq, d)
    x1, hn, aff = attention_layer(x2d, batch, seq, attn_norm_g[0], attn_w_in[0], attn_q_norm_g[0],
                                  attn_k_norm_g[0], attn_sink[0], attn_w_out[0], rel_bias,
                                  ffn_norm_g[0], router_w[0])
    x = moe_block(x1, hn, aff, expert_w1, expert_w3, expert_w2, 0, batch)
    x1, hn, aff = mlstm_layer(x.reshape(batch * seq, d), batch, seq, mlstm_norm_g[0], mlstm_w_in[0], mlstm_b_i[0],
                              mlstm_b_f[0], mlstm_out_norm_g[0], mlstm_w_out[0], ffn_norm_g[1], router_w[1])
    return moe_block(x1, hn, aff, expert_w1, expert_w3, expert_w2, 1, batch)
```

```python
import functools
import math

import jax
import jax.numpy as jnp
from jax import lax
from jax.experimental import pallas as pl
from jax.experimental.pallas import tpu as pltpu

F32 = jnp.float32
BF16 = jnp.bfloat16
I32 = jnp.int32
U32 = jnp.uint32

RMS_EPS = 1e-6
NEG = -1e30
LOG2E = 1.4426950408889634
LANE = 128
MXU_DIM = 256
VMEM_LIMIT = 56 * 1024 * 1024
PROJ_ROWS = 1024

HEAD_DIM = 64
GQA_GROUP = 4
ATT_BLOCK = 128
NUM_BUCKETS = 32
MAX_DISTANCE = 128
CAPACITY_FACTOR = 2
MLSTM_HEADS = 8


def _cparams(n_axes, vmem=VMEM_LIMIT):
    return pltpu.CompilerParams(dimension_semantics=("arbitrary",) * n_axes, vmem_limit_bytes=vmem)


def _rms(x, g):
    return x * lax.rsqrt(jnp.mean(x * x, axis=-1, keepdims=True) + RMS_EPS) * g


def _dot(a, b):
    return jnp.dot(a, b, preferred_element_type=F32)


def _pack_bf16_pairs(x):
    half = x.shape[1] // 2
    hi = pltpu.bitcast(x[:, :half].astype(BF16).astype(F32), U32)
    lo = pltpu.bitcast(x[:, half:].astype(BF16).astype(F32), U32)
    return hi | lax.shift_right_logical(lo, jnp.uint32(16))


def _unpack_bf16_pairs(w):
    left = pltpu.bitcast(w & jnp.uint32(0xFFFF0000), F32).astype(BF16)
    right = pltpu.bitcast(lax.shift_left(w, jnp.uint32(16)), F32).astype(BF16)
    return left, right


def _dot_nt(a, b):
    return lax.dot_general(a, b, (((1,), (1,)), ((), ())), preferred_element_type=F32)


def _attn_proj_kernel(x_ref, g_ref, wq_ref, wk_ref, wvt_ref, qg_ref, kg_ref, seg_ref, q_ref, k_ref, vt_ref):
    h = _rms(x_ref[...], g_ref[...]).astype(BF16)
    seg = seg_ref[...]
    w = seg.shape[0]

    def head_norm(t, gain_ref, out_ref):
        for j in range(t.shape[1] // w):
            tj = t[:, j * w:(j + 1) * w]
            ms = _dot((tj * tj).astype(BF16), seg)
            out_ref[:, j * w:(j + 1) * w] = (tj * lax.rsqrt(ms + RMS_EPS) * gain_ref[:, j * w:(j + 1) * w]
                                             ).astype(out_ref.dtype)

    head_norm(_dot(h, wq_ref[...]), qg_ref, q_ref)
    head_norm(_dot(h, wk_ref[...]), kg_ref, k_ref)
    vt_ref[...] = _dot_nt(wvt_ref[...], h).astype(vt_ref.dtype)


def attention_project(x, g, w_in, q_g, k_g, n_q, n_kv, tm=PROJ_ROWS):
    n, d = x.shape
    dh = HEAD_DIM
    dq, dk = n_q * dh, n_kv * dh
    wq = w_in[:, :dq].astype(BF16)
    wk = w_in[:, dq:dq + dk].astype(BF16)
    wvt = w_in[:, dq + dk:].T.astype(BF16)
    qg = jnp.tile(q_g.astype(F32) * (dh ** -0.5 * LOG2E), n_q).reshape(1, dq)
    kg = jnp.tile(k_g.astype(F32), n_kv).reshape(1, dk)
    assert dq % MXU_DIM == 0 and dk % MXU_DIM == 0 and MXU_DIM % dh == 0
    hid = jnp.arange(MXU_DIM) // dh
    seg = jnp.where(hid[:, None] == hid[None, :], 1.0 / dh, 0.0).astype(BF16)
    const2 = lambda i: (0, 0)
    return pl.pallas_call(
        _attn_proj_kernel,
        grid=(n // tm,),
        in_specs=[pl.BlockSpec((tm, d), lambda i: (i, 0)),
                  pl.BlockSpec((1, d), const2),
                  pl.BlockSpec((d, dq), const2),
                  pl.BlockSpec((d, dk), const2),
                  pl.BlockSpec((dk, d), const2),
                  pl.BlockSpec((1, dq), const2),
                  pl.BlockSpec((1, dk), const2),
                  pl.BlockSpec((MXU_DIM, MXU_DIM), const2)],
        out_specs=[pl.BlockSpec((tm, dq), lambda i: (i, 0)),
                   pl.BlockSpec((tm, dk), lambda i: (i, 0)),
                   pl.BlockSpec((dk, tm), lambda i: (0, i))],
        out_shape=[jax.ShapeDtypeStruct((n, dq), BF16),
                   jax.ShapeDtypeStruct((n, dk), BF16),
                   jax.ShapeDtypeStruct((dk, n), BF16)],
        compiler_params=_cparams(1),
    )(x, g.reshape(1, d), wq, wk, wvt, qg, kg, seg)


def _mm_res_router_kernel(a_ref, w_ref, x_ref, g_ref, wr_ref, x1_ref, hn_ref, aff_ref, *, a_transposed):
    if a_transposed:
        y = lax.dot_general(a_ref[...], w_ref[...], (((0,), (0,)), ((), ())), preferred_element_type=F32)
    else:
        y = _dot(a_ref[...], w_ref[...])
    x1 = x_ref[...] + y
    x1_ref[...] = x1
    hn = _rms(x1, g_ref[...])
    hn_ref[...] = _pack_bf16_pairs(hn)
    logits = _dot_nt(wr_ref[...], hn.astype(BF16))
    mx = jnp.max(logits, axis=0, keepdims=True)
    p = jnp.exp(logits - mx)
    aff_ref[0] = p / jnp.sum(p, axis=0, keepdims=True)


def mm_res_router(a, w, x, g, wr_t, batch, a_transposed=False, tm=PROJ_ROWS):
    n, d = x.shape
    k = w.shape[0]
    e = wr_t.shape[0]
    t = n // batch
    tm = min(tm, t)
    tpb = t // tm
    a_spec = pl.BlockSpec((k, tm), lambda i: (0, i)) if a_transposed else pl.BlockSpec((tm, k), lambda i: (i, 0))
    return pl.pallas_call(
        functools.partial(_mm_res_router_kernel, a_transposed=a_transposed),
        grid=(n // tm,),
        in_specs=[a_spec,
                  pl.BlockSpec((k, d), lambda i: (0, 0)),
                  pl.BlockSpec((tm, d), lambda i: (i, 0)),
                  pl.BlockSpec((1, d), lambda i: (0, 0)),
                  pl.BlockSpec((e, d), lambda i: (0, 0))],
        out_specs=[pl.BlockSpec((tm, d), lambda i: (i, 0)),
                   pl.BlockSpec((tm, d // 2), lambda i: (i, 0)),
                   pl.BlockSpec((1, e, tm), lambda i: (i // tpb, 0, i % tpb))],
        out_shape=[jax.ShapeDtypeStruct((n, d), F32),
                   jax.ShapeDtypeStruct((n, d // 2), U32),
                   jax.ShapeDtypeStruct((batch, e, t), F32)],
        compiler_params=_cparams(1),
    )(a, w, x, g.reshape(1, d), wr_t)


def _t5_bucket(rel):
    nb = NUM_BUCKETS // 2
    ret = (rel > 0).astype(jnp.int32) * nb
    n = jnp.abs(rel)
    max_exact = nb // 2
    nf = jnp.maximum(n, 1).astype(jnp.float32)
    large = max_exact + (jnp.log(nf / max_exact) / math.log(MAX_DISTANCE / max_exact)
                         * (nb - max_exact)).astype(jnp.int32)
    large = jnp.minimum(large, nb - 1)
    return ret + jnp.where(n < max_exact, n, large)


def _attn_bucket_table():
    kk = jnp.arange(3 * ATT_BLOCK)[:, None]
    qq = jnp.arange(ATT_BLOCK)[None, :]
    rel = kk - ATT_BLOCK - qq
    return jnp.where(jnp.abs(rel) <= ATT_BLOCK, _t5_bucket(rel), -1).astype(I32)


ATT_QBLOCKS = 4


def _attn_kernel(sink_ref, rb_ref, q_ref, *rest, nb, n_kv):
    nq = ATT_QBLOCKS
    k_refs = rest[:nq + 2]
    v_refs = rest[nq + 2:2 * (nq + 2)]
    bucket_ref, ot_ref, bias_sc = rest[2 * (nq + 2):]
    n = pl.program_id(1)
    blk = ATT_BLOCK
    dh = HEAD_DIM
    gw = GQA_GROUP * blk

    @pl.when(jnp.logical_and(pl.program_id(0) == 0, n == 0))
    def _bias_table():
        bk = bucket_ref[...]
        for hq in range(n_kv * GQA_GROUP):
            acc = jnp.full(bk.shape, NEG, F32)
            for k in range(NUM_BUCKETS):
                acc = jnp.where(bk == k, rb_ref[k, hq] * LOG2E, acc)
            bias_sc[hq // GQA_GROUP, :, (hq % GQA_GROUP) * blk:(hq % GQA_GROUP + 1) * blk] = acc

    kblk = [r[...] for r in k_refs]
    vblk = [r[...] for r in v_refs]
    kidx = lax.broadcasted_iota(I32, (3 * blk, gw), 0)
    ones_rows = (lax.broadcasted_iota(I32, (16, 3 * blk), 0) == 0).astype(BF16)
    lane = lax.broadcasted_iota(I32, (1, gw), 1)
    heads = [[h * GQA_GROUP + g for g in range(GQA_GROUP)] for h in range(n_kv)]
    sk = []
    for h in range(n_kv):
        row_sink = jnp.full((1, gw), sink_ref[heads[h][-1]] * LOG2E, F32)
        for g in reversed(range(GQA_GROUP - 1)):
            row_sink = jnp.where(lane < (g + 1) * blk, sink_ref[heads[h][g]] * LOG2E, row_sink)
        sk.append(row_sink)
    keys, vt, valid = [], [], []
    for j in range(nq):
        i = n * nq + j
        keys.append(jnp.concatenate(kblk[j:j + 3], axis=0))
        vt.append(jnp.concatenate(vblk[j:j + 3], axis=1))
        valid.append(jnp.logical_and(jnp.logical_or(i > 0, kidx >= blk),
                                     jnp.logical_or(i < nb - 1, kidx < 2 * blk)))
    units = [(j, h) for j in range(nq) for h in range(n_kv)]
    q = [jnp.concatenate([q_ref[j * blk:(j + 1) * blk, hq * dh:(hq + 1) * dh] for hq in heads[h]], axis=0)
         for j, h in units]
    vaug = [jnp.concatenate([vt[j][h * dh:(h + 1) * dh, :], ones_rows], axis=0) for j, h in units]
    s = [jnp.where(valid[j], _dot_nt(keys[j][:, h * dh:(h + 1) * dh], q[u]) + bias_sc[h], NEG)
         for u, (j, h) in enumerate(units)]
    m = [jnp.maximum(jnp.max(s[u], axis=0, keepdims=True), sk[h]) for u, (j, h) in enumerate(units)]
    p = [jnp.exp2(s[u] - m[u]).astype(BF16) for u in range(len(units))]
    oa = [_dot(vaug[u], p[u]) for u in range(len(units))]
    o = [oa[u][:dh] / (oa[u][dh:dh + 1] + jnp.exp2(sk[h] - m[u])) for u, (j, h) in enumerate(units)]
    for u, (j, h) in enumerate(units):
        for g, hq in enumerate(heads[h]):
            ot_ref[hq * dh:(hq + 1) * dh, j * blk:(j + 1) * blk] = o[u][:, g * blk:(g + 1) * blk].astype(ot_ref.dtype)


def attention_core(q, k, vt, rel_bias, sink, batch, seq):
    n, dq = q.shape
    dk = k.shape[1]
    hq = rel_bias.shape[1]
    n_kv = hq // GQA_GROUP
    nb = seq // ATT_BLOCK
    blk = ATT_BLOCK
    nq = ATT_QBLOCKS
    assert nb % nq == 0

    def key_block(off):
        return lambda b, i: b * nb + jnp.clip(i * nq + off, 0, nb - 1)

    offs = range(-1, nq + 1)
    return pl.pallas_call(
        functools.partial(_attn_kernel, nb=nb, n_kv=n_kv),
        grid=(batch, nb // nq),
        in_specs=[pl.BlockSpec(memory_space=pltpu.SMEM),
                  pl.BlockSpec(memory_space=pltpu.SMEM),
                  pl.BlockSpec((nq * blk, dq), lambda b, i: (b * (nb // nq) + i, 0))]
                 + [pl.BlockSpec((blk, dk), (lambda f: lambda b, i: (f(b, i), 0))(key_block(o))) for o in offs]
                 + [pl.BlockSpec((dk, blk), (lambda f: lambda b, i: (0, f(b, i)))(key_block(o))) for o in offs]
                 + [pl.BlockSpec((3 * blk, blk), lambda b, i: (0, 0))],
        out_specs=pl.BlockSpec((dq, nq * blk), lambda b, i: (0, b * (nb // nq) + i)),
        out_shape=jax.ShapeDtypeStruct((dq, n), BF16),
        scratch_shapes=[pltpu.VMEM((n_kv, 3 * blk, GQA_GROUP * blk), F32)],
        compiler_params=_cparams(2),
    )(sink.astype(F32), rel_bias.astype(F32), q, *([k] * (nq + 2)), *([vt] * (nq + 2)), _attn_bucket_table())


def _prefix_incl(x, tri):
    r, t = x.shape
    nck = t // LANE
    assert nck <= LANE
    xb = x.astype(BF16)
    local = [_dot(xb[:, c * LANE:(c + 1) * LANE], tri) for c in range(nck)]
    tot = jnp.concatenate([p[:, LANE - 1:LANE] for p in local] + [jnp.zeros((r, LANE - nck), F32)], axis=1)
    lane = lax.broadcasted_iota(I32, tot.shape, 1)
    inc = tot
    sh = 1
    while sh < nck:
        inc = inc + jnp.where(lane >= sh, pltpu.roll(inc, sh, axis=1), 0.0)
        sh *= 2
    offs = inc - tot
    return jnp.concatenate([local[c] + offs[:, c:c + 1] for c in range(nck)], axis=1)


BF16_EXACT = 256
TOK_SPLIT = 64


def _select_kernel(aff_ref, cmp_ref, idx_ref, dest_ref, offc_ref, rng_ref, pos_sc, dest_sc, vals_sc, *, cap, rchunk):
    aff = aff_ref[0]
    ne, t = aff.shape
    bidx = pl.program_id(0)
    bits = pltpu.bitcast(aff, I32)

    def search(i, lo):
        cand = lo | lax.shift_left(jnp.int32(1), 30 - i)
        cnt = jnp.sum((bits >= cand).astype(I32), axis=1, keepdims=True)
        return jnp.where(cnt >= cap, cand, lo)

    thr = lax.fori_loop(0, 31, search, jnp.zeros((ne, 1), I32))
    gt = bits > thr
    eq = bits == thr
    need = (cap - jnp.sum(gt.astype(I32), axis=1, keepdims=True)).astype(F32)
    ri = lax.broadcasted_iota(I32, (LANE, LANE), 0)
    ci = lax.broadcasted_iota(I32, (LANE, LANE), 1)
    tri = (ri <= ci).astype(BF16)
    eqf = eq.astype(F32)
    rank_eq = _prefix_incl(eqf, tri) - eqf
    sel = jnp.logical_or(gt, jnp.logical_and(eq, rank_eq < need))
    self_ = sel.astype(F32)
    kt = jnp.sum(self_, axis=0, keepdims=True)
    pre = _prefix_incl(jnp.concatenate([self_, jnp.broadcast_to(kt, (8, t))], axis=0), tri)
    pos_sc[...] = jnp.where(sel, pre[:ne] - 1.0, -1.0)
    end = pre[ne:ne + 1]
    off = end - kt
    lr = lax.broadcasted_iota(I32, (ne, ne), 0)
    lc = lax.broadcasted_iota(I32, (ne, ne), 1)
    jexp = _dot((lc < lr).astype(BF16), self_.astype(BF16))
    dest_sc[...] = off + jexp

    tok = lax.broadcasted_iota(I32, (1, t), 1)
    vals_sc[0:1, :] = (tok // TOK_SPLIT).astype(F32)
    vals_sc[1:2, :] = (tok % TOK_SPLIT).astype(F32)
    vals_sc[7:8, :] = jnp.zeros((1, t), F32)
    slot = lax.broadcasted_iota(I32, (cap, 1), 0).astype(F32)

    def compact(e, carry):
        d = dest_sc[pl.ds(e, 1), :]
        dh = jnp.floor(d * (1.0 / LANE))
        g = aff_ref[0, pl.ds(e, 1), :]
        g0 = g.astype(BF16).astype(F32)
        g1 = (g - g0).astype(BF16).astype(F32)
        vals_sc[2:3, :] = dh
        vals_sc[3:4, :] = d - dh * LANE
        vals_sc[4:5, :] = g0
        vals_sc[5:6, :] = g1
        vals_sc[6:7, :] = g - g0 - g1
        onehot = (pos_sc[pl.ds(e, 1), :] == slot).astype(BF16)
        out = _dot_nt(vals_sc[...].astype(BF16), onehot)
        cmp_ref[0, e] = out
        idx_ref[e, 0] = (out[0:1] * TOK_SPLIT + out[1:2]).astype(I32) + bidx * t
        dest_ref[e + 1, 0] = (out[2:3] * LANE + out[3:4]).astype(I32) + bidx * (ne * cap)
        return carry

    lax.fori_loop(0, ne, compact, 0)
    idx_ref[ne, 0] = idx_ref[0, 0]
    dest_ref[0, 0] = (pl.num_programs(0) * ne + bidx) * cap + lax.broadcasted_iota(I32, (1, cap), 1)

    eh = jnp.floor(end * (1.0 / LANE))
    oh = jnp.floor(off * (1.0 / LANE))
    v4 = jnp.concatenate([oh, off - oh * LANE, eh, end - eh * LANE, jnp.zeros((4, t), F32)], axis=0)
    eye = (ri == ci).astype(BF16)
    for c in range(t // LANE):
        offc_ref[0, c * LANE:(c + 1) * LANE, :] = _dot_nt(eye, v4[:, c * LANE:(c + 1) * LANE].astype(BF16))

    nchunk = (ne * cap) // rchunk
    low = (lax.broadcasted_iota(I32, (nchunk, 1), 0) * rchunk).astype(F32)
    first = jnp.sum((end <= low).astype(F32), axis=1, keepdims=True)
    last = jnp.sum((end <= low + (rchunk - 1)).astype(F32), axis=1, keepdims=True)
    lane = lax.broadcasted_iota(I32, (nchunk, LANE), 1)
    tiles = jnp.where(lane < LANE // 2, jnp.floor(first * (1.0 / LANE)), jnp.floor(last * (1.0 / LANE)))
    rng_ref[0] = tiles.astype(I32)


COMBINE_ROWS = 1024
COMBINE_TILES = 5


def route_select(aff, cap, rchunk):
    batch, ne, t = aff.shape
    nchunk = ne * cap // rchunk
    assert t <= TOK_SPLIT * BF16_EXACT and ne * cap <= LANE * BF16_EXACT and LANE <= BF16_EXACT
    return pl.pallas_call(
        functools.partial(_select_kernel, cap=cap, rchunk=rchunk),
        grid=(batch,),
        in_specs=[pl.BlockSpec((1, ne, t), lambda b: (b, 0, 0))],
        out_specs=[pl.BlockSpec((1, ne, 8, cap), lambda b: (b, 0, 0, 0)),
                   pl.BlockSpec((ne + 1, 1, 1, cap), lambda b: (0, b, 0, 0)),
                   pl.BlockSpec((ne + 1, 1, 1, cap), lambda b: (0, b, 0, 0)),
                   pl.BlockSpec((1, t, 8), lambda b: (b, 0, 0)),
                   pl.BlockSpec((1, nchunk, LANE), lambda b: (b, 0, 0))],
        out_shape=[jax.ShapeDtypeStruct((batch, ne, 8, cap), F32),
                   jax.ShapeDtypeStruct((ne + 1, batch, 1, cap), I32),
                   jax.ShapeDtypeStruct((ne + 1, batch, 1, cap), I32),
                   jax.ShapeDtypeStruct((batch, t, 8), F32),
                   jax.ShapeDtypeStruct((batch, nchunk, LANE), I32)],
        scratch_shapes=[pltpu.VMEM((ne, t), F32), pltpu.VMEM((ne, t), F32), pltpu.VMEM((8, t), F32)],
        compiler_params=_cparams(1),
    )(aff)


def _ffn_kernel(idx_sm, dest_sm, hn_hbm, cmp_ref, w1_ref, w3_ref, w2_ref, r_hbm,
                xsu, xsb, yacc, ysc, wb1, wb3, wb2, gsem, ssem, *, batch, cap, nf, rt):
    e = pl.program_id(0)
    f = pl.program_id(1)
    ne = pl.num_programs(0)
    rows = batch * cap
    sub = rows // nf

    def gather_start(block, r):
        t = idx_sm[block * rows + r]
        pltpu.make_async_copy(hn_hbm.at[pl.ds(t, 1), :], xsu.at[pl.ds(r, 1), :], gsem).start()

    def gather_wait():
        pltpu.make_async_copy(hn_hbm.at[pl.ds(0, rows), :], xsu, gsem).wait()

    def scatter_start(block, r):
        d = dest_sm[block * rows + r]
        pltpu.make_async_copy(ysc.at[pl.ds(r, 1), :], r_hbm.at[pl.ds(d, 1), :], ssem).start()

    def scatter_wait():
        pltpu.make_async_copy(ysc, r_hbm.at[pl.ds(0, rows), :], ssem).wait()

    @pl.when(jnp.logical_and(e == 0, f == 0))
    def _prologue():
        def issue(r, c):
            gather_start(0, r)
            return c
        lax.fori_loop(0, rows, issue, 0)
        ysc[...] = jnp.zeros_like(ysc)

    @pl.when(f == 0)
    def _rows_ready():
        gather_wait()
        half = xsu.shape[1]
        xsb[:, :half], xsb[:, half:] = _unpack_bf16_pairs(xsu[...])
        yacc[...] = jnp.zeros_like(yacc)

    for r in range(sub):
        gather_start(e + 1, f * sub + r)
        scatter_start(e, f * sub + r)

    wb1[...] = w1_ref[0, 0].astype(BF16)
    for r in range(rows // rt):
        xs = xsb[r * rt:(r + 1) * rt, :]
        a = _dot(xs, wb1[...])
        if r == 0:
            wb3[...] = w3_ref[0, 0].astype(BF16)
        u = _dot(xs, wb3[...])
        hmid = (a * jax.nn.sigmoid(a) * u).astype(BF16)
        if r == 0:
            wb2[...] = w2_ref[0, 0].astype(BF16)
        yacc[r * rt:(r + 1) * rt, :] += _dot(hmid, wb2[...])

    @pl.when(f == nf - 1)
    def _finish():
        scatter_wait()
        ri = lax.broadcasted_iota(I32, (cap, cap), 0)
        ci = lax.broadcasted_iota(I32, (cap, cap), 1)
        eye = (ri == ci).astype(BF16)
        rows8 = cmp_ref[...].reshape(batch * 8, cap).astype(BF16)
        gt = _dot_nt(eye, rows8)
        for b in range(batch):
            g = gt[:, 8 * b + 4:8 * b + 5] + gt[:, 8 * b + 5:8 * b + 6] + gt[:, 8 * b + 6:8 * b + 7]
            ysc[b * cap:(b + 1) * cap, :] = _pack_bf16_pairs(yacc[b * cap:(b + 1) * cap, :] * g)

        @pl.when(e == ne - 1)
        def _epilogue():
            def issue(r, c):
                scatter_start(ne, r)
                return c
            lax.fori_loop(0, rows, issue, 0)
            scatter_wait()
            gather_wait()


def expert_ffn(hn, cmp, idx_flat, dest_flat, w1, w3, w2, layer, cap, tf=512, rt=512):
    d = w1.shape[2]
    batch, ne = cmp.shape[0], cmp.shape[1]
    dff = w1.shape[3]
    nf = dff // tf
    rows = batch * cap
    rt = min(rt, rows)
    grid_spec = pltpu.PrefetchScalarGridSpec(
        num_scalar_prefetch=2,
        grid=(ne, nf),
        in_specs=[pl.BlockSpec(memory_space=pl.ANY),
                  pl.BlockSpec((batch, 1, 8, cap), lambda e, f, *_: (0, e, 0, 0)),
                  pl.BlockSpec((1, 1, d, tf), lambda e, f, *_: (layer, e, 0, f)),
                  pl.BlockSpec((1, 1, d, tf), lambda e, f, *_: (layer, e, 0, f)),
                  pl.BlockSpec((1, 1, tf, d), lambda e, f, *_: (layer, e, f, 0))],
        out_specs=pl.BlockSpec(memory_space=pl.ANY),
        scratch_shapes=[pltpu.VMEM((rows, d // 2), U32), pltpu.VMEM((rows, d), BF16),
                        pltpu.VMEM((rows, d), F32), pltpu.VMEM((rows, d // 2), U32),
                        pltpu.VMEM((d, tf), BF16), pltpu.VMEM((d, tf), BF16), pltpu.VMEM((tf, d), BF16),
                        pltpu.SemaphoreType.DMA, pltpu.SemaphoreType.DMA])
    return pl.pallas_call(
        functools.partial(_ffn_kernel, batch=batch, cap=cap, nf=nf, rt=rt),
        grid_spec=grid_spec,
        out_shape=jax.ShapeDtypeStruct((batch * ne * cap + rows, d // 2), U32),
        compiler_params=_cparams(2),
    )(idx_flat, dest_flat, hn, cmp, w1, w3, w2)


def _combine_kernel(tlo_sm, thi_sm, r_ref, offc_ref, x_hbm, o_hbm, acc, xsem, osem, *, batch, nchunk, rchunk):
    g = pl.program_id(0)
    b = g // nchunk
    j = g % nchunk
    slot = b % 2
    ntile = acc.shape[1] // LANE

    def x_copy(bb):
        return pltpu.make_async_copy(x_hbm.at[bb], acc.at[bb % 2], xsem.at[bb % 2])

    def o_copy(bb):
        return pltpu.make_async_copy(acc.at[bb % 2], o_hbm.at[bb], osem.at[bb % 2])

    @pl.when(g == 0)
    def _first():
        x_copy(0).start()

    @pl.when(j == 0)
    def _batch_start():
        x_copy(b).wait()

    @pl.when(j == nchunk // 2)
    def _mid():
        @pl.when(b > 0)
        def _():
            o_copy(b - 1).wait()

        @pl.when(b + 1 < batch)
        def _():
            x_copy(b + 1).start()

    half = r_ref.shape[1]
    left, right = _unpack_bf16_pairs(r_ref[...])
    rowid = (j * rchunk + lax.broadcasted_iota(I32, (1, rchunk), 1)).astype(F32)

    def contrib(i, valid):
        t0 = pl.multiple_of(i * LANE, LANE)
        oc = offc_ref[0, pl.ds(t0, LANE), :]
        off = oc[:, 0:1] * LANE + oc[:, 1:2]
        end = oc[:, 2:3] * LANE + oc[:, 3:4]
        p = jnp.logical_and(jnp.logical_and(rowid >= off, rowid < end), valid).astype(BF16)
        return t0, (_dot(p, left), _dot(p, right))

    def add(t0, y):
        acc[slot, pl.ds(t0, LANE), :half] += y[0]
        acc[slot, pl.ds(t0, LANE), half:] += y[1]

    def tile(i, c):
        add(*contrib(i, True))
        return c

    tlo = tlo_sm[g]
    thi = thi_sm[g]
    parts = [contrib(jnp.minimum(tlo + k, ntile - 1), tlo + k <= thi) for k in range(COMBINE_TILES)]
    for t0, y in parts:
        add(t0, y)
    lax.fori_loop(tlo + COMBINE_TILES, thi + 1, tile, 0)

    @pl.when(j == nchunk - 1)
    def _batch_end():
        o_copy(b).start()

        @pl.when(b == batch - 1)
        def _():
            o_copy(b).wait()


def combine(rbuf, offc, tlo, thi, x, rows_per_batch, rchunk):
    batch, t, d = x.shape
    nchunk = rows_per_batch // rchunk
    assert nchunk >= 2
    grid_spec = pltpu.PrefetchScalarGridSpec(
        num_scalar_prefetch=2,
        grid=(batch * nchunk,),
        in_specs=[pl.BlockSpec((rchunk, d // 2), lambda g, *_: (g, 0)),
                  pl.BlockSpec((1, t, 8), lambda g, *_: (g // nchunk, 0, 0)),
                  pl.BlockSpec(memory_space=pl.ANY)],
        out_specs=pl.BlockSpec(memory_space=pl.ANY),
        scratch_shapes=[pltpu.VMEM((2, t, d), F32), pltpu.SemaphoreType.DMA((2,)), pltpu.SemaphoreType.DMA((2,))])
    return pl.pallas_call(
        functools.partial(_combine_kernel, batch=batch, nchunk=nchunk, rchunk=rchunk),
        grid_spec=grid_spec,
        out_shape=jax.ShapeDtypeStruct((batch, t, d), F32),
        compiler_params=_cparams(1),
    )(tlo, thi, rbuf, offc, x)


def moe_block(x1, hn, aff, w1, w3, w2, layer, batch):
    n, d = x1.shape
    t = n // batch
    ne = aff.shape[1]
    cap = CAPACITY_FACTOR * t // ne
    cmp, idx, dest, offc, rng = route_select(aff, cap, COMBINE_ROWS)
    rbuf = expert_ffn(hn, cmp, idx.reshape(-1), dest.reshape(-1), w1, w3, w2, layer, cap)
    tlo = rng[:, :, 0].reshape(-1)
    thi = rng[:, :, LANE - 1].reshape(-1)
    return combine(rbuf, offc, tlo, thi, x1.reshape(batch, t, d), ne * cap, COMBINE_ROWS)


MLSTM_L = 256
MLSTM_UNROLL = 4


def _log_sigmoid(x):
    return jnp.minimum(x, 0.0) - jnp.log1p(jnp.exp(-jnp.abs(x)))


def _split3(x):
    x0 = x.astype(BF16)
    r = x - x0.astype(F32)
    x1 = r.astype(BF16)
    return x0, x1, (r - x1.astype(F32)).astype(BF16)


def _mlstm_proj_kernel(x_ref, g_ref, wm_ref, wkt_ref, wgt_ref, bt_ref, main_ref, kt_ref, gt_ref, *, nsub, nh):
    ln = MLSTM_L
    h = _rms(x_ref[...], g_ref[...]).astype(BF16)
    main_ref[...] = _dot(h, wm_ref[...]).astype(main_ref.dtype)
    kt = _dot_nt(wkt_ref[...], h).astype(kt_ref.dtype)
    pre = _dot_nt(wgt_ref[...], h) + bt_ref[...]
    lst = _log_sigmoid(pre)
    ri = lax.broadcasted_iota(I32, (ln, ln), 0)
    ci = lax.broadcasted_iota(I32, (ln, ln), 1)
    low = (ci <= ri).astype(BF16)
    upp = (ci >= ri).astype(BF16)
    row = lax.broadcasted_iota(I32, (pre.shape[0], ln), 0)
    fwd = jnp.logical_and(row >= nh, row < 2 * nh)
    bwd = row >= 3 * nh
    for j in range(nsub):
        sl = slice(j * ln, (j + 1) * ln)
        kt_ref[j] = kt[:, sl]
        pt = _split3(lst[:, sl])
        gt_ref[j] = jnp.where(fwd, sum(_dot(x, upp) for x in pt),
                              jnp.where(bwd, sum(_dot(x, low) for x in pt), pre[:, sl]))


def mlstm_project(x, g, wm, wkt, wgt, bias_t, nh, tm=PROJ_ROWS):
    n, d = x.shape
    nsub = tm // MLSTM_L
    nck = n // MLSTM_L
    const = lambda i: (0, 0)
    return pl.pallas_call(
        functools.partial(_mlstm_proj_kernel, nsub=nsub, nh=nh),
        grid=(n // tm,),
        in_specs=[pl.BlockSpec((tm, d), lambda i: (i, 0)),
                  pl.BlockSpec((1, d), const),
                  pl.BlockSpec(wm.shape, const),
                  pl.BlockSpec(wkt.shape, const),
                  pl.BlockSpec(wgt.shape, const),
                  pl.BlockSpec(bias_t.shape, const)],
        out_specs=[pl.BlockSpec((tm, wm.shape[1]), lambda i: (i, 0)),
                   pl.BlockSpec((nsub, wkt.shape[0], MLSTM_L), lambda i: (i, 0, 0)),
                   pl.BlockSpec((nsub, wgt.shape[0], MLSTM_L), lambda i: (i, 0, 0))],
        out_shape=[jax.ShapeDtypeStruct((n, wm.shape[1]), BF16),
                   jax.ShapeDtypeStruct((nck, wkt.shape[0], MLSTM_L), BF16),
                   jax.ShapeDtypeStruct((nck, wgt.shape[0], MLSTM_L), F32)],
        compiler_params=_cparams(1),
    )(x, g.reshape(1, d), wm, wkt, wgt, bias_t)


def _mlstm_kernel(q_ref, kt_ref, v_ref, og_ref, gt_ref, ng_ref, y_ref, hf, hb, cst, b_sc, u_sc, cm_sc,
                  *, nc, nh, dqk, dv):
    p = pl.program_id(1)
    ln = MLSTM_L
    ri = lax.broadcasted_iota(I32, (ln, ln), 0)
    ci = lax.broadcasted_iota(I32, (ln, ln), 1)
    masks = (ci <= ri, ci >= ri)
    ones_blk = jnp.ones((ln, dv), BF16)
    lane8 = lax.broadcasted_iota(I32, (8, ln), 1)
    row8 = lax.broadcasted_iota(I32, (8, ln), 0)
    bwd_row = (row8 % 2) == 1
    edge = lane8 == jnp.where(bwd_row, 0, ln - 1)
    kk = lax.broadcasted_iota(I32, (48, 2 * dv), 0) % 16
    cc = lax.broadcasted_iota(I32, (48, 2 * dv), 1)
    sel = [jnp.logical_or(jnp.logical_and(kk == i, cc < dv), jnp.logical_and(kk == 4 + i, cc >= dv)).astype(BF16)
           for i in range(4)]
    zero4 = jnp.zeros((4, ln), F32)
    cst[...] = jnp.zeros_like(cst)
    chains = [(hh, d) for hh in range(2) for d in range(2)]

    def gate_rows(c, which):
        return [gt_ref[c if d == 0 else nc - 1 - c, pl.ds((2 * d + which) * nh + 2 * p + hh, 1), :]
                for hh, d in chains] + [zero4]
    b_all = jnp.concatenate([x for c in range(nc) for x in gate_rows(c, 1)], axis=0)
    u_all = jnp.concatenate([x for c in range(nc) for x in gate_rows(c, 0)], axis=0) - b_all
    lane_a = lax.broadcasted_iota(I32, u_all.shape, 1)
    bwd_a = (lax.broadcasted_iota(I32, u_all.shape, 0) % 2) == 1
    cf = cb = u_all
    sh = 1
    while sh < ln:
        cf = jnp.maximum(cf, jnp.where(lane_a >= sh, pltpu.roll(cf, sh, axis=1), NEG))
        cb = jnp.maximum(cb, jnp.where(lane_a < ln - sh, pltpu.roll(cb, ln - sh, axis=1), NEG))
        sh *= 2
    b_sc[...] = b_all.reshape(nc, 8, ln)
    u_sc[...] = u_all.reshape(nc, 8, ln)
    cm_sc[...] = jnp.where(bwd_a, cb, cf).reshape(nc, 8, ln)

    def step(c, ms):
        cks = [c if d == 0 else nc - 1 - c for _, d in chains]
        r0s = [pl.multiple_of(ck * ln, ln) for ck in cks]
        cprev = [cst[i] for i in range(4)]
        q = [q_ref[pl.ds(r0s[i], ln), hh * dqk:(hh + 1) * dqk] for i, (hh, _) in enumerate(chains)]
        kt = [kt_ref[cks[i], hh * dqk:(hh + 1) * dqk, :] for i, (hh, _) in enumerate(chains)]
        vaug = [jnp.concatenate([v_ref[pl.ds(r0s[i], ln), hh * dv:(hh + 1) * dv], ones_blk], axis=1)
                for i, (hh, _) in enumerate(chains)]
        s = [_dot(q[i], kt[i]) for i in range(4)]
        inter = [_dot(q[i], cprev[i].astype(BF16)) for i in range(4)]
        b = b_sc[c]
        u = u_sc[c]
        m_run = jnp.maximum(ms, cm_sc[c])
        m_end = jnp.max(jnp.where(edge, m_run, NEG), axis=1, keepdims=True)
        b_end = jnp.sum(jnp.where(edge, b, 0.0), axis=1, keepdims=True)
        wc = jnp.exp(u - m_end)
        decay = jnp.exp(ms - m_end)
        rows = jnp.concatenate([m_run[:4] * LOG2E, jnp.exp(-(b + m_run))[:4], zero4, zero4], axis=0)
        stack = jnp.concatenate(_split3(rows), axis=0)
        u2 = u * LOG2E
        ms2 = ms * LOG2E
        bc = [lax.dot_general(stack, sel[i], (((0,), (0,)), ((), ())), preferred_element_type=F32)
              for i in range(4)]
        mb = [jnp.concatenate([bc[i][:, :dv]] * (ln // dv), axis=1) for i in range(4)]
        sw = [(jnp.exp2(jnp.where(masks[d], u2[i:i + 1, :] - mb[i], NEG)) * s[i]).astype(BF16)
              for i, (_, d) in enumerate(chains)]
        intra = [_dot(sw[i], vaug[i]) for i in range(4)]
        ea = [jnp.exp2(ms2[i:i + 1, :] - bc[i][:, :dv]) for i in range(4)]
        num = [ea[i] * inter[i][:, :dv] + intra[i][:, :dv] for i in range(4)]
        den = [ea[i] * inter[i][:, dv:] + intra[i][:, dv:] for i in range(4)]
        hc = [num[i] / jnp.maximum(jnp.abs(den[i]), bc[i][:, dv:]) for i in range(4)]
        kw = [(kt[i].astype(F32) * wc[i:i + 1, :]).astype(BF16) for i in range(4)]
        c_new = [decay[i:i + 1, :] * cprev[i] + _dot(kw[i], vaug[i]) for i in range(4)]
        for i, (hh, d) in enumerate(chains):
            cst[i] = c_new[i]
            (hf if d == 0 else hb)[pl.ds(r0s[i], ln), hh * dv:(hh + 1) * dv] = hc[i]
        return b_end + m_end

    unroll = math.gcd(nc, MLSTM_UNROLL)

    def steps(cu, ms):
        for k in range(unroll):
            ms = step(cu * unroll + k, ms)
        return ms

    lax.fori_loop(0, nc // unroll, steps, jnp.zeros((8, 1), F32))
    for hh in range(2):
        sl = slice(hh * dv, (hh + 1) * dv)
        hs = _rms(hf[:, sl] + hb[:, sl], ng_ref[:, sl])
        y_ref[:, sl] = (hs * jax.nn.sigmoid(og_ref[:, sl].astype(F32))).astype(y_ref.dtype)


def mlstm_core(main, ktc, grow, out_g, batch, seq, nh, dqk, dv):
    n = main.shape[0]
    nc = seq // MLSTM_L
    npair = nh // 2
    vblocks = (nh * dqk) // (2 * dv)
    ogblocks = (nh * dqk + nh * dv) // (2 * dv)
    return pl.pallas_call(
        functools.partial(_mlstm_kernel, nc=nc, nh=nh, dqk=dqk, dv=dv),
        grid=(batch, npair),
        in_specs=[pl.BlockSpec((seq, 2 * dqk), lambda b, p: (b, p)),
                  pl.BlockSpec((nc, 2 * dqk, MLSTM_L), lambda b, p: (b, p, 0)),
                  pl.BlockSpec((seq, 2 * dv), lambda b, p: (b, vblocks + p)),
                  pl.BlockSpec((seq, 2 * dv), lambda b, p: (b, ogblocks + p)),
                  pl.BlockSpec((nc, 4 * nh, MLSTM_L), lambda b, p: (b, 0, 0)),
                  pl.BlockSpec((1, 2 * dv), lambda b, p: (0, p))],
        out_specs=pl.BlockSpec((seq, 2 * dv), lambda b, p: (b, p)),
        out_shape=jax.ShapeDtypeStruct((n, nh * dv), BF16),
        scratch_shapes=[pltpu.VMEM((seq, 2 * dv), F32), pltpu.VMEM((seq, 2 * dv), F32),
                        pltpu.VMEM((4, dqk, 2 * dv), F32)] + [pltpu.VMEM((nc, 8, MLSTM_L), F32)] * 3,
        compiler_params=_cparams(2),
    )(main, ktc, main, main, grow, out_g.reshape(1, nh * dv).astype(F32))


def mlstm_layer(x2d, batch, seq, norm_g, w_in, b_i, b_f, out_g, w_out, ffn_g, wr):
    d = x2d.shape[1]
    nh = MLSTM_HEADS
    dv = d // nh
    dqk = dv // 2
    o1, o2, o3, o4 = nh * dqk, 2 * nh * dqk, 2 * nh * dqk + nh * dv, 2 * nh * dqk + 2 * nh * dv
    wm = jnp.concatenate([w_in[:, :o1], w_in[:, o2:o4]], axis=1).astype(BF16)
    wkt = (w_in[:, o1:o2] * (dqk ** -0.5)).T.astype(BF16)
    wgt = w_in[:, o4:].T.astype(BF16)
    bias_t = jnp.concatenate([b_i[0], b_f[0], b_i[1], b_f[1]]).reshape(4 * nh, 1).astype(F32)
    main, ktc, grow = mlstm_project(x2d, norm_g, wm, wkt, wgt, bias_t, nh)
    y = mlstm_core(main, ktc, grow, out_g, batch, seq, nh, dqk, dv)
    return mm_res_router(y, w_out.astype(BF16), x2d, ffn_g, wr.T.astype(BF16), batch)


def attention_layer(x2d, batch, seq, norm_g, w_in, q_g, k_g, sink, w_out, rel_bias, ffn_g, wr):
    n_q = rel_bias.shape[1]
    q, k, vt = attention_project(x2d, norm_g, w_in, q_g, k_g, n_q, n_q // GQA_GROUP)
    ot = attention_core(q, k, vt, rel_bias, sink, batch, seq)
    return mm_res_router(ot, w_out.astype(BF16), x2d, ffn_g, wr.T.astype(BF16), batch, a_transposed=True)


def kernel(x, rel_bias, attn_norm_g, attn_w_in, attn_q_norm_g, attn_k_norm_g, attn_sink, attn_w_out, mlstm_norm_g, mlstm_w_in, mlstm_b_i, mlstm_b_f, mlstm_out_norm_g, mlstm_w_out, ffn_norm_g, router_w, expert_w1, expert_w3, expert_w2):
    batch, seq, d = x.shape
    x2d = x.reshape(batch * seq, d)
    x1, hn, aff = attention_layer(x2d, batch, seq, attn_norm_g[0], attn_w_in[0], attn_q_norm_g[0],
                                  attn_k_norm_g[0], attn_sink[0], attn_w_out[0], rel_bias,
                                  ffn_norm_g[0], router_w[0])
    x = moe_block(x1, hn, aff, expert_w1, expert_w3, expert_w2, 0, batch)
    x1, hn, aff = mlstm_layer(x.reshape(batch * seq, d), batch, seq, mlstm_norm_g[0], mlstm_w_in[0], mlstm_b_i[0],
                              mlstm_b_f[0], mlstm_out_norm_g[0], mlstm_w_out[0], ffn_norm_g[1], router_w[1])
    return moe_block(x1, hn, aff, expert_w1, expert_w3, expert_w2, 1, batch)
```

```python
import functools
import math

import jax
import jax.numpy as jnp
from jax import lax
from jax.experimental import pallas as pl
from jax.experimental.pallas import tpu as pltpu

F32 = jnp.float32
BF16 = jnp.bfloat16
I32 = jnp.int32
U32 = jnp.uint32

RMS_EPS = 1e-6
NEG = -1e30
LOG2E = 1.4426950408889634
LANE = 128
MXU_DIM = 256
VMEM_LIMIT = 56 * 1024 * 1024
PROJ_ROWS = 1024

HEAD_DIM = 64
GQA_GROUP = 4
ATT_BLOCK = 128
NUM_BUCKETS = 32
MAX_DISTANCE = 128
CAPACITY_FACTOR = 2
MLSTM_HEADS = 8


def _cparams(n_axes, vmem=VMEM_LIMIT):
    return pltpu.CompilerParams(dimension_semantics=("arbitrary",) * n_axes, vmem_limit_bytes=vmem)


def _rms(x, g):
    return x * lax.rsqrt(jnp.mean(x * x, axis=-1, keepdims=True) + RMS_EPS) * g


def _dot(a, b):
    return jnp.dot(a, b, preferred_element_type=F32)


def _pack_bf16_pairs(x):
    half = x.shape[1] // 2
    hi = pltpu.bitcast(x[:, :half].astype(BF16).astype(F32), U32)
    lo = pltpu.bitcast(x[:, half:].astype(BF16).astype(F32), U32)
    return hi | lax.shift_right_logical(lo, jnp.uint32(16))


def _unpack_bf16_pairs(w):
    left = pltpu.bitcast(w & jnp.uint32(0xFFFF0000), F32).astype(BF16)
    right = pltpu.bitcast(lax.shift_left(w, jnp.uint32(16)), F32).astype(BF16)
    return left, right


def _dot_nt(a, b):
    return lax.dot_general(a, b, (((1,), (1,)), ((), ())), preferred_element_type=F32)


def _attn_proj_kernel(x_ref, g_ref, wq_ref, wk_ref, wvt_ref, qg_ref, kg_ref, seg_ref, q_ref, k_ref, vt_ref):
    h = _rms(x_ref[...], g_ref[...]).astype(BF16)
    seg = seg_ref[...]
    w = seg.shape[0]

    def head_norm(t, gain_ref, out_ref):
        for j in range(t.shape[1] // w):
            tj = t[:, j * w:(j + 1) * w]
            ms = _dot((tj * tj).astype(BF16), seg)
            out_ref[:, j * w:(j + 1) * w] = (tj * lax.rsqrt(ms + RMS_EPS) * gain_ref[:, j * w:(j + 1) * w]
                                             ).astype(out_ref.dtype)

    head_norm(_dot(h, wq_ref[...]), qg_ref, q_ref)
    head_norm(_dot(h, wk_ref[...]), kg_ref, k_ref)
    vt = _dot_nt(wvt_ref[...], h).astype(vt_ref.dtype)
    for j in range(vt_ref.shape[0]):
        vt_ref[j] = vt[:, j * ATT_BLOCK:(j + 1) * ATT_BLOCK]


def attention_project(x, g, w_in, q_g, k_g, n_q, n_kv, tm=PROJ_ROWS):
    n, d = x.shape
    dh = HEAD_DIM
    dq, dk = n_q * dh, n_kv * dh
    wq = w_in[:, :dq].astype(BF16)
    wk = w_in[:, dq:dq + dk].astype(BF16)
    wvt = w_in[:, dq + dk:].T.astype(BF16)
    qg = jnp.tile(q_g.astype(F32) * (dh ** -0.5 * LOG2E), n_q).reshape(1, dq)
    kg = jnp.tile(k_g.astype(F32), n_kv).reshape(1, dk)
    assert dq % MXU_DIM == 0 and dk % MXU_DIM == 0 and MXU_DIM % dh == 0
    hid = jnp.arange(MXU_DIM) // dh
    seg = jnp.where(hid[:, None] == hid[None, :], 1.0 / dh, 0.0).astype(BF16)
    const2 = lambda i: (0, 0)
    return pl.pallas_call(
        _attn_proj_kernel,
        grid=(n // tm,),
        in_specs=[pl.BlockSpec((tm, d), lambda i: (i, 0)),
                  pl.BlockSpec((1, d), const2),
                  pl.BlockSpec((d, dq), const2),
                  pl.BlockSpec((d, dk), const2),
                  pl.BlockSpec((dk, d), const2),
                  pl.BlockSpec((1, dq), const2),
                  pl.BlockSpec((1, dk), const2),
                  pl.BlockSpec((MXU_DIM, MXU_DIM), const2)],
        out_specs=[pl.BlockSpec((tm, dq), lambda i: (i, 0)),
                   pl.BlockSpec((tm, dk), lambda i: (i, 0)),
                   pl.BlockSpec((tm // ATT_BLOCK, dk, ATT_BLOCK), lambda i: (i, 0, 0))],
        out_shape=[jax.ShapeDtypeStruct((n, dq), BF16),
                   jax.ShapeDtypeStruct((n, dk), BF16),
                   jax.ShapeDtypeStruct((n // ATT_BLOCK, dk, ATT_BLOCK), BF16)],
        compiler_params=_cparams(1),
    )(x, g.reshape(1, d), wq, wk, wvt, qg, kg, seg)


def _mm_res_router_kernel(a_ref, w_ref, x_ref, g_ref, wr_ref, x1_ref, hn_ref, aff_ref, *, a_transposed):
    if a_transposed:
        a = jnp.concatenate([a_ref[j] for j in range(a_ref.shape[0])], axis=1)
        y = lax.dot_general(a, w_ref[...], (((0,), (0,)), ((), ())), preferred_element_type=F32)
    else:
        y = _dot(a_ref[...], w_ref[...])
    x1 = x_ref[...] + y
    x1_ref[...] = x1
    hn = _rms(x1, g_ref[...])
    hn_ref[...] = _pack_bf16_pairs(hn)
    logits = _dot_nt(wr_ref[...], hn.astype(BF16))
    mx = jnp.max(logits, axis=0, keepdims=True)
    p = jnp.exp(logits - mx)
    aff_ref[0] = p / jnp.sum(p, axis=0, keepdims=True)


def mm_res_router(a, w, x, g, wr_t, batch, a_transposed=False, tm=PROJ_ROWS):
    n, d = x.shape
    k = w.shape[0]
    e = wr_t.shape[0]
    t = n // batch
    tm = min(tm, t)
    tpb = t // tm
    if a_transposed:
        w_slab = a.shape[2]
        a_spec = pl.BlockSpec((tm // w_slab, k, w_slab), lambda i: (i, 0, 0))
    else:
        a_spec = pl.BlockSpec((tm, k), lambda i: (i, 0))
    return pl.pallas_call(
        functools.partial(_mm_res_router_kernel, a_transposed=a_transposed),
        grid=(n // tm,),
        in_specs=[a_spec,
                  pl.BlockSpec((k, d), lambda i: (0, 0)),
                  pl.BlockSpec((tm, d), lambda i: (i, 0)),
                  pl.BlockSpec((1, d), lambda i: (0, 0)),
                  pl.BlockSpec((e, d), lambda i: (0, 0))],
        out_specs=[pl.BlockSpec((tm, d), lambda i: (i, 0)),
                   pl.BlockSpec((tm, d // 2), lambda i: (i, 0)),
                   pl.BlockSpec((1, e, tm), lambda i: (i // tpb, 0, i % tpb))],
        out_shape=[jax.ShapeDtypeStruct((n, d), F32),
                   jax.ShapeDtypeStruct((n, d // 2), U32),
                   jax.ShapeDtypeStruct((batch, e, t), F32)],
        compiler_params=_cparams(1),
    )(a, w, x, g.reshape(1, d), wr_t)


def _t5_bucket(rel):
    nb = NUM_BUCKETS // 2
    ret = (rel > 0).astype(jnp.int32) * nb
    n = jnp.abs(rel)
    max_exact = nb // 2
    nf = jnp.maximum(n, 1).astype(jnp.float32)
    large = max_exact + (jnp.log(nf / max_exact) / math.log(MAX_DISTANCE / max_exact)
                         * (nb - max_exact)).astype(jnp.int32)
    large = jnp.minimum(large, nb - 1)
    return ret + jnp.where(n < max_exact, n, large)


def _attn_bucket_table():
    kk = jnp.arange(3 * ATT_BLOCK)[:, None]
    qq = jnp.arange(ATT_BLOCK)[None, :]
    rel = kk - ATT_BLOCK - qq
    return jnp.where(jnp.abs(rel) <= ATT_BLOCK, _t5_bucket(rel), -1).astype(I32)


ATT_QBLOCKS = 4


def _attn_kernel(sink_ref, rb_ref, q_ref, *rest, nb, n_kv):
    nq = ATT_QBLOCKS
    k_refs = rest[:nq + 2]
    v_refs = rest[nq + 2:2 * (nq + 2)]
    bucket_ref, ot_ref, bias_sc = rest[2 * (nq + 2):]
    n = pl.program_id(1)
    blk = ATT_BLOCK
    dh = HEAD_DIM
    gw = GQA_GROUP * blk

    @pl.when(jnp.logical_and(pl.program_id(0) == 0, n == 0))
    def _bias_table():
        bk = bucket_ref[...]
        for hq in range(n_kv * GQA_GROUP):
            acc = jnp.full(bk.shape, NEG, F32)
            for k in range(NUM_BUCKETS):
                acc = jnp.where(bk == k, rb_ref[k, hq] * LOG2E, acc)
            bias_sc[hq // GQA_GROUP, :, (hq % GQA_GROUP) * blk:(hq % GQA_GROUP + 1) * blk] = acc

    kblk = [r[...] for r in k_refs]
    vblk = [r[0] for r in v_refs]
    kidx = lax.broadcasted_iota(I32, (3 * blk, gw), 0)
    ones_rows = (lax.broadcasted_iota(I32, (16, 3 * blk), 0) == 0).astype(BF16)
    lane = lax.broadcasted_iota(I32, (1, gw), 1)
    heads = [[h * GQA_GROUP + g for g in range(GQA_GROUP)] for h in range(n_kv)]
    sk = []
    for h in range(n_kv):
        row_sink = jnp.full((1, gw), sink_ref[heads[h][-1]] * LOG2E, F32)
        for g in reversed(range(GQA_GROUP - 1)):
            row_sink = jnp.where(lane < (g + 1) * blk, sink_ref[heads[h][g]] * LOG2E, row_sink)
        sk.append(row_sink)
    keys, vt, valid = [], [], []
    for j in range(nq):
        i = n * nq + j
        keys.append(jnp.concatenate(kblk[j:j + 3], axis=0))
        vt.append(jnp.concatenate(vblk[j:j + 3], axis=1))
        valid.append(jnp.logical_and(jnp.logical_or(i > 0, kidx >= blk),
                                     jnp.logical_or(i < nb - 1, kidx < 2 * blk)))
    units = [(j, h) for j in range(nq) for h in range(n_kv)]
    q = [jnp.concatenate([q_ref[j * blk:(j + 1) * blk, hq * dh:(hq + 1) * dh] for hq in heads[h]], axis=0)
         for j, h in units]
    vaug = [jnp.concatenate([vt[j][h * dh:(h + 1) * dh, :], ones_rows], axis=0) for j, h in units]
    s = [jnp.where(valid[j], _dot_nt(keys[j][:, h * dh:(h + 1) * dh], q[u]) + bias_sc[h], NEG)
         for u, (j, h) in enumerate(units)]
    m = [jnp.maximum(jnp.max(s[u], axis=0, keepdims=True), sk[h]) for u, (j, h) in enumerate(units)]
    p = [jnp.exp2(s[u] - m[u]).astype(BF16) for u in range(len(units))]
    oa = [_dot(vaug[u], p[u]) for u in range(len(units))]
    o = [oa[u][:dh] / (oa[u][dh:dh + 1] + jnp.exp2(sk[h] - m[u])) for u, (j, h) in enumerate(units)]
    for u, (j, h) in enumerate(units):
        for g, hq in enumerate(heads[h]):
            ot_ref[0, hq * dh:(hq + 1) * dh, j * blk:(j + 1) * blk] = o[u][:, g * blk:(g + 1) * blk].astype(ot_ref.dtype)


def attention_core(q, k, vt, rel_bias, sink, batch, seq):
    n, dq = q.shape
    dk = k.shape[1]
    hq = rel_bias.shape[1]
    n_kv = hq // GQA_GROUP
    nb = seq // ATT_BLOCK
    blk = ATT_BLOCK
    nq = ATT_QBLOCKS
    assert nb % nq == 0

    def key_block(off):
        return lambda b, i: b * nb + jnp.clip(i * nq + off, 0, nb - 1)

    offs = range(-1, nq + 1)
    return pl.pallas_call(
        functools.partial(_attn_kernel, nb=nb, n_kv=n_kv),
        grid=(batch, nb // nq),
        in_specs=[pl.BlockSpec(memory_space=pltpu.SMEM),
                  pl.BlockSpec(memory_space=pltpu.SMEM),
                  pl.BlockSpec((nq * blk, dq), lambda b, i: (b * (nb // nq) + i, 0))]
                 + [pl.BlockSpec((blk, dk), (lambda f: lambda b, i: (f(b, i), 0))(key_block(o))) for o in offs]
                 + [pl.BlockSpec((1, dk, blk), (lambda f: lambda b, i: (f(b, i), 0, 0))(key_block(o))) for o in offs]
                 + [pl.BlockSpec((3 * blk, blk), lambda b, i: (0, 0))],
        out_specs=pl.BlockSpec((1, dq, nq * blk), lambda b, i: (b * (nb // nq) + i, 0, 0)),
        out_shape=jax.ShapeDtypeStruct((n // (nq * blk), dq, nq * blk), BF16),
        scratch_shapes=[pltpu.VMEM((n_kv, 3 * blk, GQA_GROUP * blk), F32)],
        compiler_params=_cparams(2),
    )(sink.astype(F32), rel_bias.astype(F32), q, *([k] * (nq + 2)), *([vt] * (nq + 2)), _attn_bucket_table())


def _prefix_incl(x, tri):
    r, t = x.shape
    nck = t // LANE
    assert nck <= LANE
    xb = x.astype(BF16)
    local = [_dot(xb[:, c * LANE:(c + 1) * LANE], tri) for c in range(nck)]
    tot = jnp.concatenate([p[:, LANE - 1:LANE] for p in local] + [jnp.zeros((r, LANE - nck), F32)], axis=1)
    lane = lax.broadcasted_iota(I32, tot.shape, 1)
    inc = tot
    sh = 1
    while sh < nck:
        inc = inc + jnp.where(lane >= sh, pltpu.roll(inc, sh, axis=1), 0.0)
        sh *= 2
    offs = inc - tot
    return jnp.concatenate([local[c] + offs[:, c:c + 1] for c in range(nck)], axis=1)


BF16_EXACT = 256
TOK_SPLIT = 64


def _select_kernel(aff_ref, cmp_ref, idx_ref, dest_ref, offc_ref, rng_ref, pos_sc, dest_sc, vals_sc, *, cap, rchunk):
    aff = aff_ref[0]
    ne, t = aff.shape
    bidx = pl.program_id(0)
    bits = pltpu.bitcast(aff, I32)

    def search(i, lo):
        cand = lo | lax.shift_left(jnp.int32(1), 30 - i)
        cnt = jnp.sum((bits >= cand).astype(I32), axis=1, keepdims=True)
        return jnp.where(cnt >= cap, cand, lo)

    thr = lax.fori_loop(0, 31, search, jnp.zeros((ne, 1), I32))
    gt = bits > thr
    eq = bits == thr
    need = (cap - jnp.sum(gt.astype(I32), axis=1, keepdims=True)).astype(F32)
    ri = lax.broadcasted_iota(I32, (LANE, LANE), 0)
    ci = lax.broadcasted_iota(I32, (LANE, LANE), 1)
    tri = (ri <= ci).astype(BF16)
    eqf = eq.astype(F32)
    rank_eq = _prefix_incl(eqf, tri) - eqf
    sel = jnp.logical_or(gt, jnp.logical_and(eq, rank_eq < need))
    self_ = sel.astype(F32)
    kt = jnp.sum(self_, axis=0, keepdims=True)
    pre = _prefix_incl(jnp.concatenate([self_, jnp.broadcast_to(kt, (8, t))], axis=0), tri)
    pos_sc[...] = jnp.where(sel, pre[:ne] - 1.0, -1.0)
    end = pre[ne:ne + 1]
    off = end - kt
    lr = lax.broadcasted_iota(I32, (ne, ne), 0)
    lc = lax.broadcasted_iota(I32, (ne, ne), 1)
    jexp = _dot((lc < lr).astype(BF16), self_.astype(BF16))
    dest_sc[...] = off + jexp

    tok = lax.broadcasted_iota(I32, (1, t), 1)
    vals_sc[0:1, :] = (tok // TOK_SPLIT).astype(F32)
    vals_sc[1:2, :] = (tok % TOK_SPLIT).astype(F32)
    vals_sc[7:8, :] = jnp.zeros((1, t), F32)
    slot = lax.broadcasted_iota(I32, (cap, 1), 0).astype(F32)

    def compact(e, carry):
        d = dest_sc[pl.ds(e, 1), :]
        dh = jnp.floor(d * (1.0 / LANE))
        g = aff_ref[0, pl.ds(e, 1), :]
        g0 = g.astype(BF16).astype(F32)
        g1 = (g - g0).astype(BF16).astype(F32)
        vals_sc[2:3, :] = dh
        vals_sc[3:4, :] = d - dh * LANE
        vals_sc[4:5, :] = g0
        vals_sc[5:6, :] = g1
        vals_sc[6:7, :] = g - g0 - g1
        onehot = (pos_sc[pl.ds(e, 1), :] == slot).astype(BF16)
        out = _dot_nt(vals_sc[...].astype(BF16), onehot)
        cmp_ref[0, e] = out
        idx_ref[e, 0] = (out[0:1] * TOK_SPLIT + out[1:2]).astype(I32) + bidx * t
        dest_ref[e + 1, 0] = (out[2:3] * LANE + out[3:4]).astype(I32) + bidx * (ne * cap)
        return carry

    lax.fori_loop(0, ne, compact, 0)
    idx_ref[ne, 0] = idx_ref[0, 0]
    dest_ref[0, 0] = (pl.num_programs(0) * ne + bidx) * cap + lax.broadcasted_iota(I32, (1, cap), 1)

    eh = jnp.floor(end * (1.0 / LANE))
    oh = jnp.floor(off * (1.0 / LANE))
    v4 = jnp.concatenate([oh, off - oh * LANE, eh, end - eh * LANE, jnp.zeros((4, t), F32)], axis=0)
    eye = (ri == ci).astype(BF16)
    for c in range(t // LANE):
        offc_ref[0, c * LANE:(c + 1) * LANE, :] = _dot_nt(eye, v4[:, c * LANE:(c + 1) * LANE].astype(BF16))

    nchunk = (ne * cap) // rchunk
    low = (lax.broadcasted_iota(I32, (nchunk, 1), 0) * rchunk).astype(F32)
    first = jnp.sum((end <= low).astype(F32), axis=1, keepdims=True)
    last = jnp.sum((end <= low + (rchunk - 1)).astype(F32), axis=1, keepdims=True)
    lane = lax.broadcasted_iota(I32, (nchunk, LANE), 1)
    tiles = jnp.where(lane < LANE // 2, jnp.floor(first * (1.0 / LANE)), jnp.floor(last * (1.0 / LANE)))
    rng_ref[0] = tiles.astype(I32)


COMBINE_ROWS = 1024
COMBINE_TILES = 5


def route_select(aff, cap, rchunk):
    batch, ne, t = aff.shape
    nchunk = ne * cap // rchunk
    assert t <= TOK_SPLIT * BF16_EXACT and ne * cap <= LANE * BF16_EXACT and LANE <= BF16_EXACT
    return pl.pallas_call(
        functools.partial(_select_kernel, cap=cap, rchunk=rchunk),
        grid=(batch,),
        in_specs=[pl.BlockSpec((1, ne, t), lambda b: (b, 0, 0))],
        out_specs=[pl.BlockSpec((1, ne, 8, cap), lambda b: (b, 0, 0, 0)),
                   pl.BlockSpec((ne + 1, 1, 1, cap), lambda b: (0, b, 0, 0)),
                   pl.BlockSpec((ne + 1, 1, 1, cap), lambda b: (0, b, 0, 0)),
                   pl.BlockSpec((1, t, 8), lambda b: (b, 0, 0)),
                   pl.BlockSpec((1, nchunk, LANE), lambda b: (b, 0, 0))],
        out_shape=[jax.ShapeDtypeStruct((batch, ne, 8, cap), F32),
                   jax.ShapeDtypeStruct((ne + 1, batch, 1, cap), I32),
                   jax.ShapeDtypeStruct((ne + 1, batch, 1, cap), I32),
                   jax.ShapeDtypeStruct((batch, t, 8), F32),
                   jax.ShapeDtypeStruct((batch, nchunk, LANE), I32)],
        scratch_shapes=[pltpu.VMEM((ne, t), F32), pltpu.VMEM((ne, t), F32), pltpu.VMEM((8, t), F32)],
        compiler_params=_cparams(1),
    )(aff)


def _ffn_kernel(idx_sm, dest_sm, hn_hbm, cmp_ref, w1_ref, w3_ref, w2_ref, r_hbm,
                xsu, xsb, yacc, ysc, wb1, wb3, wb2, gsem, ssem, *, batch, cap, nf, rt):
    e = pl.program_id(0)
    f = pl.program_id(1)
    ne = pl.num_programs(0)
    rows = batch * cap
    sub = rows // nf

    def gather_start(block, r):
        t = idx_sm[block * rows + r]
        pltpu.make_async_copy(hn_hbm.at[pl.ds(t, 1), :], xsu.at[pl.ds(r, 1), :], gsem).start()

    def gather_wait():
        pltpu.make_async_copy(hn_hbm.at[pl.ds(0, rows), :], xsu, gsem).wait()

    def scatter_start(block, r):
        d = dest_sm[block * rows + r]
        pltpu.make_async_copy(ysc.at[pl.ds(r, 1), :], r_hbm.at[pl.ds(d, 1), :], ssem).start()

    def scatter_wait():
        pltpu.make_async_copy(ysc, r_hbm.at[pl.ds(0, rows), :], ssem).wait()

    @pl.when(jnp.logical_and(e == 0, f == 0))
    def _prologue():
        def issue(r, c):
            gather_start(0, r)
            return c
        lax.fori_loop(0, rows, issue, 0)
        ysc[...] = jnp.zeros_like(ysc)

    @pl.when(f == 0)
    def _rows_ready():
        gather_wait()
        half = xsu.shape[1]
        xsb[:, :half], xsb[:, half:] = _unpack_bf16_pairs(xsu[...])
        yacc[...] = jnp.zeros_like(yacc)

    for r in range(sub):
        gather_start(e + 1, f * sub + r)
        scatter_start(e, f * sub + r)

    wb1[...] = w1_ref[0, 0].astype(BF16)
    for r in range(rows // rt):
        xs = xsb[r * rt:(r + 1) * rt, :]
        a = _dot(xs, wb1[...])
        if r == 0:
            wb3[...] = w3_ref[0, 0].astype(BF16)
        u = _dot(xs, wb3[...])
        hmid = (a * jax.nn.sigmoid(a) * u).astype(BF16)
        if r == 0:
            wb2[...] = w2_ref[0, 0].astype(BF16)
        yacc[r * rt:(r + 1) * rt, :] += _dot(hmid, wb2[...])

    @pl.when(f == nf - 1)
    def _finish():
        scatter_wait()
        ri = lax.broadcasted_iota(I32, (cap, cap), 0)
        ci = lax.broadcasted_iota(I32, (cap, cap), 1)
        eye = (ri == ci).astype(BF16)
        rows8 = cmp_ref[...].reshape(batch * 8, cap).astype(BF16)
        gt = _dot_nt(eye, rows8)
        for b in range(batch):
            g = gt[:, 8 * b + 4:8 * b + 5] + gt[:, 8 * b + 5:8 * b + 6] + gt[:, 8 * b + 6:8 * b + 7]
            ysc[b * cap:(b + 1) * cap, :] = _pack_bf16_pairs(yacc[b * cap:(b + 1) * cap, :] * g)

        @pl.when(e == ne - 1)
        def _epilogue():
            def issue(r, c):
                scatter_start(ne, r)
                return c
            lax.fori_loop(0, rows, issue, 0)
            scatter_wait()
            gather_wait()


def expert_ffn(hn, cmp, idx_flat, dest_flat, w1, w3, w2, layer, cap, tf=512, rt=512):
    d = w1.shape[2]
    batch, ne = cmp.shape[0], cmp.shape[1]
    dff = w1.shape[3]
    nf = dff // tf
    rows = batch * cap
    rt = min(rt, rows)
    grid_spec = pltpu.PrefetchScalarGridSpec(
        num_scalar_prefetch=2,
        grid=(ne, nf),
        in_specs=[pl.BlockSpec(memory_space=pl.ANY),
                  pl.BlockSpec((batch, 1, 8, cap), lambda e, f, *_: (0, e, 0, 0)),
                  pl.BlockSpec((1, 1, d, tf), lambda e, f, *_: (layer, e, 0, f)),
                  pl.BlockSpec((1, 1, d, tf), lambda e, f, *_: (layer, e, 0, f)),
                  pl.BlockSpec((1, 1, tf, d), lambda e, f, *_: (layer, e, f, 0))],
        out_specs=pl.BlockSpec(memory_space=pl.ANY),
        scratch_shapes=[pltpu.VMEM((rows, d // 2), U32), pltpu.VMEM((rows, d), BF16),
                        pltpu.VMEM((rows, d), F32), pltpu.VMEM((rows, d // 2), U32),
                        pltpu.VMEM((d, tf), BF16), pltpu.VMEM((d, tf), BF16), pltpu.VMEM((tf, d), BF16),
                        pltpu.SemaphoreType.DMA, pltpu.SemaphoreType.DMA])
    return pl.pallas_call(
        functools.partial(_ffn_kernel, batch=batch, cap=cap, nf=nf, rt=rt),
        grid_spec=grid_spec,
        out_shape=jax.ShapeDtypeStruct((batch * ne * cap + rows, d // 2), U32),
        compiler_params=_cparams(2),
    )(idx_flat, dest_flat, hn, cmp, w1, w3, w2)


def _combine_kernel(tlo_sm, thi_sm, r_ref, offc_ref, x_hbm, o_hbm, acc, xsem, osem, *, batch, nchunk, rchunk):
    g = pl.program_id(0)
    b = g // nchunk
    j = g % nchunk
    slot = b % 2
    ntile = acc.shape[1] // LANE

    def x_copy(bb):
        return pltpu.make_async_copy(x_hbm.at[bb], acc.at[bb % 2], xsem.at[bb % 2])

    def o_copy(bb):
        return pltpu.make_async_copy(acc.at[bb % 2], o_hbm.at[bb], osem.at[bb % 2])

    @pl.when(g == 0)
    def _first():
        x_copy(0).start()

    @pl.when(j == 0)
    def _batch_start():
        x_copy(b).wait()

    @pl.when(j == nchunk // 2)
    def _mid():
        @pl.when(b > 0)
        def _():
            o_copy(b - 1).wait()

        @pl.when(b + 1 < batch)
        def _():
            x_copy(b + 1).start()

    half = r_ref.shape[1]
    left, right = _unpack_bf16_pairs(r_ref[...])
    rowid = (j * rchunk + lax.broadcasted_iota(I32, (1, rchunk), 1)).astype(F32)

    def contrib(i, valid):
        t0 = pl.multiple_of(i * LANE, LANE)
        oc = offc_ref[0, pl.ds(t0, LANE), :]
        off = oc[:, 0:1] * LANE + oc[:, 1:2]
        end = oc[:, 2:3] * LANE + oc[:, 3:4]
        p = jnp.logical_and(jnp.logical_and(rowid >= off, rowid < end), valid).astype(BF16)
        return t0, (_dot(p, left), _dot(p, right))

    def add(t0, y):
        acc[slot, pl.ds(t0, LANE), :half] += y[0]
        acc[slot, pl.ds(t0, LANE), half:] += y[1]

    def tile(i, c):
        add(*contrib(i, True))
        return c

    tlo = tlo_sm[g]
    thi = thi_sm[g]
    parts = [contrib(jnp.minimum(tlo + k, ntile - 1), tlo + k <= thi) for k in range(COMBINE_TILES)]
    for t0, y in parts:
        add(t0, y)
    lax.fori_loop(tlo + COMBINE_TILES, thi + 1, tile, 0)

    @pl.when(j == nchunk - 1)
    def _batch_end():
        o_copy(b).start()

        @pl.when(b == batch - 1)
        def _():
            o_copy(b).wait()


def combine(rbuf, offc, tlo, thi, x, rows_per_batch, rchunk):
    batch, t, d = x.shape
    nchunk = rows_per_batch // rchunk
    assert nchunk >= 2
    grid_spec = pltpu.PrefetchScalarGridSpec(
        num_scalar_prefetch=2,
        grid=(batch * nchunk,),
        in_specs=[pl.BlockSpec((rchunk, d // 2), lambda g, *_: (g, 0)),
                  pl.BlockSpec((1, t, 8), lambda g, *_: (g // nchunk, 0, 0)),
                  pl.BlockSpec(memory_space=pl.ANY)],
        out_specs=pl.BlockSpec(memory_space=pl.ANY),
        scratch_shapes=[pltpu.VMEM((2, t, d), F32), pltpu.SemaphoreType.DMA((2,)), pltpu.SemaphoreType.DMA((2,))])
    return pl.pallas_call(
        functools.partial(_combine_kernel, batch=batch, nchunk=nchunk, rchunk=rchunk),
        grid_spec=grid_spec,
        out_shape=jax.ShapeDtypeStruct((batch, t, d), F32),
        compiler_params=_cparams(1),
    )(tlo, thi, rbuf, offc, x)


def moe_block(x1, hn, aff, w1, w3, w2, layer, batch):
    n, d = x1.shape
    t = n // batch
    ne = aff.shape[1]
    cap = CAPACITY_FACTOR * t // ne
    cmp, idx, dest, offc, rng = route_select(aff, cap, COMBINE_ROWS)
    rbuf = expert_ffn(hn, cmp, idx.reshape(-1), dest.reshape(-1), w1, w3, w2, layer, cap)
    tlo = rng[:, :, 0].reshape(-1)
    thi = rng[:, :, LANE - 1].reshape(-1)
    return combine(rbuf, offc, tlo, thi, x1.reshape(batch, t, d), ne * cap, COMBINE_ROWS)


MLSTM_L = 256
MLSTM_UNROLL = 4


def _log_sigmoid(x):
    return jnp.minimum(x, 0.0) - jnp.log1p(jnp.exp(-jnp.abs(x)))


def _split3(x):
    x0 = x.astype(BF16)
    r = x - x0.astype(F32)
    x1 = r.astype(BF16)
    return x0, x1, (r - x1.astype(F32)).astype(BF16)


def _mlstm_proj_kernel(x_ref, g_ref, wm_ref, wkt_ref, wgt_ref, bt_ref, main_ref, kt_ref, gt_ref, *, nsub, nh):
    ln = MLSTM_L
    h = _rms(x_ref[...], g_ref[...]).astype(BF16)
    main_ref[...] = _dot(h, wm_ref[...]).astype(main_ref.dtype)
    kt = _dot_nt(wkt_ref[...], h).astype(kt_ref.dtype)
    pre = _dot_nt(wgt_ref[...], h) + bt_ref[...]
    lst = _log_sigmoid(pre)
    ri = lax.broadcasted_iota(I32, (ln, ln), 0)
    ci = lax.broadcasted_iota(I32, (ln, ln), 1)
    low = (ci <= ri).astype(BF16)
    upp = (ci >= ri).astype(BF16)
    row = lax.broadcasted_iota(I32, (pre.shape[0], ln), 0)
    fwd = jnp.logical_and(row >= nh, row < 2 * nh)
    bwd = row >= 3 * nh
    for j in range(nsub):
        sl = slice(j * ln, (j + 1) * ln)
        kt_ref[j] = kt[:, sl]
        pt = _split3(lst[:, sl])
        gt_ref[j] = jnp.where(fwd, sum(_dot(x, upp) for x in pt),
                              jnp.where(bwd, sum(_dot(x, low) for x in pt), pre[:, sl]))


def mlstm_project(x, g, wm, wkt, wgt, bias_t, nh, tm=PROJ_ROWS):
    n, d = x.shape
    nsub = tm // MLSTM_L
    nck = n // MLSTM_L
    const = lambda i: (0, 0)
    return pl.pallas_call(
        functools.partial(_mlstm_proj_kernel, nsub=nsub, nh=nh),
        grid=(n // tm,),
        in_specs=[pl.BlockSpec((tm, d), lambda i: (i, 0)),
                  pl.BlockSpec((1, d), const),
                  pl.BlockSpec(wm.shape, const),
                  pl.BlockSpec(wkt.shape, const),
                  pl.BlockSpec(wgt.shape, const),
                  pl.BlockSpec(bias_t.shape, const)],
        out_specs=[pl.BlockSpec((tm, wm.shape[1]), lambda i: (i, 0)),
                   pl.BlockSpec((nsub, wkt.shape[0], MLSTM_L), lambda i: (i, 0, 0)),
                   pl.BlockSpec((nsub, wgt.shape[0], MLSTM_L), lambda i: (i, 0, 0))],
        out_shape=[jax.ShapeDtypeStruct((n, wm.shape[1]), BF16),
                   jax.ShapeDtypeStruct((nck, wkt.shape[0], MLSTM_L), BF16),
                   jax.ShapeDtypeStruct((nck, wgt.shape[0], MLSTM_L), F32)],
        compiler_params=_cparams(1),
    )(x, g.reshape(1, d), wm, wkt, wgt, bias_t)


def _mlstm_kernel(q_ref, kt_ref, v_ref, og_ref, gt_ref, ng_ref, y_ref, hf, hb, cst, b_sc, u_sc, cm_sc,
                  *, nc, nh, dqk, dv):
    p = pl.program_id(1)
    ln = MLSTM_L
    ri = lax.broadcasted_iota(I32, (ln, ln), 0)
    ci = lax.broadcasted_iota(I32, (ln, ln), 1)
    masks = (ci <= ri, ci >= ri)
    ones_blk = jnp.ones((ln, dv), BF16)
    lane8 = lax.broadcasted_iota(I32, (8, ln), 1)
    row8 = lax.broadcasted_iota(I32, (8, ln), 0)
    bwd_row = (row8 % 2) == 1
    edge = lane8 == jnp.where(bwd_row, 0, ln - 1)
    kk = lax.broadcasted_iota(I32, (48, 2 * dv), 0) % 16
    cc = lax.broadcasted_iota(I32, (48, 2 * dv), 1)
    sel = [jnp.logical_or(jnp.logical_and(kk == i, cc < dv), jnp.logical_and(kk == 4 + i, cc >= dv)).astype(BF16)
           for i in range(4)]
    zero4 = jnp.zeros((4, ln), F32)
    cst[...] = jnp.zeros_like(cst)
    chains = [(hh, d) for hh in range(2) for d in range(2)]

    def gate_rows(c, which):
        return [gt_ref[c if d == 0 else nc - 1 - c, pl.ds((2 * d + which) * nh + 2 * p + hh, 1), :]
                for hh, d in chains] + [zero4]
    b_all = jnp.concatenate([x for c in range(nc) for x in gate_rows(c, 1)], axis=0)
    u_all = jnp.concatenate([x for c in range(nc) for x in gate_rows(c, 0)], axis=0) - b_all
    lane_a = lax.broadcasted_iota(I32, u_all.shape, 1)
    bwd_a = (lax.broadcasted_iota(I32, u_all.shape, 0) % 2) == 1
    cf = cb = u_all
    sh = 1
    while sh < ln:
        cf = jnp.maximum(cf, jnp.where(lane_a >= sh, pltpu.roll(cf, sh, axis=1), NEG))
        cb = jnp.maximum(cb, jnp.where(lane_a < ln - sh, pltpu.roll(cb, ln - sh, axis=1), NEG))
        sh *= 2
    b_sc[...] = b_all.reshape(nc, 8, ln)
    u_sc[...] = u_all.reshape(nc, 8, ln)
    cm_sc[...] = jnp.where(bwd_a, cb, cf).reshape(nc, 8, ln)

    def step(c, ms):
        cks = [c if d == 0 else nc - 1 - c for _, d in chains]
        r0s = [pl.multiple_of(ck * ln, ln) for ck in cks]
        cprev = [cst[i] for i in range(4)]
        q = [q_ref[pl.ds(r0s[i], ln), hh * dqk:(hh + 1) * dqk] for i, (hh, _) in enumerate(chains)]
        kt = [kt_ref[cks[i], hh * dqk:(hh + 1) * dqk, :] for i, (hh, _) in enumerate(chains)]
        vaug = [jnp.concatenate([v_ref[pl.ds(r0s[i], ln), hh * dv:(hh + 1) * dv], ones_blk], axis=1)
                for i, (hh, _) in enumerate(chains)]
        s = [_dot(q[i], kt[i]) for i in range(4)]
        inter = [_dot(q[i], cprev[i].astype(BF16)) for i in range(4)]
        b = b_sc[c]
        u = u_sc[c]
        m_run = jnp.maximum(ms, cm_sc[c])
        m_end = jnp.max(jnp.where(edge, m_run, NEG), axis=1, keepdims=True)
        b_end = jnp.sum(jnp.where(edge, b, 0.0), axis=1, keepdims=True)
        wc = jnp.exp(u - m_end)
        decay = jnp.exp(ms - m_end)
        rows = jnp.concatenate([m_run[:4] * LOG2E, jnp.exp(-(b + m_run))[:4], zero4, zero4], axis=0)
        stack = jnp.concatenate(_split3(rows), axis=0)
        u2 = u * LOG2E
        ms2 = ms * LOG2E
        bc = [lax.dot_general(stack, sel[i], (((0,), (0,)), ((), ())), preferred_element_type=F32)
              for i in range(4)]
        mb = [jnp.concatenate([bc[i][:, :dv]] * (ln // dv), axis=1) for i in range(4)]
        sw = [(jnp.exp2(jnp.where(masks[d], u2[i:i + 1, :] - mb[i], NEG)) * s[i]).astype(BF16)
              for i, (_, d) in enumerate(chains)]
        intra = [_dot(sw[i], vaug[i]) for i in range(4)]
        ea = [jnp.exp2(ms2[i:i + 1, :] - bc[i][:, :dv]) for i in range(4)]
        num = [ea[i] * inter[i][:, :dv] + intra[i][:, :dv] for i in range(4)]
        den = [ea[i] * inter[i][:, dv:] + intra[i][:, dv:] for i in range(4)]
        hc = [num[i] / jnp.maximum(jnp.abs(den[i]), bc[i][:, dv:]) for i in range(4)]
        kw = [(kt[i].astype(F32) * wc[i:i + 1, :]).astype(BF16) for i in range(4)]
        c_new = [decay[i:i + 1, :] * cprev[i] + _dot(kw[i], vaug[i]) for i in range(4)]
        for i, (hh, d) in enumerate(chains):
            cst[i] = c_new[i]
            (hf if d == 0 else hb)[pl.ds(r0s[i], ln), hh * dv:(hh + 1) * dv] = hc[i]
        return b_end + m_end

    unroll = math.gcd(nc, MLSTM_UNROLL)

    def steps(cu, ms):
        for k in range(unroll):
            ms = step(cu * unroll + k, ms)
        return ms

    lax.fori_loop(0, nc // unroll, steps, jnp.zeros((8, 1), F32))
    for hh in range(2):
        sl = slice(hh * dv, (hh + 1) * dv)
        hs = _rms(hf[:, sl] + hb[:, sl], ng_ref[:, sl])
        y_ref[:, sl] = (hs * jax.nn.sigmoid(og_ref[:, sl].astype(F32))).astype(y_ref.dtype)


def mlstm_core(main, ktc, grow, out_g, batch, seq, nh, dqk, dv):
    n = main.shape[0]
    nc = seq // MLSTM_L
    npair = nh // 2
    vblocks = (nh * dqk) // (2 * dv)
    ogblocks = (nh * dqk + nh * dv) // (2 * dv)
    return pl.pallas_call(
        functools.partial(_mlstm_kernel, nc=nc, nh=nh, dqk=dqk, dv=dv),
        grid=(batch, npair),
        in_specs=[pl.BlockSpec((seq, 2 * dqk), lambda b, p: (b, p)),
                  pl.BlockSpec((nc, 2 * dqk, MLSTM_L), lambda b, p: (b, p, 0)),
                  pl.BlockSpec((seq, 2 * dv), lambda b, p: (b, vblocks + p)),
                  pl.BlockSpec((seq, 2 * dv), lambda b, p: (b, ogblocks + p)),
                  pl.BlockSpec((nc, 4 * nh, MLSTM_L), lambda b, p: (b, 0, 0)),
                  pl.BlockSpec((1, 2 * dv), lambda b, p: (0, p))],
        out_specs=pl.BlockSpec((seq, 2 * dv), lambda b, p: (b, p)),
        out_shape=jax.ShapeDtypeStruct((n, nh * dv), BF16),
        scratch_shapes=[pltpu.VMEM((seq, 2 * dv), F32), pltpu.VMEM((seq, 2 * dv), F32),
                        pltpu.VMEM((4, dqk, 2 * dv), F32)] + [pltpu.VMEM((nc, 8, MLSTM_L), F32)] * 3,
        compiler_params=_cparams(2),
    )(main, ktc, main, main, grow, out_g.reshape(1, nh * dv).astype(F32))


def mlstm_layer(x2d, batch, seq, norm_g, w_in, b_i, b_f, out_g, w_out, ffn_g, wr):
    d = x2d.shape[1]
    nh = MLSTM_HEADS
    dv = d // nh
    dqk = dv // 2
    o1, o2, o3, o4 = nh * dqk, 2 * nh * dqk, 2 * nh * dqk + nh * dv, 2 * nh * dqk + 2 * nh * dv
    wm = jnp.concatenate([w_in[:, :o1], w_in[:, o2:o4]], axis=1).astype(BF16)
    wkt = (w_in[:, o1:o2] * (dqk ** -0.5)).T.astype(BF16)
    wgt = w_in[:, o4:].T.astype(BF16)
    bias_t = jnp.concatenate([b_i[0], b_f[0], b_i[1], b_f[1]]).reshape(4 * nh, 1).astype(F32)
    main, ktc, grow = mlstm_project(x2d, norm_g, wm, wkt, wgt, bias_t, nh)
    y = mlstm_core(main, ktc, grow, out_g, batch, seq, nh, dqk, dv)
    return mm_res_router(y, w_out.astype(BF16), x2d, ffn_g, wr.T.astype(BF16), batch)


def attention_layer(x2d, batch, seq, norm_g, w_in, q_g, k_g, sink, w_out, rel_bias, ffn_g, wr):
    n_q = rel_bias.shape[1]
    q, k, vt = attention_project(x2d, norm_g, w_in, q_g, k_g, n_q, n_q // GQA_GROUP)
    ot = attention_core(q, k, vt, rel_bias, sink, batch, seq)
    return mm_res_router(ot, w_out.astype(BF16), x2d, ffn_g, wr.T.astype(BF16), batch, a_transposed=True)


def kernel(x, rel_bias, attn_norm_g, attn_w_in, attn_q_norm_g, attn_k_norm_g, attn_sink, attn_w_out, mlstm_norm_g, mlstm_w_in, mlstm_b_i, mlstm_b_f, mlstm_out_norm_g, mlstm_w_out, ffn_norm_g, router_w, expert_w1, expert_w3, expert_w2):
    batch, seq, d = x.shape
    x2d = x.reshape(batch * seq, d)
    x1, hn, aff = attention_layer(x2d, batch, seq, attn_norm_g[0], attn_w_in[0], attn_q_norm_g[0],
                                  attn_k_norm_g[0], attn_sink[0], attn_w_out[0], rel_bias,
                                  ffn_norm_g[0], router_w[0])
    x = moe_block(x1, hn, aff, expert_w1, expert_w3, expert_w2, 0, batch)
    x1, hn, aff = mlstm_layer(x.reshape(batch * seq, d), batch, seq, mlstm_norm_g[0], mlstm_w_in[0], mlstm_b_i[0],
                              mlstm_b_f[0], mlstm_out_norm_g[0], mlstm_w_out[0], ffn_norm_g[1], router_w[1])
    return moe_block(x1, hn, aff, expert_w1, expert_w3, expert_w2, 1, batch)
```

```python
import functools
import math

import jax
import jax.numpy as jnp
from jax import lax
from jax.experimental import pallas as pl
from jax.experimental.pallas import tpu as pltpu

F32 = jnp.float32
BF16 = jnp.bfloat16
I32 = jnp.int32
U32 = jnp.uint32

RMS_EPS = 1e-6
NEG = -1e30
LOG2E = 1.4426950408889634
LANE = 128
MXU_DIM = 256
VMEM_LIMIT = 56 * 1024 * 1024
PROJ_ROWS = 1024

HEAD_DIM = 64
GQA_GROUP = 4
ATT_BLOCK = 128
NUM_BUCKETS = 32
MAX_DISTANCE = 128
CAPACITY_FACTOR = 2
MLSTM_HEADS = 8


def _cparams(n_axes, vmem=VMEM_LIMIT):
    return pltpu.CompilerParams(dimension_semantics=("arbitrary",) * n_axes, vmem_limit_bytes=vmem)


def _rms(x, g):
    return x * lax.rsqrt(jnp.mean(x * x, axis=-1, keepdims=True) + RMS_EPS) * g


def _dot(a, b):
    return jnp.dot(a, b, preferred_element_type=F32)


def _pack_bf16_pairs(x):
    half = x.shape[1] // 2
    hi = pltpu.bitcast(x[:, :half].astype(BF16).astype(F32), U32)
    lo = pltpu.bitcast(x[:, half:].astype(BF16).astype(F32), U32)
    return hi | lax.shift_right_logical(lo, jnp.uint32(16))


def _unpack_bf16_pairs(w):
    left = pltpu.bitcast(w & jnp.uint32(0xFFFF0000), F32).astype(BF16)
    right = pltpu.bitcast(lax.shift_left(w, jnp.uint32(16)), F32).astype(BF16)
    return left, right


def _dot_nt(a, b):
    return lax.dot_general(a, b, (((1,), (1,)), ((), ())), preferred_element_type=F32)


def _attn_proj_kernel(x_ref, g_ref, wq_ref, wk_ref, wvt_ref, qg_ref, kg_ref, seg_ref, q_ref, k_ref, vt_ref):
    h = _rms(x_ref[...], g_ref[...]).astype(BF16)
    seg = seg_ref[...]
    w = seg.shape[0]

    def head_norm(t, gain_ref, out_ref):
        for j in range(t.shape[1] // w):
            tj = t[:, j * w:(j + 1) * w]
            ms = _dot((tj * tj).astype(BF16), seg)
            out_ref[:, j * w:(j + 1) * w] = (tj * lax.rsqrt(ms + RMS_EPS) * gain_ref[:, j * w:(j + 1) * w]
                                             ).astype(out_ref.dtype)

    head_norm(_dot(h, wq_ref[...]), qg_ref, q_ref)
    head_norm(_dot(h, wk_ref[...]), kg_ref, k_ref)
    vt = _dot_nt(wvt_ref[...], h).astype(vt_ref.dtype)
    for j in range(vt_ref.shape[0]):
        vt_ref[j] = vt[:, j * ATT_BLOCK:(j + 1) * ATT_BLOCK]


def attention_project(x, g, w_in, q_g, k_g, n_q, n_kv, tm=PROJ_ROWS):
    n, d = x.shape
    dh = HEAD_DIM
    dq, dk = n_q * dh, n_kv * dh
    wq = w_in[:, :dq].astype(BF16)
    wk = w_in[:, dq:dq + dk].astype(BF16)
    wvt = w_in[:, dq + dk:].T.astype(BF16)
    qg = jnp.tile(q_g.astype(F32) * (dh ** -0.5 * LOG2E), n_q).reshape(1, dq)
    kg = jnp.tile(k_g.astype(F32), n_kv).reshape(1, dk)
    assert dq % MXU_DIM == 0 and dk % MXU_DIM == 0 and MXU_DIM % dh == 0
    hid = jnp.arange(MXU_DIM) // dh
    seg = jnp.where(hid[:, None] == hid[None, :], 1.0 / dh, 0.0).astype(BF16)
    const2 = lambda i: (0, 0)
    return pl.pallas_call(
        _attn_proj_kernel,
        grid=(n // tm,),
        in_specs=[pl.BlockSpec((tm, d), lambda i: (i, 0)),
                  pl.BlockSpec((1, d), const2),
                  pl.BlockSpec((d, dq), const2),
                  pl.BlockSpec((d, dk), const2),
                  pl.BlockSpec((dk, d), const2),
                  pl.BlockSpec((1, dq), const2),
                  pl.BlockSpec((1, dk), const2),
                  pl.BlockSpec((MXU_DIM, MXU_DIM), const2)],
        out_specs=[pl.BlockSpec((tm, dq), lambda i: (i, 0)),
                   pl.BlockSpec((tm, dk), lambda i: (i, 0)),
                   pl.BlockSpec((tm // ATT_BLOCK, dk, ATT_BLOCK), lambda i: (i, 0, 0))],
        out_shape=[jax.ShapeDtypeStruct((n, dq), BF16),
                   jax.ShapeDtypeStruct((n, dk), BF16),
                   jax.ShapeDtypeStruct((n // ATT_BLOCK, dk, ATT_BLOCK), BF16)],
        compiler_params=_cparams(1),
    )(x, g.reshape(1, d), wq, wk, wvt, qg, kg, seg)


def _mm_res_router_kernel(a_ref, w_ref, x_ref, g_ref, wr_ref, x1_ref, hn_ref, aff_ref, *, a_transposed):
    if a_transposed:
        a = jnp.concatenate([a_ref[j] for j in range(a_ref.shape[0])], axis=1)
        y = lax.dot_general(a, w_ref[...], (((0,), (0,)), ((), ())), preferred_element_type=F32)
    else:
        y = _dot(a_ref[...], w_ref[...])
    x1 = x_ref[...] + y
    x1_ref[...] = x1
    hn = _rms(x1, g_ref[...])
    hn_ref[...] = _pack_bf16_pairs(hn)
    logits = _dot_nt(wr_ref[...], hn.astype(BF16))
    mx = jnp.max(logits, axis=0, keepdims=True)
    p = jnp.exp(logits - mx)
    aff_ref[0] = p / jnp.sum(p, axis=0, keepdims=True)


def mm_res_router(a, w, x, g, wr_t, batch, a_transposed=False, tm=PROJ_ROWS):
    n, d = x.shape
    k = w.shape[0]
    e = wr_t.shape[0]
    t = n // batch
    tm = min(tm, t)
    tpb = t // tm
    if a_transposed:
        w_slab = a.shape[2]
        a_spec = pl.BlockSpec((tm // w_slab, k, w_slab), lambda i: (i, 0, 0))
    else:
        a_spec = pl.BlockSpec((tm, k), lambda i: (i, 0))
    return pl.pallas_call(
        functools.partial(_mm_res_router_kernel, a_transposed=a_transposed),
        grid=(n // tm,),
        in_specs=[a_spec,
                  pl.BlockSpec((k, d), lambda i: (0, 0)),
                  pl.BlockSpec((tm, d), lambda i: (i, 0)),
                  pl.BlockSpec((1, d), lambda i: (0, 0)),
                  pl.BlockSpec((e, d), lambda i: (0, 0))],
        out_specs=[pl.BlockSpec((tm, d), lambda i: (i, 0)),
                   pl.BlockSpec((tm, d // 2), lambda i: (i, 0)),
                   pl.BlockSpec((1, e, tm), lambda i: (i // tpb, 0, i % tpb))],
        out_shape=[jax.ShapeDtypeStruct((n, d), F32),
                   jax.ShapeDtypeStruct((n, d // 2), U32),
                   jax.ShapeDtypeStruct((batch, e, t), F32)],
        compiler_params=_cparams(1),
    )(a, w, x, g.reshape(1, d), wr_t)


def _t5_bucket(rel):
    nb = NUM_BUCKETS // 2
    ret = (rel > 0).astype(jnp.int32) * nb
    n = jnp.abs(rel)
    max_exact = nb // 2
    nf = jnp.maximum(n, 1).astype(jnp.float32)
    large = max_exact + (jnp.log(nf / max_exact) / math.log(MAX_DISTANCE / max_exact)
                         * (nb - max_exact)).astype(jnp.int32)
    large = jnp.minimum(large, nb - 1)
    return ret + jnp.where(n < max_exact, n, large)


def _attn_bucket_table():
    kk = jnp.arange(3 * ATT_BLOCK)[:, None]
    qq = jnp.arange(ATT_BLOCK)[None, :]
    rel = kk - ATT_BLOCK - qq
    return jnp.where(jnp.abs(rel) <= ATT_BLOCK, _t5_bucket(rel), -1).astype(I32)


ATT_QBLOCKS = 4


def _attn_kernel(sink_ref, rb_ref, q_ref, *rest, nb, n_kv):
    nq = ATT_QBLOCKS
    k_refs = rest[:nq + 2]
    v_refs = rest[nq + 2:2 * (nq + 2)]
    bucket_ref, ot_ref, bias_sc = rest[2 * (nq + 2):]
    n = pl.program_id(1)
    blk = ATT_BLOCK
    dh = HEAD_DIM
    gw = GQA_GROUP * blk

    @pl.when(jnp.logical_and(pl.program_id(0) == 0, n == 0))
    def _bias_table():
        bk = bucket_ref[...]
        for hq in range(n_kv * GQA_GROUP):
            acc = jnp.full(bk.shape, NEG, F32)
            for k in range(NUM_BUCKETS):
                acc = jnp.where(bk == k, rb_ref[k, hq] * LOG2E, acc)
            bias_sc[hq // GQA_GROUP, :, (hq % GQA_GROUP) * blk:(hq % GQA_GROUP + 1) * blk] = acc

    kblk = [r[...] for r in k_refs]
    vblk = [r[0] for r in v_refs]
    kidx = lax.broadcasted_iota(I32, (3 * blk, gw), 0)
    ones_rows = (lax.broadcasted_iota(I32, (16, 3 * blk), 0) == 0).astype(BF16)
    lane = lax.broadcasted_iota(I32, (1, gw), 1)
    heads = [[h * GQA_GROUP + g for g in range(GQA_GROUP)] for h in range(n_kv)]
    sk = []
    for h in range(n_kv):
        row_sink = jnp.full((1, gw), sink_ref[heads[h][-1]] * LOG2E, F32)
        for g in reversed(range(GQA_GROUP - 1)):
            row_sink = jnp.where(lane < (g + 1) * blk, sink_ref[heads[h][g]] * LOG2E, row_sink)
        sk.append(row_sink)
    keys, vt, valid = [], [], []
    for j in range(nq):
        i = n * nq + j
        keys.append(jnp.concatenate(kblk[j:j + 3], axis=0))
        vt.append(jnp.concatenate(vblk[j:j + 3], axis=1))
        valid.append(jnp.logical_and(jnp.logical_or(i > 0, kidx >= blk),
                                     jnp.logical_or(i < nb - 1, kidx < 2 * blk)))
    units = [(j, h) for j in range(nq) for h in range(n_kv)]
    q = [jnp.concatenate([q_ref[j * blk:(j + 1) * blk, hq * dh:(hq + 1) * dh] for hq in heads[h]], axis=0)
         for j, h in units]
    vaug = [jnp.concatenate([vt[j][h * dh:(h + 1) * dh, :], ones_rows], axis=0) for j, h in units]
    s = [jnp.where(valid[j], _dot_nt(keys[j][:, h * dh:(h + 1) * dh], q[u]) + bias_sc[h], NEG)
         for u, (j, h) in enumerate(units)]
    m = [jnp.maximum(jnp.max(s[u], axis=0, keepdims=True), sk[h]) for u, (j, h) in enumerate(units)]
    p = [jnp.exp2(s[u] - m[u]).astype(BF16) for u in range(len(units))]
    oa = [_dot(vaug[u], p[u]) for u in range(len(units))]
    o = [oa[u][:dh] / (oa[u][dh:dh + 1] + jnp.exp2(sk[h] - m[u])) for u, (j, h) in enumerate(units)]
    for u, (j, h) in enumerate(units):
        for g, hq in enumerate(heads[h]):
            ot_ref[0, hq * dh:(hq + 1) * dh, j * blk:(j + 1) * blk] = o[u][:, g * blk:(g + 1) * blk].astype(ot_ref.dtype)


def attention_core(q, k, vt, rel_bias, sink, batch, seq):
    n, dq = q.shape
    dk = k.shape[1]
    hq = rel_bias.shape[1]
    n_kv = hq // GQA_GROUP
    nb = seq // ATT_BLOCK
    blk = ATT_BLOCK
    nq = ATT_QBLOCKS
    assert nb % nq == 0

    def key_block(off):
        return lambda b, i: b * nb + jnp.clip(i * nq + off, 0, nb - 1)

    offs = range(-1, nq + 1)
    return pl.pallas_call(
        functools.partial(_attn_kernel, nb=nb, n_kv=n_kv),
        grid=(batch, nb // nq),
        in_specs=[pl.BlockSpec(memory_space=pltpu.SMEM),
                  pl.BlockSpec(memory_space=pltpu.SMEM),
                  pl.BlockSpec((nq * blk, dq), lambda b, i: (b * (nb // nq) + i, 0))]
                 + [pl.BlockSpec((blk, dk), (lambda f: lambda b, i: (f(b, i), 0))(key_block(o))) for o in offs]
                 + [pl.BlockSpec((1, dk, blk), (lambda f: lambda b, i: (f(b, i), 0, 0))(key_block(o))) for o in offs]
                 + [pl.BlockSpec((3 * blk, blk), lambda b, i: (0, 0))],
        out_specs=pl.BlockSpec((1, dq, nq * blk), lambda b, i: (b * (nb // nq) + i, 0, 0)),
        out_shape=jax.ShapeDtypeStruct((n // (nq * blk), dq, nq * blk), BF16),
        scratch_shapes=[pltpu.VMEM((n_kv, 3 * blk, GQA_GROUP * blk), F32)],
        compiler_params=_cparams(2),
    )(sink.astype(F32), rel_bias.astype(F32), q, *([k] * (nq + 2)), *([vt] * (nq + 2)), _attn_bucket_table())


def _prefix_incl(x, tri):
    r, t = x.shape
    nck = t // LANE
    assert nck <= LANE
    xb = x.astype(BF16)
    local = [_dot(xb[:, c * LANE:(c + 1) * LANE], tri) for c in range(nck)]
    tot = jnp.concatenate([p[:, LANE - 1:LANE] for p in local] + [jnp.zeros((r, LANE - nck), F32)], axis=1)
    lane = lax.broadcasted_iota(I32, tot.shape, 1)
    inc = tot
    sh = 1
    while sh < nck:
        inc = inc + jnp.where(lane >= sh, pltpu.roll(inc, sh, axis=1), 0.0)
        sh *= 2
    offs = inc - tot
    return jnp.concatenate([local[c] + offs[:, c:c + 1] for c in range(nck)], axis=1)


BF16_EXACT = 256
TOK_SPLIT = 64


def _select_kernel(aff_ref, cmp_ref, idx_ref, dest_ref, offc_ref, rng_ref, pos_sc, dest_sc, vals_sc, *, cap, rchunk):
    aff = aff_ref[0]
    ne, t = aff.shape
    bidx = pl.program_id(0)
    bits = pltpu.bitcast(aff, I32)

    def search(i, lo):
        cand = lo | lax.shift_left(jnp.int32(1), 30 - i)
        cnt = jnp.sum((bits >= cand).astype(I32), axis=1, keepdims=True)
        return jnp.where(cnt >= cap, cand, lo)

    thr = lax.fori_loop(0, 31, search, jnp.zeros((ne, 1), I32))
    gt = bits > thr
    eq = bits == thr
    need = (cap - jnp.sum(gt.astype(I32), axis=1, keepdims=True)).astype(F32)
    ri = lax.broadcasted_iota(I32, (LANE, LANE), 0)
    ci = lax.broadcasted_iota(I32, (LANE, LANE), 1)
    tri = (ri <= ci).astype(BF16)
    eqf = eq.astype(F32)
    rank_eq = _prefix_incl(eqf, tri) - eqf
    sel = jnp.logical_or(gt, jnp.logical_and(eq, rank_eq < need))
    self_ = sel.astype(F32)
    kt = jnp.sum(self_, axis=0, keepdims=True)
    pre = _prefix_incl(jnp.concatenate([self_, jnp.broadcast_to(kt, (8, t))], axis=0), tri)
    pos_sc[...] = jnp.where(sel, pre[:ne] - 1.0, -1.0)
    end = pre[ne:ne + 1]
    off = end - kt
    lr = lax.broadcasted_iota(I32, (ne, ne), 0)
    lc = lax.broadcasted_iota(I32, (ne, ne), 1)
    jexp = _dot((lc < lr).astype(BF16), self_.astype(BF16))
    dest_sc[...] = off + jexp

    tok = lax.broadcasted_iota(I32, (1, t), 1)
    vals_sc[0:1, :] = (tok // TOK_SPLIT).astype(F32)
    vals_sc[1:2, :] = (tok % TOK_SPLIT).astype(F32)
    vals_sc[7:8, :] = jnp.zeros((1, t), F32)
    slot = lax.broadcasted_iota(I32, (cap, 1), 0).astype(F32)

    def compact(e, carry):
        d = dest_sc[pl.ds(e, 1), :]
        dh = jnp.floor(d * (1.0 / LANE))
        g = aff_ref[0, pl.ds(e, 1), :]
        g0 = g.astype(BF16).astype(F32)
        g1 = (g - g0).astype(BF16).astype(F32)
        vals_sc[2:3, :] = dh
        vals_sc[3:4, :] = d - dh * LANE
        vals_sc[4:5, :] = g0
        vals_sc[5:6, :] = g1
        vals_sc[6:7, :] = g - g0 - g1
        onehot = (pos_sc[pl.ds(e, 1), :] == slot).astype(BF16)
        out = _dot_nt(vals_sc[...].astype(BF16), onehot)
        cmp_ref[0, e] = out
        idx_ref[e, 0] = (out[0:1] * TOK_SPLIT + out[1:2]).astype(I32) + bidx * t
        dest_ref[e + 1, 0] = (out[2:3] * LANE + out[3:4]).astype(I32) + bidx * (ne * cap)
        return carry

    lax.fori_loop(0, ne, compact, 0)
    idx_ref[ne, 0] = idx_ref[0, 0]
    dest_ref[0, 0] = (pl.num_programs(0) * ne + bidx) * cap + lax.broadcasted_iota(I32, (1, cap), 1)

    eh = jnp.floor(end * (1.0 / LANE))
    oh = jnp.floor(off * (1.0 / LANE))
    v4 = jnp.concatenate([oh, off - oh * LANE, eh, end - eh * LANE, jnp.zeros((4, t), F32)], axis=0)
    eye = (ri == ci).astype(BF16)
    for c in range(t // LANE):
        offc_ref[0, c * LANE:(c + 1) * LANE, :] = _dot_nt(eye, v4[:, c * LANE:(c + 1) * LANE].astype(BF16))

    nchunk = (ne * cap) // rchunk
    low = (lax.broadcasted_iota(I32, (nchunk, 1), 0) * rchunk).astype(F32)
    first = jnp.sum((end <= low).astype(F32), axis=1, keepdims=True)
    last = jnp.sum((end <= low + (rchunk - 1)).astype(F32), axis=1, keepdims=True)
    lane = lax.broadcasted_iota(I32, (nchunk, LANE), 1)
    tiles = jnp.where(lane < LANE // 2, jnp.floor(first * (1.0 / LANE)), jnp.floor(last * (1.0 / LANE)))
    rng_ref[0] = tiles.astype(I32)


COMBINE_ROWS = 1024
COMBINE_TILES = 5


def route_select(aff, cap, rchunk):
    batch, ne, t = aff.shape
    nchunk = ne * cap // rchunk
    assert t <= TOK_SPLIT * BF16_EXACT and ne * cap <= LANE * BF16_EXACT and LANE <= BF16_EXACT
    return pl.pallas_call(
        functools.partial(_select_kernel, cap=cap, rchunk=rchunk),
        grid=(batch,),
        in_specs=[pl.BlockSpec((1, ne, t), lambda b: (b, 0, 0))],
        out_specs=[pl.BlockSpec((1, ne, 8, cap), lambda b: (b, 0, 0, 0)),
                   pl.BlockSpec((ne + 1, 1, 1, cap), lambda b: (0, b, 0, 0)),
                   pl.BlockSpec((ne + 1, 1, 1, cap), lambda b: (0, b, 0, 0)),
                   pl.BlockSpec((1, t, 8), lambda b: (b, 0, 0)),
                   pl.BlockSpec((1, nchunk, LANE), lambda b: (b, 0, 0))],
        out_shape=[jax.ShapeDtypeStruct((batch, ne, 8, cap), F32),
                   jax.ShapeDtypeStruct((ne + 1, batch, 1, cap), I32),
                   jax.ShapeDtypeStruct((ne + 1, batch, 1, cap), I32),
                   jax.ShapeDtypeStruct((batch, t, 8), F32),
                   jax.ShapeDtypeStruct((batch, nchunk, LANE), I32)],
        scratch_shapes=[pltpu.VMEM((ne, t), F32), pltpu.VMEM((ne, t), F32), pltpu.VMEM((8, t), F32)],
        compiler_params=_cparams(1),
    )(aff)


def _ffn_kernel(idx_sm, dest_sm, hn_hbm, cmp_ref, w1_ref, w3_ref, w2_ref, r_hbm,
                xsu, xsb, yacc, ysc, wb1, wb3, wb2, gsem, ssem, *, batch, cap, nf, rt):
    e = pl.program_id(0)
    f = pl.program_id(1)
    ne = pl.num_programs(0)
    rows = batch * cap
    sub = rows // nf

    def gather_start(block, r):
        t = idx_sm[block * rows + r]
        pltpu.make_async_copy(hn_hbm.at[pl.ds(t, 1), :], xsu.at[pl.ds(r, 1), :], gsem).start()

    def gather_wait():
        pltpu.make_async_copy(hn_hbm.at[pl.ds(0, rows), :], xsu, gsem).wait()

    def scatter_start(block, r):
        d = dest_sm[block * rows + r]
        pltpu.make_async_copy(ysc.at[pl.ds(r, 1), :], r_hbm.at[pl.ds(d, 1), :], ssem).start()

    def scatter_wait():
        pltpu.make_async_copy(ysc, r_hbm.at[pl.ds(0, rows), :], ssem).wait()

    @pl.when(jnp.logical_and(e == 0, f == 0))
    def _prologue():
        def issue(r, c):
            gather_start(0, r)
            return c
        lax.fori_loop(0, rows, issue, 0)
        ysc[...] = jnp.zeros_like(ysc)
        yacc[...] = jnp.zeros_like(yacc)

    @pl.when(f == 0)
    def _rows_ready():
        gather_wait()
        half = xsu.shape[1]
        xsb[:, :half], xsb[:, half:] = _unpack_bf16_pairs(xsu[...])

    for r in range(sub):
        gather_start(e + 1, f * sub + r)
        scatter_start(e, f * sub + r)

    wb1[...] = w1_ref[0, 0].astype(BF16)
    for r in range(rows // rt):
        xs = xsb[r * rt:(r + 1) * rt, :]
        a = _dot(xs, wb1[...])
        if r == 0:
            wb3[...] = w3_ref[0, 0].astype(BF16)
        u = _dot(xs, wb3[...])
        hmid = (a * jax.nn.sigmoid(a) * u).astype(BF16)
        if r == 0:
            wb2[...] = w2_ref[0, 0].astype(BF16)
        y = _dot(hmid, wb2[...])
        yacc[r * rt:(r + 1) * rt, :] = jnp.where(f == 0, y, yacc[r * rt:(r + 1) * rt, :] + y)

    @pl.when(f == nf - 1)
    def _finish():
        scatter_wait()
        ri = lax.broadcasted_iota(I32, (cap, cap), 0)
        ci = lax.broadcasted_iota(I32, (cap, cap), 1)
        eye = (ri == ci).astype(BF16)
        rows8 = cmp_ref[...].reshape(batch * 8, cap).astype(BF16)
        gt = _dot_nt(eye, rows8)
        for b in range(batch):
            g = gt[:, 8 * b + 4:8 * b + 5] + gt[:, 8 * b + 5:8 * b + 6] + gt[:, 8 * b + 6:8 * b + 7]
            ysc[b * cap:(b + 1) * cap, :] = _pack_bf16_pairs(yacc[b * cap:(b + 1) * cap, :] * g)

        @pl.when(e == ne - 1)
        def _epilogue():
            def issue(r, c):
                scatter_start(ne, r)
                return c
            lax.fori_loop(0, rows, issue, 0)
            scatter_wait()
            gather_wait()


def expert_ffn(hn, cmp, idx_flat, dest_flat, w1, w3, w2, layer, cap, tf=512, rt=512):
    d = w1.shape[2]
    batch, ne = cmp.shape[0], cmp.shape[1]
    dff = w1.shape[3]
    nf = dff // tf
    rows = batch * cap
    rt = min(rt, rows)
    grid_spec = pltpu.PrefetchScalarGridSpec(
        num_scalar_prefetch=2,
        grid=(ne, nf),
        in_specs=[pl.BlockSpec(memory_space=pl.ANY),
                  pl.BlockSpec((batch, 1, 8, cap), lambda e, f, *_: (0, e, 0, 0)),
                  pl.BlockSpec((1, 1, d, tf), lambda e, f, *_: (layer, e, 0, f)),
                  pl.BlockSpec((1, 1, d, tf), lambda e, f, *_: (layer, e, 0, f)),
                  pl.BlockSpec((1, 1, tf, d), lambda e, f, *_: (layer, e, f, 0))],
        out_specs=pl.BlockSpec(memory_space=pl.ANY),
        scratch_shapes=[pltpu.VMEM((rows, d // 2), U32), pltpu.VMEM((rows, d), BF16),
                        pltpu.VMEM((rows, d), F32), pltpu.VMEM((rows, d // 2), U32),
                        pltpu.VMEM((d, tf), BF16), pltpu.VMEM((d, tf), BF16), pltpu.VMEM((tf, d), BF16),
                        pltpu.SemaphoreType.DMA, pltpu.SemaphoreType.DMA])
    return pl.pallas_call(
        functools.partial(_ffn_kernel, batch=batch, cap=cap, nf=nf, rt=rt),
        grid_spec=grid_spec,
        out_shape=jax.ShapeDtypeStruct((batch * ne * cap + rows, d // 2), U32),
        compiler_params=_cparams(2),
    )(idx_flat, dest_flat, hn, cmp, w1, w3, w2)


def _combine_kernel(tlo_sm, thi_sm, r_ref, offc_ref, x_hbm, o_hbm, acc, xsem, osem, *, batch, nchunk, rchunk):
    g = pl.program_id(0)
    b = g // nchunk
    j = g % nchunk
    slot = b % 2
    ntile = acc.shape[1] // LANE

    def x_copy(bb):
        return pltpu.make_async_copy(x_hbm.at[bb], acc.at[bb % 2], xsem.at[bb % 2])

    def o_copy(bb):
        return pltpu.make_async_copy(acc.at[bb % 2], o_hbm.at[bb], osem.at[bb % 2])

    @pl.when(g == 0)
    def _first():
        x_copy(0).start()

    @pl.when(j == 0)
    def _batch_start():
        x_copy(b).wait()

    @pl.when(j == nchunk // 2)
    def _mid():
        @pl.when(b > 0)
        def _():
            o_copy(b - 1).wait()

        @pl.when(b + 1 < batch)
        def _():
            x_copy(b + 1).start()

    half = r_ref.shape[1]
    left, right = _unpack_bf16_pairs(r_ref[...])
    rowid = (j * rchunk + lax.broadcasted_iota(I32, (1, rchunk), 1)).astype(F32)

    def contrib(i, valid):
        t0 = pl.multiple_of(i * LANE, LANE)
        oc = offc_ref[0, pl.ds(t0, LANE), :]
        off = oc[:, 0:1] * LANE + oc[:, 1:2]
        end = oc[:, 2:3] * LANE + oc[:, 3:4]
        p = jnp.logical_and(jnp.logical_and(rowid >= off, rowid < end), valid).astype(BF16)
        return t0, (_dot(p, left), _dot(p, right))

    def add(t0, y):
        acc[slot, pl.ds(t0, LANE), :half] += y[0]
        acc[slot, pl.ds(t0, LANE), half:] += y[1]

    def tile(i, c):
        add(*contrib(i, True))
        return c

    tlo = tlo_sm[g]
    thi = thi_sm[g]
    parts = [contrib(jnp.minimum(tlo + k, ntile - 1), tlo + k <= thi) for k in range(COMBINE_TILES)]
    for t0, y in parts:
        add(t0, y)
    lax.fori_loop(tlo + COMBINE_TILES, thi + 1, tile, 0)

    @pl.when(j == nchunk - 1)
    def _batch_end():
        o_copy(b).start()

        @pl.when(b == batch - 1)
        def _():
            o_copy(b).wait()


def combine(rbuf, offc, tlo, thi, x, rows_per_batch, rchunk):
    batch, t, d = x.shape
    nchunk = rows_per_batch // rchunk
    assert nchunk >= 2
    grid_spec = pltpu.PrefetchScalarGridSpec(
        num_scalar_prefetch=2,
        grid=(batch * nchunk,),
        in_specs=[pl.BlockSpec((rchunk, d // 2), lambda g, *_: (g, 0)),
                  pl.BlockSpec((1, t, 8), lambda g, *_: (g // nchunk, 0, 0)),
                  pl.BlockSpec(memory_space=pl.ANY)],
        out_specs=pl.BlockSpec(memory_space=pl.ANY),
        scratch_shapes=[pltpu.VMEM((2, t, d), F32), pltpu.SemaphoreType.DMA((2,)), pltpu.SemaphoreType.DMA((2,))])
    return pl.pallas_call(
        functools.partial(_combine_kernel, batch=batch, nchunk=nchunk, rchunk=rchunk),
        grid_spec=grid_spec,
        out_shape=jax.ShapeDtypeStruct((batch, t, d), F32),
        compiler_params=_cparams(1),
    )(tlo, thi, rbuf, offc, x)


def moe_block(x1, hn, aff, w1, w3, w2, layer, batch):
    n, d = x1.shape
    t = n // batch
    ne = aff.shape[1]
    cap = CAPACITY_FACTOR * t // ne
    cmp, idx, dest, offc, rng = route_select(aff, cap, COMBINE_ROWS)
    rbuf = expert_ffn(hn, cmp, idx.reshape(-1), dest.reshape(-1), w1, w3, w2, layer, cap)
    tlo = rng[:, :, 0].reshape(-1)
    thi = rng[:, :, LANE - 1].reshape(-1)
    return combine(rbuf, offc, tlo, thi, x1.reshape(batch, t, d), ne * cap, COMBINE_ROWS)


MLSTM_L = 256
MLSTM_UNROLL = 4


def _log_sigmoid(x):
    return jnp.minimum(x, 0.0) - jnp.log1p(jnp.exp(-jnp.abs(x)))


def _split3(x):
    x0 = x.astype(BF16)
    r = x - x0.astype(F32)
    x1 = r.astype(BF16)
    return x0, x1, (r - x1.astype(F32)).astype(BF16)


def _mlstm_proj_kernel(x_ref, g_ref, wm_ref, wkt_ref, wgt_ref, bt_ref, main_ref, kt_ref, gt_ref, *, nsub, nh):
    ln = MLSTM_L
    h = _rms(x_ref[...], g_ref[...]).astype(BF16)
    main_ref[...] = _dot(h, wm_ref[...]).astype(main_ref.dtype)
    kt = _dot_nt(wkt_ref[...], h).astype(kt_ref.dtype)
    pre = _dot_nt(wgt_ref[...], h) + bt_ref[...]
    lst = _log_sigmoid(pre)
    ri = lax.broadcasted_iota(I32, (ln, ln), 0)
    ci = lax.broadcasted_iota(I32, (ln, ln), 1)
    low = (ci <= ri).astype(BF16)
    upp = (ci >= ri).astype(BF16)
    row = lax.broadcasted_iota(I32, (pre.shape[0], ln), 0)
    fwd = jnp.logical_and(row >= nh, row < 2 * nh)
    bwd = row >= 3 * nh
    for j in range(nsub):
        sl = slice(j * ln, (j + 1) * ln)
        kt_ref[j] = kt[:, sl]
        pt = _split3(lst[:, sl])
        gt_ref[j] = jnp.where(fwd, sum(_dot(x, upp) for x in pt),
                              jnp.where(bwd, sum(_dot(x, low) for x in pt), pre[:, sl]))


def mlstm_project(x, g, wm, wkt, wgt, bias_t, nh, tm=PROJ_ROWS):
    n, d = x.shape
    nsub = tm // MLSTM_L
    nck = n // MLSTM_L
    const = lambda i: (0, 0)
    return pl.pallas_call(
        functools.partial(_mlstm_proj_kernel, nsub=nsub, nh=nh),
        grid=(n // tm,),
        in_specs=[pl.BlockSpec((tm, d), lambda i: (i, 0)),
                  pl.BlockSpec((1, d), const),
                  pl.BlockSpec(wm.shape, const),
                  pl.BlockSpec(wkt.shape, const),
                  pl.BlockSpec(wgt.shape, const),
                  pl.BlockSpec(bias_t.shape, const)],
        out_specs=[pl.BlockSpec((tm, wm.shape[1]), lambda i: (i, 0)),
                   pl.BlockSpec((nsub, wkt.shape[0], MLSTM_L), lambda i: (i, 0, 0)),
                   pl.BlockSpec((nsub, wgt.shape[0], MLSTM_L), lambda i: (i, 0, 0))],
        out_shape=[jax.ShapeDtypeStruct((n, wm.shape[1]), BF16),
                   jax.ShapeDtypeStruct((nck, wkt.shape[0], MLSTM_L), BF16),
                   jax.ShapeDtypeStruct((nck, wgt.shape[0], MLSTM_L), F32)],
        compiler_params=_cparams(1),
    )(x, g.reshape(1, d), wm, wkt, wgt, bias_t)


def _mlstm_kernel(q_ref, kt_ref, v_ref, og_ref, gt_ref, ng_ref, y_ref, hf, hb, cst, b_sc, u_sc, cm_sc,
                  *, nc, nh, dqk, dv):
    p = pl.program_id(1)
    ln = MLSTM_L
    ri = lax.broadcasted_iota(I32, (ln, ln), 0)
    ci = lax.broadcasted_iota(I32, (ln, ln), 1)
    masks = (ci <= ri, ci >= ri)
    ones_blk = jnp.ones((ln, dv), BF16)
    lane8 = lax.broadcasted_iota(I32, (8, ln), 1)
    row8 = lax.broadcasted_iota(I32, (8, ln), 0)
    bwd_row = (row8 % 2) == 1
    edge = lane8 == jnp.where(bwd_row, 0, ln - 1)
    kk = lax.broadcasted_iota(I32, (48, 2 * dv), 0) % 16
    cc = lax.broadcasted_iota(I32, (48, 2 * dv), 1)
    sel = [jnp.logical_or(jnp.logical_and(kk == i, cc < dv), jnp.logical_and(kk == 4 + i, cc >= dv)).astype(BF16)
           for i in range(4)]
    zero4 = jnp.zeros((4, ln), F32)
    cst[...] = jnp.zeros_like(cst)
    chains = [(hh, d) for hh in range(2) for d in range(2)]

    def gate_rows(c, which):
        return [gt_ref[c if d == 0 else nc - 1 - c, pl.ds((2 * d + which) * nh + 2 * p + hh, 1), :]
                for hh, d in chains] + [zero4]
    b_all = jnp.concatenate([x for c in range(nc) for x in gate_rows(c, 1)], axis=0)
    u_all = jnp.concatenate([x for c in range(nc) for x in gate_rows(c, 0)], axis=0) - b_all
    lane_a = lax.broadcasted_iota(I32, u_all.shape, 1)
    bwd_a = (lax.broadcasted_iota(I32, u_all.shape, 0) % 2) == 1
    cf = cb = u_all
    sh = 1
    while sh < ln:
        cf = jnp.maximum(cf, jnp.where(lane_a >= sh, pltpu.roll(cf, sh, axis=1), NEG))
        cb = jnp.maximum(cb, jnp.where(lane_a < ln - sh, pltpu.roll(cb, ln - sh, axis=1), NEG))
        sh *= 2
    b_sc[...] = b_all.reshape(nc, 8, ln)
    u_sc[...] = u_all.reshape(nc, 8, ln)
    cm_sc[...] = jnp.where(bwd_a, cb, cf).reshape(nc, 8, ln)

    def step(c, ms):
        cks = [c if d == 0 else nc - 1 - c for _, d in chains]
        r0s = [pl.multiple_of(ck * ln, ln) for ck in cks]
        cprev = [cst[i] for i in range(4)]
        q = [q_ref[pl.ds(r0s[i], ln), hh * dqk:(hh + 1) * dqk] for i, (hh, _) in enumerate(chains)]
        kt = [kt_ref[cks[i], hh * dqk:(hh + 1) * dqk, :] for i, (hh, _) in enumerate(chains)]
        vaug = [jnp.concatenate([v_ref[pl.ds(r0s[i], ln), hh * dv:(hh + 1) * dv], ones_blk], axis=1)
                for i, (hh, _) in enumerate(chains)]
        s = [_dot(q[i], kt[i]) for i in range(4)]
        inter = [_dot(q[i], cprev[i].astype(BF16)) for i in range(4)]
        b = b_sc[c]
        u = u_sc[c]
        m_run = jnp.maximum(ms, cm_sc[c])
        m_end = jnp.max(jnp.where(edge, m_run, NEG), axis=1, keepdims=True)
        b_end = jnp.sum(jnp.where(edge, b, 0.0), axis=1, keepdims=True)
        wc = jnp.exp(u - m_end)
        decay = jnp.exp(ms - m_end)
        rows = jnp.concatenate([m_run[:4] * LOG2E, jnp.exp(-(b + m_run))[:4], zero4, zero4], axis=0)
        stack = jnp.concatenate(_split3(rows), axis=0)
        u2 = u * LOG2E
        ms2 = ms * LOG2E
        bc = [lax.dot_general(stack, sel[i], (((0,), (0,)), ((), ())), preferred_element_type=F32)
              for i in range(4)]
        mb = [jnp.concatenate([bc[i][:, :dv]] * (ln // dv), axis=1) for i in range(4)]
        sw = [(jnp.exp2(jnp.where(masks[d], u2[i:i + 1, :] - mb[i], NEG)) * s[i]).astype(BF16)
              for i, (_, d) in enumerate(chains)]
        intra = [_dot(sw[i], vaug[i]) for i in range(4)]
        ea = [jnp.exp2(ms2[i:i + 1, :] - bc[i][:, :dv]) for i in range(4)]
        num = [ea[i] * inter[i][:, :dv] + intra[i][:, :dv] for i in range(4)]
        den = [ea[i] * inter[i][:, dv:] + intra[i][:, dv:] for i in range(4)]
        hc = [num[i] / jnp.maximum(jnp.abs(den[i]), bc[i][:, dv:]) for i in range(4)]
        kw = [(kt[i].astype(F32) * wc[i:i + 1, :]).astype(BF16) for i in range(4)]
        c_new = [decay[i:i + 1, :] * cprev[i] + _dot(kw[i], vaug[i]) for i in range(4)]
        for i, (hh, d) in enumerate(chains):
            cst[i] = c_new[i]
            (hf if d == 0 else hb)[pl.ds(r0s[i], ln), hh * dv:(hh + 1) * dv] = hc[i]
        return b_end + m_end

    unroll = math.gcd(nc, MLSTM_UNROLL)

    def steps(cu, ms):
        for k in range(unroll):
            ms = step(cu * unroll + k, ms)
        return ms

    lax.fori_loop(0, nc // unroll, steps, jnp.zeros((8, 1), F32))
    for hh in range(2):
        sl = slice(hh * dv, (hh + 1) * dv)
        hs = _rms(hf[:, sl] + hb[:, sl], ng_ref[:, sl])
        y_ref[:, sl] = (hs * jax.nn.sigmoid(og_ref[:, sl].astype(F32))).astype(y_ref.dtype)


def mlstm_core(main, ktc, grow, out_g, batch, seq, nh, dqk, dv):
    n = main.shape[0]
    nc = seq // MLSTM_L
    npair = nh // 2
    vblocks = (nh * dqk) // (2 * dv)
    ogblocks = (nh * dqk + nh * dv) // (2 * dv)
    return pl.pallas_call(
        functools.partial(_mlstm_kernel, nc=nc, nh=nh, dqk=dqk, dv=dv),
        grid=(batch, npair),
        in_specs=[pl.BlockSpec((seq, 2 * dqk), lambda b, p: (b, p)),
                  pl.BlockSpec((nc, 2 * dqk, MLSTM_L), lambda b, p: (b, p, 0)),
                  pl.BlockSpec((seq, 2 * dv), lambda b, p: (b, vblocks + p)),
                  pl.BlockSpec((seq, 2 * dv), lambda b, p: (b, ogblocks + p)),
                  pl.BlockSpec((nc, 4 * nh, MLSTM_L), lambda b, p: (b, 0, 0)),
                  pl.BlockSpec((1, 2 * dv), lambda b, p: (0, p))],
        out_specs=pl.BlockSpec((seq, 2 * dv), lambda b, p: (b, p)),
        out_shape=jax.ShapeDtypeStruct((n, nh * dv), BF16),
        scratch_shapes=[pltpu.VMEM((seq, 2 * dv), F32), pltpu.VMEM((seq, 2 * dv), F32),
                        pltpu.VMEM((4, dqk, 2 * dv), F32)] + [pltpu.VMEM((nc, 8, MLSTM_L), F32)] * 3,
        compiler_params=_cparams(2),
    )(main, ktc, main, main, grow, out_g.reshape(1, nh * dv).astype(F32))


def mlstm_layer(x2d, batch, seq, norm_g, w_in, b_i, b_f, out_g, w_out, ffn_g, wr):
    d = x2d.shape[1]
    nh = MLSTM_HEADS
    dv = d // nh
    dqk = dv // 2
    o1, o2, o3, o4 = nh * dqk, 2 * nh * dqk, 2 * nh * dqk + nh * dv, 2 * nh * dqk + 2 * nh * dv
    wm = jnp.concatenate([w_in[:, :o1], w_in[:, o2:o4]], axis=1).astype(BF16)
    wkt = (w_in[:, o1:o2] * (dqk ** -0.5)).T.astype(BF16)
    wgt = w_in[:, o4:].T.astype(BF16)
    bias_t = jnp.concatenate([b_i[0], b_f[0], b_i[1], b_f[1]]).reshape(4 * nh, 1).astype(F32)
    main, ktc, grow = mlstm_project(x2d, norm_g, wm, wkt, wgt, bias_t, nh)
    y = mlstm_core(main, ktc, grow, out_g, batch, seq, nh, dqk, dv)
    return mm_res_router(y, w_out.astype(BF16), x2d, ffn_g, wr.T.astype(BF16), batch)


def attention_layer(x2d, batch, seq, norm_g, w_in, q_g, k_g, sink, w_out, rel_bias, ffn_g, wr):
    n_q = rel_bias.shape[1]
    q, k, vt = attention_project(x2d, norm_g, w_in, q_g, k_g, n_q, n_q // GQA_GROUP)
    ot = attention_core(q, k, vt, rel_bias, sink, batch, seq)
    return mm_res_router(ot, w_out.astype(BF16), x2d, ffn_g, wr.T.astype(BF16), batch, a_transposed=True)


def kernel(x, rel_bias, attn_norm_g, attn_w_in, attn_q_norm_g, attn_k_norm_g, attn_sink, attn_w_out, mlstm_norm_g, mlstm_w_in, mlstm_b_i, mlstm_b_f, mlstm_out_norm_g, mlstm_w_out, ffn_norm_g, router_w, expert_w1, expert_w3, expert_w2):
    batch, seq, d = x.shape
    x2d = x.reshape(batch * seq, d)
    x1, hn, aff = attention_layer(x2d, batch, seq, attn_norm_g[0], attn_w_in[0], attn_q_norm_g[0],
                                  attn_k_norm_g[0], attn_sink[0], attn_w_out[0], rel_bias,
                                  ffn_norm_g[0], router_w[0])
    x = moe_block(x1, hn, aff, expert_w1, expert_w3, expert_w2, 0, batch)
    x1, hn, aff = mlstm_layer(x.reshape(batch * seq, d), batch, seq, mlstm_norm_g[0], mlstm_w_in[0], mlstm_b_i[0],
                              mlstm_b_f[0], mlstm_out_norm_g[0], mlstm_w_out[0], ffn_norm_g[1], router_w[1])
    return moe_block(x1, hn, aff, expert_w1, expert_w3, expert_w2, 1, batch)
```

```python
import functools
import math

import jax
import jax.numpy as jnp
from jax import lax
from jax.experimental import pallas as pl
from jax.experimental.pallas import tpu as pltpu

F32 = jnp.float32
BF16 = jnp.bfloat16
I32 = jnp.int32
U32 = jnp.uint32

RMS_EPS = 1e-6
NEG = -1e30
LOG2E = 1.4426950408889634
LANE = 128
MXU_DIM = 256
VMEM_LIMIT = 56 * 1024 * 1024
PROJ_ROWS = 1024

HEAD_DIM = 64
GQA_GROUP = 4
ATT_BLOCK = 128
NUM_BUCKETS = 32
MAX_DISTANCE = 128
CAPACITY_FACTOR = 2
MLSTM_HEADS = 8


def _cparams(n_axes, vmem=VMEM_LIMIT):
    return pltpu.CompilerParams(dimension_semantics=("arbitrary",) * n_axes, vmem_limit_bytes=vmem)


def _rms(x, g):
    return x * lax.rsqrt(jnp.mean(x * x, axis=-1, keepdims=True) + RMS_EPS) * g


def _dot(a, b):
    return jnp.dot(a, b, preferred_element_type=F32)


def _pack_bf16_pairs(x):
    half = x.shape[1] // 2
    hi = pltpu.bitcast(x[:, :half].astype(BF16).astype(F32), U32)
    lo = pltpu.bitcast(x[:, half:].astype(BF16).astype(F32), U32)
    return hi | lax.shift_right_logical(lo, jnp.uint32(16))


def _unpack_bf16_pairs(w):
    left = pltpu.bitcast(w & jnp.uint32(0xFFFF0000), F32).astype(BF16)
    right = pltpu.bitcast(lax.shift_left(w, jnp.uint32(16)), F32).astype(BF16)
    return left, right


def _dot_nt(a, b):
    return lax.dot_general(a, b, (((1,), (1,)), ((), ())), preferred_element_type=F32)


def _attn_proj_kernel(x_ref, g_ref, wq_ref, wk_ref, wvt_ref, qg_ref, kg_ref, seg_ref, q_ref, k_ref, vt_ref):
    h = _rms(x_ref[...], g_ref[...]).astype(BF16)
    seg = seg_ref[...]
    w = seg.shape[0]

    def head_norm(t, gain_ref, out_ref):
        for j in range(t.shape[1] // w):
            tj = t[:, j * w:(j + 1) * w]
            ms = _dot((tj * tj).astype(BF16), seg)
            out_ref[:, j * w:(j + 1) * w] = (tj * lax.rsqrt(ms + RMS_EPS) * gain_ref[:, j * w:(j + 1) * w]
                                             ).astype(out_ref.dtype)

    head_norm(_dot(h, wq_ref[...]), qg_ref, q_ref)
    head_norm(_dot(h, wk_ref[...]), kg_ref, k_ref)
    vt = _dot_nt(wvt_ref[...], h).astype(vt_ref.dtype)
    for j in range(vt_ref.shape[0]):
        vt_ref[j] = vt[:, j * ATT_BLOCK:(j + 1) * ATT_BLOCK]


def attention_project(x, g, w_in, q_g, k_g, n_q, n_kv, tm=PROJ_ROWS):
    n, d = x.shape
    dh = HEAD_DIM
    dq, dk = n_q * dh, n_kv * dh
    wq = w_in[:, :dq].astype(BF16)
    wk = w_in[:, dq:dq + dk].astype(BF16)
    wvt = w_in[:, dq + dk:].T.astype(BF16)
    qg = jnp.tile(q_g.astype(F32) * (dh ** -0.5 * LOG2E), n_q).reshape(1, dq)
    kg = jnp.tile(k_g.astype(F32), n_kv).reshape(1, dk)
    assert dq % MXU_DIM == 0 and dk % MXU_DIM == 0 and MXU_DIM % dh == 0
    hid = jnp.arange(MXU_DIM) // dh
    seg = jnp.where(hid[:, None] == hid[None, :], 1.0 / dh, 0.0).astype(BF16)
    const2 = lambda i: (0, 0)
    return pl.pallas_call(
        _attn_proj_kernel,
        grid=(n // tm,),
        in_specs=[pl.BlockSpec((tm, d), lambda i: (i, 0)),
                  pl.BlockSpec((1, d), const2),
                  pl.BlockSpec((d, dq), const2),
                  pl.BlockSpec((d, dk), const2),
                  pl.BlockSpec((dk, d), const2),
                  pl.BlockSpec((1, dq), const2),
                  pl.BlockSpec((1, dk), const2),
                  pl.BlockSpec((MXU_DIM, MXU_DIM), const2)],
        out_specs=[pl.BlockSpec((tm, dq), lambda i: (i, 0)),
                   pl.BlockSpec((tm, dk), lambda i: (i, 0)),
                   pl.BlockSpec((tm // ATT_BLOCK, dk, ATT_BLOCK), lambda i: (i, 0, 0))],
        out_shape=[jax.ShapeDtypeStruct((n, dq), BF16),
                   jax.ShapeDtypeStruct((n, dk), BF16),
                   jax.ShapeDtypeStruct((n // ATT_BLOCK, dk, ATT_BLOCK), BF16)],
        compiler_params=_cparams(1),
    )(x, g.reshape(1, d), wq, wk, wvt, qg, kg, seg)


def _dot_tn(a, b):
    return lax.dot_general(a, b, (((0,), (0,)), ((), ())), preferred_element_type=F32)


def _mm_res_router_kernel(a_ref, w_ref, x_ref, g_ref, wr_ref, x1_ref, hn_ref, aff_ref):
    _residual_norm_router(_dot(a_ref[...], w_ref[...]), x_ref, g_ref, wr_ref, x1_ref, hn_ref, aff_ref)


def _residual_norm_router(y, x_ref, g_ref, wr_ref, x1_ref, hn_ref, aff_ref):
    x1 = x_ref[...] + y
    x1_ref[...] = x1
    hn = _rms(x1, g_ref[...])
    hn_ref[...] = _pack_bf16_pairs(hn)
    logits = _dot_nt(wr_ref[...], hn.astype(BF16))
    mx = jnp.max(logits, axis=0, keepdims=True)
    p = jnp.exp(logits - mx)
    aff_ref[0] = p / jnp.sum(p, axis=0, keepdims=True)


def mm_res_router(a, w, x, g, wr_t, batch, tm=PROJ_ROWS):
    n, d = x.shape
    k = w.shape[0]
    e = wr_t.shape[0]
    t = n // batch
    tm = min(tm, t)
    tpb = t // tm
    return pl.pallas_call(
        _mm_res_router_kernel,
        grid=(n // tm,),
        in_specs=[pl.BlockSpec((tm, k), lambda i: (i, 0)),
                  pl.BlockSpec((k, d), lambda i: (0, 0)),
                  pl.BlockSpec((tm, d), lambda i: (i, 0)),
                  pl.BlockSpec((1, d), lambda i: (0, 0)),
                  pl.BlockSpec((e, d), lambda i: (0, 0))],
        out_specs=[pl.BlockSpec((tm, d), lambda i: (i, 0)),
                   pl.BlockSpec((tm, d // 2), lambda i: (i, 0)),
                   pl.BlockSpec((1, e, tm), lambda i: (i // tpb, 0, i % tpb))],
        out_shape=[jax.ShapeDtypeStruct((n, d), F32),
                   jax.ShapeDtypeStruct((n, d // 2), U32),
                   jax.ShapeDtypeStruct((batch, e, t), F32)],
        compiler_params=_cparams(1),
    )(a, w, x, g.reshape(1, d), wr_t)


def _t5_bucket(rel):
    nb = NUM_BUCKETS // 2
    ret = (rel > 0).astype(jnp.int32) * nb
    n = jnp.abs(rel)
    max_exact = nb // 2
    nf = jnp.maximum(n, 1).astype(jnp.float32)
    large = max_exact + (jnp.log(nf / max_exact) / math.log(MAX_DISTANCE / max_exact)
                         * (nb - max_exact)).astype(jnp.int32)
    large = jnp.minimum(large, nb - 1)
    return ret + jnp.where(n < max_exact, n, large)


def _attn_bucket_table():
    kk = jnp.arange(3 * ATT_BLOCK)[:, None]
    qq = jnp.arange(ATT_BLOCK)[None, :]
    rel = kk - ATT_BLOCK - qq
    return jnp.where(jnp.abs(rel) <= ATT_BLOCK, _t5_bucket(rel), -1).astype(I32)


ATT_QBLOCKS = 4


def _attn_kernel(sink_ref, rb_ref, q_ref, *rest, nb, n_kv):
    nq = ATT_QBLOCKS
    k_refs = rest[:nq + 2]
    v_refs = rest[nq + 2:2 * (nq + 2)]
    bucket_ref, wo_ref, x_ref, g_ref, wr_ref, x1_ref, hn_ref, aff_ref, bias_sc, ot_sc = rest[2 * (nq + 2):]
    n = pl.program_id(1)
    blk = ATT_BLOCK
    dh = HEAD_DIM
    gw = GQA_GROUP * blk

    @pl.when(jnp.logical_and(pl.program_id(0) == 0, n == 0))
    def _bias_table():
        bk = bucket_ref[...]
        for hq in range(n_kv * GQA_GROUP):
            acc = jnp.full(bk.shape, NEG, F32)
            for k in range(NUM_BUCKETS):
                acc = jnp.where(bk == k, rb_ref[k, hq] * LOG2E, acc)
            bias_sc[hq // GQA_GROUP, :, (hq % GQA_GROUP) * blk:(hq % GQA_GROUP + 1) * blk] = acc

    kblk = [r[...] for r in k_refs]
    vblk = [r[0] for r in v_refs]
    kidx = lax.broadcasted_iota(I32, (3 * blk, gw), 0)
    ones_rows = (lax.broadcasted_iota(I32, (16, 3 * blk), 0) == 0).astype(BF16)
    lane = lax.broadcasted_iota(I32, (1, gw), 1)
    heads = [[h * GQA_GROUP + g for g in range(GQA_GROUP)] for h in range(n_kv)]
    sk = []
    for h in range(n_kv):
        row_sink = jnp.full((1, gw), sink_ref[heads[h][-1]] * LOG2E, F32)
        for g in reversed(range(GQA_GROUP - 1)):
            row_sink = jnp.where(lane < (g + 1) * blk, sink_ref[heads[h][g]] * LOG2E, row_sink)
        sk.append(row_sink)
    keys, vt, valid = [], [], []
    for j in range(nq):
        i = n * nq + j
        keys.append(jnp.concatenate(kblk[j:j + 3], axis=0))
        vt.append(jnp.concatenate(vblk[j:j + 3], axis=1))
        valid.append(jnp.logical_and(jnp.logical_or(i > 0, kidx >= blk),
                                     jnp.logical_or(i < nb - 1, kidx < 2 * blk)))
    units = [(j, h) for j in range(nq) for h in range(n_kv)]
    q = [jnp.concatenate([q_ref[j * blk:(j + 1) * blk, hq * dh:(hq + 1) * dh] for hq in heads[h]], axis=0)
         for j, h in units]
    vaug = [jnp.concatenate([vt[j][h * dh:(h + 1) * dh, :], ones_rows], axis=0) for j, h in units]
    s = [jnp.where(valid[j], _dot_nt(keys[j][:, h * dh:(h + 1) * dh], q[u]) + bias_sc[h], NEG)
         for u, (j, h) in enumerate(units)]
    m = [jnp.maximum(jnp.max(s[u], axis=0, keepdims=True), sk[h]) for u, (j, h) in enumerate(units)]
    p = [jnp.exp2(s[u] - m[u]).astype(BF16) for u in range(len(units))]
    oa = [_dot(vaug[u], p[u]) for u in range(len(units))]
    o = [oa[u][:dh] / (oa[u][dh:dh + 1] + jnp.exp2(sk[h] - m[u])) for u, (j, h) in enumerate(units)]
    for u, (j, h) in enumerate(units):
        for g, hq in enumerate(heads[h]):
            ot_sc[hq * dh:(hq + 1) * dh, j * blk:(j + 1) * blk] = o[u][:, g * blk:(g + 1) * blk].astype(ot_sc.dtype)
    _residual_norm_router(_dot_tn(ot_sc[...], wo_ref[...]), x_ref, g_ref, wr_ref, x1_ref, hn_ref, aff_ref)


def attention_core(q, k, vt, rel_bias, sink, w_out, x, ffn_g, wr_t, batch, seq):
    d = x.shape[1]
    ne = wr_t.shape[0]
    n, dq = q.shape
    dk = k.shape[1]
    hq = rel_bias.shape[1]
    n_kv = hq // GQA_GROUP
    nb = seq // ATT_BLOCK
    blk = ATT_BLOCK
    nq = ATT_QBLOCKS
    assert nb % nq == 0

    def key_block(off):
        return lambda b, i: b * nb + jnp.clip(i * nq + off, 0, nb - 1)

    offs = range(-1, nq + 1)
    return pl.pallas_call(
        functools.partial(_attn_kernel, nb=nb, n_kv=n_kv),
        grid=(batch, nb // nq),
        in_specs=[pl.BlockSpec(memory_space=pltpu.SMEM),
                  pl.BlockSpec(memory_space=pltpu.SMEM),
                  pl.BlockSpec((nq * blk, dq), lambda b, i: (b * (nb // nq) + i, 0))]
                 + [pl.BlockSpec((blk, dk), (lambda f: lambda b, i: (f(b, i), 0))(key_block(o))) for o in offs]
                 + [pl.BlockSpec((1, dk, blk), (lambda f: lambda b, i: (f(b, i), 0, 0))(key_block(o))) for o in offs]
                 + [pl.BlockSpec((3 * blk, blk), lambda b, i: (0, 0)),
                    pl.BlockSpec((dq, d), lambda b, i: (0, 0)),
                    pl.BlockSpec((nq * blk, d), lambda b, i: (b * (nb // nq) + i, 0)),
                    pl.BlockSpec((1, d), lambda b, i: (0, 0)),
                    pl.BlockSpec((ne, d), lambda b, i: (0, 0))],
        out_specs=[pl.BlockSpec((nq * blk, d), lambda b, i: (b * (nb // nq) + i, 0)),
                   pl.BlockSpec((nq * blk, d // 2), lambda b, i: (b * (nb // nq) + i, 0)),
                   pl.BlockSpec((1, ne, nq * blk), lambda b, i: (b, 0, i))],
        out_shape=[jax.ShapeDtypeStruct((n, d), F32),
                   jax.ShapeDtypeStruct((n, d // 2), U32),
                   jax.ShapeDtypeStruct((batch, ne, seq), F32)],
        scratch_shapes=[pltpu.VMEM((n_kv, 3 * blk, GQA_GROUP * blk), F32), pltpu.VMEM((dq, nq * blk), BF16)],
        compiler_params=_cparams(2),
    )(sink.astype(F32), rel_bias.astype(F32), q, *([k] * (nq + 2)), *([vt] * (nq + 2)), _attn_bucket_table(),
      w_out, x, ffn_g.reshape(1, d), wr_t)


def _prefix_incl(x, tri):
    r, t = x.shape
    nck = t // LANE
    assert nck <= LANE
    xb = x.astype(BF16)
    local = [_dot(xb[:, c * LANE:(c + 1) * LANE], tri) for c in range(nck)]
    tot = jnp.concatenate([p[:, LANE - 1:LANE] for p in local] + [jnp.zeros((r, LANE - nck), F32)], axis=1)
    lane = lax.broadcasted_iota(I32, tot.shape, 1)
    inc = tot
    sh = 1
    while sh < nck:
        inc = inc + jnp.where(lane >= sh, pltpu.roll(inc, sh, axis=1), 0.0)
        sh *= 2
    offs = inc - tot
    return jnp.concatenate([local[c] + offs[:, c:c + 1] for c in range(nck)], axis=1)


BF16_EXACT = 256
TOK_SPLIT = 64


def _select_kernel(aff_ref, cmp_ref, idx_ref, dest_ref, offc_ref, rng_ref, pos_sc, dest_sc, vals_sc, *, cap, rchunk):
    aff = aff_ref[0]
    ne, t = aff.shape
    bidx = pl.program_id(0)
    bits = pltpu.bitcast(aff, I32)

    def enough(cand):
        return jnp.sum((bits >= cand).astype(I32), axis=1, keepdims=True) >= cap

    def search(i, lo):
        hb = lax.shift_left(jnp.int32(1), 30 - 2 * i)
        lb = lax.shift_left(jnp.int32(1), 29 - 2 * i)
        c1, c2, c3 = lo | lb, lo | hb, lo | hb | lb
        return jnp.where(enough(c3), c3, jnp.where(enough(c2), c2, jnp.where(enough(c1), c1, lo)))

    thr = lax.fori_loop(0, 15, search, jnp.zeros((ne, 1), I32))
    thr = jnp.where(enough(thr | 1), thr | 1, thr)
    gt = bits > thr
    eq = bits == thr
    need = (cap - jnp.sum(gt.astype(I32), axis=1, keepdims=True)).astype(F32)
    ri = lax.broadcasted_iota(I32, (LANE, LANE), 0)
    ci = lax.broadcasted_iota(I32, (LANE, LANE), 1)
    tri = (ri <= ci).astype(BF16)
    eqf = eq.astype(F32)
    rank_eq = _prefix_incl(eqf, tri) - eqf
    sel = jnp.logical_or(gt, jnp.logical_and(eq, rank_eq < need))
    self_ = sel.astype(F32)
    kt = jnp.sum(self_, axis=0, keepdims=True)
    pre = _prefix_incl(jnp.concatenate([self_, jnp.broadcast_to(kt, (8, t))], axis=0), tri)
    pos_sc[...] = jnp.where(sel, pre[:ne] - 1.0, -1.0)
    end = pre[ne:ne + 1]
    off = end - kt
    lr = lax.broadcasted_iota(I32, (ne, ne), 0)
    lc = lax.broadcasted_iota(I32, (ne, ne), 1)
    jexp = _dot((lc < lr).astype(BF16), self_.astype(BF16))
    dest_sc[...] = off + jexp

    tok = lax.broadcasted_iota(I32, (1, t), 1)
    vals_sc[0:1, :] = (tok // TOK_SPLIT).astype(F32)
    vals_sc[1:2, :] = (tok % TOK_SPLIT).astype(F32)
    vals_sc[7:8, :] = jnp.zeros((1, t), F32)
    slot = lax.broadcasted_iota(I32, (cap, 1), 0).astype(F32)

    def compact(e, carry):
        d = dest_sc[pl.ds(e, 1), :]
        dh = jnp.floor(d * (1.0 / LANE))
        g = aff_ref[0, pl.ds(e, 1), :]
        g0 = g.astype(BF16).astype(F32)
        g1 = (g - g0).astype(BF16).astype(F32)
        vals_sc[2:3, :] = dh
        vals_sc[3:4, :] = d - dh * LANE
        vals_sc[4:5, :] = g0
        vals_sc[5:6, :] = g1
        vals_sc[6:7, :] = g - g0 - g1
        onehot = (pos_sc[pl.ds(e, 1), :] == slot).astype(BF16)
        out = _dot_nt(vals_sc[...].astype(BF16), onehot)
        cmp_ref[0, e] = out
        idx_ref[e, 0] = (out[0:1] * TOK_SPLIT + out[1:2]).astype(I32) + bidx * t
        dest_ref[e + 1, 0] = (out[2:3] * LANE + out[3:4]).astype(I32) + bidx * (ne * cap)
        return carry

    assert ne % 2 == 0
    lax.fori_loop(0, ne // 2, lambda e2, c: compact(2 * e2 + 1, compact(2 * e2, c)), 0)
    idx_ref[ne, 0] = idx_ref[0, 0]
    dest_ref[0, 0] = (pl.num_programs(0) * ne + bidx) * cap + lax.broadcasted_iota(I32, (1, cap), 1)

    eh = jnp.floor(end * (1.0 / LANE))
    oh = jnp.floor(off * (1.0 / LANE))
    v4 = jnp.concatenate([oh, off - oh * LANE, eh, end - eh * LANE, jnp.zeros((4, t), F32)], axis=0)
    eye = (ri == ci).astype(BF16)
    for c in range(t // LANE):
        offc_ref[0, c * LANE:(c + 1) * LANE, :] = _dot_nt(eye, v4[:, c * LANE:(c + 1) * LANE].astype(BF16))

    nchunk = (ne * cap) // rchunk
    low = (lax.broadcasted_iota(I32, (nchunk, 1), 0) * rchunk).astype(F32)
    first = jnp.sum((end <= low).astype(F32), axis=1, keepdims=True)
    last = jnp.sum((end <= low + (rchunk - 1)).astype(F32), axis=1, keepdims=True)
    lane = lax.broadcasted_iota(I32, (nchunk, LANE), 1)
    tiles = jnp.where(lane < LANE // 2, jnp.floor(first * (1.0 / LANE)), jnp.floor(last * (1.0 / LANE)))
    rng_ref[0] = tiles.astype(I32)


COMBINE_ROWS = 1024
COMBINE_TILES = 5


def route_select(aff, cap, rchunk):
    batch, ne, t = aff.shape
    nchunk = ne * cap // rchunk
    assert t <= TOK_SPLIT * BF16_EXACT and ne * cap <= LANE * BF16_EXACT and LANE <= BF16_EXACT
    return pl.pallas_call(
        functools.partial(_select_kernel, cap=cap, rchunk=rchunk),
        grid=(batch,),
        in_specs=[pl.BlockSpec((1, ne, t), lambda b: (b, 0, 0))],
        out_specs=[pl.BlockSpec((1, ne, 8, cap), lambda b: (b, 0, 0, 0)),
                   pl.BlockSpec((ne + 1, 1, 1, cap), lambda b: (0, b, 0, 0)),
                   pl.BlockSpec((ne + 1, 1, 1, cap), lambda b: (0, b, 0, 0)),
                   pl.BlockSpec((1, t, 8), lambda b: (b, 0, 0)),
                   pl.BlockSpec((1, nchunk, LANE), lambda b: (b, 0, 0))],
        out_shape=[jax.ShapeDtypeStruct((batch, ne, 8, cap), F32),
                   jax.ShapeDtypeStruct((ne + 1, batch, 1, cap), I32),
                   jax.ShapeDtypeStruct((ne + 1, batch, 1, cap), I32),
                   jax.ShapeDtypeStruct((batch, t, 8), F32),
                   jax.ShapeDtypeStruct((batch, nchunk, LANE), I32)],
        scratch_shapes=[pltpu.VMEM((ne, t), F32), pltpu.VMEM((ne, t), F32), pltpu.VMEM((8, t), F32)],
        compiler_params=_cparams(1),
    )(aff)


def _ffn_kernel(idx_sm, dest_sm, hn_hbm, cmp_ref, w1_ref, w3_ref, w2_ref, r_hbm,
                xsu, xsb, yacc, ysc, wb1, wb3, wb2, gsem, ssem, *, batch, cap, nf, rt):
    e = pl.program_id(0)
    f = pl.program_id(1)
    ne = pl.num_programs(0)
    rows = batch * cap
    sub = rows // nf

    def gather_start(block, r):
        t = idx_sm[block * rows + r]
        pltpu.make_async_copy(hn_hbm.at[pl.ds(t, 1), :], xsu.at[pl.ds(r, 1), :], gsem).start()

    def gather_wait():
        pltpu.make_async_copy(hn_hbm.at[pl.ds(0, rows), :], xsu, gsem).wait()

    def scatter_start(block, r):
        d = dest_sm[block * rows + r]
        pltpu.make_async_copy(ysc.at[pl.ds(r, 1), :], r_hbm.at[pl.ds(d, 1), :], ssem).start()

    def scatter_wait():
        pltpu.make_async_copy(ysc, r_hbm.at[pl.ds(0, rows), :], ssem).wait()

    @pl.when(jnp.logical_and(e == 0, f == 0))
    def _prologue():
        def issue(r, c):
            gather_start(0, r)
            return c
        lax.fori_loop(0, rows, issue, 0)
        ysc[...] = jnp.zeros_like(ysc)
        yacc[...] = jnp.zeros_like(yacc)

    @pl.when(f == 0)
    def _rows_ready():
        gather_wait()
        half = xsu.shape[1]
        xsb[:, :half], xsb[:, half:] = _unpack_bf16_pairs(xsu[...])

    for r in range(sub):
        gather_start(e + 1, f * sub + r)
        scatter_start(e, f * sub + r)

    wb1[...] = w1_ref[0, 0].astype(BF16)
    for r in range(rows // rt):
        xs = xsb[r * rt:(r + 1) * rt, :]
        a = _dot(xs, wb1[...])
        if r == 0:
            wb3[...] = w3_ref[0, 0].astype(BF16)
        u = _dot(xs, wb3[...])
        hmid = (a * jax.nn.sigmoid(a) * u).astype(BF16)
        if r == 0:
            wb2[...] = w2_ref[0, 0].astype(BF16)
        y = _dot(hmid, wb2[...])
        yacc[r * rt:(r + 1) * rt, :] = jnp.where(f == 0, y, yacc[r * rt:(r + 1) * rt, :] + y)

    @pl.when(f == nf - 1)
    def _finish():
        scatter_wait()
        ri = lax.broadcasted_iota(I32, (cap, cap), 0)
        ci = lax.broadcasted_iota(I32, (cap, cap), 1)
        eye = (ri == ci).astype(BF16)
        rows8 = cmp_ref[...].reshape(batch * 8, cap).astype(BF16)
        gt = _dot_nt(eye, rows8)
        for b in range(batch):
            g = gt[:, 8 * b + 4:8 * b + 5] + gt[:, 8 * b + 5:8 * b + 6] + gt[:, 8 * b + 6:8 * b + 7]
            ysc[b * cap:(b + 1) * cap, :] = _pack_bf16_pairs(yacc[b * cap:(b + 1) * cap, :] * g)

        @pl.when(e == ne - 1)
        def _epilogue():
            def issue(r, c):
                scatter_start(ne, r)
                return c
            lax.fori_loop(0, rows, issue, 0)
            scatter_wait()
            gather_wait()


def expert_ffn(hn, cmp, idx_flat, dest_flat, w1, w3, w2, layer, cap, tf=512, rt=512):
    d = w1.shape[2]
    batch, ne = cmp.shape[0], cmp.shape[1]
    dff = w1.shape[3]
    nf = dff // tf
    rows = batch * cap
    rt = min(rt, rows)
    grid_spec = pltpu.PrefetchScalarGridSpec(
        num_scalar_prefetch=2,
        grid=(ne, nf),
        in_specs=[pl.BlockSpec(memory_space=pl.ANY),
                  pl.BlockSpec((batch, 1, 8, cap), lambda e, f, *_: (0, e, 0, 0)),
                  pl.BlockSpec((1, 1, d, tf), lambda e, f, *_: (layer, e, 0, f)),
                  pl.BlockSpec((1, 1, d, tf), lambda e, f, *_: (layer, e, 0, f)),
                  pl.BlockSpec((1, 1, tf, d), lambda e, f, *_: (layer, e, f, 0))],
        out_specs=pl.BlockSpec(memory_space=pl.ANY),
        scratch_shapes=[pltpu.VMEM((rows, d // 2), U32), pltpu.VMEM((rows, d), BF16),
                        pltpu.VMEM((rows, d), F32), pltpu.VMEM((rows, d // 2), U32),
                        pltpu.VMEM((d, tf), BF16), pltpu.VMEM((d, tf), BF16), pltpu.VMEM((tf, d), BF16),
                        pltpu.SemaphoreType.DMA, pltpu.SemaphoreType.DMA])
    return pl.pallas_call(
        functools.partial(_ffn_kernel, batch=batch, cap=cap, nf=nf, rt=rt),
        grid_spec=grid_spec,
        out_shape=jax.ShapeDtypeStruct((batch * ne * cap + rows, d // 2), U32),
        compiler_params=_cparams(2),
    )(idx_flat, dest_flat, hn, cmp, w1, w3, w2)


def _combine_kernel(tlo_sm, thi_sm, r_ref, offc_ref, x_hbm, o_hbm, acc, xsem, osem, *, batch, nchunk, rchunk):
    g = pl.program_id(0)
    b = g // nchunk
    j = g % nchunk
    slot = b % 2
    ntile = acc.shape[1] // LANE

    def x_copy(bb):
        return pltpu.make_async_copy(x_hbm.at[bb], acc.at[bb % 2], xsem.at[bb % 2])

    def o_copy(bb):
        return pltpu.make_async_copy(acc.at[bb % 2], o_hbm.at[bb], osem.at[bb % 2])

    @pl.when(g == 0)
    def _first():
        x_copy(0).start()

    @pl.when(j == 0)
    def _batch_start():
        x_copy(b).wait()

    @pl.when(j == nchunk // 2)
    def _mid():
        @pl.when(b > 0)
        def _():
            o_copy(b - 1).wait()

        @pl.when(b + 1 < batch)
        def _():
            x_copy(b + 1).start()

    half = r_ref.shape[1]
    left, right = _unpack_bf16_pairs(r_ref[...])
    rowid = (j * rchunk + lax.broadcasted_iota(I32, (1, rchunk), 1)).astype(F32)

    def contrib(i, valid):
        t0 = pl.multiple_of(i * LANE, LANE)
        oc = offc_ref[0, pl.ds(t0, LANE), :]
        off = oc[:, 0:1] * LANE + oc[:, 1:2]
        end = oc[:, 2:3] * LANE + oc[:, 3:4]
        p = jnp.logical_and(jnp.logical_and(rowid >= off, rowid < end), valid).astype(BF16)
        return t0, (_dot(p, left), _dot(p, right))

    def add(t0, y):
        acc[slot, pl.ds(t0, LANE), :half] += y[0]
        acc[slot, pl.ds(t0, LANE), half:] += y[1]

    def tile(i, c):
        add(*contrib(i, True))
        return c

    tlo = tlo_sm[g]
    thi = thi_sm[g]
    parts = [contrib(jnp.minimum(tlo + k, ntile - 1), tlo + k <= thi) for k in range(COMBINE_TILES)]
    for t0, y in parts:
        add(t0, y)
    lax.fori_loop(tlo + COMBINE_TILES, thi + 1, tile, 0)

    @pl.when(j == nchunk - 1)
    def _batch_end():
        o_copy(b).start()

        @pl.when(b == batch - 1)
        def _():
            o_copy(b).wait()


def combine(rbuf, offc, tlo, thi, x, rows_per_batch, rchunk):
    batch, t, d = x.shape
    nchunk = rows_per_batch // rchunk
    assert nchunk >= 2
    grid_spec = pltpu.PrefetchScalarGridSpec(
        num_scalar_prefetch=2,
        grid=(batch * nchunk,),
        in_specs=[pl.BlockSpec((rchunk, d // 2), lambda g, *_: (g, 0)),
                  pl.BlockSpec((1, t, 8), lambda g, *_: (g // nchunk, 0, 0)),
                  pl.BlockSpec(memory_space=pl.ANY)],
        out_specs=pl.BlockSpec(memory_space=pl.ANY),
        scratch_shapes=[pltpu.VMEM((2, t, d), F32), pltpu.SemaphoreType.DMA((2,)), pltpu.SemaphoreType.DMA((2,))])
    return pl.pallas_call(
        functools.partial(_combine_kernel, batch=batch, nchunk=nchunk, rchunk=rchunk),
        grid_spec=grid_spec,
        out_shape=jax.ShapeDtypeStruct((batch, t, d), F32),
        compiler_params=_cparams(1),
    )(tlo, thi, rbuf, offc, x)


def moe_block(x1, hn, aff, w1, w3, w2, layer, batch):
    n, d = x1.shape
    t = n // batch
    ne = aff.shape[1]
    cap = CAPACITY_FACTOR * t // ne
    cmp, idx, dest, offc, rng = route_select(aff, cap, COMBINE_ROWS)
    rbuf = expert_ffn(hn, cmp, idx.reshape(-1), dest.reshape(-1), w1, w3, w2, layer, cap)
    tlo = rng[:, :, 0].reshape(-1)
    thi = rng[:, :, LANE - 1].reshape(-1)
    return combine(rbuf, offc, tlo, thi, x1.reshape(batch, t, d), ne * cap, COMBINE_ROWS)


MLSTM_L = 256
MLSTM_UNROLL = 4


def _log_sigmoid(x):
    return jnp.minimum(x, 0.0) - jnp.log1p(jnp.exp(-jnp.abs(x)))


def _split3(x):
    x0 = x.astype(BF16)
    r = x - x0.astype(F32)
    x1 = r.astype(BF16)
    return x0, x1, (r - x1.astype(F32)).astype(BF16)


def _mlstm_proj_kernel(x_ref, g_ref, wm_ref, wkt_ref, wgt_ref, bt_ref, main_ref, kt_ref, gt_ref, *, nsub, nh):
    ln = MLSTM_L
    h = _rms(x_ref[...], g_ref[...]).astype(BF16)
    main_ref[...] = _dot(h, wm_ref[...]).astype(main_ref.dtype)
    kt = _dot_nt(wkt_ref[...], h).astype(kt_ref.dtype)
    pre = _dot_nt(wgt_ref[...], h) + bt_ref[...]
    lst = _log_sigmoid(pre)
    ri = lax.broadcasted_iota(I32, (ln, ln), 0)
    ci = lax.broadcasted_iota(I32, (ln, ln), 1)
    low = (ci <= ri).astype(BF16)
    upp = (ci >= ri).astype(BF16)
    row = lax.broadcasted_iota(I32, (pre.shape[0], ln), 0)
    fwd = jnp.logical_and(row >= nh, row < 2 * nh)
    bwd = row >= 3 * nh
    for j in range(nsub):
        sl = slice(j * ln, (j + 1) * ln)
        kt_ref[j] = kt[:, sl]
        pt = _split3(lst[:, sl])
        gt_ref[j] = jnp.where(fwd, sum(_dot(x, upp) for x in pt),
                              jnp.where(bwd, sum(_dot(x, low) for x in pt), pre[:, sl]))


def mlstm_project(x, g, wm, wkt, wgt, bias_t, nh, tm=PROJ_ROWS):
    n, d = x.shape
    nsub = tm // MLSTM_L
    nck = n // MLSTM_L
    const = lambda i: (0, 0)
    return pl.pallas_call(
        functools.partial(_mlstm_proj_kernel, nsub=nsub, nh=nh),
        grid=(n // tm,),
        in_specs=[pl.BlockSpec((tm, d), lambda i: (i, 0)),
                  pl.BlockSpec((1, d), const),
                  pl.BlockSpec(wm.shape, const),
                  pl.BlockSpec(wkt.shape, const),
                  pl.BlockSpec(wgt.shape, const),
                  pl.BlockSpec(bias_t.shape, const)],
        out_specs=[pl.BlockSpec((tm, wm.shape[1]), lambda i: (i, 0)),
                   pl.BlockSpec((nsub, wkt.shape[0], MLSTM_L), lambda i: (i, 0, 0)),
                   pl.BlockSpec((nsub, wgt.shape[0], MLSTM_L), lambda i: (i, 0, 0))],
        out_shape=[jax.ShapeDtypeStruct((n, wm.shape[1]), BF16),
                   jax.ShapeDtypeStruct((nck, wkt.shape[0], MLSTM_L), BF16),
                   jax.ShapeDtypeStruct((nck, wgt.shape[0], MLSTM_L), F32)],
        compiler_params=_cparams(1),
    )(x, g.reshape(1, d), wm, wkt, wgt, bias_t)


def _mlstm_kernel(q_ref, kt_ref, v_ref, og_ref, gt_ref, ng_ref, y_ref, hf, hb, cst, b_sc, u_sc, cm_sc,
                  *, nc, nh, dqk, dv):
    p = pl.program_id(1)
    ln = MLSTM_L
    ri = lax.broadcasted_iota(I32, (ln, ln), 0)
    ci = lax.broadcasted_iota(I32, (ln, ln), 1)
    masks = (ci <= ri, ci >= ri)
    ones_blk = jnp.ones((ln, dv), BF16)
    lane8 = lax.broadcasted_iota(I32, (8, ln), 1)
    row8 = lax.broadcasted_iota(I32, (8, ln), 0)
    bwd_row = (row8 % 2) == 1
    edge = lane8 == jnp.where(bwd_row, 0, ln - 1)
    kk = lax.broadcasted_iota(I32, (48, 2 * dv), 0) % 16
    cc = lax.broadcasted_iota(I32, (48, 2 * dv), 1)
    sel = [jnp.logical_or(jnp.logical_and(kk == i, cc < dv), jnp.logical_and(kk == 4 + i, cc >= dv)).astype(BF16)
           for i in range(4)]
    zero4 = jnp.zeros((4, ln), F32)
    cst[...] = jnp.zeros_like(cst)
    chains = [(hh, d) for hh in range(2) for d in range(2)]

    def gate_rows(c, which):
        return [gt_ref[c if d == 0 else nc - 1 - c, pl.ds((2 * d + which) * nh + 2 * p + hh, 1), :]
                for hh, d in chains] + [zero4]
    b_all = jnp.concatenate([x for c in range(nc) for x in gate_rows(c, 1)], axis=0)
    u_all = jnp.concatenate([x for c in range(nc) for x in gate_rows(c, 0)], axis=0) - b_all
    lane_a = lax.broadcasted_iota(I32, u_all.shape, 1)
    bwd_a = (lax.broadcasted_iota(I32, u_all.shape, 0) % 2) == 1
    cf = cb = u_all
    sh = 1
    while sh < ln:
        cf = jnp.maximum(cf, jnp.where(lane_a >= sh, pltpu.roll(cf, sh, axis=1), NEG))
        cb = jnp.maximum(cb, jnp.where(lane_a < ln - sh, pltpu.roll(cb, ln - sh, axis=1), NEG))
        sh *= 2
    b_sc[...] = b_all.reshape(nc, 8, ln)
    u_sc[...] = u_all.reshape(nc, 8, ln)
    cm_sc[...] = jnp.where(bwd_a, cb, cf).reshape(nc, 8, ln)

    def step(c, ms):
        cks = [c if d == 0 else nc - 1 - c for _, d in chains]
        r0s = [pl.multiple_of(ck * ln, ln) for ck in cks]
        cprev = [cst[i] for i in range(4)]
        q = [q_ref[pl.ds(r0s[i], ln), hh * dqk:(hh + 1) * dqk] for i, (hh, _) in enumerate(chains)]
        kt = [kt_ref[cks[i], hh * dqk:(hh + 1) * dqk, :] for i, (hh, _) in enumerate(chains)]
        vaug = [jnp.concatenate([v_ref[pl.ds(r0s[i], ln), hh * dv:(hh + 1) * dv], ones_blk], axis=1)
                for i, (hh, _) in enumerate(chains)]
        s = [_dot(q[i], kt[i]) for i in range(4)]
        inter = [_dot(q[i], cprev[i].astype(BF16)) for i in range(4)]
        b = b_sc[c]
        u = u_sc[c]
        m_run = jnp.maximum(ms, cm_sc[c])
        m_end = jnp.max(jnp.where(edge, m_run, NEG), axis=1, keepdims=True)
        b_end = jnp.sum(jnp.where(edge, b, 0.0), axis=1, keepdims=True)
        wc = jnp.exp(u - m_end)
        decay = jnp.exp(ms - m_end)
        rows = jnp.concatenate([m_run[:4] * LOG2E, jnp.exp(-(b + m_run))[:4], zero4, zero4], axis=0)
        stack = jnp.concatenate(_split3(rows), axis=0)
        u2 = u * LOG2E
        ms2 = ms * LOG2E
        bc = [lax.dot_general(stack, sel[i], (((0,), (0,)), ((), ())), preferred_element_type=F32)
              for i in range(4)]
        mb = [jnp.concatenate([bc[i][:, :dv]] * (ln // dv), axis=1) for i in range(4)]
        sw = [(jnp.exp2(jnp.where(masks[d], u2[i:i + 1, :] - mb[i], NEG)) * s[i]).astype(BF16)
              for i, (_, d) in enumerate(chains)]
        intra = [_dot(sw[i], vaug[i]) for i in range(4)]
        ea = [jnp.exp2(ms2[i:i + 1, :] - bc[i][:, :dv]) for i in range(4)]
        num = [ea[i] * inter[i][:, :dv] + intra[i][:, :dv] for i in range(4)]
        den = [ea[i] * inter[i][:, dv:] + intra[i][:, dv:] for i in range(4)]
        hc = [num[i] / jnp.maximum(jnp.abs(den[i]), bc[i][:, dv:]) for i in range(4)]
        kw = [(kt[i].astype(F32) * wc[i:i + 1, :]).astype(BF16) for i in range(4)]
        c_new = [decay[i:i + 1, :] * cprev[i] + _dot(kw[i], vaug[i]) for i in range(4)]
        for i, (hh, d) in enumerate(chains):
            cst[i] = c_new[i]
            (hf if d == 0 else hb)[pl.ds(r0s[i], ln), hh * dv:(hh + 1) * dv] = hc[i]
        return b_end + m_end

    unroll = math.gcd(nc, MLSTM_UNROLL)

    def steps(cu, ms):
        for k in range(unroll):
            ms = step(cu * unroll + k, ms)
        return ms

    lax.fori_loop(0, nc // unroll, steps, jnp.zeros((8, 1), F32))
    for hh in range(2):
        sl = slice(hh * dv, (hh + 1) * dv)
        hs = _rms(hf[:, sl] + hb[:, sl], ng_ref[:, sl])
        y_ref[:, sl] = (hs * jax.nn.sigmoid(og_ref[:, sl].astype(F32))).astype(y_ref.dtype)


def mlstm_core(main, ktc, grow, out_g, batch, seq, nh, dqk, dv):
    n = main.shape[0]
    nc = seq // MLSTM_L
    npair = nh // 2
    vblocks = (nh * dqk) // (2 * dv)
    ogblocks = (nh * dqk + nh * dv) // (2 * dv)
    return pl.pallas_call(
        functools.partial(_mlstm_kernel, nc=nc, nh=nh, dqk=dqk, dv=dv),
        grid=(batch, npair),
        in_specs=[pl.BlockSpec((seq, 2 * dqk), lambda b, p: (b, p)),
                  pl.BlockSpec((nc, 2 * dqk, MLSTM_L), lambda b, p: (b, p, 0)),
                  pl.BlockSpec((seq, 2 * dv), lambda b, p: (b, vblocks + p)),
                  pl.BlockSpec((seq, 2 * dv), lambda b, p: (b, ogblocks + p)),
                  pl.BlockSpec((nc, 4 * nh, MLSTM_L), lambda b, p: (b, 0, 0)),
                  pl.BlockSpec((1, 2 * dv), lambda b, p: (0, p))],
        out_specs=pl.BlockSpec((seq, 2 * dv), lambda b, p: (b, p)),
        out_shape=jax.ShapeDtypeStruct((n, nh * dv), BF16),
        scratch_shapes=[pltpu.VMEM((seq, 2 * dv), F32), pltpu.VMEM((seq, 2 * dv), F32),
                        pltpu.VMEM((4, dqk, 2 * dv), F32)] + [pltpu.VMEM((nc, 8, MLSTM_L), F32)] * 3,
        compiler_params=_cparams(2),
    )(main, ktc, main, main, grow, out_g.reshape(1, nh * dv).astype(F32))


def mlstm_layer(x2d, batch, seq, norm_g, w_in, b_i, b_f, out_g, w_out, ffn_g, wr):
    d = x2d.shape[1]
    nh = MLSTM_HEADS
    dv = d // nh
    dqk = dv // 2
    o1, o2, o4 = nh * dqk, 2 * nh * dqk, 2 * nh * dqk + 2 * nh * dv
    wm = jnp.concatenate([w_in[:, :o1], w_in[:, o2:o4]], axis=1).astype(BF16)
    wkt = (w_in[:, o1:o2] * (dqk ** -0.5)).T.astype(BF16)
    wgt = w_in[:, o4:].T.astype(BF16)
    bias_t = jnp.concatenate([b_i[0], b_f[0], b_i[1], b_f[1]]).reshape(4 * nh, 1).astype(F32)
    main, ktc, grow = mlstm_project(x2d, norm_g, wm, wkt, wgt, bias_t, nh)
    y = mlstm_core(main, ktc, grow, out_g, batch, seq, nh, dqk, dv)
    return mm_res_router(y, w_out.astype(BF16), x2d, ffn_g, wr.T.astype(BF16), batch)


def attention_layer(x2d, batch, seq, norm_g, w_in, q_g, k_g, sink, w_out, rel_bias, ffn_g, wr):
    n_q = rel_bias.shape[1]
    q, k, vt = attention_project(x2d, norm_g, w_in, q_g, k_g, n_q, n_q // GQA_GROUP)
    return attention_core(q, k, vt, rel_bias, sink, w_out.astype(BF16), x2d, ffn_g, wr.T.astype(BF16), batch, seq)


def kernel(x, rel_bias, attn_norm_g, attn_w_in, attn_q_norm_g, attn_k_norm_g, attn_sink, attn_w_out, mlstm_norm_g, mlstm_w_in, mlstm_b_i, mlstm_b_f, mlstm_out_norm_g, mlstm_w_out, ffn_norm_g, router_w, expert_w1, expert_w3, expert_w2):
    batch, seq, d = x.shape
    x2d = x.reshape(batch * seq, d)
    x1, hn, aff = attention_layer(x2d, batch, seq, attn_norm_g[0], attn_w_in[0], attn_q_norm_g[0],
                                  attn_k_norm_g[0], attn_sink[0], attn_w_out[0], rel_bias,
                                  ffn_norm_g[0], router_w[0])
    x = moe_block(x1, hn, aff, expert_w1, expert_w3, expert_w2, 0, batch)
    x1, hn, aff = mlstm_layer(x.reshape(batch * seq, d), batch, seq, mlstm_norm_g[0], mlstm_w_in[0], mlstm_b_i[0],
                              mlstm_b_f[0], mlstm_out_norm_g[0], mlstm_w_out[0], ffn_norm_g[1], router_w[1])
    return moe_block(x1, hn, aff, expert_w1, expert_w3, expert_w2, 1, batch)
```

```python
import functools
import math

import jax
import jax.numpy as jnp
from jax import lax
from jax.experimental import pallas as pl
from jax.experimental.pallas import tpu as pltpu

F32 = jnp.float32
BF16 = jnp.bfloat16
I32 = jnp.int32
U32 = jnp.uint32

RMS_EPS = 1e-6
NEG = -1e30
LOG2E = 1.4426950408889634
LANE = 128
MXU_DIM = 256
VMEM_LIMIT = 56 * 1024 * 1024
PROJ_ROWS = 1024

HEAD_DIM = 64
GQA_GROUP = 4
ATT_BLOCK = 128
NUM_BUCKETS = 32
MAX_DISTANCE = 128
CAPACITY_FACTOR = 2
MLSTM_HEADS = 8


def _cparams(n_axes, vmem=VMEM_LIMIT):
    return pltpu.CompilerParams(dimension_semantics=("arbitrary",) * n_axes, vmem_limit_bytes=vmem)


def _rms(x, g):
    return x * lax.rsqrt(jnp.mean(x * x, axis=-1, keepdims=True) + RMS_EPS) * g


def _dot(a, b):
    return jnp.dot(a, b, preferred_element_type=F32)


def _pack_bf16_pairs(x):
    half = x.shape[1] // 2
    hi = pltpu.bitcast(x[:, :half].astype(BF16).astype(F32), U32)
    lo = pltpu.bitcast(x[:, half:].astype(BF16).astype(F32), U32)
    return hi | lax.shift_right_logical(lo, jnp.uint32(16))


def _unpack_bf16_pairs(w):
    left = pltpu.bitcast(w & jnp.uint32(0xFFFF0000), F32).astype(BF16)
    right = pltpu.bitcast(lax.shift_left(w, jnp.uint32(16)), F32).astype(BF16)
    return left, right


def _dot_nt(a, b):
    return lax.dot_general(a, b, (((1,), (1,)), ((), ())), preferred_element_type=F32)


def _attn_proj_kernel(x_ref, g_ref, wq_ref, wk_ref, wvt_ref, qg_ref, kg_ref, seg_ref, q_ref, k_ref, vt_ref):
    h = _rms(x_ref[...], g_ref[...]).astype(BF16)
    seg = seg_ref[...]
    w = seg.shape[0]

    def head_norm(t, gain_ref, out_ref):
        for j in range(t.shape[1] // w):
            tj = t[:, j * w:(j + 1) * w]
            ms = _dot((tj * tj).astype(BF16), seg)
            out_ref[:, j * w:(j + 1) * w] = (tj * lax.rsqrt(ms + RMS_EPS) * gain_ref[:, j * w:(j + 1) * w]
                                             ).astype(out_ref.dtype)

    head_norm(_dot(h, wq_ref[...]), qg_ref, q_ref)
    head_norm(_dot(h, wk_ref[...]), kg_ref, k_ref)
    vt = _dot_nt(wvt_ref[...], h).astype(vt_ref.dtype)
    for j in range(vt_ref.shape[0]):
        vt_ref[j] = vt[:, j * ATT_BLOCK:(j + 1) * ATT_BLOCK]


def attention_project(x, g, w_in, q_g, k_g, n_q, n_kv, tm=PROJ_ROWS):
    n, d = x.shape
    dh = HEAD_DIM
    dq, dk = n_q * dh, n_kv * dh
    wq = w_in[:, :dq].astype(BF16)
    wk = w_in[:, dq:dq + dk].astype(BF16)
    wvt = w_in[:, dq + dk:].T.astype(BF16)
    qg = jnp.tile(q_g.astype(F32) * (dh ** -0.5 * LOG2E), n_q).reshape(1, dq)
    kg = jnp.tile(k_g.astype(F32), n_kv).reshape(1, dk)
    assert dq % MXU_DIM == 0 and dk % MXU_DIM == 0 and MXU_DIM % dh == 0
    hid = jnp.arange(MXU_DIM) // dh
    seg = jnp.where(hid[:, None] == hid[None, :], 1.0 / dh, 0.0).astype(BF16)
    const2 = lambda i: (0, 0)
    return pl.pallas_call(
        _attn_proj_kernel,
        grid=(n // tm,),
        in_specs=[pl.BlockSpec((tm, d), lambda i: (i, 0)),
                  pl.BlockSpec((1, d), const2),
                  pl.BlockSpec((d, dq), const2),
                  pl.BlockSpec((d, dk), const2),
                  pl.BlockSpec((dk, d), const2),
                  pl.BlockSpec((1, dq), const2),
                  pl.BlockSpec((1, dk), const2),
                  pl.BlockSpec((MXU_DIM, MXU_DIM), const2)],
        out_specs=[pl.BlockSpec((tm, dq), lambda i: (i, 0)),
                   pl.BlockSpec((tm, dk), lambda i: (i, 0)),
                   pl.BlockSpec((tm // ATT_BLOCK, dk, ATT_BLOCK), lambda i: (i, 0, 0))],
        out_shape=[jax.ShapeDtypeStruct((n, dq), BF16),
                   jax.ShapeDtypeStruct((n, dk), BF16),
                   jax.ShapeDtypeStruct((n // ATT_BLOCK, dk, ATT_BLOCK), BF16)],
        compiler_params=_cparams(1),
    )(x, g.reshape(1, d), wq, wk, wvt, qg, kg, seg)


def _dot_tn(a, b):
    return lax.dot_general(a, b, (((0,), (0,)), ((), ())), preferred_element_type=F32)


def _mm_res_router_kernel(a_ref, w_ref, x_ref, g_ref, wr_ref, x1_ref, hn_ref, aff_ref):
    _residual_norm_router(_dot(a_ref[...], w_ref[...]), x_ref, g_ref, wr_ref, x1_ref, hn_ref, aff_ref)


def _residual_norm_router(y, x_ref, g_ref, wr_ref, x1_ref, hn_ref, aff_ref):
    x1 = x_ref[...] + y
    x1_ref[...] = x1
    hn = _rms(x1, g_ref[...])
    hn_ref[...] = _pack_bf16_pairs(hn)
    logits = _dot_nt(wr_ref[...], hn.astype(BF16))
    mx = jnp.max(logits, axis=0, keepdims=True)
    p = jnp.exp(logits - mx)
    aff_ref[0] = p / jnp.sum(p, axis=0, keepdims=True)


def mm_res_router(a, w, x, g, wr_t, batch, tm=PROJ_ROWS):
    n, d = x.shape
    k = w.shape[0]
    e = wr_t.shape[0]
    t = n // batch
    tm = min(tm, t)
    tpb = t // tm
    return pl.pallas_call(
        _mm_res_router_kernel,
        grid=(n // tm,),
        in_specs=[pl.BlockSpec((tm, k), lambda i: (i, 0)),
                  pl.BlockSpec((k, d), lambda i: (0, 0)),
                  pl.BlockSpec((tm, d), lambda i: (i, 0)),
                  pl.BlockSpec((1, d), lambda i: (0, 0)),
                  pl.BlockSpec((e, d), lambda i: (0, 0))],
        out_specs=[pl.BlockSpec((tm, d), lambda i: (i, 0)),
                   pl.BlockSpec((tm, d // 2), lambda i: (i, 0)),
                   pl.BlockSpec((1, e, tm), lambda i: (i // tpb, 0, i % tpb))],
        out_shape=[jax.ShapeDtypeStruct((n, d), F32),
                   jax.ShapeDtypeStruct((n, d // 2), U32),
                   jax.ShapeDtypeStruct((batch, e, t), F32)],
        compiler_params=_cparams(1),
    )(a, w, x, g.reshape(1, d), wr_t)


def _t5_bucket(rel):
    nb = NUM_BUCKETS // 2
    ret = (rel > 0).astype(jnp.int32) * nb
    n = jnp.abs(rel)
    max_exact = nb // 2
    nf = jnp.maximum(n, 1).astype(jnp.float32)
    large = max_exact + (jnp.log(nf / max_exact) / math.log(MAX_DISTANCE / max_exact)
                         * (nb - max_exact)).astype(jnp.int32)
    large = jnp.minimum(large, nb - 1)
    return ret + jnp.where(n < max_exact, n, large)


def _attn_bucket_table():
    kk = jnp.arange(3 * ATT_BLOCK)[:, None]
    qq = jnp.arange(ATT_BLOCK)[None, :]
    rel = kk - ATT_BLOCK - qq
    return jnp.where(jnp.abs(rel) <= ATT_BLOCK, _t5_bucket(rel), -1).astype(I32)


ATT_QBLOCKS = 4


def _attn_kernel(sink_ref, rb_ref, q_ref, *rest, nb, n_kv):
    nq = ATT_QBLOCKS
    k_refs = rest[:nq + 2]
    v_refs = rest[nq + 2:2 * (nq + 2)]
    bucket_ref, wo_ref, x_ref, g_ref, wr_ref, x1_ref, hn_ref, aff_ref, bias_sc, ot_sc = rest[2 * (nq + 2):]
    n = pl.program_id(1)
    blk = ATT_BLOCK
    dh = HEAD_DIM
    gw = GQA_GROUP * blk

    @pl.when(jnp.logical_and(pl.program_id(0) == 0, n == 0))
    def _bias_table():
        bk = bucket_ref[...]
        for hq in range(n_kv * GQA_GROUP):
            acc = jnp.full(bk.shape, NEG, F32)
            for k in range(NUM_BUCKETS):
                acc = jnp.where(bk == k, rb_ref[k, hq] * LOG2E, acc)
            bias_sc[hq // GQA_GROUP, :, (hq % GQA_GROUP) * blk:(hq % GQA_GROUP + 1) * blk] = acc

    kblk = [r[...] for r in k_refs]
    vblk = [r[0] for r in v_refs]
    kidx = lax.broadcasted_iota(I32, (3 * blk, gw), 0)
    ones_rows = (lax.broadcasted_iota(I32, (16, 3 * blk), 0) == 0).astype(BF16)
    lane = lax.broadcasted_iota(I32, (1, gw), 1)
    heads = [[h * GQA_GROUP + g for g in range(GQA_GROUP)] for h in range(n_kv)]
    sk = []
    for h in range(n_kv):
        row_sink = jnp.full((1, gw), sink_ref[heads[h][-1]] * LOG2E, F32)
        for g in reversed(range(GQA_GROUP - 1)):
            row_sink = jnp.where(lane < (g + 1) * blk, sink_ref[heads[h][g]] * LOG2E, row_sink)
        sk.append(row_sink)
    keys, vt, valid = [], [], []
    for j in range(nq):
        i = n * nq + j
        keys.append(jnp.concatenate(kblk[j:j + 3], axis=0))
        vt.append(jnp.concatenate(vblk[j:j + 3], axis=1))
        valid.append(jnp.logical_and(jnp.logical_or(i > 0, kidx >= blk),
                                     jnp.logical_or(i < nb - 1, kidx < 2 * blk)))
    units = [(j, h) for j in range(nq) for h in range(n_kv)]
    q = [jnp.concatenate([q_ref[j * blk:(j + 1) * blk, hq * dh:(hq + 1) * dh] for hq in heads[h]], axis=0)
         for j, h in units]
    vaug = [jnp.concatenate([vt[j][h * dh:(h + 1) * dh, :], ones_rows], axis=0) for j, h in units]
    s = [jnp.where(valid[j], _dot_nt(keys[j][:, h * dh:(h + 1) * dh], q[u]) + bias_sc[h], NEG)
         for u, (j, h) in enumerate(units)]
    m = [jnp.maximum(jnp.max(s[u], axis=0, keepdims=True), sk[h]) for u, (j, h) in enumerate(units)]
    p = [jnp.exp2(s[u] - m[u]).astype(BF16) for u in range(len(units))]
    oa = [_dot(vaug[u], p[u]) for u in range(len(units))]
    o = [oa[u][:dh] / (oa[u][dh:dh + 1] + jnp.exp2(sk[h] - m[u])) for u, (j, h) in enumerate(units)]
    for u, (j, h) in enumerate(units):
        for g, hq in enumerate(heads[h]):
            ot_sc[hq * dh:(hq + 1) * dh, j * blk:(j + 1) * blk] = o[u][:, g * blk:(g + 1) * blk].astype(ot_sc.dtype)
    _residual_norm_router(_dot_tn(ot_sc[...], wo_ref[...]), x_ref, g_ref, wr_ref, x1_ref, hn_ref, aff_ref)


def attention_core(q, k, vt, rel_bias, sink, w_out, x, ffn_g, wr_t, batch, seq):
    d = x.shape[1]
    ne = wr_t.shape[0]
    n, dq = q.shape
    dk = k.shape[1]
    hq = rel_bias.shape[1]
    n_kv = hq // GQA_GROUP
    nb = seq // ATT_BLOCK
    blk = ATT_BLOCK
    nq = ATT_QBLOCKS
    assert nb % nq == 0

    def key_block(off):
        return lambda b, i: b * nb + jnp.clip(i * nq + off, 0, nb - 1)

    offs = range(-1, nq + 1)
    return pl.pallas_call(
        functools.partial(_attn_kernel, nb=nb, n_kv=n_kv),
        grid=(batch, nb // nq),
        in_specs=[pl.BlockSpec(memory_space=pltpu.SMEM),
                  pl.BlockSpec(memory_space=pltpu.SMEM),
                  pl.BlockSpec((nq * blk, dq), lambda b, i: (b * (nb // nq) + i, 0))]
                 + [pl.BlockSpec((blk, dk), (lambda f: lambda b, i: (f(b, i), 0))(key_block(o))) for o in offs]
                 + [pl.BlockSpec((1, dk, blk), (lambda f: lambda b, i: (f(b, i), 0, 0))(key_block(o))) for o in offs]
                 + [pl.BlockSpec((3 * blk, blk), lambda b, i: (0, 0)),
                    pl.BlockSpec((dq, d), lambda b, i: (0, 0)),
                    pl.BlockSpec((nq * blk, d), lambda b, i: (b * (nb // nq) + i, 0)),
                    pl.BlockSpec((1, d), lambda b, i: (0, 0)),
                    pl.BlockSpec((ne, d), lambda b, i: (0, 0))],
        out_specs=[pl.BlockSpec((nq * blk, d), lambda b, i: (b * (nb // nq) + i, 0)),
                   pl.BlockSpec((nq * blk, d // 2), lambda b, i: (b * (nb // nq) + i, 0)),
                   pl.BlockSpec((1, ne, nq * blk), lambda b, i: (b, 0, i))],
        out_shape=[jax.ShapeDtypeStruct((n, d), F32),
                   jax.ShapeDtypeStruct((n, d // 2), U32),
                   jax.ShapeDtypeStruct((batch, ne, seq), F32)],
        scratch_shapes=[pltpu.VMEM((n_kv, 3 * blk, GQA_GROUP * blk), F32), pltpu.VMEM((dq, nq * blk), BF16)],
        compiler_params=_cparams(2),
    )(sink.astype(F32), rel_bias.astype(F32), q, *([k] * (nq + 2)), *([vt] * (nq + 2)), _attn_bucket_table(),
      w_out, x, ffn_g.reshape(1, d), wr_t)


def _prefix_incl(x, tri):
    r, t = x.shape
    nck = t // LANE
    assert nck <= LANE
    xb = x.astype(BF16)
    local = [_dot(xb[:, c * LANE:(c + 1) * LANE], tri) for c in range(nck)]
    tot = jnp.concatenate([p[:, LANE - 1:LANE] for p in local] + [jnp.zeros((r, LANE - nck), F32)], axis=1)
    lane = lax.broadcasted_iota(I32, tot.shape, 1)
    inc = tot
    sh = 1
    while sh < nck:
        inc = inc + jnp.where(lane >= sh, pltpu.roll(inc, sh, axis=1), 0.0)
        sh *= 2
    offs = inc - tot
    return jnp.concatenate([local[c] + offs[:, c:c + 1] for c in range(nck)], axis=1)


BF16_EXACT = 256
TOK_SPLIT = 64


def _select_kernel(aff_ref, cmp_ref, idx_ref, dest_ref, offc_ref, rng_ref, pos_sc, dest_sc, vals_sc, *, cap, rchunk):
    aff = aff_ref[0]
    ne, t = aff.shape
    bidx = pl.program_id(0)
    bits = pltpu.bitcast(aff, I32)

    def enough(cand):
        return jnp.sum((bits >= cand).astype(I32), axis=1, keepdims=True) >= cap

    def search(i, lo):
        hb = lax.shift_left(jnp.int32(1), 30 - 2 * i)
        lb = lax.shift_left(jnp.int32(1), 29 - 2 * i)
        c1, c2, c3 = lo | lb, lo | hb, lo | hb | lb
        return jnp.where(enough(c3), c3, jnp.where(enough(c2), c2, jnp.where(enough(c1), c1, lo)))

    thr = lax.fori_loop(0, 15, search, jnp.zeros((ne, 1), I32))
    thr = jnp.where(enough(thr | 1), thr | 1, thr)
    gt = bits > thr
    eq = bits == thr
    need = (cap - jnp.sum(gt.astype(I32), axis=1, keepdims=True)).astype(F32)
    ri = lax.broadcasted_iota(I32, (LANE, LANE), 0)
    ci = lax.broadcasted_iota(I32, (LANE, LANE), 1)
    tri = (ri <= ci).astype(BF16)
    eqf = eq.astype(F32)
    rank_eq = _prefix_incl(eqf, tri) - eqf
    sel = jnp.logical_or(gt, jnp.logical_and(eq, rank_eq < need))
    self_ = sel.astype(F32)
    kt = jnp.sum(self_, axis=0, keepdims=True)
    pre = _prefix_incl(jnp.concatenate([self_, jnp.broadcast_to(kt, (8, t))], axis=0), tri)
    pos_sc[...] = jnp.where(sel, pre[:ne] - 1.0, -1.0)
    end = pre[ne:ne + 1]
    off = end - kt
    lr = lax.broadcasted_iota(I32, (ne, ne), 0)
    lc = lax.broadcasted_iota(I32, (ne, ne), 1)
    jexp = _dot((lc < lr).astype(BF16), self_.astype(BF16))
    dest_sc[...] = off + jexp

    tok = lax.broadcasted_iota(I32, (1, t), 1)
    vals_sc[0:1, :] = (tok // TOK_SPLIT).astype(F32)
    vals_sc[1:2, :] = (tok % TOK_SPLIT).astype(F32)
    vals_sc[7:8, :] = jnp.zeros((1, t), F32)
    slot = lax.broadcasted_iota(I32, (cap, 1), 0).astype(F32)

    def compact(e, carry):
        d = dest_sc[pl.ds(e, 1), :]
        dh = jnp.floor(d * (1.0 / LANE))
        g = aff_ref[0, pl.ds(e, 1), :]
        g0 = g.astype(BF16).astype(F32)
        g1 = (g - g0).astype(BF16).astype(F32)
        vals_sc[2:3, :] = dh
        vals_sc[3:4, :] = d - dh * LANE
        vals_sc[4:5, :] = g0
        vals_sc[5:6, :] = g1
        vals_sc[6:7, :] = g - g0 - g1
        onehot = (pos_sc[pl.ds(e, 1), :] == slot).astype(BF16)
        out = _dot_nt(vals_sc[...].astype(BF16), onehot)
        cmp_ref[0, e] = out
        idx_ref[e, 0] = (out[0:1] * TOK_SPLIT + out[1:2]).astype(I32) + bidx * t
        dest_ref[e + 1, 0] = (out[2:3] * LANE + out[3:4]).astype(I32) + bidx * (ne * cap)
        return carry

    assert ne % 2 == 0
    lax.fori_loop(0, ne // 2, lambda e2, c: compact(2 * e2 + 1, compact(2 * e2, c)), 0)
    idx_ref[ne, 0] = idx_ref[0, 0]
    dest_ref[0, 0] = (pl.num_programs(0) * ne + bidx) * cap + lax.broadcasted_iota(I32, (1, cap), 1)

    eh = jnp.floor(end * (1.0 / LANE))
    oh = jnp.floor(off * (1.0 / LANE))
    v4 = jnp.concatenate([oh, off - oh * LANE, eh, end - eh * LANE, jnp.zeros((4, t), F32)], axis=0)
    eye = (ri == ci).astype(BF16)
    for c in range(t // LANE):
        offc_ref[0, c * LANE:(c + 1) * LANE, :] = _dot_nt(eye, v4[:, c * LANE:(c + 1) * LANE].astype(BF16))

    nchunk = (ne * cap) // rchunk
    low = (lax.broadcasted_iota(I32, (nchunk, 1), 0) * rchunk).astype(F32)
    first = jnp.sum((end <= low).astype(F32), axis=1, keepdims=True)
    last = jnp.sum((end <= low + (rchunk - 1)).astype(F32), axis=1, keepdims=True)
    lane = lax.broadcasted_iota(I32, (nchunk, LANE), 1)
    tiles = jnp.where(lane < LANE // 2, jnp.floor(first * (1.0 / LANE)), jnp.floor(last * (1.0 / LANE)))
    rng_ref[0] = tiles.astype(I32)


COMBINE_ROWS = 1024
COMBINE_TILES = 5


def route_select(aff, cap, rchunk):
    batch, ne, t = aff.shape
    nchunk = ne * cap // rchunk
    assert t <= TOK_SPLIT * BF16_EXACT and ne * cap <= LANE * BF16_EXACT and LANE <= BF16_EXACT
    return pl.pallas_call(
        functools.partial(_select_kernel, cap=cap, rchunk=rchunk),
        grid=(batch,),
        in_specs=[pl.BlockSpec((1, ne, t), lambda b: (b, 0, 0))],
        out_specs=[pl.BlockSpec((1, ne, 8, cap), lambda b: (b, 0, 0, 0)),
                   pl.BlockSpec((ne + 1, 1, 1, cap), lambda b: (0, b, 0, 0)),
                   pl.BlockSpec((ne + 1, 1, 1, cap), lambda b: (0, b, 0, 0)),
                   pl.BlockSpec((1, t, 8), lambda b: (b, 0, 0)),
                   pl.BlockSpec((1, nchunk, LANE), lambda b: (b, 0, 0))],
        out_shape=[jax.ShapeDtypeStruct((batch, ne, 8, cap), F32),
                   jax.ShapeDtypeStruct((ne + 1, batch, 1, cap), I32),
                   jax.ShapeDtypeStruct((ne + 1, batch, 1, cap), I32),
                   jax.ShapeDtypeStruct((batch, t, 8), F32),
                   jax.ShapeDtypeStruct((batch, nchunk, LANE), I32)],
        scratch_shapes=[pltpu.VMEM((ne, t), F32), pltpu.VMEM((ne, t), F32), pltpu.VMEM((8, t), F32)],
        compiler_params=_cparams(1),
    )(aff)


def _ffn_kernel(idx_sm, dest_sm, hn_hbm, cmp_ref, w1_ref, w3_ref, w2_ref, r_hbm,
                xsu, xsb, yacc, ysc, wb1, wb3, wb2, gsem, ssem, *, batch, cap, nf, rt):
    e = pl.program_id(0)
    f = pl.program_id(1)
    ne = pl.num_programs(0)
    rows = batch * cap
    sub = rows // nf

    def gather_start(block, r):
        t = idx_sm[block * rows + r]
        pltpu.make_async_copy(hn_hbm.at[pl.ds(t, 1), :], xsu.at[pl.ds(r, 1), :], gsem).start(priority=1)

    def gather_wait():
        pltpu.make_async_copy(hn_hbm.at[pl.ds(0, rows), :], xsu, gsem).wait()

    def scatter_start(block, r):
        d = dest_sm[block * rows + r]
        pltpu.make_async_copy(ysc.at[pl.ds(r, 1), :], r_hbm.at[pl.ds(d, 1), :], ssem).start(priority=1)

    def scatter_wait():
        pltpu.make_async_copy(ysc, r_hbm.at[pl.ds(0, rows), :], ssem).wait()

    @pl.when(jnp.logical_and(e == 0, f == 0))
    def _prologue():
        def issue(r, c):
            gather_start(0, r)
            return c
        lax.fori_loop(0, rows, issue, 0)
        ysc[...] = jnp.zeros_like(ysc)
        yacc[...] = jnp.zeros_like(yacc)

    @pl.when(f == 0)
    def _rows_ready():
        gather_wait()
        half = xsu.shape[1]
        xsb[:, :half], xsb[:, half:] = _unpack_bf16_pairs(xsu[...])

    for r in range(sub):
        gather_start(e + 1, f * sub + r)
        scatter_start(e, f * sub + r)

    wb1[...] = w1_ref[0, 0].astype(BF16)
    for r in range(rows // rt):
        xs = xsb[r * rt:(r + 1) * rt, :]
        a = _dot(xs, wb1[...])
        if r == 0:
            wb3[...] = w3_ref[0, 0].astype(BF16)
        u = _dot(xs, wb3[...])
        hmid = (a * jax.nn.sigmoid(a) * u).astype(BF16)
        if r == 0:
            wb2[...] = w2_ref[0, 0].astype(BF16)
        y = _dot(hmid, wb2[...])
        yacc[r * rt:(r + 1) * rt, :] = jnp.where(f == 0, y, yacc[r * rt:(r + 1) * rt, :] + y)

    @pl.when(f == nf - 1)
    def _finish():
        scatter_wait()
        ri = lax.broadcasted_iota(I32, (cap, cap), 0)
        ci = lax.broadcasted_iota(I32, (cap, cap), 1)
        eye = (ri == ci).astype(BF16)
        rows8 = cmp_ref[...].reshape(batch * 8, cap).astype(BF16)
        gt = _dot_nt(eye, rows8)
        for b in range(batch):
            g = gt[:, 8 * b + 4:8 * b + 5] + gt[:, 8 * b + 5:8 * b + 6] + gt[:, 8 * b + 6:8 * b + 7]
            ysc[b * cap:(b + 1) * cap, :] = _pack_bf16_pairs(yacc[b * cap:(b + 1) * cap, :] * g)

        @pl.when(e == ne - 1)
        def _epilogue():
            def issue(r, c):
                scatter_start(ne, r)
                return c
            lax.fori_loop(0, rows, issue, 0)
            scatter_wait()
            gather_wait()


def expert_ffn(hn, cmp, idx_flat, dest_flat, w1, w3, w2, layer, cap, tf=512, rt=512):
    d = w1.shape[2]
    batch, ne = cmp.shape[0], cmp.shape[1]
    dff = w1.shape[3]
    nf = dff // tf
    rows = batch * cap
    rt = min(rt, rows)
    grid_spec = pltpu.PrefetchScalarGridSpec(
        num_scalar_prefetch=2,
        grid=(ne, nf),
        in_specs=[pl.BlockSpec(memory_space=pl.ANY),
                  pl.BlockSpec((batch, 1, 8, cap), lambda e, f, *_: (0, e, 0, 0)),
                  pl.BlockSpec((1, 1, d, tf), lambda e, f, *_: (layer, e, 0, f)),
                  pl.BlockSpec((1, 1, d, tf), lambda e, f, *_: (layer, e, 0, f)),
                  pl.BlockSpec((1, 1, tf, d), lambda e, f, *_: (layer, e, f, 0))],
        out_specs=pl.BlockSpec(memory_space=pl.ANY),
        scratch_shapes=[pltpu.VMEM((rows, d // 2), U32), pltpu.VMEM((rows, d), BF16),
                        pltpu.VMEM((rows, d), F32), pltpu.VMEM((rows, d // 2), U32),
                        pltpu.VMEM((d, tf), BF16), pltpu.VMEM((d, tf), BF16), pltpu.VMEM((tf, d), BF16),
                        pltpu.SemaphoreType.DMA, pltpu.SemaphoreType.DMA])
    return pl.pallas_call(
        functools.partial(_ffn_kernel, batch=batch, cap=cap, nf=nf, rt=rt),
        grid_spec=grid_spec,
        out_shape=jax.ShapeDtypeStruct((batch * ne * cap + rows, d // 2), U32),
        compiler_params=_cparams(2),
    )(idx_flat, dest_flat, hn, cmp, w1, w3, w2)


def _combine_kernel(tlo_sm, thi_sm, r_ref, offc_ref, x_hbm, o_hbm, acc, xsem, osem, *, batch, nchunk, rchunk):
    g = pl.program_id(0)
    b = g // nchunk
    j = g % nchunk
    slot = b % 2
    ntile = acc.shape[1] // LANE

    def x_copy(bb):
        return pltpu.make_async_copy(x_hbm.at[bb], acc.at[bb % 2], xsem.at[bb % 2])

    def o_copy(bb):
        return pltpu.make_async_copy(acc.at[bb % 2], o_hbm.at[bb], osem.at[bb % 2])

    @pl.when(g == 0)
    def _first():
        x_copy(0).start()

    @pl.when(j == 0)
    def _batch_start():
        x_copy(b).wait()

    @pl.when(j == nchunk // 2)
    def _mid():
        @pl.when(b > 0)
        def _():
            o_copy(b - 1).wait()

        @pl.when(b + 1 < batch)
        def _():
            x_copy(b + 1).start()

    half = r_ref.shape[1]
    left, right = _unpack_bf16_pairs(r_ref[...])
    rowid = (j * rchunk + lax.broadcasted_iota(I32, (1, rchunk), 1)).astype(F32)

    def contrib(i, valid):
        t0 = pl.multiple_of(i * LANE, LANE)
        oc = offc_ref[0, pl.ds(t0, LANE), :]
        off = oc[:, 0:1] * LANE + oc[:, 1:2]
        end = oc[:, 2:3] * LANE + oc[:, 3:4]
        p = jnp.logical_and(jnp.logical_and(rowid >= off, rowid < end), valid).astype(BF16)
        return t0, (_dot(p, left), _dot(p, right))

    def add(t0, y):
        acc[slot, pl.ds(t0, LANE), :half] += y[0]
        acc[slot, pl.ds(t0, LANE), half:] += y[1]

    def tile(i, c):
        add(*contrib(i, True))
        return c

    tlo = tlo_sm[g]
    thi = thi_sm[g]
    parts = [contrib(jnp.minimum(tlo + k, ntile - 1), tlo + k <= thi) for k in range(COMBINE_TILES)]
    for t0, y in parts:
        add(t0, y)
    lax.fori_loop(tlo + COMBINE_TILES, thi + 1, tile, 0)

    @pl.when(j == nchunk - 1)
    def _batch_end():
        o_copy(b).start()

        @pl.when(b == batch - 1)
        def _():
            o_copy(b).wait()


def combine(rbuf, offc, tlo, thi, x, rows_per_batch, rchunk):
    batch, t, d = x.shape
    nchunk = rows_per_batch // rchunk
    assert nchunk >= 2
    grid_spec = pltpu.PrefetchScalarGridSpec(
        num_scalar_prefetch=2,
        grid=(batch * nchunk,),
        in_specs=[pl.BlockSpec((rchunk, d // 2), lambda g, *_: (g, 0)),
                  pl.BlockSpec((1, t, 8), lambda g, *_: (g // nchunk, 0, 0)),
                  pl.BlockSpec(memory_space=pl.ANY)],
        out_specs=pl.BlockSpec(memory_space=pl.ANY),
        scratch_shapes=[pltpu.VMEM((2, t, d), F32), pltpu.SemaphoreType.DMA((2,)), pltpu.SemaphoreType.DMA((2,))])
    return pl.pallas_call(
        functools.partial(_combine_kernel, batch=batch, nchunk=nchunk, rchunk=rchunk),
        grid_spec=grid_spec,
        out_shape=jax.ShapeDtypeStruct((batch, t, d), F32),
        compiler_params=_cparams(1),
    )(tlo, thi, rbuf, offc, x)


def moe_block(x1, hn, aff, w1, w3, w2, layer, batch):
    n, d = x1.shape
    t = n // batch
    ne = aff.shape[1]
    cap = CAPACITY_FACTOR * t // ne
    cmp, idx, dest, offc, rng = route_select(aff, cap, COMBINE_ROWS)
    rbuf = expert_ffn(hn, cmp, idx.reshape(-1), dest.reshape(-1), w1, w3, w2, layer, cap)
    tlo = rng[:, :, 0].reshape(-1)
    thi = rng[:, :, LANE - 1].reshape(-1)
    return combine(rbuf, offc, tlo, thi, x1.reshape(batch, t, d), ne * cap, COMBINE_ROWS)


MLSTM_L = 256
MLSTM_UNROLL = 4


def _log_sigmoid(x):
    return jnp.minimum(x, 0.0) - jnp.log1p(jnp.exp(-jnp.abs(x)))


def _split3(x):
    x0 = x.astype(BF16)
    r = x - x0.astype(F32)
    x1 = r.astype(BF16)
    return x0, x1, (r - x1.astype(F32)).astype(BF16)


def _mlstm_proj_kernel(x_ref, g_ref, wm_ref, wkt_ref, wgt_ref, bt_ref, main_ref, kt_ref, gt_ref, *, nsub, nh):
    ln = MLSTM_L
    h = _rms(x_ref[...], g_ref[...]).astype(BF16)
    main_ref[...] = _dot(h, wm_ref[...]).astype(main_ref.dtype)
    kt = _dot_nt(wkt_ref[...], h).astype(kt_ref.dtype)
    pre = _dot_nt(wgt_ref[...], h) + bt_ref[...]
    lst = _log_sigmoid(pre)
    ri = lax.broadcasted_iota(I32, (ln, ln), 0)
    ci = lax.broadcasted_iota(I32, (ln, ln), 1)
    low = (ci <= ri).astype(BF16)
    upp = (ci >= ri).astype(BF16)
    row = lax.broadcasted_iota(I32, (pre.shape[0], ln), 0)
    fwd = jnp.logical_and(row >= nh, row < 2 * nh)
    bwd = row >= 3 * nh
    for j in range(nsub):
        sl = slice(j * ln, (j + 1) * ln)
        kt_ref[j] = kt[:, sl]
        pt = _split3(lst[:, sl])
        gt_ref[j] = jnp.where(fwd, sum(_dot(x, upp) for x in pt),
                              jnp.where(bwd, sum(_dot(x, low) for x in pt), pre[:, sl]))


def mlstm_project(x, g, wm, wkt, wgt, bias_t, nh, tm=PROJ_ROWS):
    n, d = x.shape
    nsub = tm // MLSTM_L
    nck = n // MLSTM_L
    const = lambda i: (0, 0)
    return pl.pallas_call(
        functools.partial(_mlstm_proj_kernel, nsub=nsub, nh=nh),
        grid=(n // tm,),
        in_specs=[pl.BlockSpec((tm, d), lambda i: (i, 0)),
                  pl.BlockSpec((1, d), const),
                  pl.BlockSpec(wm.shape, const),
                  pl.BlockSpec(wkt.shape, const),
                  pl.BlockSpec(wgt.shape, const),
                  pl.BlockSpec(bias_t.shape, const)],
        out_specs=[pl.BlockSpec((tm, wm.shape[1]), lambda i: (i, 0)),
                   pl.BlockSpec((nsub, wkt.shape[0], MLSTM_L), lambda i: (i, 0, 0)),
                   pl.BlockSpec((nsub, wgt.shape[0], MLSTM_L), lambda i: (i, 0, 0))],
        out_shape=[jax.ShapeDtypeStruct((n, wm.shape[1]), BF16),
                   jax.ShapeDtypeStruct((nck, wkt.shape[0], MLSTM_L), BF16),
                   jax.ShapeDtypeStruct((nck, wgt.shape[0], MLSTM_L), F32)],
        compiler_params=_cparams(1),
    )(x, g.reshape(1, d), wm, wkt, wgt, bias_t)


def _mlstm_kernel(q_ref, kt_ref, v_ref, og_ref, gt_ref, ng_ref, y_ref, hf, hb, cst, b_sc, u_sc, cm_sc,
                  *, nc, nh, dqk, dv):
    p = pl.program_id(1)
    ln = MLSTM_L
    ri = lax.broadcasted_iota(I32, (ln, ln), 0)
    ci = lax.broadcasted_iota(I32, (ln, ln), 1)
    masks = (ci <= ri, ci >= ri)
    ones_blk = jnp.ones((ln, dv), BF16)
    lane8 = lax.broadcasted_iota(I32, (8, ln), 1)
    row8 = lax.broadcasted_iota(I32, (8, ln), 0)
    bwd_row = (row8 % 2) == 1
    edge = lane8 == jnp.where(bwd_row, 0, ln - 1)
    kk = lax.broadcasted_iota(I32, (48, 2 * dv), 0) % 16
    cc = lax.broadcasted_iota(I32, (48, 2 * dv), 1)
    sel = [jnp.logical_or(jnp.logical_and(kk == i, cc < dv), jnp.logical_and(kk == 4 + i, cc >= dv)).astype(BF16)
           for i in range(4)]
    zero4 = jnp.zeros((4, ln), F32)
    cst[...] = jnp.zeros_like(cst)
    chains = [(hh, d) for hh in range(2) for d in range(2)]

    def gate_rows(c, which):
        return [gt_ref[c if d == 0 else nc - 1 - c, pl.ds((2 * d + which) * nh + 2 * p + hh, 1), :]
                for hh, d in chains] + [zero4]
    b_all = jnp.concatenate([x for c in range(nc) for x in gate_rows(c, 1)], axis=0)
    u_all = jnp.concatenate([x for c in range(nc) for x in gate_rows(c, 0)], axis=0) - b_all
    lane_a = lax.broadcasted_iota(I32, u_all.shape, 1)
    bwd_a = (lax.broadcasted_iota(I32, u_all.shape, 0) % 2) == 1
    cf = cb = u_all
    sh = 1
    while sh < ln:
        cf = jnp.maximum(cf, jnp.where(lane_a >= sh, pltpu.roll(cf, sh, axis=1), NEG))
        cb = jnp.maximum(cb, jnp.where(lane_a < ln - sh, pltpu.roll(cb, ln - sh, axis=1), NEG))
        sh *= 2
    b_sc[...] = b_all.reshape(nc, 8, ln)
    u_sc[...] = u_all.reshape(nc, 8, ln)
    cm_sc[...] = jnp.where(bwd_a, cb, cf).reshape(nc, 8, ln)

    def step(c, ms):
        cks = [c if d == 0 else nc - 1 - c for _, d in chains]
        r0s = [pl.multiple_of(ck * ln, ln) for ck in cks]
        cprev = [cst[i] for i in range(4)]
        q = [q_ref[pl.ds(r0s[i], ln), hh * dqk:(hh + 1) * dqk] for i, (hh, _) in enumerate(chains)]
        kt = [kt_ref[cks[i], hh * dqk:(hh + 1) * dqk, :] for i, (hh, _) in enumerate(chains)]
        vaug = [jnp.concatenate([v_ref[pl.ds(r0s[i], ln), hh * dv:(hh + 1) * dv], ones_blk], axis=1)
                for i, (hh, _) in enumerate(chains)]
        s = [_dot(q[i], kt[i]) for i in range(4)]
        inter = [_dot(q[i], cprev[i].astype(BF16)) for i in range(4)]
        b = b_sc[c]
        u = u_sc[c]
        m_run = jnp.maximum(ms, cm_sc[c])
        m_end = jnp.max(jnp.where(edge, m_run, NEG), axis=1, keepdims=True)
        b_end = jnp.sum(jnp.where(edge, b, 0.0), axis=1, keepdims=True)
        wc = jnp.exp(u - m_end)
        decay = jnp.exp(ms - m_end)
        rows = jnp.concatenate([m_run[:4] * LOG2E, jnp.exp(-(b + m_run))[:4], zero4, zero4], axis=0)
        stack = jnp.concatenate(_split3(rows), axis=0)
        u2 = u * LOG2E
        ms2 = ms * LOG2E
        bc = [lax.dot_general(stack, sel[i], (((0,), (0,)), ((), ())), preferred_element_type=F32)
              for i in range(4)]
        mb = [jnp.concatenate([bc[i][:, :dv]] * (ln // dv), axis=1) for i in range(4)]
        sw = [(jnp.exp2(jnp.where(masks[d], u2[i:i + 1, :] - mb[i], NEG)) * s[i]).astype(BF16)
              for i, (_, d) in enumerate(chains)]
        intra = [_dot(sw[i], vaug[i]) for i in range(4)]
        ea = [jnp.exp2(ms2[i:i + 1, :] - bc[i][:, :dv]) for i in range(4)]
        num = [ea[i] * inter[i][:, :dv] + intra[i][:, :dv] for i in range(4)]
        den = [ea[i] * inter[i][:, dv:] + intra[i][:, dv:] for i in range(4)]
        hc = [num[i] / jnp.maximum(jnp.abs(den[i]), bc[i][:, dv:]) for i in range(4)]
        kw = [(kt[i].astype(F32) * wc[i:i + 1, :]).astype(BF16) for i in range(4)]
        c_new = [decay[i:i + 1, :] * cprev[i] + _dot(kw[i], vaug[i]) for i in range(4)]
        for i, (hh, d) in enumerate(chains):
            cst[i] = c_new[i]
            (hf if d == 0 else hb)[pl.ds(r0s[i], ln), hh * dv:(hh + 1) * dv] = hc[i]
        return b_end + m_end

    unroll = math.gcd(nc, MLSTM_UNROLL)

    def steps(cu, ms):
        for k in range(unroll):
            ms = step(cu * unroll + k, ms)
        return ms

    lax.fori_loop(0, nc // unroll, steps, jnp.zeros((8, 1), F32))
    for hh in range(2):
        sl = slice(hh * dv, (hh + 1) * dv)
        hs = _rms(hf[:, sl] + hb[:, sl], ng_ref[:, sl])
        y_ref[:, sl] = (hs * jax.nn.sigmoid(og_ref[:, sl].astype(F32))).astype(y_ref.dtype)


def mlstm_core(main, ktc, grow, out_g, batch, seq, nh, dqk, dv):
    n = main.shape[0]
    nc = seq // MLSTM_L
    npair = nh // 2
    vblocks = (nh * dqk) // (2 * dv)
    ogblocks = (nh * dqk + nh * dv) // (2 * dv)
    return pl.pallas_call(
        functools.partial(_mlstm_kernel, nc=nc, nh=nh, dqk=dqk, dv=dv),
        grid=(batch, npair),
        in_specs=[pl.BlockSpec((seq, 2 * dqk), lambda b, p: (b, p)),
                  pl.BlockSpec((nc, 2 * dqk, MLSTM_L), lambda b, p: (b, p, 0)),
                  pl.BlockSpec((seq, 2 * dv), lambda b, p: (b, vblocks + p)),
                  pl.BlockSpec((seq, 2 * dv), lambda b, p: (b, ogblocks + p)),
                  pl.BlockSpec((nc, 4 * nh, MLSTM_L), lambda b, p: (b, 0, 0)),
                  pl.BlockSpec((1, 2 * dv), lambda b, p: (0, p))],
        out_specs=pl.BlockSpec((seq, 2 * dv), lambda b, p: (b, p)),
        out_shape=jax.ShapeDtypeStruct((n, nh * dv), BF16),
        scratch_shapes=[pltpu.VMEM((seq, 2 * dv), F32), pltpu.VMEM((seq, 2 * dv), F32),
                        pltpu.VMEM((4, dqk, 2 * dv), F32)] + [pltpu.VMEM((nc, 8, MLSTM_L), F32)] * 3,
        compiler_params=_cparams(2),
    )(main, ktc, main, main, grow, out_g.reshape(1, nh * dv).astype(F32))


def mlstm_layer(x2d, batch, seq, norm_g, w_in, b_i, b_f, out_g, w_out, ffn_g, wr):
    d = x2d.shape[1]
    nh = MLSTM_HEADS
    dv = d // nh
    dqk = dv // 2
    o1, o2, o4 = nh * dqk, 2 * nh * dqk, 2 * nh * dqk + 2 * nh * dv
    wm = jnp.concatenate([w_in[:, :o1], w_in[:, o2:o4]], axis=1).astype(BF16)
    wkt = (w_in[:, o1:o2] * (dqk ** -0.5)).T.astype(BF16)
    wgt = w_in[:, o4:].T.astype(BF16)
    bias_t = jnp.concatenate([b_i[0], b_f[0], b_i[1], b_f[1]]).reshape(4 * nh, 1).astype(F32)
    main, ktc, grow = mlstm_project(x2d, norm_g, wm, wkt, wgt, bias_t, nh)
    y = mlstm_core(main, ktc, grow, out_g, batch, seq, nh, dqk, dv)
    return mm_res_router(y, w_out.astype(BF16), x2d, ffn_g, wr.T.astype(BF16), batch)


def attention_layer(x2d, batch, seq, norm_g, w_in, q_g, k_g, sink, w_out, rel_bias, ffn_g, wr):
    n_q = rel_bias.shape[1]
    q, k, vt = attention_project(x2d, norm_g, w_in, q_g, k_g, n_q, n_q // GQA_GROUP)
    return attention_core(q, k, vt, rel_bias, sink, w_out.astype(BF16), x2d, ffn_g, wr.T.astype(BF16), batch, seq)


def kernel(x, rel_bias, attn_norm_g, attn_w_in, attn_q_norm_g, attn_k_norm_g, attn_sink, attn_w_out, mlstm_norm_g, mlstm_w_in, mlstm_b_i, mlstm_b_f, mlstm_out_norm_g, mlstm_w_out, ffn_norm_g, router_w, expert_w1, expert_w3, expert_w2):
    batch, seq, d = x.shape
    x2d = x.reshape(batch * seq, d)
    x1, hn, aff = attention_layer(x2d, batch, seq, attn_norm_g[0], attn_w_in[0], attn_q_norm_g[0],
                                  attn_k_norm_g[0], attn_sink[0], attn_w_out[0], rel_bias,
                                  ffn_norm_g[0], router_w[0])
    x = moe_block(x1, hn, aff, expert_w1, expert_w3, expert_w2, 0, batch)
    x1, hn, aff = mlstm_layer(x.reshape(batch * seq, d), batch, seq, mlstm_norm_g[0], mlstm_w_in[0], mlstm_b_i[0],
                              mlstm_b_f[0], mlstm_out_norm_g[0], mlstm_w_out[0], ffn_norm_g[1], router_w[1])
    return moe_block(x1, hn, aff, expert_w1, expert_w3, expert_w2, 1, batch)
```
